```python
import math
import jax, jax.numpy as jnp
from jax import lax
import numpy as np

D_MODEL = 1024
BATCH = 4
SEQ = 8192
DEPTH = 2

GRID_W = 64
CTX_LEN = 256
N_EVEN = (DEPTH + 1) // 2
N_ODD = DEPTH // 2
N_MOD = 6

HY_WIDTH = 512
HY_SHORT_CONV = 3
HY_EMB = 33
HY_BANDS = (HY_EMB - 1) // 2
HY_FFN = 64
HY_TARGET = 1e-2
HY_FAST_DECAY = 0.3
HY_SLOW_DECAY = 1.5
HY_DECAY_MIN = math.log(HY_TARGET) / HY_SLOW_DECAY
HY_DECAY_MAX = math.log(HY_TARGET) / HY_FAST_DECAY

MLA_HEADS = 8
MLA_NOPE = 64
MLA_ROPE = 32
MLA_V = 64
MLA_Q_RANK = 256
MLA_KV_RANK = 128
MLA_SCALE = (MLA_NOPE + MLA_ROPE) ** -0.5
Q_BLOCK = 128
ROPE_THETA = 10000.0

OFF_Q = 3 * HY_WIDTH
OFF_KV = OFF_Q + MLA_Q_RANK
OFF_KPE = OFF_KV + MLA_KV_RANK
IN_A = OFF_KPE + MLA_ROPE
MIX_A_OUT = HY_WIDTH + MLA_HEADS * MLA_V

SSD_INNER = 2 * D_MODEL
SSD_HEADDIM = 64
SSD_HEADS = SSD_INNER // SSD_HEADDIM
SSD_GROUPS = 4
SSD_STATE = 128
SSD_CONV = 3
SSD_CHUNK = 128
SSD_XBC = SSD_INNER + 2 * SSD_GROUPS * SSD_STATE
SSD_IN = SSD_INNER + SSD_XBC + 2 * SSD_HEADS

N_EXPERTS = 256
TOP_K = 8
N_EXPERT_GROUPS = 8
TOPK_GROUPS = 4
EXPERT_DIM = 256
SHARED_DIM = 256
ROUTED_SCALE = 2.5
MOE_BLOCK = 128

DN_ALPHA = (2 * DEPTH) ** 0.25
DN_BETA = (8 * DEPTH) ** -0.25
LN_EPS = 1e-5
RMS_EPS = 1e-6

kernel_name = "hybrid_hyena_mla_ssd_moe_diffusion"


def layer_norm(x, g, b):
    xf = x.astype(jnp.float32)
    mu = jnp.mean(xf, -1, keepdims=True)
    var = jnp.mean(jnp.square(xf - mu), -1, keepdims=True)
    return ((xf - mu) * lax.rsqrt(var + LN_EPS) * g.astype(jnp.float32) + b.astype(jnp.float32)).astype(x.dtype)


def rms_norm(x, g):
    xf = x.astype(jnp.float32)
    return (xf * lax.rsqrt(jnp.mean(xf * xf, -1, keepdims=True) + RMS_EPS) * g.astype(jnp.float32)).astype(x.dtype)


def modulate(x, shift, scale):
    return x * (1 + scale) + shift


def dwconv_centered(x, w, b):
    pad = w.shape[0] // 2
    y = lax.conv_general_dilated(x, w[:, None, :], window_strides=(1,), padding=[(pad, pad)],
                                 dimension_numbers=('NWC', 'WIO', 'NWC'), feature_group_count=x.shape[-1])
    return y + b


def swiglu(x, w_gu, w_down):
    gu = x @ w_gu
    g, u = jnp.split(gu, 2, axis=-1)
    return (jax.nn.silu(g) * u) @ w_down


def axial_rope_tables(n_tokens):
    rows = n_tokens // GRID_W
    row = jnp.repeat(jnp.arange(rows), GRID_W).astype(jnp.float32)
    col = jnp.tile(jnp.arange(GRID_W), rows).astype(jnp.float32)
    half = MLA_ROPE // 2
    inv = ROPE_THETA ** (-jnp.arange(0, half, 2, dtype=jnp.float32) / half)
    ang = jnp.concatenate([row[:, None] * inv, col[:, None] * inv], -1)
    return jnp.cos(ang), jnp.sin(ang)


def apply_rope(x, cos, sin):
    xf = x.astype(jnp.float32)
    x1, x2 = xf[..., 0::2], xf[..., 1::2]
    out = jnp.stack([x1 * cos - x2 * sin, x1 * sin + x2 * cos], -1).reshape(x.shape)
    return out.astype(x.dtype)


def hyena_filters(n, w1, b1, f1, w2, b2, f2, w3):
    f32 = jnp.float32
    t = jnp.linspace(0.0, 1.0, n, dtype=f32)[:, None]
    w = 2.0 * math.pi * jnp.arange(n, dtype=f32)[:, None] / n
    fr = jnp.linspace(1e-4, HY_BANDS - 1, HY_BANDS, dtype=f32)
    z = jnp.concatenate([t, jnp.cos(fr * w), -jnp.sin(fr * w)], -1)
    h = jnp.sin(f1.astype(f32) * (z @ w1.astype(f32) + b1.astype(f32)))
    h = jnp.sin(f2.astype(f32) * (h @ w2.astype(f32) + b2.astype(f32)))
    h = h @ w3.astype(f32)
    deltas = jnp.abs(jnp.linspace(HY_DECAY_MIN, HY_DECAY_MAX, HY_WIDTH, dtype=f32))
    decay = jnp.exp(-t * deltas)
    h_f = h[:, :HY_WIDTH] * decay
    h_b = h[:, HY_WIDTH:] * decay
    k = jnp.concatenate([h_f, jnp.zeros((1, HY_WIDTH), f32), h_b[:0:-1]], 0)
    return k / jnp.sum(jnp.abs(k), 0, keepdims=True)


def hyena_sequence(proj, conv_w, conv_b, filt, skip):
    n = proj.shape[1]
    u = dwconv_centered(proj, conv_w, conv_b)
    x0, x1, v = jnp.split(u, 3, axis=-1)
    z = (v * x1).astype(jnp.float32)
    k = hyena_filters(n, *filt)
    y = jnp.fft.irfft(jnp.fft.rfft(z, n=2 * n, axis=1) * jnp.fft.rfft(k, n=2 * n, axis=0)[None],
                      n=2 * n, axis=1)[:, :n]
    y = y + z * skip.astype(jnp.float32)
    return (y * x0.astype(jnp.float32)).astype(proj.dtype)


def mla_attend(q_nope, q_pe, k_nope, k_pe, v):
    s = (jnp.einsum('bhqd,bhkd->bhqk', q_nope, k_nope, preferred_element_type=jnp.float32)
         + jnp.einsum('bhqd,bkd->bhqk', q_pe, k_pe, preferred_element_type=jnp.float32)) * MLA_SCALE
    p = jax.nn.softmax(s, axis=-1).astype(v.dtype)
    return jnp.einsum('bhqk,bhkd->bhqd', p, v)


def attend_blocks(q_nope, q_pe, k_nope, k_pe, v):
    bsz, h, n, _ = q_nope.shape
    nb = n // Q_BLOCK

    def to_blocks(t):
        return jnp.moveaxis(t.reshape(bsz, h, nb, Q_BLOCK, t.shape[-1]), 2, 0)

    out = lax.map(lambda qb: mla_attend(qb[0], qb[1], k_nope, k_pe, v), (to_blocks(q_nope), to_blocks(q_pe)))
    return jnp.moveaxis(out, 0, 2).reshape(bsz, h, n, MLA_V)


def mixer_hyena_mla(h_lat, h_ctx, p, with_ctx_out):
    filt = (p['filt_w1'], p['filt_b1'], p['filt_freq1'], p['filt_w2'], p['filt_b2'], p['filt_freq2'], p['filt_w3'])

    def project(h):
        bsz, n = h.shape[:2]
        u = h @ p['w_in']
        q = (rms_norm(u[..., OFF_Q:OFF_KV], p['q_norm']) @ p['w_qb']).reshape(bsz, n, MLA_HEADS, -1).transpose(0, 2, 1, 3)
        kv = (rms_norm(u[..., OFF_KV:OFF_KPE], p['kv_norm']) @ p['w_kvb']).reshape(bsz, n, MLA_HEADS, -1).transpose(0, 2, 1, 3)
        return (u[..., :OFF_Q], q[..., :MLA_NOPE], q[..., MLA_NOPE:], kv[..., :MLA_NOPE], kv[..., MLA_NOPE:],
                u[..., OFF_KPE:])

    def merge(hy_out, att):
        bsz, _, n, _ = att.shape
        return jnp.concatenate([hy_out, att.transpose(0, 2, 1, 3).reshape(bsz, n, MLA_HEADS * MLA_V)], -1) @ p['w_out']

    hy_l, qn_l, qp_l, kn_l, v_l, kp_l = project(h_lat)
    hy_c, qn_c, qp_c, kn_c, v_c, kp_c = project(h_ctx)
    cos, sin = axial_rope_tables(h_lat.shape[1])
    qp_l = apply_rope(qp_l, cos, sin)
    kp_l = apply_rope(kp_l, cos, sin)
    kn = jnp.concatenate([kn_c, kn_l], 2)
    kp = jnp.concatenate([kp_c, kp_l], 1)
    vv = jnp.concatenate([v_c, v_l], 2)
    att_l = attend_blocks(qn_l, qp_l, kn, kp, vv)
    hyo_l = hyena_sequence(hy_l, p['conv_w'], p['conv_b'], filt, p['skip'])
    y_lat = merge(hyo_l, att_l)
    y_ctx = None
    if with_ctx_out:
        att_c = mla_attend(qn_c, qp_c, kn_c, kp_c, v_c)
        hyo_c = hyena_sequence(hy_c, p['conv_w'], p['conv_b'], filt, p['skip'])
        y_ctx = merge(hyo_c, att_c)
    return y_lat, y_ctx


def ssd_scan(x, dt, a_coef, bm, cm, s0):
    f32 = jnp.float32
    bsz, n, h, pd = x.shape
    g, ns = bm.shape[2], bm.shape[3]
    r = h // g
    nc = n // SSD_CHUNK
    a = dt.astype(f32) * a_coef.astype(f32)
    xdt = x.astype(f32) * dt.astype(f32)[..., None]

    def chunks(t):
        return jnp.moveaxis(t.reshape((bsz, nc, SSD_CHUNK) + t.shape[2:]), 1, 0)

    xs = (chunks(xdt.reshape(bsz, n, g, r, pd)), chunks(a.reshape(bsz, n, g, r)),
          chunks(bm.astype(f32)), chunks(cm.astype(f32)))
    tri = jnp.tril(jnp.ones((SSD_CHUNK, SSD_CHUNK), bool))

    def step(s, inp):
        xc, ac, bc, cc = inp
        acs = jnp.cumsum(ac, axis=1)
        acs_t = jnp.transpose(acs, (0, 2, 3, 1))
        seg = acs_t[..., :, None] - acs_t[..., None, :]
        decay = jnp.exp(jnp.where(tri, seg, -jnp.inf))
        cb = jnp.einsum('btgn,bsgn->bgts', cc, bc)
        y_diag = jnp.einsum('bgrts,bsgrp->btgrp', cb[:, :, None] * decay, xc)
        y_off = jnp.einsum('btgn,bgrpn->btgrp', cc, s) * jnp.exp(acs)[..., None]
        to_end = jnp.exp(acs[:, -1:] - acs)
        s_new = (jnp.exp(acs[:, -1])[..., None, None] * s
                 + jnp.einsum('bsgn,bsgrp->bgrpn', bc, xc * to_end[..., None]))
        return s_new, y_diag + y_off

    s_fin, ys = lax.scan(step, s0.astype(f32).reshape(bsz, g, r, pd, ns), xs)
    y = jnp.moveaxis(ys, 0, 1).reshape(bsz, n, h, pd)
    return y.astype(x.dtype), s_fin.reshape(bsz, h, pd, ns)


def mixer_ssd(h_lat, h_ctx, p, with_ctx_out):
    a_f = -jnp.exp(p['a_log_f'].astype(jnp.float32))
    a_b = -jnp.exp(p['a_log_b'].astype(jnp.float32))

    def pre(h):
        bsz, n = h.shape[:2]
        u = h @ p['w_in']
        z = u[..., :SSD_INNER]
        xbc = jax.nn.silu(dwconv_centered(u[..., SSD_INNER:SSD_INNER + SSD_XBC], p['conv_w'], p['conv_b']))
        dt = u[..., SSD_INNER + SSD_XBC:].astype(jnp.float32)
        xs = xbc[..., :SSD_INNER].reshape(bsz, n, SSD_HEADS, SSD_HEADDIM)
        gn = SSD_GROUPS * SSD_STATE
        bm = xbc[..., SSD_INNER:SSD_INNER + gn].reshape(bsz, n, SSD_GROUPS, SSD_STATE)
        cm = xbc[..., SSD_INNER + gn:].reshape(bsz, n, SSD_GROUPS, SSD_STATE)
        dt_f = jax.nn.softplus(dt[..., :SSD_HEADS] + p['dt_bias_f'].astype(jnp.float32))
        dt_b = jax.nn.softplus(dt[..., SSD_HEADS:] + p['dt_bias_b'].astype(jnp.float32))
        return z, xs, bm, cm, dt_f, dt_b

    def flip(t):
        return jnp.flip(t, axis=1)

    def post(y_f, y_b, xs, z):
        bsz, n = z.shape[:2]
        y = y_f + y_b + xs * p['d'][:, None]
        y = y.reshape(bsz, n, SSD_INNER) * jax.nn.silu(z)
        y = rms_norm(y.reshape(bsz, n, SSD_GROUPS, SSD_INNER // SSD_GROUPS),
                     p['norm_g'].reshape(SSD_GROUPS, -1)).reshape(bsz, n, SSD_INNER)
        return y @ p['w_out']

    zc, xc, bc, cc, dfc, dbc = pre(h_ctx)
    zl, xl, bl, cl, dfl, dbl = pre(h_lat)
    s0 = jnp.zeros((h_ctx.shape[0], SSD_HEADS, SSD_HEADDIM, SSD_STATE), jnp.float32)
    yc_f, sc_f = ssd_scan(xc, dfc, a_f, bc, cc, s0)
    yc_b, sc_b = ssd_scan(flip(xc), flip(dbc), a_b, flip(bc), flip(cc), s0)
    yl_f, _ = ssd_scan(xl, dfl, a_f, bl, cl, sc_f)
    yl_b, _ = ssd_scan(flip(xl), flip(dbl), a_b, flip(bl), flip(cl), sc_b)
    y_lat = post(yl_f, flip(yl_b), xl, zl)
    y_ctx = post(yc_f, flip(yc_b), xc, zc) if with_ctx_out else None
    return y_lat, y_ctx


def routed_experts(xf, idx, w, w_gu, w_down):
    t = xf.shape[0]
    tk = t * TOP_K
    n_blocks = -(-tk // MOE_BLOCK) + N_EXPERTS
    cap = n_blocks * MOE_BLOCK
    flat_e = idx.reshape(-1)
    order = jnp.argsort(flat_e)
    sorted_e = flat_e[order]
    counts = jnp.bincount(flat_e, length=N_EXPERTS)
    padded = (counts + MOE_BLOCK - 1) // MOE_BLOCK * MOE_BLOCK
    pad_end = jnp.cumsum(padded)
    pad_start = pad_end - padded
    start = jnp.cumsum(counts) - counts
    dest = pad_start[sorted_e] + jnp.arange(tk) - start[sorted_e]
    row_token = jnp.full((cap,), t, jnp.int32).at[dest].set((order // TOP_K).astype(jnp.int32))
    row_w = jnp.zeros((cap,), jnp.float32).at[dest].set(w.reshape(-1)[order])
    block_e = jnp.minimum(jnp.searchsorted(pad_end, jnp.arange(n_blocks) * MOE_BLOCK, side='right'), N_EXPERTS - 1)
    xpad = jnp.concatenate([xf, jnp.zeros((1, xf.shape[1]), xf.dtype)], 0)
    xb = xpad[row_token].reshape(n_blocks, MOE_BLOCK, xf.shape[1])
    yb = lax.map(lambda a: swiglu(a[0], w_gu[a[1]], w_down[a[1]]), (xb, block_e)).reshape(cap, -1)
    return jax.ops.segment_sum(yb * row_w[:, None].astype(yb.dtype), row_token, num_segments=t + 1)[:t]


def moe(xf, router_w, router_bias, w_gu, w_down, sh_gu, sh_down):
    t = xf.shape[0]
    scores = jax.nn.sigmoid(jnp.dot(xf, router_w, preferred_element_type=jnp.float32))
    choice = scores + router_bias.astype(jnp.float32)
    grp_score = lax.top_k(choice.reshape(t, N_EXPERT_GROUPS, -1), 2)[0].sum(-1)
    _, top_g = lax.top_k(grp_score, TOPK_GROUPS)
    gmask = jax.nn.one_hot(top_g, N_EXPERT_GROUPS, dtype=jnp.float32).sum(1) > 0
    masked = jnp.where(jnp.repeat(gmask, N_EXPERTS // N_EXPERT_GROUPS, axis=1), choice, -jnp.inf)
    _, idx = lax.top_k(masked, TOP_K)
    w = jnp.take_along_axis(scores, idx, axis=1)
    w = w / jnp.sum(w, -1, keepdims=True) * ROUTED_SCALE
    return routed_experts(xf, idx, w, w_gu, w_down) + swiglu(xf, sh_gu, sh_down)


def setup_inputs(seed: int = 0) -> dict:
    key = jax.random.key(seed)
    ks = iter(jax.random.split(key, 64))
    f32 = jnp.float32

    def nrm(shape, scale=1.0):
        return jax.random.normal(next(ks), shape, f32) * scale

    def gain(shape):
        return 1.0 + 0.01 * nrm(shape)

    def dt_bias(shape):
        u = jax.random.uniform(next(ks), shape, f32)
        dt = jnp.exp(u * (math.log(0.1) - math.log(1e-3)) + math.log(1e-3))
        return dt + jnp.log(-jnp.expm1(-dt))

    def a_log(shape):
        return jnp.log(jax.random.uniform(next(ks), shape, f32, 1.0, 16.0))

    D = D_MODEL
    return {
        "x": nrm((BATCH, SEQ, D)),
        "c": nrm((BATCH, D)),
        "ctx": nrm((BATCH, CTX_LEN, D)),
        "c_ctx": nrm((D,)),
        "mod_w": nrm((DEPTH, D, N_MOD * D), 0.5 * D ** -0.5),
        "mod_b": nrm((DEPTH, N_MOD * D), 0.01),
        "ln_mix_g": gain((DEPTH, D)),
        "ln_mix_b": nrm((DEPTH, D), 0.01),
        "ln_ffn_g": gain((DEPTH, D)),
        "ln_ffn_b": nrm((DEPTH, D), 0.01),
        "a_w_in": nrm((N_EVEN, D, IN_A), D ** -0.5),
        "hy_conv_w": nrm((N_EVEN, HY_SHORT_CONV, 3 * HY_WIDTH), HY_SHORT_CONV ** -0.5),
        "hy_conv_b": nrm((N_EVEN, 3 * HY_WIDTH), 0.01),
        "hy_filt_w1": nrm((N_EVEN, HY_EMB, HY_FFN), HY_EMB ** -0.5),
        "hy_filt_b1": nrm((N_EVEN, HY_FFN), 0.01),
        "hy_filt_freq1": gain((N_EVEN, HY_FFN)),
        "hy_filt_w2": nrm((N_EVEN, HY_FFN, HY_FFN), HY_FFN ** -0.5),
        "hy_filt_b2": nrm((N_EVEN, HY_FFN), 0.01),
        "hy_filt_freq2": gain((N_EVEN, HY_FFN)),
        "hy_filt_w3": nrm((N_EVEN, HY_FFN, 2 * HY_WIDTH), HY_FFN ** -0.5),
        "hy_skip": nrm((N_EVEN, HY_WIDTH), 0.5),
        "mla_q_norm": gain((N_EVEN, MLA_Q_RANK)),
        "mla_w_qb": nrm((N_EVEN, MLA_Q_RANK, MLA_HEADS * (MLA_NOPE + MLA_ROPE)), MLA_Q_RANK ** -0.5),
        "mla_kv_norm": gain((N_EVEN, MLA_KV_RANK)),
        "mla_w_kvb": nrm((N_EVEN, MLA_KV_RANK, MLA_HEADS * (MLA_NOPE + MLA_V)), MLA_KV_RANK ** -0.5),
        "a_w_out": nrm((N_EVEN, MIX_A_OUT, D), DN_BETA * MIX_A_OUT ** -0.5),
        "ssd_w_in": nrm((N_ODD, D, SSD_IN), D ** -0.5),
        "ssd_conv_w": nrm((N_ODD, SSD_CONV, SSD_XBC), SSD_CONV ** -0.5),
        "ssd_conv_b": nrm((N_ODD, SSD_XBC), 0.01),
        "ssd_dt_bias_f": dt_bias((N_ODD, SSD_HEADS)),
        "ssd_dt_bias_b": dt_bias((N_ODD, SSD_HEADS)),
        "ssd_a_log_f": a_log((N_ODD, SSD_HEADS)),
        "ssd_a_log_b": a_log((N_ODD, SSD_HEADS)),
        "ssd_d": gain((N_ODD, SSD_HEADS)),
        "ssd_norm_g": gain((N_ODD, SSD_INNER)),
        "ssd_w_out": nrm((N_ODD, SSD_INNER, D), DN_BETA * SSD_INNER ** -0.5),
        "router_w": nrm((DEPTH, D, N_EXPERTS), D ** -0.5),
        "router_bias": nrm((DEPTH, N_EXPERTS), 0.01),
        "exp_w_gu": nrm((DEPTH, N_EXPERTS, D, 2 * EXPERT_DIM), D ** -0.5),
        "exp_w_down": nrm((DEPTH, N_EXPERTS, EXPERT_DIM, D), DN_BETA * EXPERT_DIM ** -0.5),
        "sh_w_gu": nrm((DEPTH, D, 2 * SHARED_DIM), D ** -0.5),
        "sh_w_down": nrm((DEPTH, SHARED_DIM, D), DN_BETA * SHARED_DIM ** -0.5),
    }


def reference(x, c, ctx, c_ctx, mod_w, mod_b, ln_mix_g, ln_mix_b, ln_ffn_g, ln_ffn_b,
              a_w_in, hy_conv_w, hy_conv_b, hy_filt_w1, hy_filt_b1, hy_filt_freq1, hy_filt_w2, hy_filt_b2,
              hy_filt_freq2, hy_filt_w3, hy_skip, mla_q_norm, mla_w_qb, mla_kv_norm, mla_w_kvb, a_w_out,
              ssd_w_in, ssd_conv_w, ssd_conv_b, ssd_dt_bias_f, ssd_dt_bias_b, ssd_a_log_f, ssd_a_log_b, ssd_d,
              ssd_norm_g, ssd_w_out, router_w, router_bias, exp_w_gu, exp_w_down, sh_w_gu, sh_w_down):
    bsz, n_lat, d = x.shape
    n_ctx = ctx.shape[1]
    for l in range(DEPTH):
        last = l == DEPTH - 1
        i = l // 2
        mod = (jax.nn.silu(c) @ mod_w[l] + mod_b[l]).reshape(bsz, 1, N_MOD, d)
        mod_c = (jax.nn.silu(c_ctx) @ mod_w[l] + mod_b[l]).reshape(1, 1, N_MOD, d)
        sh1, sc1, g1, sh2, sc2, g2 = jnp.moveaxis(mod, 2, 0)
        csh1, csc1, cg1, csh2, csc2, cg2 = jnp.moveaxis(mod_c, 2, 0)
        h_lat = modulate(x, sh1, sc1)
        h_ctx = modulate(ctx, csh1, csc1)
        if l % 2 == 0:
            p = {"w_in": a_w_in[i], "conv_w": hy_conv_w[i], "conv_b": hy_conv_b[i],
                 "filt_w1": hy_filt_w1[i], "filt_b1": hy_filt_b1[i], "filt_freq1": hy_filt_freq1[i],
                 "filt_w2": hy_filt_w2[i], "filt_b2": hy_filt_b2[i], "filt_freq2": hy_filt_freq2[i],
                 "filt_w3": hy_filt_w3[i], "skip": hy_skip[i], "q_norm": mla_q_norm[i], "w_qb": mla_w_qb[i],
                 "kv_norm": mla_kv_norm[i], "w_kvb": mla_w_kvb[i], "w_out": a_w_out[i]}
            y_lat, y_ctx = mixer_hyena_mla(h_lat, h_ctx, p, not last)
        else:
            p = {"w_in": ssd_w_in[i], "conv_w": ssd_conv_w[i], "conv_b": ssd_conv_b[i],
                 "dt_bias_f": ssd_dt_bias_f[i], "dt_bias_b": ssd_dt_bias_b[i],
                 "a_log_f": ssd_a_log_f[i], "a_log_b": ssd_a_log_b[i], "d": ssd_d[i],
                 "norm_g": ssd_norm_g[i], "w_out": ssd_w_out[i]}
            y_lat, y_ctx = mixer_ssd(h_lat, h_ctx, p, not last)
        x = layer_norm(DN_ALPHA * x + g1 * y_lat, ln_mix_g[l], ln_mix_b[l])
        ff_x = modulate(x, sh2, sc2)
        moe_w = (router_w[l], router_bias[l], exp_w_gu[l], exp_w_down[l], sh_w_gu[l], sh_w_down[l])
        if last:
            out = moe(ff_x.reshape(-1, d), *moe_w).reshape(bsz, n_lat, d)
            x = layer_norm(DN_ALPHA * x + g2 * out, ln_ffn_g[l], ln_ffn_b[l])
        else:
            ctx = layer_norm(DN_ALPHA * ctx + cg1 * y_ctx, ln_mix_g[l], ln_mix_b[l])
            ff_c = modulate(ctx, csh2, csc2)
            out = moe(jnp.concatenate([ff_c, ff_x], 1).reshape(-1, d), *moe_w).reshape(bsz, n_ctx + n_lat, d)
            ctx = layer_norm(DN_ALPHA * ctx + cg2 * out[:, :n_ctx], ln_ffn_g[l], ln_ffn_b[l])
            x = layer_norm(DN_ALPHA * x + g2 * out[:, n_ctx:], ln_ffn_g[l], ln_ffn_b[l])
    return x
```

```python
import functools
import math

import jax
import jax.numpy as jnp
from jax import lax
from jax.experimental import pallas as pl
from jax.experimental.pallas import tpu as pltpu

F32 = jnp.float32
BF16 = jnp.bfloat16

D_MODEL = 1024
DEPTH = 2
GRID_W = 64
N_MOD = 6

HY_WIDTH = 512
HY_EMB = 33
HY_BANDS = (HY_EMB - 1) // 2
HY_TARGET = 1e-2
HY_FAST_DECAY = 0.3
HY_SLOW_DECAY = 1.5
HY_DECAY_MIN = math.log(HY_TARGET) / HY_SLOW_DECAY
HY_DECAY_MAX = math.log(HY_TARGET) / HY_FAST_DECAY

MLA_HEADS = 8
MLA_NOPE = 64
MLA_ROPE = 32
MLA_V = 64
MLA_Q_RANK = 256
MLA_KV_RANK = 128
MLA_QK = MLA_NOPE + MLA_ROPE
MLA_SCALE = MLA_QK ** -0.5
ROPE_THETA = 10000.0

OFF_Q = 3 * HY_WIDTH
OFF_KV = OFF_Q + MLA_Q_RANK
OFF_KPE = OFF_KV + MLA_KV_RANK

SSD_INNER = 2 * D_MODEL
SSD_HEADDIM = 64
SSD_HEADS = SSD_INNER // SSD_HEADDIM
SSD_GROUPS = 4
SSD_STATE = 128
SSD_CHUNK = 128
SSD_XBC = SSD_INNER + 2 * SSD_GROUPS * SSD_STATE

N_EXPERTS = 256
TOP_K = 8
N_EXPERT_GROUPS = 8
TOPK_GROUPS = 4
EXPERT_DIM = 256
ROUTED_SCALE = 2.5

DN_ALPHA = (2 * DEPTH) ** 0.25
LN_EPS = 1e-5
RMS_EPS = 1e-6

V7X_VMEM_CAP = 56 * 1024 * 1024
MOE_ROWS = 256


def _params(semantics, vmem_bytes):
    limit = int(min(max(vmem_bytes * 5 // 4, 32 * 1024 * 1024), V7X_VMEM_CAP))
    return pltpu.CompilerParams(dimension_semantics=semantics, vmem_limit_bytes=limit)


def _bcast_spec(a):
    if a.shape[0] == 1:
        return pl.BlockSpec((1, 1, a.shape[2]), lambda b, i: (0, 0, 0))
    return pl.BlockSpec((1, 1, a.shape[2]), lambda b, i: (b, 0, 0))


def _linear_kernel(*refs, pre, n_pre, n_w):
    x = refs[0][0]
    pre_refs = refs[1:1 + n_pre]
    w_refs = refs[1 + n_pre:1 + n_pre + n_w]
    o_refs = refs[1 + n_pre + n_w:]
    if pre == "mod":
        h = x * (1.0 + pre_refs[1][0]) + pre_refs[0][0]
    elif pre == "rms":
        h = x * lax.rsqrt(jnp.mean(x * x, -1, keepdims=True) + RMS_EPS) * pre_refs[0][0]
    else:
        h = x
    hb = h.astype(BF16)
    for w_ref, o_ref in zip(w_refs, o_refs):
        o_ref[0] = jnp.dot(hb, w_ref[...], preferred_element_type=F32).astype(o_ref.dtype)


def _linear(x, ws, pre=None, pre_args=(), tm=256, out_dtypes=None):
    bsz, n, k = x.shape
    tm = min(tm, n)
    assert n % tm == 0
    out_dtypes = out_dtypes or [F32] * len(ws)
    in_specs = [pl.BlockSpec((1, tm, k), lambda b, i: (b, i, 0))]
    in_specs += [_bcast_spec(a) for a in pre_args]
    in_specs += [pl.BlockSpec(w.shape, lambda b, i: (0, 0)) for w in ws]
    out_specs = [pl.BlockSpec((1, tm, w.shape[1]), lambda b, i: (b, i, 0)) for w in ws]
    out_shape = [jax.ShapeDtypeStruct((bsz, n, w.shape[1]), dt) for w, dt in zip(ws, out_dtypes)]
    vmem = 2 * (tm * k * 4 + sum(w.size * 2 + tm * w.shape[1] * 4 for w in ws))
    return pl.pallas_call(
        functools.partial(_linear_kernel, pre=pre, n_pre=len(pre_args), n_w=len(ws)),
        grid=(bsz, n // tm), in_specs=in_specs, out_specs=out_specs, out_shape=out_shape,
        compiler_params=_params(("parallel", "parallel"), vmem),
    )(x, *pre_args, *ws)


def _layer_norm(r, g, b):
    mu = jnp.mean(r, -1, keepdims=True)
    c = r - mu
    var = jnp.mean(c * c, -1, keepdims=True)
    return c * lax.rsqrt(var + LN_EPS) * g + b


def _mix_out_kernel(*refs, n_y):
    x_ref = refs[0]
    y_refs = refs[1:1 + n_y]
    w_refs = refs[1 + n_y:1 + 2 * n_y]
    gate_ref, g_ref, b_ref, sh_ref, sc_ref, xo_ref, ff_ref = refs[1 + 2 * n_y:]
    y = None
    for y_ref, w_ref in zip(y_refs, w_refs):
        t = jnp.dot(y_ref[0].astype(BF16), w_ref[...], preferred_element_type=F32)
        y = t if y is None else y + t
    xn = _layer_norm(DN_ALPHA * x_ref[0] + gate_ref[0] * y, g_ref[0], b_ref[0])
    xo_ref[0] = xn
    ff_ref[0] = xn * (1.0 + sc_ref[0]) + sh_ref[0]


def _mix_out(x, ys, ws, gate, ln_g, ln_b, shift2, scale2, tm=256):
    bsz, n, d = x.shape
    tm = min(tm, n)
    assert n % tm == 0
    row = lambda c: pl.BlockSpec((1, tm, c), lambda b, i: (b, i, 0))
    vecs = [gate, ln_g, ln_b, shift2, scale2]
    in_specs = [row(d)] + [row(y.shape[2]) for y in ys]
    in_specs += [pl.BlockSpec(w.shape, lambda b, i: (0, 0)) for w in ws]
    in_specs += [_bcast_spec(a) for a in vecs]
    vmem = 2 * (3 * tm * d * 4 + sum(tm * y.shape[2] * 4 + w.size * 2 for y, w in zip(ys, ws)))
    return pl.pallas_call(
        functools.partial(_mix_out_kernel, n_y=len(ys)),
        grid=(bsz, n // tm), in_specs=in_specs, out_specs=[row(d), row(d)],
        out_shape=[jax.ShapeDtypeStruct((bsz, n, d), F32)] * 2,
        compiler_params=_params(("parallel", "parallel"), vmem),
    )(x, *ys, *ws, *vecs)


def _swiglu_rows(xb, w_gu, w_down):
    h = jnp.dot(xb, w_gu, preferred_element_type=F32)
    half = h.shape[1] // 2
    g, u = h[:, :half], h[:, half:]
    a = (g * jax.nn.sigmoid(g) * u).astype(BF16)
    return jnp.dot(a, w_down, preferred_element_type=F32)


def _ffn_out_kernel(x_ref, ff_ref, r_ref, wgu_ref, wdn_ref, gate_ref, g_ref, b_ref, xo_ref):
    shared = _swiglu_rows(ff_ref[0].astype(BF16), wgu_ref[...], wdn_ref[...])
    out = r_ref[0] + shared
    xo_ref[0] = _layer_norm(DN_ALPHA * x_ref[0] + gate_ref[0] * out, g_ref[0], b_ref[0])


def _ffn_out(x, ff, routed, sh_gu, sh_down, gate, ln_g, ln_b, tm=256):
    bsz, n, d = x.shape
    tm = min(tm, n)
    assert n % tm == 0
    row = pl.BlockSpec((1, tm, d), lambda b, i: (b, i, 0))
    vecs = [gate, ln_g, ln_b]
    in_specs = [row, row, row, pl.BlockSpec(sh_gu.shape, lambda b, i: (0, 0)),
                pl.BlockSpec(sh_down.shape, lambda b, i: (0, 0))] + [_bcast_spec(a) for a in vecs]
    vmem = 2 * (4 * tm * d * 4 + sh_gu.size * 2 + sh_down.size * 2)
    return pl.pallas_call(
        _ffn_out_kernel, grid=(bsz, n // tm), in_specs=in_specs, out_specs=row,
        out_shape=jax.ShapeDtypeStruct((bsz, n, d), F32),
        compiler_params=_params(("parallel", "parallel"), vmem),
    )(x, ff, routed, sh_gu, sh_down, *vecs)


def _attn_kernel(q_ref, k_ref, v_ref, o_ref, m_sc, l_sc, acc_sc):
    kv = pl.program_id(3)

    @pl.when(kv == 0)
    def _():
        m_sc[...] = jnp.full(m_sc.shape, -jnp.inf, F32)
        l_sc[...] = jnp.zeros(l_sc.shape, F32)
        acc_sc[...] = jnp.zeros(acc_sc.shape, F32)

    s = lax.dot_general(q_ref[0, 0], k_ref[0, 0], (((1,), (1,)), ((), ())), preferred_element_type=F32)
    m_prev = m_sc[:, :1]
    m_new = jnp.maximum(m_prev, jnp.max(s, -1, keepdims=True))
    alpha = jnp.exp(m_prev - m_new)
    p = jnp.exp(s - m_new)
    l_sc[...] = jnp.broadcast_to(alpha * l_sc[:, :1] + jnp.sum(p, -1, keepdims=True), l_sc.shape)
    acc_sc[...] = alpha * acc_sc[...] + jnp.dot(p.astype(BF16), v_ref[0, 0], preferred_element_type=F32)
    m_sc[...] = jnp.broadcast_to(m_new, m_sc.shape)

    @pl.when(kv == pl.num_programs(3) - 1)
    def _():
        o_ref[0, 0] = acc_sc[...] / l_sc[:, :1]


def _attention(q, k, v, tq, tk):
    bsz, h, nq, dk = q.shape
    nk, dv = k.shape[2], v.shape[3]
    assert nq % tq == 0 and nk % tk == 0
    lanes = 128
    vmem = 2 * (tq * lanes * 2 + 2 * tk * lanes * 2 + tq * lanes * 4) + 3 * tq * lanes * 4 + 4 * tq * tk * 4
    return pl.pallas_call(
        _attn_kernel, grid=(bsz, h, nq // tq, nk // tk),
        in_specs=[pl.BlockSpec((1, 1, tq, dk), lambda b, hh, i, j: (b, hh, i, 0)),
                  pl.BlockSpec((1, 1, tk, dk), lambda b, hh, i, j: (b, hh, j, 0)),
                  pl.BlockSpec((1, 1, tk, dv), lambda b, hh, i, j: (b, hh, j, 0))],
        out_specs=pl.BlockSpec((1, 1, tq, dv), lambda b, hh, i, j: (b, hh, i, 0)),
        out_shape=jax.ShapeDtypeStruct((bsz, h, nq, dv), F32),
        scratch_shapes=[pltpu.VMEM((tq, lanes), F32), pltpu.VMEM((tq, lanes), F32), pltpu.VMEM((tq, dv), F32)],
        compiler_params=_params(("parallel", "parallel", "parallel", "arbitrary"), vmem),
    )(q, k, v)


def _moe_kernel(be_ref, nv_ref, x_ref, rw_ref, wgu_ref, wdn_ref, o_ref):
    i = pl.program_id(0)

    @pl.when(i < nv_ref[0])
    def _():
        y = _swiglu_rows(x_ref[...], wgu_ref[0].astype(BF16), wdn_ref[0].astype(BF16))
        o_ref[...] = y * rw_ref[...]

    @pl.when(i >= nv_ref[0])
    def _():
        o_ref[...] = jnp.zeros(o_ref.shape, F32)


def _moe_experts(xb, row_w, block_e, n_valid, w_gu, w_down):
    cap, d = xb.shape
    n_blocks = cap // MOE_ROWS
    e, _, gu = w_gu.shape
    ed = w_down.shape[1]
    live = lambda i, be, nv: jnp.minimum(i, nv[0] - 1)
    grid_spec = pltpu.PrefetchScalarGridSpec(
        num_scalar_prefetch=2, grid=(n_blocks,),
        in_specs=[pl.BlockSpec((MOE_ROWS, d), lambda i, be, nv: (live(i, be, nv), 0)),
                  pl.BlockSpec((MOE_ROWS, 1), lambda i, be, nv: (live(i, be, nv), 0)),
                  pl.BlockSpec((1, d, gu), lambda i, be, nv: (be[i], 0, 0)),
                  pl.BlockSpec((1, ed, d), lambda i, be, nv: (be[i], 0, 0))],
        out_specs=pl.BlockSpec((MOE_ROWS, d), lambda i, be, nv: (i, 0)))
    vmem = 2 * (MOE_ROWS * d * 2 + MOE_ROWS * 128 * 4 + d * gu * 4 + ed * d * 4 + MOE_ROWS * d * 4) + d * gu * 4
    return pl.pallas_call(
        _moe_kernel, grid_spec=grid_spec, out_shape=jax.ShapeDtypeStruct((cap, d), F32),
        compiler_params=_params(("arbitrary",), vmem),
    )(block_e, n_valid, xb, row_w, w_gu, w_down)


def _route(xf, router_w, router_bias):
    t = xf.shape[0]
    scores = jax.nn.sigmoid(jnp.dot(xf, router_w, precision=lax.Precision.HIGHEST, preferred_element_type=F32))
    choice = scores + router_bias.astype(F32)
    grp_score = lax.top_k(choice.reshape(t, N_EXPERT_GROUPS, -1), 2)[0].sum(-1)
    _, top_g = lax.top_k(grp_score, TOPK_GROUPS)
    gmask = jax.nn.one_hot(top_g, N_EXPERT_GROUPS, dtype=F32).sum(1) > 0
    masked = jnp.where(jnp.repeat(gmask, N_EXPERTS // N_EXPERT_GROUPS, axis=1), choice, -jnp.inf)
    _, idx = lax.top_k(masked, TOP_K)
    w = jnp.take_along_axis(scores, idx, axis=1)
    w = w / jnp.sum(w, -1, keepdims=True) * ROUTED_SCALE
    return idx, w


def _moe_routed(xf, router_w, router_bias, w_gu, w_down):
    t, d = xf.shape
    idx, w = _route(xf, router_w, router_bias)
    tk = t * TOP_K
    n_blocks = -(-tk // MOE_ROWS) + N_EXPERTS
    cap = n_blocks * MOE_ROWS
    flat_e = idx.reshape(-1)
    order = jnp.argsort(flat_e)
    sorted_e = flat_e[order]
    counts = jnp.bincount(flat_e, length=N_EXPERTS)
    padded = (counts + MOE_ROWS - 1) // MOE_ROWS * MOE_ROWS
    pad_end = jnp.cumsum(padded)
    pad_start = pad_end - padded
    start = jnp.cumsum(counts) - counts
    dest = (pad_start[sorted_e] + jnp.arange(tk) - start[sorted_e]).astype(jnp.int32)
    row_token = jnp.full((cap,), t, jnp.int32).at[dest].set((order // TOP_K).astype(jnp.int32))
    row_w = jnp.zeros((cap,), F32).at[dest].set(w.reshape(-1)[order])
    pos = jnp.zeros((tk,), jnp.int32).at[order].set(dest)
    block_e = jnp.minimum(jnp.searchsorted(pad_end, jnp.arange(n_blocks) * MOE_ROWS, side='right'),
                          N_EXPERTS - 1).astype(jnp.int32)
    n_valid = (pad_end[-1:] // MOE_ROWS).astype(jnp.int32)
    xpad = jnp.concatenate([xf.astype(BF16), jnp.zeros((1, d), BF16)], 0)
    xb = xpad[row_token]
    yb = _moe_experts(xb, row_w[:, None], block_e, n_valid, w_gu, w_down)
    return yb[pos.reshape(t, TOP_K)].sum(1)


def _rope_tables(n):
    rows = n // GRID_W
    row = jnp.repeat(jnp.arange(rows), GRID_W).astype(F32)
    col = jnp.tile(jnp.arange(GRID_W), rows).astype(F32)
    half = MLA_ROPE // 2
    inv = ROPE_THETA ** (-jnp.arange(0, half, 2, dtype=F32) / half)
    ang = jnp.concatenate([row[:, None] * inv, col[:, None] * inv], -1)
    return jnp.cos(ang), jnp.sin(ang)


def _rope(x, cos, sin):
    x1, x2 = x[..., 0::2], x[..., 1::2]
    return jnp.stack([x1 * cos - x2 * sin, x1 * sin + x2 * cos], -1).reshape(x.shape)


def _dwconv(x, w, b):
    pad = w.shape[0] // 2
    y = lax.conv_general_dilated(x, w[:, None, :], window_strides=(1,), padding=[(pad, pad)],
                                 dimension_numbers=('NWC', 'WIO', 'NWC'), feature_group_count=x.shape[-1])
    return y + b


def _hyena_filters(n, w1, b1, f1, w2, b2, f2, w3):
    t = jnp.linspace(0.0, 1.0, n, dtype=F32)[:, None]
    w = 2.0 * math.pi * jnp.arange(n, dtype=F32)[:, None] / n
    fr = jnp.linspace(1e-4, HY_BANDS - 1, HY_BANDS, dtype=F32)
    z = jnp.concatenate([t, jnp.cos(fr * w), -jnp.sin(fr * w)], -1)
    hp = lax.Precision.HIGHEST
    h = jnp.sin(f1 * (jnp.dot(z, w1, precision=hp) + b1))
    h = jnp.sin(f2 * (jnp.dot(h, w2, precision=hp) + b2))
    h = jnp.dot(h, w3, precision=hp)
    deltas = jnp.abs(jnp.linspace(HY_DECAY_MIN, HY_DECAY_MAX, HY_WIDTH, dtype=F32))
    decay = jnp.exp(-t * deltas)
    h_f = h[:, :HY_WIDTH] * decay
    h_b = h[:, HY_WIDTH:] * decay
    k = jnp.concatenate([h_f, jnp.zeros((1, HY_WIDTH), F32), h_b[:0:-1]], 0)
    return k / jnp.sum(jnp.abs(k), 0, keepdims=True)


def _hyena_sequence(proj, conv_w, conv_b, filt, skip):
    n = proj.shape[1]
    u = _dwconv(proj, conv_w, conv_b)
    x0, x1, v = jnp.split(u, 3, axis=-1)
    z = v * x1
    k = _hyena_filters(n, *filt)
    y = jnp.fft.irfft(jnp.fft.rfft(z, n=2 * n, axis=1) * jnp.fft.rfft(k, n=2 * n, axis=0)[None],
                      n=2 * n, axis=1)[:, :n]
    return (y + z * skip) * x0


def _mixer_hyena_mla(x, ctx, sh_l, sc_l, sh_c, sc_c, p):
    bsz, n, _ = x.shape
    nc = ctx.shape[1]
    w_in = p['w_in'].astype(BF16)
    w_parts = [w_in[:, :OFF_Q], w_in[:, OFF_Q:OFF_KV], w_in[:, OFF_KV:OFF_KPE], w_in[:, OFF_KPE:]]
    w_qb = p['w_qb'].astype(BF16)
    w_kvb = p['w_kvb'].astype(BF16)
    qg = p['q_norm'].reshape(1, 1, -1)
    kvg = p['kv_norm'].reshape(1, 1, -1)

    def project(h, shift, scale):
        hy, ql, kvl, kpe = _linear(h, w_parts, pre="mod", pre_args=(shift, scale))
        q, = _linear(ql, [w_qb], pre="rms", pre_args=(qg,))
        kv, = _linear(kvl, [w_kvb], pre="rms", pre_args=(kvg,))
        m = h.shape[1]
        q = q.reshape(bsz, m, MLA_HEADS, MLA_QK).transpose(0, 2, 1, 3)
        kv = kv.reshape(bsz, m, MLA_HEADS, MLA_NOPE + MLA_V).transpose(0, 2, 1, 3)
        return hy, q[..., :MLA_NOPE], q[..., MLA_NOPE:], kv[..., :MLA_NOPE], kv[..., MLA_NOPE:], kpe

    hy_l, qn_l, qp_l, kn_l, v_l, kp_l = project(x, sh_l, sc_l)
    hy_c, qn_c, qp_c, kn_c, v_c, kp_c = project(ctx, sh_c, sc_c)
    cos, sin = _rope_tables(n)
    qp_l = _rope(qp_l, cos, sin)
    kp_l = _rope(kp_l, cos, sin)

    def heads(kp):
        return jnp.broadcast_to(kp[:, None], (bsz, MLA_HEADS) + kp.shape[1:])

    q_l = (jnp.concatenate([qn_l, qp_l], -1) * MLA_SCALE).astype(BF16)
    q_c = (jnp.concatenate([qn_c, qp_c], -1) * MLA_SCALE).astype(BF16)
    k_c = jnp.concatenate([kn_c, heads(kp_c)], -1).astype(BF16)
    k_l = jnp.concatenate([kn_l, heads(kp_l)], -1).astype(BF16)
    k_all = jnp.concatenate([k_c, k_l], 2)
    v_all = jnp.concatenate([v_c, v_l], 2).astype(BF16)
    att_l = _attention(q_l, k_all, v_all, tq=512, tk=_key_tile(nc + n))
    att_c = _attention(q_c, k_c, v_c.astype(BF16), tq=nc, tk=nc)

    filt = (p['filt_w1'], p['filt_b1'], p['filt_freq1'], p['filt_w2'], p['filt_b2'], p['filt_freq2'], p['filt_w3'])
    hyo_l = _hyena_sequence(hy_l, p['conv_w'], p['conv_b'], filt, p['skip'])
    hyo_c = _hyena_sequence(hy_c, p['conv_w'], p['conv_b'], filt, p['skip'])

    def flat(att):
        return att.transpose(0, 2, 1, 3).reshape(bsz, att.shape[2], MLA_HEADS * MLA_V)

    return (hyo_l, flat(att_l)), (hyo_c, flat(att_c))


def _key_tile(nk):
    for t in (1024, 768, 512, 256, 128):
        if nk % t == 0:
            return t
    return nk


def _ssd_scan(x, dt, a_coef, bm, cm, s0):
    bsz, n, h, pd = x.shape
    g, ns = bm.shape[2], bm.shape[3]
    r = h // g
    nc = n // SSD_CHUNK
    a = dt * a_coef
    xdt = x * dt[..., None]

    def chunks(t):
        return jnp.moveaxis(t.reshape((bsz, nc, SSD_CHUNK) + t.shape[2:]), 1, 0)

    xs = (chunks(xdt.reshape(bsz, n, g, r, pd)), chunks(a.reshape(bsz, n, g, r)), chunks(bm), chunks(cm))
    tri = jnp.tril(jnp.ones((SSD_CHUNK, SSD_CHUNK), bool))

    def step(s, inp):
        xc, ac, bc, cc = inp
        acs = jnp.cumsum(ac, axis=1)
        acs_t = jnp.transpose(acs, (0, 2, 3, 1))
        seg = acs_t[..., :, None] - acs_t[..., None, :]
        decay = jnp.exp(jnp.where(tri, seg, -jnp.inf))
        cb = jnp.einsum('btgn,bsgn->bgts', cc, bc)
        y_diag = jnp.einsum('bgrts,bsgrp->btgrp', cb[:, :, None] * decay, xc)
        y_off = jnp.einsum('btgn,bgrpn->btgrp', cc, s) * jnp.exp(acs)[..., None]
        to_end = jnp.exp(acs[:, -1:] - acs)
        s_new = (jnp.exp(acs[:, -1])[..., None, None] * s
                 + jnp.einsum('bsgn,bsgrp->bgrpn', bc, xc * to_end[..., None]))
        return s_new, y_diag + y_off

    s_fin, ys = lax.scan(step, s0.reshape(bsz, g, r, pd, ns), xs)
    y = jnp.moveaxis(ys, 0, 1).reshape(bsz, n, h, pd)
    return y, s_fin.reshape(bsz, h, pd, ns)


def _mixer_ssd(x, ctx, sh_l, sc_l, sh_c, sc_c, p):
    a_f = -jnp.exp(p['a_log_f'])
    a_b = -jnp.exp(p['a_log_b'])
    w_in = p['w_in'].astype(BF16)
    w_parts = [w_in[:, :SSD_INNER], w_in[:, SSD_INNER:SSD_INNER + SSD_XBC], w_in[:, SSD_INNER + SSD_XBC:]]

    def pre(h, shift, scale):
        bsz, n = h.shape[:2]
        z, xbc, dt = _linear(h, w_parts, pre="mod", pre_args=(shift, scale))
        xbc = jax.nn.silu(_dwconv(xbc, p['conv_w'], p['conv_b']))
        xs = xbc[..., :SSD_INNER].reshape(bsz, n, SSD_HEADS, SSD_HEADDIM)
        gn = SSD_GROUPS * SSD_STATE
        bm = xbc[..., SSD_INNER:SSD_INNER + gn].reshape(bsz, n, SSD_GROUPS, SSD_STATE)
        cm = xbc[..., SSD_INNER + gn:].reshape(bsz, n, SSD_GROUPS, SSD_STATE)
        dt_f = jax.nn.softplus(dt[..., :SSD_HEADS] + p['dt_bias_f'])
        dt_b = jax.nn.softplus(dt[..., SSD_HEADS:] + p['dt_bias_b'])
        return z, xs, bm, cm, dt_f, dt_b

    def flip(t):
        return jnp.flip(t, axis=1)

    def post(y_f, y_b, xs, z):
        bsz, n = z.shape[:2]
        y = y_f + y_b + xs * p['d'][:, None]
        y = y.reshape(bsz, n, SSD_INNER) * jax.nn.silu(z)
        yg = y.reshape(bsz, n, SSD_GROUPS, SSD_INNER // SSD_GROUPS)
        yg = yg * lax.rsqrt(jnp.mean(yg * yg, -1, keepdims=True) + RMS_EPS) * p['norm_g'].reshape(SSD_GROUPS, -1)
        return yg.reshape(bsz, n, SSD_INNER)

    zc, xc, bc, cc, dfc, dbc = pre(ctx, sh_c, sc_c)
    zl, xl, bl, cl, dfl, dbl = pre(x, sh_l, sc_l)
    s0 = jnp.zeros((ctx.shape[0], SSD_HEADS, SSD_HEADDIM, SSD_STATE), F32)
    _, sc_f = _ssd_scan(xc, dfc, a_f, bc, cc, s0)
    _, sc_b = _ssd_scan(flip(xc), flip(dbc), a_b, flip(bc), flip(cc), s0)
    yl_f, _ = _ssd_scan(xl, dfl, a_f, bl, cl, sc_f)
    yl_b, _ = _ssd_scan(flip(xl), flip(dbl), a_b, flip(bl), flip(cl), sc_b)
    return post(yl_f, flip(yl_b), xl, zl)


def kernel(x, c, ctx, c_ctx, mod_w, mod_b, ln_mix_g, ln_mix_b, ln_ffn_g, ln_ffn_b, a_w_in, hy_conv_w, hy_conv_b, hy_filt_w1, hy_filt_b1, hy_filt_freq1, hy_filt_w2, hy_filt_b2, hy_filt_freq2, hy_filt_w3, hy_skip, mla_q_norm, mla_w_qb, mla_kv_norm, mla_w_kvb, a_w_out, ssd_w_in, ssd_conv_w, ssd_conv_b, ssd_dt_bias_f, ssd_dt_bias_b, ssd_a_log_f, ssd_a_log_b, ssd_d, ssd_norm_g, ssd_w_out, router_w, router_bias, exp_w_gu, exp_w_down, sh_w_gu, sh_w_down):
    bsz, n_lat, d = x.shape
    n_ctx = ctx.shape[1]
    hp = lax.Precision.HIGHEST
    for l in range(DEPTH):
        last = l == DEPTH - 1
        i = l // 2
        mod = (jnp.dot(jax.nn.silu(c), mod_w[l], precision=hp) + mod_b[l]).reshape(bsz, N_MOD, 1, d)
        mod_c = (jnp.dot(jax.nn.silu(c_ctx), mod_w[l], precision=hp) + mod_b[l]).reshape(1, N_MOD, 1, d)
        sh1, sc1, g1, sh2, sc2, g2 = [mod[:, j] for j in range(N_MOD)]
        csh1, csc1, cg1, csh2, csc2, cg2 = [mod_c[:, j] for j in range(N_MOD)]
        vec = lambda a: a.reshape(1, 1, d)
        if l % 2 == 0:
            p = {"w_in": a_w_in[i], "conv_w": hy_conv_w[i], "conv_b": hy_conv_b[i],
                 "filt_w1": hy_filt_w1[i], "filt_b1": hy_filt_b1[i], "filt_freq1": hy_filt_freq1[i],
                 "filt_w2": hy_filt_w2[i], "filt_b2": hy_filt_b2[i], "filt_freq2": hy_filt_freq2[i],
                 "filt_w3": hy_filt_w3[i], "skip": hy_skip[i], "q_norm": mla_q_norm[i], "w_qb": mla_w_qb[i],
                 "kv_norm": mla_kv_norm[i], "w_kvb": mla_w_kvb[i]}
            ys_l, ys_c = _mixer_hyena_mla(x, ctx, sh1, sc1, csh1, csc1, p)
            w_out = a_w_out[i].astype(BF16)
            ws = [w_out[:HY_WIDTH], w_out[HY_WIDTH:]]
        else:
            p = {"w_in": ssd_w_in[i], "conv_w": ssd_conv_w[i], "conv_b": ssd_conv_b[i],
                 "dt_bias_f": ssd_dt_bias_f[i], "dt_bias_b": ssd_dt_bias_b[i],
                 "a_log_f": ssd_a_log_f[i], "a_log_b": ssd_a_log_b[i], "d": ssd_d[i],
                 "norm_g": ssd_norm_g[i]}
            assert last
            ys_l, ys_c = (_mixer_ssd(x, ctx, sh1, sc1, csh1, csc1, p),), None
            ws = [ssd_w_out[i].astype(BF16)]
        x, ff_x = _mix_out(x, ys_l, ws, g1, vec(ln_mix_g[l]), vec(ln_mix_b[l]), sh2, sc2)
        sh_gu = sh_w_gu[l].astype(BF16)
        sh_down = sh_w_down[l].astype(BF16)
        if last:
            routed = _moe_routed(ff_x.reshape(-1, d), router_w[l], router_bias[l], exp_w_gu[l], exp_w_down[l])
            x = _ffn_out(x, ff_x, routed.reshape(bsz, n_lat, d), sh_gu, sh_down, g2,
                         vec(ln_ffn_g[l]), vec(ln_ffn_b[l]))
        else:
            ctx, ff_c = _mix_out(ctx, ys_c, ws, cg1, vec(ln_mix_g[l]), vec(ln_mix_b[l]), csh2, csc2)
            tokens = jnp.concatenate([ff_c.reshape(-1, d), ff_x.reshape(-1, d)], 0)
            routed = _moe_routed(tokens, router_w[l], router_bias[l], exp_w_gu[l], exp_w_down[l])
            r_c = routed[:bsz * n_ctx].reshape(bsz, n_ctx, d)
            r_x = routed[bsz * n_ctx:].reshape(bsz, n_lat, d)
            ctx = _ffn_out(ctx, ff_c, r_c, sh_gu, sh_down, cg2, vec(ln_ffn_g[l]), vec(ln_ffn_b[l]))
            x = _ffn_out(x, ff_x, r_x, sh_gu, sh_down, g2, vec(ln_ffn_g[l]), vec(ln_ffn_b[l]))
    return x
```

```python
import functools
import math

import jax
import jax.numpy as jnp
from jax import lax
from jax.experimental import pallas as pl
from jax.experimental.pallas import tpu as pltpu

F32 = jnp.float32
BF16 = jnp.bfloat16
I32 = jnp.int32

D_MODEL = 1024
DEPTH = 2
GRID_W = 64
N_MOD = 6

HY_WIDTH = 512
HY_EMB = 33
HY_BANDS = (HY_EMB - 1) // 2
HY_TARGET = 1e-2
HY_FAST_DECAY = 0.3
HY_SLOW_DECAY = 1.5
HY_DECAY_MIN = math.log(HY_TARGET) / HY_SLOW_DECAY
HY_DECAY_MAX = math.log(HY_TARGET) / HY_FAST_DECAY

MLA_HEADS = 8
MLA_NOPE = 64
MLA_ROPE = 32
MLA_V = 64
MLA_Q_RANK = 256
MLA_KV_RANK = 128
MLA_QK = MLA_NOPE + MLA_ROPE
MLA_SCALE = MLA_QK ** -0.5
ROPE_THETA = 10000.0
LOG2E = math.log2(math.e)

OFF_Q = 3 * HY_WIDTH
OFF_KV = OFF_Q + MLA_Q_RANK
OFF_KPE = OFF_KV + MLA_KV_RANK

SSD_INNER = 2 * D_MODEL
SSD_HEADDIM = 64
SSD_HEADS = SSD_INNER // SSD_HEADDIM
SSD_GROUPS = 4
SSD_STATE = 128
SSD_CHUNK = 128
SSD_XBC = SSD_INNER + 2 * SSD_GROUPS * SSD_STATE

N_EXPERTS = 256
TOP_K = 8
N_EXPERT_GROUPS = 8
TOPK_GROUPS = 4
EXPERT_DIM = 256
ROUTED_SCALE = 2.5

DN_ALPHA = (2 * DEPTH) ** 0.25
LN_EPS = 1e-5
RMS_EPS = 1e-6

LANES = 128
SUBLANES = 8
V7X_VMEM_CAP = 56 * 1024 * 1024

MOE_ROWS = 256
ROUTE_TOKENS = 256
ROW_TILE = 256
ATT_Q_TILE = 512
ATT_HEADS_PER_STEP = 2


def _params(semantics, vmem_bytes, **kw):
    limit = int(min(max(vmem_bytes * 5 // 4, 32 * 1024 * 1024), V7X_VMEM_CAP))
    return pltpu.CompilerParams(dimension_semantics=semantics, vmem_limit_bytes=limit, **kw)


def _bcast_spec(a):
    if a.shape[0] == 1:
        return pl.BlockSpec((1, 1, a.shape[2]), lambda b, i: (0, 0, 0))
    return pl.BlockSpec((1, 1, a.shape[2]), lambda b, i: (b, 0, 0))


def _rows_2d(ref3):
    return jnp.concatenate([ref3[:, c, :] for c in range(ref3.shape[1])], axis=1)


def _store_rows(ref3, val):
    for c in range(ref3.shape[1]):
        ref3[:, c, :] = val[:, c * LANES:(c + 1) * LANES]


def _layer_norm(r, g, b):
    mu = jnp.mean(r, -1, keepdims=True)
    c = r - mu
    var = jnp.mean(c * c, -1, keepdims=True)
    return c * lax.rsqrt(var + LN_EPS) * g + b


def _swiglu_rows(xb, w_gu, w_down):
    h = jnp.dot(xb, w_gu, preferred_element_type=F32)
    half = h.shape[1] // 2
    g, u = h[:, :half], h[:, half:]
    a = (g * jax.nn.sigmoid(g) * u).astype(BF16)
    return jnp.dot(a, w_down, preferred_element_type=F32)


def _linear_kernel(*refs, pre, n_pre, n_w):
    x = refs[0][0]
    pre_refs = refs[1:1 + n_pre]
    w_refs = refs[1 + n_pre:1 + n_pre + n_w]
    o_refs = refs[1 + n_pre + n_w:]
    if pre == "mod":
        h = x * (1.0 + pre_refs[1][0]) + pre_refs[0][0]
    elif pre == "rms":
        h = x * lax.rsqrt(jnp.mean(x * x, -1, keepdims=True) + RMS_EPS) * pre_refs[0][0]
    else:
        h = x
    hb = h.astype(BF16)
    for w_ref, o_ref in zip(w_refs, o_refs):
        o_ref[0] = jnp.dot(hb, w_ref[...], preferred_element_type=F32).astype(o_ref.dtype)


def _linear(x, ws, pre=None, pre_args=(), name="linear"):
    bsz, n, k = x.shape
    tm = min(ROW_TILE, n)
    assert n % tm == 0
    in_specs = [pl.BlockSpec((1, tm, k), lambda b, i: (b, i, 0))]
    in_specs += [_bcast_spec(a) for a in pre_args]
    in_specs += [pl.BlockSpec(w.shape, lambda b, i: (0, 0)) for w in ws]
    out_specs = [pl.BlockSpec((1, tm, w.shape[1]), lambda b, i: (b, i, 0)) for w in ws]
    out_shape = [jax.ShapeDtypeStruct((bsz, n, w.shape[1]), F32) for w in ws]
    vmem = 2 * (tm * k * 4 + sum(w.size * 2 + tm * w.shape[1] * 4 for w in ws))
    return pl.pallas_call(
        functools.partial(_linear_kernel, pre=pre, n_pre=len(pre_args), n_w=len(ws)),
        grid=(bsz, n // tm), in_specs=in_specs, out_specs=out_specs, out_shape=out_shape,
        compiler_params=_params(("parallel", "parallel"), vmem), name=name,
    )(x, *pre_args, *ws)


def _mix_out_kernel(*refs, n_y):
    x_ref = refs[0]
    y_refs = refs[1:1 + n_y]
    w_refs = refs[1 + n_y:1 + 2 * n_y]
    gate_ref, g_ref, b_ref, sh_ref, sc_ref, xo_ref, ff_ref = refs[1 + 2 * n_y:]
    y = None
    for y_ref, w_ref in zip(y_refs, w_refs):
        t = jnp.dot(y_ref[0].astype(BF16), w_ref[...], preferred_element_type=F32)
        y = t if y is None else y + t
    xn = _layer_norm(DN_ALPHA * x_ref[0] + gate_ref[0] * y, g_ref[0], b_ref[0])
    xo_ref[0] = xn
    _store_rows(ff_ref.at[0], xn * (1.0 + sc_ref[0]) + sh_ref[0])


def _mix_out(x, ys, ws, gate, ln_g, ln_b, shift2, scale2):
    bsz, n, d = x.shape
    tm = min(ROW_TILE, n)
    assert n % tm == 0
    row = lambda c: pl.BlockSpec((1, tm, c), lambda b, i: (b, i, 0))
    vecs = [gate, ln_g, ln_b, shift2, scale2]
    in_specs = [row(d)] + [row(y.shape[2]) for y in ys]
    in_specs += [pl.BlockSpec(w.shape, lambda b, i: (0, 0)) for w in ws]
    in_specs += [_bcast_spec(a) for a in vecs]
    out_specs = [row(d), pl.BlockSpec((1, tm, d // LANES, LANES), lambda b, i: (b, i, 0, 0))]
    vmem = 2 * (3 * tm * d * 4 + sum(tm * y.shape[2] * 4 + w.size * 2 for y, w in zip(ys, ws)))
    return pl.pallas_call(
        functools.partial(_mix_out_kernel, n_y=len(ys)),
        grid=(bsz, n // tm), in_specs=in_specs, out_specs=out_specs,
        out_shape=[jax.ShapeDtypeStruct((bsz, n, d), F32), jax.ShapeDtypeStruct((bsz, n, d // LANES, LANES), F32)],
        compiler_params=_params(("parallel", "parallel"), vmem), name="mix_out",
    )(x, *ys, *ws, *vecs)


def _attn_kernel(q_ref, k_ref, vt_ref, o_ref, *, tk, hp):
    nk = k_ref.shape[2]
    tq = q_ref.shape[2]
    dva = vt_ref.shape[2]
    dv = dva - SUBLANES

    def body(j, carry):
        off = pl.multiple_of(j * tk, tk)
        new = []
        for h in range(hp):
            m_prev, acc = carry[h]
            st = lax.dot_general(k_ref[0, h, pl.ds(off, tk), :], q_ref[0, h], (((1,), (1,)), ((), ())),
                                 preferred_element_type=F32)
            m_new = jnp.maximum(m_prev, jnp.max(st, 0, keepdims=True))
            p = jnp.exp2(st - m_new).astype(BF16)
            alpha = jnp.exp2(m_prev - m_new)
            acc = alpha * acc + jnp.dot(vt_ref[0, h, :, pl.ds(off, tk)], p, preferred_element_type=F32)
            new.append((m_new, acc))
        return tuple(new)

    init = tuple((jnp.full((1, tq), -jnp.inf, F32), jnp.zeros((dva, tq), F32)) for _ in range(hp))
    fin = lax.fori_loop(0, nk // tk, body, init)
    outs = [acc[:dv] / acc[dv:dv + 1] for _, acc in fin]
    o_ref[0] = jnp.concatenate(outs, 0).T


def _attention(q, k, vt, tq, tk):
    bsz, h, nq, dk = q.shape
    nk, dva = k.shape[2], vt.shape[2]
    dv = dva - SUBLANES
    hp = ATT_HEADS_PER_STEP
    assert nq % tq == 0 and nk % tk == 0 and h % hp == 0
    vmem = 2 * hp * (tq * LANES * 2 + nk * LANES * 2 + dva * nk * 2) + 2 * tq * hp * dv * 4 + 6 * hp * tk * tq * 4
    return pl.pallas_call(
        functools.partial(_attn_kernel, tk=tk, hp=hp), grid=(bsz, h // hp, nq // tq),
        in_specs=[pl.BlockSpec((1, hp, tq, dk), lambda b, g, i: (b, g, i, 0)),
                  pl.BlockSpec((1, hp, nk, dk), lambda b, g, i: (b, g, 0, 0)),
                  pl.BlockSpec((1, hp, dva, nk), lambda b, g, i: (b, g, 0, 0))],
        out_specs=pl.BlockSpec((1, tq, hp * dv), lambda b, g, i: (b, i, g)),
        out_shape=jax.ShapeDtypeStruct((bsz, nq, h * dv), F32),
        compiler_params=_params(("parallel", "parallel", "arbitrary"), vmem), name="mla_attention",
    )(q, k, vt)


def _key_tile(nk):
    for t in (768, 512, 384, 256, 128):
        if nk % t == 0:
            return t
    return nk


def _router_kernel(x_ref, wt_ref, bias_ref, upper_ref, idx_ref, w_ref, rank_ref, cnt_ref, run_sc):
    i = pl.program_id(0)
    tm = x_ref.shape[0]

    @pl.when(i == 0)
    def _():
        run_sc[...] = jnp.zeros(run_sc.shape, F32)

    logits = lax.dot_general(wt_ref[...], _rows_2d(x_ref), (((1,), (1,)), ((), ())),
                             precision=lax.Precision.HIGHEST, preferred_element_type=F32)
    sc = jax.nn.sigmoid(logits)
    ch = sc + bias_ref[:, :1]
    neg = -jnp.inf
    chg = ch.reshape(N_EXPERT_GROUPS, N_EXPERTS // N_EXPERT_GROUPS, tm)
    m1 = jnp.max(chg, axis=1)
    eq = chg == m1[:, None, :]
    cnt = jnp.sum(eq.astype(F32), axis=1)
    m2 = jnp.max(jnp.where(eq, neg, chg), axis=1)
    g2 = m1 + jnp.where(cnt >= 2.0, m1, m2)
    gi = lax.broadcasted_iota(I32, g2.shape, 0)
    beaten = jnp.zeros(g2.shape, F32)
    for g in range(N_EXPERT_GROUPS):
        row = g2[g:g + 1, :]
        beaten = beaten + jnp.where(row > g2, 1.0, jnp.where(row == g2, jnp.where(gi > g, 1.0, 0.0), 0.0))
    keep = beaten < float(TOPK_GROUPS)
    cur = jnp.where(keep[:, None, :], chg, neg).reshape(N_EXPERTS, tm)
    eidx = lax.broadcasted_iota(I32, (N_EXPERTS, tm), 0)
    multi = jnp.zeros((N_EXPERTS, tm), F32)
    hits, idx_rows, w_rows = [], [], []
    for _ in range(TOP_K):
        m = jnp.max(cur, axis=0, keepdims=True)
        sel = jnp.min(jnp.where(cur == m, eidx, N_EXPERTS), axis=0, keepdims=True)
        hit = eidx == sel
        idx_rows.append(sel)
        w_rows.append(jnp.sum(jnp.where(hit, sc, 0.0), axis=0, keepdims=True))
        cur = jnp.where(hit, neg, cur)
        multi = multi + jnp.where(hit, 1.0, 0.0)
        hits.append(hit)
    base = jnp.concatenate([run_sc[...]] * (tm // LANES), axis=1)
    before = jnp.dot(multi.astype(BF16), upper_ref[...], preferred_element_type=F32) + base
    rank_rows = [jnp.sum(jnp.where(hit, before, 0.0), axis=0, keepdims=True) for hit in hits]
    w = jnp.concatenate(w_rows, axis=0)
    idx_ref[...] = jnp.concatenate(idx_rows, axis=0)
    w_ref[...] = w / jnp.sum(w, axis=0, keepdims=True) * ROUTED_SCALE
    rank_ref[...] = jnp.concatenate(rank_rows, axis=0).astype(I32)
    run_sc[...] = run_sc[...] + jnp.dot(multi.astype(BF16), jnp.ones((tm, LANES), BF16), preferred_element_type=F32)
    cnt_ref[...] = run_sc[...]


def _router(tokens3, router_w, router_bias):
    t, dc, _ = tokens3.shape
    tm = ROUTE_TOKENS
    assert t % tm == 0
    wt = router_w.T
    bias = jnp.broadcast_to(router_bias.astype(F32)[:, None], (N_EXPERTS, LANES))
    r = jnp.arange(tm)
    upper = (r[:, None] < r[None, :]).astype(BF16)
    col = pl.BlockSpec((TOP_K, tm), lambda i: (0, i))
    full = lambda a: pl.BlockSpec(a.shape, lambda i: (0,) * a.ndim)
    vmem = 2 * (tm * dc * LANES * 4 + wt.size * 4) + 40 * N_EXPERTS * tm * 4
    return pl.pallas_call(
        _router_kernel, grid=(t // tm,),
        in_specs=[pl.BlockSpec((tm, dc, LANES), lambda i: (i, 0, 0)), full(wt), full(bias), full(upper)],
        out_specs=[col, col, col, pl.BlockSpec((N_EXPERTS, LANES), lambda i: (0, 0))],
        out_shape=[jax.ShapeDtypeStruct((TOP_K, t), I32), jax.ShapeDtypeStruct((TOP_K, t), F32),
                   jax.ShapeDtypeStruct((TOP_K, t), I32), jax.ShapeDtypeStruct((N_EXPERTS, LANES), F32)],
        scratch_shapes=[pltpu.VMEM((N_EXPERTS, LANES), F32)],
        compiler_params=_params(("arbitrary",), vmem), name="moe_router",
    )(tokens3, wt, bias, upper)


def _positions_kernel(idx_ref, rank_ref, start_ref, pos_ref):
    tm = idx_ref.shape[1]
    eidx = lax.broadcasted_iota(I32, (N_EXPERTS, tm), 0)
    start = jnp.concatenate([start_ref[...]] * (tm // LANES), axis=1)
    rows = [jnp.sum(jnp.where(eidx == idx_ref[k:k + 1, :], start, 0), axis=0, keepdims=True) for k in range(TOP_K)]
    pos_ref[...] = jnp.concatenate(rows, axis=0) + rank_ref[...]


def _positions(idx, rank, pad_start):
    t = idx.shape[1]
    tm = ROUTE_TOKENS
    start = jnp.broadcast_to(pad_start.astype(I32)[:, None], (N_EXPERTS, LANES))
    col = pl.BlockSpec((TOP_K, tm), lambda i: (0, i))
    return pl.pallas_call(
        _positions_kernel, grid=(t // tm,),
        in_specs=[col, col, pl.BlockSpec((N_EXPERTS, LANES), lambda i: (0, 0))], out_specs=col,
        out_shape=jax.ShapeDtypeStruct((TOP_K, t), I32),
        compiler_params=_params(("parallel",), 8 * N_EXPERTS * tm * 4), name="moe_positions",
    )(idx, rank, start)


def _moe_plan(counts, t):
    n_blocks = -(-(t * TOP_K) // MOE_ROWS) + N_EXPERTS
    c = counts.astype(I32)
    padded = (c + MOE_ROWS - 1) // MOE_ROWS * MOE_ROWS
    pad_end = jnp.cumsum(padded)
    pad_start = pad_end - padded
    block_e = jnp.minimum(jnp.searchsorted(pad_end, jnp.arange(n_blocks, dtype=I32) * MOE_ROWS, side='right'),
                          N_EXPERTS - 1).astype(I32)
    n_valid = (pad_end[-1:] // MOE_ROWS).astype(I32)
    return pad_start, block_e, n_valid, n_blocks * MOE_ROWS


def _row_copy(src, dst, sem):
    return pltpu.make_async_copy(src, dst, sem)


def _dispatch_kernel(pos_ref, x_ref, buf_in_ref, buf_ref, sem):
    del buf_in_ref
    tm = x_ref.shape[0]

    def issue(t, carry):
        for k in range(TOP_K):
            _row_copy(x_ref.at[t], buf_ref.at[pos_ref[k, t]], sem).start()
        return carry

    lax.fori_loop(0, tm, issue, 0)
    for k in range(TOP_K):
        _row_copy(x_ref, buf_ref.at[pl.ds(0, tm)], sem).wait()


def _dispatch(pos, tokens3, cap):
    t, dc, _ = tokens3.shape
    tm = ROUTE_TOKENS
    zeros = jnp.zeros((cap, dc, LANES), tokens3.dtype)
    return pl.pallas_call(
        _dispatch_kernel, grid=(t // tm,),
        in_specs=[pl.BlockSpec((TOP_K, tm), lambda i: (0, i), memory_space=pltpu.SMEM),
                  pl.BlockSpec((tm, dc, LANES), lambda i: (i, 0, 0)),
                  pl.BlockSpec(memory_space=pl.ANY)],
        out_specs=pl.BlockSpec(memory_space=pl.ANY),
        out_shape=jax.ShapeDtypeStruct((cap, dc, LANES), tokens3.dtype),
        scratch_shapes=[pltpu.SemaphoreType.DMA(())],
        input_output_aliases={2: 0},
        compiler_params=_params(("arbitrary",), 2 * tm * dc * LANES * 4, has_side_effects=True),
        name="moe_dispatch",
    )(pos, tokens3, zeros)


def _experts_kernel(be_ref, nv_ref, x_ref, wgu_ref, wdn_ref, o_ref, wgu_sc, wdn_sc):
    i = pl.program_id(0)

    @pl.when(i < nv_ref[0])
    def _():
        @pl.when(jnp.logical_or(i == 0, be_ref[i] != be_ref[jnp.maximum(i - 1, 0)]))
        def _():
            wgu_sc[...] = wgu_ref[0].astype(BF16)
            wdn_sc[...] = wdn_ref[0].astype(BF16)

        _store_rows(o_ref, _swiglu_rows(_rows_2d(x_ref).astype(BF16), wgu_sc[...], wdn_sc[...]))

    @pl.when(i >= nv_ref[0])
    def _():
        o_ref[...] = jnp.zeros(o_ref.shape, F32)


def _moe_experts(xb, block_e, n_valid, w_gu, w_down):
    cap, dc, _ = xb.shape
    d = dc * LANES
    gu = w_gu.shape[2]
    ed = w_down.shape[1]
    live = lambda i, be, nv: jnp.minimum(i, nv[0] - 1)
    grid_spec = pltpu.PrefetchScalarGridSpec(
        num_scalar_prefetch=2, grid=(cap // MOE_ROWS,),
        in_specs=[pl.BlockSpec((MOE_ROWS, dc, LANES), lambda i, be, nv: (live(i, be, nv), 0, 0)),
                  pl.BlockSpec((1, d, gu), lambda i, be, nv: (be[i], 0, 0)),
                  pl.BlockSpec((1, ed, d), lambda i, be, nv: (be[i], 0, 0))],
        out_specs=pl.BlockSpec((MOE_ROWS, dc, LANES), lambda i, be, nv: (i, 0, 0)),
        scratch_shapes=[pltpu.VMEM((d, gu), BF16), pltpu.VMEM((ed, d), BF16)])
    vmem = 2 * (2 * MOE_ROWS * d * 4 + d * gu * 4 + ed * d * 4) + (d * gu + ed * d) * 2 + 4 * MOE_ROWS * d * 4
    return pl.pallas_call(
        _experts_kernel, grid_spec=grid_spec, out_shape=jax.ShapeDtypeStruct((cap, dc, LANES), F32),
        compiler_params=_params(("arbitrary",), vmem), name="moe_experts",
    )(block_e, n_valid, xb, w_gu, w_down)


def _ffn_out_kernel(pos_ref, w_ref, x_ref, ff_ref, yb_ref, wgu_ref, wdn_ref, gate_ref, g_ref, b_ref, xo_ref,
                    rows_sc, sum_sc, sem):
    tm = x_ref.shape[1]

    def issue(t, carry):
        for k in range(TOP_K):
            _row_copy(yb_ref.at[pos_ref[k, t]], rows_sc.at[k, t], sem).start()
        return carry

    lax.fori_loop(0, tm, issue, 0)
    shared = _swiglu_rows(_rows_2d(ff_ref.at[0]).astype(BF16), wgu_ref[...], wdn_ref[...])
    for k in range(TOP_K):
        _row_copy(yb_ref.at[pl.ds(0, tm)], rows_sc.at[k], sem).wait()

    def combine(t, carry):
        acc = w_ref[0, t] * rows_sc[0, t]
        for k in range(1, TOP_K):
            acc = acc + w_ref[k, t] * rows_sc[k, t]
        sum_sc[t] = acc
        return carry

    lax.fori_loop(0, tm, combine, 0)
    out = _rows_2d(sum_sc) + shared
    xo_ref[0] = _layer_norm(DN_ALPHA * x_ref[0] + gate_ref[0] * out, g_ref[0], b_ref[0])


def _ffn_out(x, ff3, yb, pos, w, tile0, sh_gu, sh_down, gate, ln_g, ln_b):
    bsz, n, d = x.shape
    dc = d // LANES
    tm = min(ROUTE_TOKENS, n)
    nt = n // tm
    row = pl.BlockSpec((1, tm, d), lambda b, i: (b, i, 0))
    col = lambda b, i: (0, tile0 + b * nt + i)
    vecs = [gate, ln_g, ln_b]
    in_specs = [pl.BlockSpec((TOP_K, tm), col, memory_space=pltpu.SMEM),
                pl.BlockSpec((TOP_K, tm), col, memory_space=pltpu.SMEM),
                row, pl.BlockSpec((1, tm, dc, LANES), lambda b, i: (b, i, 0, 0)),
                pl.BlockSpec(memory_space=pl.ANY),
                pl.BlockSpec(sh_gu.shape, lambda b, i: (0, 0)), pl.BlockSpec(sh_down.shape, lambda b, i: (0, 0))]
    in_specs += [_bcast_spec(a) for a in vecs]
    vmem = (TOP_K + 1) * tm * d * 4 + 2 * (3 * tm * d * 4 + sh_gu.size * 2 + sh_down.size * 2) + 4 * tm * d * 4
    return pl.pallas_call(
        _ffn_out_kernel, grid=(bsz, nt), in_specs=in_specs, out_specs=row,
        out_shape=jax.ShapeDtypeStruct((bsz, n, d), F32),
        scratch_shapes=[pltpu.VMEM((TOP_K, tm, dc, LANES), F32), pltpu.VMEM((tm, dc, LANES), F32),
                        pltpu.SemaphoreType.DMA(())],
        compiler_params=_params(("arbitrary", "arbitrary"), vmem), name="moe_combine_ffn_out",
    )(pos, w, x, ff3, yb, sh_gu, sh_down, *vecs)


def _moe_dispatch_experts(tokens3, router_w, router_bias, w_gu, w_down):
    t = tokens3.shape[0]
    idx, w, rank, counts = _router(tokens3, router_w, router_bias)
    pad_start, block_e, n_valid, cap = _moe_plan(counts[:, 0], t)
    pos = _positions(idx, rank, pad_start)
    xb = _dispatch(pos, tokens3, cap)
    return _moe_experts(xb, block_e, n_valid, w_gu, w_down), pos, w


def _rope_tables(n):
    rows = n // GRID_W
    row = jnp.repeat(jnp.arange(rows), GRID_W).astype(F32)
    col = jnp.tile(jnp.arange(GRID_W), rows).astype(F32)
    half = MLA_ROPE // 2
    inv = ROPE_THETA ** (-jnp.arange(0, half, 2, dtype=F32) / half)
    ang = jnp.concatenate([row[:, None] * inv, col[:, None] * inv], -1)
    return jnp.cos(ang), jnp.sin(ang)


def _rope(x, cos, sin):
    x1, x2 = x[..., 0::2], x[..., 1::2]
    return jnp.stack([x1 * cos - x2 * sin, x1 * sin + x2 * cos], -1).reshape(x.shape)


def _dwconv(x, w, b):
    pad = w.shape[0] // 2
    y = lax.conv_general_dilated(x, w[:, None, :], window_strides=(1,), padding=[(pad, pad)],
                                 dimension_numbers=('NWC', 'WIO', 'NWC'), feature_group_count=x.shape[-1])
    return y + b


def _hyena_filters(n, w1, b1, f1, w2, b2, f2, w3):
    t = jnp.linspace(0.0, 1.0, n, dtype=F32)[:, None]
    w = 2.0 * math.pi * jnp.arange(n, dtype=F32)[:, None] / n
    fr = jnp.linspace(1e-4, HY_BANDS - 1, HY_BANDS, dtype=F32)
    z = jnp.concatenate([t, jnp.cos(fr * w), -jnp.sin(fr * w)], -1)
    hp = lax.Precision.HIGHEST
    h = jnp.sin(f1 * (jnp.dot(z, w1, precision=hp) + b1))
    h = jnp.sin(f2 * (jnp.dot(h, w2, precision=hp) + b2))
    h = jnp.dot(h, w3, precision=hp)
    deltas = jnp.abs(jnp.linspace(HY_DECAY_MIN, HY_DECAY_MAX, HY_WIDTH, dtype=F32))
    decay = jnp.exp(-t * deltas)
    h_f = h[:, :HY_WIDTH] * decay
    h_b = h[:, HY_WIDTH:] * decay
    k = jnp.concatenate([h_f, jnp.zeros((1, HY_WIDTH), F32), h_b[:0:-1]], 0)
    return k / jnp.sum(jnp.abs(k), 0, keepdims=True)


def _hyena_sequence(proj, conv_w, conv_b, filt, skip):
    n = proj.shape[1]
    u = _dwconv(proj, conv_w, conv_b)
    x0, x1, v = jnp.split(u, 3, axis=-1)
    z = v * x1
    k = _hyena_filters(n, *filt)
    y = jnp.fft.irfft(jnp.fft.rfft(z, n=2 * n, axis=1) * jnp.fft.rfft(k, n=2 * n, axis=0)[None],
                      n=2 * n, axis=1)[:, :n]
    return (y + z * skip) * x0


def _mixer_hyena_mla(x, ctx, sh_l, sc_l, sh_c, sc_c, p):
    bsz, n, _ = x.shape
    nc = ctx.shape[1]
    w_in = p['w_in'].astype(BF16)
    w_parts = [w_in[:, :OFF_Q], w_in[:, OFF_Q:OFF_KV], w_in[:, OFF_KV:OFF_KPE], w_in[:, OFF_KPE:]]
    w_qb = p['w_qb'].astype(BF16)
    w_kvb = p['w_kvb'].astype(BF16)
    qg = p['q_norm'].reshape(1, 1, -1)
    kvg = p['kv_norm'].reshape(1, 1, -1)

    def project(h, shift, scale):
        hy, ql, kvl, kpe = _linear(h, w_parts, pre="mod", pre_args=(shift, scale), name="in_proj")
        q, = _linear(ql, [w_qb], pre="rms", pre_args=(qg,), name="q_proj")
        kv, = _linear(kvl, [w_kvb], pre="rms", pre_args=(kvg,), name="kv_proj")
        m = h.shape[1]
        q = q.reshape(bsz, m, MLA_HEADS, MLA_QK).transpose(0, 2, 1, 3)
        kv = kv.reshape(bsz, m, MLA_HEADS, MLA_NOPE + MLA_V).transpose(0, 2, 1, 3)
        return hy, q[..., :MLA_NOPE], q[..., MLA_NOPE:], kv[..., :MLA_NOPE], kv[..., MLA_NOPE:], kpe

    hy_l, qn_l, qp_l, kn_l, v_l, kp_l = project(x, sh_l, sc_l)
    hy_c, qn_c, qp_c, kn_c, v_c, kp_c = project(ctx, sh_c, sc_c)
    cos, sin = _rope_tables(n)
    qp_l = _rope(qp_l, cos, sin)
    kp_l = _rope(kp_l, cos, sin)

    def heads(kp):
        return jnp.broadcast_to(kp[:, None], (bsz, MLA_HEADS) + kp.shape[1:])

    def values_t(v):
        ones = jnp.ones((bsz, MLA_HEADS, SUBLANES, v.shape[2]), F32)
        return jnp.concatenate([v.transpose(0, 1, 3, 2), ones], 2).astype(BF16)

    q_l = (jnp.concatenate([qn_l, qp_l], -1) * (MLA_SCALE * LOG2E)).astype(BF16)
    q_c = (jnp.concatenate([qn_c, qp_c], -1) * (MLA_SCALE * LOG2E)).astype(BF16)
    k_c = jnp.concatenate([kn_c, heads(kp_c)], -1).astype(BF16)
    k_l = jnp.concatenate([kn_l, heads(kp_l)], -1).astype(BF16)
    k_all = jnp.concatenate([k_c, k_l], 2)
    v_all = jnp.concatenate([v_c, v_l], 2)
    att_l = _attention(q_l, k_all, values_t(v_all), tq=min(ATT_Q_TILE, n), tk=_key_tile(nc + n))
    att_c = _attention(q_c, k_c, values_t(v_c), tq=nc, tk=_key_tile(nc))

    filt = (p['filt_w1'], p['filt_b1'], p['filt_freq1'], p['filt_w2'], p['filt_b2'], p['filt_freq2'], p['filt_w3'])
    hyo_l = _hyena_sequence(hy_l, p['conv_w'], p['conv_b'], filt, p['skip'])
    hyo_c = _hyena_sequence(hy_c, p['conv_w'], p['conv_b'], filt, p['skip'])
    return (hyo_l, att_l), (hyo_c, att_c)


def _ssd_scan(x, dt, a_coef, bm, cm, s0):
    bsz, n, h, pd = x.shape
    g, ns = bm.shape[2], bm.shape[3]
    r = h // g
    nc = n // SSD_CHUNK
    a = dt * a_coef
    xdt = x * dt[..., None]

    def chunks(t):
        return jnp.moveaxis(t.reshape((bsz, nc, SSD_CHUNK) + t.shape[2:]), 1, 0)

    xs = (chunks(xdt.reshape(bsz, n, g, r, pd)), chunks(a.reshape(bsz, n, g, r)), chunks(bm), chunks(cm))
    tri = jnp.tril(jnp.ones((SSD_CHUNK, SSD_CHUNK), bool))

    def step(s, inp):
        xc, ac, bc, cc = inp
        acs = jnp.cumsum(ac, axis=1)
        acs_t = jnp.transpose(acs, (0, 2, 3, 1))
        seg = acs_t[..., :, None] - acs_t[..., None, :]
        decay = jnp.exp(jnp.where(tri, seg, -jnp.inf))
        cb = jnp.einsum('btgn,bsgn->bgts', cc, bc)
        y_diag = jnp.einsum('bgrts,bsgrp->btgrp', cb[:, :, None] * decay, xc)
        y_off = jnp.einsum('btgn,bgrpn->btgrp', cc, s) * jnp.exp(acs)[..., None]
        to_end = jnp.exp(acs[:, -1:] - acs)
        s_new = (jnp.exp(acs[:, -1])[..., None, None] * s
                 + jnp.einsum('bsgn,bsgrp->bgrpn', bc, xc * to_end[..., None]))
        return s_new, y_diag + y_off

    s_fin, ys = lax.scan(step, s0.reshape(bsz, g, r, pd, ns), xs)
    y = jnp.moveaxis(ys, 0, 1).reshape(bsz, n, h, pd)
    return y, s_fin.reshape(bsz, h, pd, ns)


def _mixer_ssd(x, ctx, sh_l, sc_l, sh_c, sc_c, p):
    a_f = -jnp.exp(p['a_log_f'])
    a_b = -jnp.exp(p['a_log_b'])
    w_in = p['w_in'].astype(BF16)
    w_parts = [w_in[:, :SSD_INNER], w_in[:, SSD_INNER:SSD_INNER + SSD_XBC], w_in[:, SSD_INNER + SSD_XBC:]]

    def pre(h, shift, scale):
        bsz, n = h.shape[:2]
        z, xbc, dt = _linear(h, w_parts, pre="mod", pre_args=(shift, scale), name="ssd_in_proj")
        xbc = jax.nn.silu(_dwconv(xbc, p['conv_w'], p['conv_b']))
        xs = xbc[..., :SSD_INNER].reshape(bsz, n, SSD_HEADS, SSD_HEADDIM)
        gn = SSD_GROUPS * SSD_STATE
        bm = xbc[..., SSD_INNER:SSD_INNER + gn].reshape(bsz, n, SSD_GROUPS, SSD_STATE)
        cm = xbc[..., SSD_INNER + gn:].reshape(bsz, n, SSD_GROUPS, SSD_STATE)
        dt_f = jax.nn.softplus(dt[..., :SSD_HEADS] + p['dt_bias_f'])
        dt_b = jax.nn.softplus(dt[..., SSD_HEADS:] + p['dt_bias_b'])
        return z, xs, bm, cm, dt_f, dt_b

    def flip(t):
        return jnp.flip(t, axis=1)

    def post(y_f, y_b, xs, z):
        bsz, n = z.shape[:2]
        y = y_f + y_b + xs * p['d'][:, None]
        y = y.reshape(bsz, n, SSD_INNER) * jax.nn.silu(z)
        yg = y.reshape(bsz, n, SSD_GROUPS, SSD_INNER // SSD_GROUPS)
        yg = yg * lax.rsqrt(jnp.mean(yg * yg, -1, keepdims=True) + RMS_EPS) * p['norm_g'].reshape(SSD_GROUPS, -1)
        return yg.reshape(bsz, n, SSD_INNER)

    zc, xc, bc, cc, dfc, dbc = pre(ctx, sh_c, sc_c)
    zl, xl, bl, cl, dfl, dbl = pre(x, sh_l, sc_l)
    s0 = jnp.zeros((ctx.shape[0], SSD_HEADS, SSD_HEADDIM, SSD_STATE), F32)
    _, sc_f = _ssd_scan(xc, dfc, a_f, bc, cc, s0)
    _, sc_b = _ssd_scan(flip(xc), flip(dbc), a_b, flip(bc), flip(cc), s0)
    yl_f, _ = _ssd_scan(xl, dfl, a_f, bl, cl, sc_f)
    yl_b, _ = _ssd_scan(flip(xl), flip(dbl), a_b, flip(bl), flip(cl), sc_b)
    return post(yl_f, flip(yl_b), xl, zl)


def kernel(x, c, ctx, c_ctx, mod_w, mod_b, ln_mix_g, ln_mix_b, ln_ffn_g, ln_ffn_b, a_w_in, hy_conv_w, hy_conv_b, hy_filt_w1, hy_filt_b1, hy_filt_freq1, hy_filt_w2, hy_filt_b2, hy_filt_freq2, hy_filt_w3, hy_skip, mla_q_norm, mla_w_qb, mla_kv_norm, mla_w_kvb, a_w_out, ssd_w_in, ssd_conv_w, ssd_conv_b, ssd_dt_bias_f, ssd_dt_bias_b, ssd_a_log_f, ssd_a_log_b, ssd_d, ssd_norm_g, ssd_w_out, router_w, router_bias, exp_w_gu, exp_w_down, sh_w_gu, sh_w_down):
    bsz, n_lat, d = x.shape
    n_ctx = ctx.shape[1]
    dc = d // LANES
    hp = lax.Precision.HIGHEST
    for l in range(DEPTH):
        last = l == DEPTH - 1
        i = l // 2
        mod = (jnp.dot(jax.nn.silu(c), mod_w[l], precision=hp) + mod_b[l]).reshape(bsz, N_MOD, 1, d)
        mod_c = (jnp.dot(jax.nn.silu(c_ctx), mod_w[l], precision=hp) + mod_b[l]).reshape(1, N_MOD, 1, d)
        sh1, sc1, g1, sh2, sc2, g2 = [mod[:, j] for j in range(N_MOD)]
        csh1, csc1, cg1, csh2, csc2, cg2 = [mod_c[:, j] for j in range(N_MOD)]
        vec = lambda a: a.reshape(1, 1, d)
        if l % 2 == 0:
            p = {"w_in": a_w_in[i], "conv_w": hy_conv_w[i], "conv_b": hy_conv_b[i],
                 "filt_w1": hy_filt_w1[i], "filt_b1": hy_filt_b1[i], "filt_freq1": hy_filt_freq1[i],
                 "filt_w2": hy_filt_w2[i], "filt_b2": hy_filt_b2[i], "filt_freq2": hy_filt_freq2[i],
                 "filt_w3": hy_filt_w3[i], "skip": hy_skip[i], "q_norm": mla_q_norm[i], "w_qb": mla_w_qb[i],
                 "kv_norm": mla_kv_norm[i], "w_kvb": mla_w_kvb[i]}
            ys_l, ys_c = _mixer_hyena_mla(x, ctx, sh1, sc1, csh1, csc1, p)
            w_out = a_w_out[i].astype(BF16)
            ws = [w_out[:HY_WIDTH], w_out[HY_WIDTH:]]
        else:
            p = {"w_in": ssd_w_in[i], "conv_w": ssd_conv_w[i], "conv_b": ssd_conv_b[i],
                 "dt_bias_f": ssd_dt_bias_f[i], "dt_bias_b": ssd_dt_bias_b[i],
                 "a_log_f": ssd_a_log_f[i], "a_log_b": ssd_a_log_b[i], "d": ssd_d[i],
                 "norm_g": ssd_norm_g[i]}
            assert last
            ys_l, ys_c = (_mixer_ssd(x, ctx, sh1, sc1, csh1, csc1, p),), None
            ws = [ssd_w_out[i].astype(BF16)]
        x, ff_x = _mix_out(x, ys_l, ws, g1, vec(ln_mix_g[l]), vec(ln_mix_b[l]), sh2, sc2)
        sh_gu = sh_w_gu[l].astype(BF16)
        sh_down = sh_w_down[l].astype(BF16)
        ln_g, ln_b = vec(ln_ffn_g[l]), vec(ln_ffn_b[l])
        moe_w = (router_w[l], router_bias[l], exp_w_gu[l], exp_w_down[l])
        if last:
            yb, pos, w = _moe_dispatch_experts(ff_x.reshape(-1, dc, LANES), *moe_w)
            x = _ffn_out(x, ff_x, yb, pos, w, 0, sh_gu, sh_down, g2, ln_g, ln_b)
        else:
            ctx, ff_c = _mix_out(ctx, ys_c, ws, cg1, vec(ln_mix_g[l]), vec(ln_mix_b[l]), csh2, csc2)
            tokens = jnp.concatenate([ff_c.reshape(-1, dc, LANES), ff_x.reshape(-1, dc, LANES)], 0)
            yb, pos, w = _moe_dispatch_experts(tokens, *moe_w)
            assert (bsz * n_ctx) % ROUTE_TOKENS == 0
            ctx = _ffn_out(ctx, ff_c, yb, pos, w, 0, sh_gu, sh_down, cg2, ln_g, ln_b)
            x = _ffn_out(x, ff_x, yb, pos, w, bsz * n_ctx // ROUTE_TOKENS, sh_gu, sh_down, g2, ln_g, ln_b)
    return x
```

```python
import functools
import math

import jax
import jax.numpy as jnp
from jax import lax
from jax.experimental import pallas as pl
from jax.experimental.pallas import tpu as pltpu

F32 = jnp.float32
BF16 = jnp.bfloat16
I32 = jnp.int32

D_MODEL = 1024
DEPTH = 2
GRID_W = 64
N_MOD = 6

HY_WIDTH = 512
HY_EMB = 33
HY_BANDS = (HY_EMB - 1) // 2
HY_TARGET = 1e-2
HY_FAST_DECAY = 0.3
HY_SLOW_DECAY = 1.5
HY_DECAY_MIN = math.log(HY_TARGET) / HY_SLOW_DECAY
HY_DECAY_MAX = math.log(HY_TARGET) / HY_FAST_DECAY

MLA_HEADS = 8
MLA_NOPE = 64
MLA_ROPE = 32
MLA_V = 64
MLA_Q_RANK = 256
MLA_KV_RANK = 128
MLA_QK = MLA_NOPE + MLA_ROPE
MLA_SCALE = MLA_QK ** -0.5
ROPE_THETA = 10000.0
LOG2E = math.log2(math.e)

OFF_Q = 3 * HY_WIDTH
OFF_KV = OFF_Q + MLA_Q_RANK
OFF_KPE = OFF_KV + MLA_KV_RANK

SSD_INNER = 2 * D_MODEL
SSD_HEADDIM = 64
SSD_HEADS = SSD_INNER // SSD_HEADDIM
SSD_GROUPS = 4
SSD_STATE = 128
SSD_CHUNK = 128
SSD_BC = SSD_GROUPS * SSD_STATE
SSD_XBC = SSD_INNER + 2 * SSD_BC
SSD_HPG = SSD_HEADS // SSD_GROUPS

N_EXPERTS = 256
TOP_K = 8
N_EXPERT_GROUPS = 8
TOPK_GROUPS = 4
EXPERT_DIM = 256
ROUTED_SCALE = 2.5

DN_ALPHA = (2 * DEPTH) ** 0.25
LN_EPS = 1e-5
RMS_EPS = 1e-6

LANES = 128
SUBLANES = 8
V7X_VMEM_CAP = 56 * 1024 * 1024

MOE_ROWS = 256
ROUTE_TOKENS = 256
ROW_TILE = 256
ATT_Q_TILE = 512
ATT_HEADS_PER_STEP = 2


def _params(semantics, vmem_bytes, **kw):
    limit = int(min(max(vmem_bytes * 5 // 4, 32 * 1024 * 1024), V7X_VMEM_CAP))
    return pltpu.CompilerParams(dimension_semantics=semantics, vmem_limit_bytes=limit, **kw)


def _bcast_spec(a):
    if a.shape[0] == 1:
        return pl.BlockSpec((1, 1, a.shape[2]), lambda b, i: (0, 0, 0))
    return pl.BlockSpec((1, 1, a.shape[2]), lambda b, i: (b, 0, 0))


def _rows_2d(ref3):
    return jnp.concatenate([ref3[:, c, :] for c in range(ref3.shape[1])], axis=1)


def _store_rows(ref3, val):
    for c in range(ref3.shape[1]):
        ref3[:, c, :] = val[:, c * LANES:(c + 1) * LANES]


def _layer_norm(r, g, b):
    mu = jnp.mean(r, -1, keepdims=True)
    c = r - mu
    var = jnp.mean(c * c, -1, keepdims=True)
    return c * lax.rsqrt(var + LN_EPS) * g + b


def _swiglu_rows(xb, w_gu, w_down):
    h = jnp.dot(xb, w_gu, preferred_element_type=F32)
    half = h.shape[1] // 2
    g, u = h[:, :half], h[:, half:]
    a = (g * jax.nn.sigmoid(g) * u).astype(BF16)
    return jnp.dot(a, w_down, preferred_element_type=F32)


def _linear_kernel(*refs, pre, n_pre, n_w):
    x = refs[0][0]
    pre_refs = refs[1:1 + n_pre]
    w_refs = refs[1 + n_pre:1 + n_pre + n_w]
    o_refs = refs[1 + n_pre + n_w:]
    if pre == "mod":
        h = x * (1.0 + pre_refs[1][0]) + pre_refs[0][0]
    elif pre == "rms":
        h = x * lax.rsqrt(jnp.mean(x * x, -1, keepdims=True) + RMS_EPS) * pre_refs[0][0]
    else:
        h = x
    hb = h.astype(BF16)
    for w_ref, o_ref in zip(w_refs, o_refs):
        o_ref[0] = jnp.dot(hb, w_ref[...], preferred_element_type=F32).astype(o_ref.dtype)


def _linear(x, ws, pre=None, pre_args=(), name="linear"):
    bsz, n, k = x.shape
    tm = min(ROW_TILE, n)
    assert n % tm == 0
    in_specs = [pl.BlockSpec((1, tm, k), lambda b, i: (b, i, 0))]
    in_specs += [_bcast_spec(a) for a in pre_args]
    in_specs += [pl.BlockSpec(w.shape, lambda b, i: (0, 0)) for w in ws]
    out_specs = [pl.BlockSpec((1, tm, w.shape[1]), lambda b, i: (b, i, 0)) for w in ws]
    out_shape = [jax.ShapeDtypeStruct((bsz, n, w.shape[1]), F32) for w in ws]
    vmem = 2 * (tm * k * 4 + sum(w.size * 2 + tm * w.shape[1] * 4 for w in ws))
    return pl.pallas_call(
        functools.partial(_linear_kernel, pre=pre, n_pre=len(pre_args), n_w=len(ws)),
        grid=(bsz, n // tm), in_specs=in_specs, out_specs=out_specs, out_shape=out_shape,
        compiler_params=_params(("parallel", "parallel"), vmem), name=name,
    )(x, *pre_args, *ws)


def _mix_out_kernel(*refs, n_y):
    x_ref = refs[0]
    y_refs = refs[1:1 + n_y]
    w_refs = refs[1 + n_y:1 + 2 * n_y]
    gate_ref, g_ref, b_ref, sh_ref, sc_ref, xo_ref, ff_ref = refs[1 + 2 * n_y:]
    y = None
    for y_ref, w_ref in zip(y_refs, w_refs):
        t = jnp.dot(y_ref[0].astype(BF16), w_ref[...], preferred_element_type=F32)
        y = t if y is None else y + t
    xn = _layer_norm(DN_ALPHA * x_ref[0] + gate_ref[0] * y, g_ref[0], b_ref[0])
    xo_ref[0] = xn
    _store_rows(ff_ref.at[0], xn * (1.0 + sc_ref[0]) + sh_ref[0])


def _mix_out(x, ys, ws, gate, ln_g, ln_b, shift2, scale2):
    bsz, n, d = x.shape
    tm = min(ROW_TILE, n)
    assert n % tm == 0
    row = lambda c: pl.BlockSpec((1, tm, c), lambda b, i: (b, i, 0))
    vecs = [gate, ln_g, ln_b, shift2, scale2]
    in_specs = [row(d)] + [row(y.shape[2]) for y in ys]
    in_specs += [pl.BlockSpec(w.shape, lambda b, i: (0, 0)) for w in ws]
    in_specs += [_bcast_spec(a) for a in vecs]
    out_specs = [row(d), pl.BlockSpec((1, tm, d // LANES, LANES), lambda b, i: (b, i, 0, 0))]
    vmem = 2 * (3 * tm * d * 4 + sum(tm * y.shape[2] * 4 + w.size * 2 for y, w in zip(ys, ws)))
    return pl.pallas_call(
        functools.partial(_mix_out_kernel, n_y=len(ys)),
        grid=(bsz, n // tm), in_specs=in_specs, out_specs=out_specs,
        out_shape=[jax.ShapeDtypeStruct((bsz, n, d), F32), jax.ShapeDtypeStruct((bsz, n, d // LANES, LANES), F32)],
        compiler_params=_params(("parallel", "parallel"), vmem), name="mix_out",
    )(x, *ys, *ws, *vecs)


def _attn_kernel(q_ref, k_ref, vt_ref, o_ref, *, tk, hp):
    nk = k_ref.shape[2]
    tq = q_ref.shape[2]
    dva = vt_ref.shape[2]
    dv = dva - SUBLANES

    def body(j, carry):
        off = pl.multiple_of(j * tk, tk)
        new = []
        for h in range(hp):
            m_prev, acc = carry[h]
            st = lax.dot_general(k_ref[0, h, pl.ds(off, tk), :], q_ref[0, h], (((1,), (1,)), ((), ())),
                                 preferred_element_type=F32)
            m_new = jnp.maximum(m_prev, jnp.max(st, 0, keepdims=True))
            p = jnp.exp2(st - m_new).astype(BF16)
            alpha = jnp.exp2(m_prev - m_new)
            acc = alpha * acc + jnp.dot(vt_ref[0, h, :, pl.ds(off, tk)], p, preferred_element_type=F32)
            new.append((m_new, acc))
        return tuple(new)

    init = tuple((jnp.full((1, tq), -jnp.inf, F32), jnp.zeros((dva, tq), F32)) for _ in range(hp))
    fin = lax.fori_loop(0, nk // tk, body, init)
    outs = [acc[:dv] / acc[dv:dv + 1] for _, acc in fin]
    o_ref[0] = jnp.concatenate(outs, 0).T


def _attention(q, k, vt, tq, tk):
    bsz, h, nq, dk = q.shape
    nk, dva = k.shape[2], vt.shape[2]
    dv = dva - SUBLANES
    hp = ATT_HEADS_PER_STEP
    assert nq % tq == 0 and nk % tk == 0 and h % hp == 0
    vmem = 2 * hp * (tq * LANES * 2 + nk * LANES * 2 + dva * nk * 2) + 2 * tq * hp * dv * 4 + 6 * hp * tk * tq * 4
    return pl.pallas_call(
        functools.partial(_attn_kernel, tk=tk, hp=hp), grid=(bsz, h // hp, nq // tq),
        in_specs=[pl.BlockSpec((1, hp, tq, dk), lambda b, g, i: (b, g, i, 0)),
                  pl.BlockSpec((1, hp, nk, dk), lambda b, g, i: (b, g, 0, 0)),
                  pl.BlockSpec((1, hp, dva, nk), lambda b, g, i: (b, g, 0, 0))],
        out_specs=pl.BlockSpec((1, tq, hp * dv), lambda b, g, i: (b, i, g)),
        out_shape=jax.ShapeDtypeStruct((bsz, nq, h * dv), F32),
        compiler_params=_params(("parallel", "parallel", "arbitrary"), vmem), name="mla_attention",
    )(q, k, vt)


def _key_tile(nk):
    for t in (768, 512, 384, 256, 128):
        if nk % t == 0:
            return t
    return nk


def _router_kernel(x_ref, wt_ref, bias_ref, upper_ref, idx_ref, w_ref, rank_ref, cnt_ref, run_sc):
    i = pl.program_id(0)
    tm = x_ref.shape[0]

    @pl.when(i == 0)
    def _():
        run_sc[...] = jnp.zeros(run_sc.shape, F32)

    logits = lax.dot_general(wt_ref[...], _rows_2d(x_ref), (((1,), (1,)), ((), ())),
                             precision=lax.Precision.HIGHEST, preferred_element_type=F32)
    sc = jax.nn.sigmoid(logits)
    ch = sc + bias_ref[:, :1]
    neg = -jnp.inf
    chg = ch.reshape(N_EXPERT_GROUPS, N_EXPERTS // N_EXPERT_GROUPS, tm)
    m1 = jnp.max(chg, axis=1)
    eq = chg == m1[:, None, :]
    cnt = jnp.sum(eq.astype(F32), axis=1)
    m2 = jnp.max(jnp.where(eq, neg, chg), axis=1)
    g2 = m1 + jnp.where(cnt >= 2.0, m1, m2)
    gi = lax.broadcasted_iota(I32, g2.shape, 0)
    beaten = jnp.zeros(g2.shape, F32)
    for g in range(N_EXPERT_GROUPS):
        row = g2[g:g + 1, :]
        beaten = beaten + jnp.where(row > g2, 1.0, jnp.where(row == g2, jnp.where(gi > g, 1.0, 0.0), 0.0))
    keep = beaten < float(TOPK_GROUPS)
    cur = jnp.where(keep[:, None, :], chg, neg).reshape(N_EXPERTS, tm)
    eidx = lax.broadcasted_iota(I32, (N_EXPERTS, tm), 0)
    multi = jnp.zeros((N_EXPERTS, tm), F32)
    hits, idx_rows, w_rows = [], [], []
    for _ in range(TOP_K):
        m = jnp.max(cur, axis=0, keepdims=True)
        sel = jnp.min(jnp.where(cur == m, eidx, N_EXPERTS), axis=0, keepdims=True)
        hit = eidx == sel
        idx_rows.append(sel)
        w_rows.append(jnp.sum(jnp.where(hit, sc, 0.0), axis=0, keepdims=True))
        cur = jnp.where(hit, neg, cur)
        multi = multi + jnp.where(hit, 1.0, 0.0)
        hits.append(hit)
    base = jnp.concatenate([run_sc[...]] * (tm // LANES), axis=1)
    before = jnp.dot(multi.astype(BF16), upper_ref[...], preferred_element_type=F32) + base
    rank_rows = [jnp.sum(jnp.where(hit, before, 0.0), axis=0, keepdims=True) for hit in hits]
    w = jnp.concatenate(w_rows, axis=0)
    idx_ref[...] = jnp.concatenate(idx_rows, axis=0)
    w_ref[...] = w / jnp.sum(w, axis=0, keepdims=True) * ROUTED_SCALE
    rank_ref[...] = jnp.concatenate(rank_rows, axis=0).astype(I32)
    run_sc[...] = run_sc[...] + jnp.dot(multi.astype(BF16), jnp.ones((tm, LANES), BF16), preferred_element_type=F32)
    cnt_ref[...] = run_sc[...]


def _router(tokens3, router_w, router_bias):
    t, dc, _ = tokens3.shape
    tm = ROUTE_TOKENS
    assert t % tm == 0
    wt = router_w.T
    bias = jnp.broadcast_to(router_bias.astype(F32)[:, None], (N_EXPERTS, LANES))
    r = jnp.arange(tm)
    upper = (r[:, None] < r[None, :]).astype(BF16)
    col = pl.BlockSpec((TOP_K, tm), lambda i: (0, i))
    full = lambda a: pl.BlockSpec(a.shape, lambda i: (0,) * a.ndim)
    vmem = 2 * (tm * dc * LANES * 4 + wt.size * 4) + 40 * N_EXPERTS * tm * 4
    return pl.pallas_call(
        _router_kernel, grid=(t // tm,),
        in_specs=[pl.BlockSpec((tm, dc, LANES), lambda i: (i, 0, 0)), full(wt), full(bias), full(upper)],
        out_specs=[col, col, col, pl.BlockSpec((N_EXPERTS, LANES), lambda i: (0, 0))],
        out_shape=[jax.ShapeDtypeStruct((TOP_K, t), I32), jax.ShapeDtypeStruct((TOP_K, t), F32),
                   jax.ShapeDtypeStruct((TOP_K, t), I32), jax.ShapeDtypeStruct((N_EXPERTS, LANES), F32)],
        scratch_shapes=[pltpu.VMEM((N_EXPERTS, LANES), F32)],
        compiler_params=_params(("arbitrary",), vmem), name="moe_router",
    )(tokens3, wt, bias, upper)


def _positions_kernel(idx_ref, rank_ref, start_ref, pos_ref):
    tm = idx_ref.shape[1]
    eidx = lax.broadcasted_iota(I32, (N_EXPERTS, tm), 0)
    start = jnp.concatenate([start_ref[...]] * (tm // LANES), axis=1)
    rows = [jnp.sum(jnp.where(eidx == idx_ref[k:k + 1, :], start, 0), axis=0, keepdims=True) for k in range(TOP_K)]
    pos_ref[...] = jnp.concatenate(rows, axis=0) + rank_ref[...]


def _positions(idx, rank, pad_start):
    t = idx.shape[1]
    tm = ROUTE_TOKENS
    start = jnp.broadcast_to(pad_start.astype(I32)[:, None], (N_EXPERTS, LANES))
    col = pl.BlockSpec((TOP_K, tm), lambda i: (0, i))
    return pl.pallas_call(
        _positions_kernel, grid=(t // tm,),
        in_specs=[col, col, pl.BlockSpec((N_EXPERTS, LANES), lambda i: (0, 0))], out_specs=col,
        out_shape=jax.ShapeDtypeStruct((TOP_K, t), I32),
        compiler_params=_params(("parallel",), 8 * N_EXPERTS * tm * 4), name="moe_positions",
    )(idx, rank, start)


def _moe_plan(counts, t):
    n_blocks = -(-(t * TOP_K) // MOE_ROWS) + N_EXPERTS
    c = counts.astype(I32)
    padded = (c + MOE_ROWS - 1) // MOE_ROWS * MOE_ROWS
    pad_end = jnp.cumsum(padded)
    pad_start = pad_end - padded
    block_e = jnp.minimum(jnp.searchsorted(pad_end, jnp.arange(n_blocks, dtype=I32) * MOE_ROWS, side='right'),
                          N_EXPERTS - 1).astype(I32)
    n_valid = (pad_end[-1:] // MOE_ROWS).astype(I32)
    return pad_start, block_e, n_valid, n_blocks * MOE_ROWS


def _row_copy(src, dst, sem):
    return pltpu.make_async_copy(src, dst, sem)


def _dispatch_kernel(pos_ref, x_ref, buf_in_ref, buf_ref, sem):
    del buf_in_ref
    tm = x_ref.shape[0]

    def issue(t, carry):
        for k in range(TOP_K):
            _row_copy(x_ref.at[t], buf_ref.at[pos_ref[k, t]], sem).start()
        return carry

    lax.fori_loop(0, tm, issue, 0)
    for k in range(TOP_K):
        _row_copy(x_ref, buf_ref.at[pl.ds(0, tm)], sem).wait()


def _dispatch(pos, tokens3, cap):
    t, dc, _ = tokens3.shape
    tm = ROUTE_TOKENS
    zeros = jnp.zeros((cap, dc, LANES), tokens3.dtype)
    return pl.pallas_call(
        _dispatch_kernel, grid=(t // tm,),
        in_specs=[pl.BlockSpec((TOP_K, tm), lambda i: (0, i), memory_space=pltpu.SMEM),
                  pl.BlockSpec((tm, dc, LANES), lambda i: (i, 0, 0)),
                  pl.BlockSpec(memory_space=pl.ANY)],
        out_specs=pl.BlockSpec(memory_space=pl.ANY),
        out_shape=jax.ShapeDtypeStruct((cap, dc, LANES), tokens3.dtype),
        scratch_shapes=[pltpu.SemaphoreType.DMA(())],
        input_output_aliases={2: 0},
        compiler_params=_params(("arbitrary",), 2 * tm * dc * LANES * 4, has_side_effects=True),
        name="moe_dispatch",
    )(pos, tokens3, zeros)


def _experts_kernel(be_ref, nv_ref, x_ref, wgu_ref, wdn_ref, o_ref, wgu_sc, wdn_sc):
    i = pl.program_id(0)

    @pl.when(i < nv_ref[0])
    def _():
        @pl.when(jnp.logical_or(i == 0, be_ref[i] != be_ref[jnp.maximum(i - 1, 0)]))
        def _():
            wgu_sc[...] = wgu_ref[0].astype(BF16)
            wdn_sc[...] = wdn_ref[0].astype(BF16)

        _store_rows(o_ref, _swiglu_rows(_rows_2d(x_ref).astype(BF16), wgu_sc[...], wdn_sc[...]))

    @pl.when(i >= nv_ref[0])
    def _():
        o_ref[...] = jnp.zeros(o_ref.shape, F32)


def _moe_experts(xb, block_e, n_valid, w_gu, w_down):
    cap, dc, _ = xb.shape
    d = dc * LANES
    gu = w_gu.shape[2]
    ed = w_down.shape[1]
    live = lambda i, be, nv: jnp.maximum(jnp.minimum(i, nv[0] - 1), 0)
    grid_spec = pltpu.PrefetchScalarGridSpec(
        num_scalar_prefetch=2, grid=(cap // MOE_ROWS,),
        in_specs=[pl.BlockSpec((MOE_ROWS, dc, LANES), lambda i, be, nv: (live(i, be, nv), 0, 0)),
                  pl.BlockSpec((1, d, gu), lambda i, be, nv: (be[i], 0, 0)),
                  pl.BlockSpec((1, ed, d), lambda i, be, nv: (be[i], 0, 0))],
        out_specs=pl.BlockSpec((MOE_ROWS, dc, LANES), lambda i, be, nv: (i, 0, 0)),
        scratch_shapes=[pltpu.VMEM((d, gu), BF16), pltpu.VMEM((ed, d), BF16)])
    vmem = 2 * (2 * MOE_ROWS * d * 4 + d * gu * 4 + ed * d * 4) + (d * gu + ed * d) * 2 + 4 * MOE_ROWS * d * 4
    return pl.pallas_call(
        _experts_kernel, grid_spec=grid_spec, out_shape=jax.ShapeDtypeStruct((cap, dc, LANES), F32),
        compiler_params=_params(("arbitrary",), vmem), name="moe_experts",
    )(block_e, n_valid, xb, w_gu, w_down)


def _ffn_out_kernel(pos_ref, w_ref, x_ref, ff_ref, yb_ref, wgu_ref, wdn_ref, gate_ref, g_ref, b_ref, xo_ref,
                    rows_sc, sum_sc, sem):
    tm = x_ref.shape[1]

    def issue(t, carry):
        for k in range(TOP_K):
            _row_copy(yb_ref.at[pos_ref[k, t]], rows_sc.at[k, t], sem).start()
        return carry

    lax.fori_loop(0, tm, issue, 0)
    shared = _swiglu_rows(_rows_2d(ff_ref.at[0]).astype(BF16), wgu_ref[...], wdn_ref[...])
    for k in range(TOP_K):
        _row_copy(yb_ref.at[pl.ds(0, tm)], rows_sc.at[k], sem).wait()

    def combine(t, carry):
        acc = w_ref[0, t] * rows_sc[0, t]
        for k in range(1, TOP_K):
            acc = acc + w_ref[k, t] * rows_sc[k, t]
        sum_sc[t] = acc
        return carry

    lax.fori_loop(0, tm, combine, 0)
    out = _rows_2d(sum_sc) + shared
    xo_ref[0] = _layer_norm(DN_ALPHA * x_ref[0] + gate_ref[0] * out, g_ref[0], b_ref[0])


def _ffn_out(x, ff3, yb, pos, w, tile0, sh_gu, sh_down, gate, ln_g, ln_b):
    bsz, n, d = x.shape
    dc = d // LANES
    tm = min(ROUTE_TOKENS, n)
    nt = n // tm
    row = pl.BlockSpec((1, tm, d), lambda b, i: (b, i, 0))
    col = lambda b, i: (0, tile0 + b * nt + i)
    vecs = [gate, ln_g, ln_b]
    in_specs = [pl.BlockSpec((TOP_K, tm), col, memory_space=pltpu.SMEM),
                pl.BlockSpec((TOP_K, tm), col, memory_space=pltpu.SMEM),
                row, pl.BlockSpec((1, tm, dc, LANES), lambda b, i: (b, i, 0, 0)),
                pl.BlockSpec(memory_space=pl.ANY),
                pl.BlockSpec(sh_gu.shape, lambda b, i: (0, 0)), pl.BlockSpec(sh_down.shape, lambda b, i: (0, 0))]
    in_specs += [_bcast_spec(a) for a in vecs]
    vmem = (TOP_K + 1) * tm * d * 4 + 2 * (3 * tm * d * 4 + sh_gu.size * 2 + sh_down.size * 2) + 4 * tm * d * 4
    return pl.pallas_call(
        _ffn_out_kernel, grid=(bsz, nt), in_specs=in_specs, out_specs=row,
        out_shape=jax.ShapeDtypeStruct((bsz, n, d), F32),
        scratch_shapes=[pltpu.VMEM((TOP_K, tm, dc, LANES), F32), pltpu.VMEM((tm, dc, LANES), F32),
                        pltpu.SemaphoreType.DMA(())],
        compiler_params=_params(("arbitrary", "arbitrary"), vmem), name="moe_combine_ffn_out",
    )(pos, w, x, ff3, yb, sh_gu, sh_down, *vecs)


def _moe_dispatch_experts(tokens3, router_w, router_bias, w_gu, w_down):
    t = tokens3.shape[0]
    idx, w, rank, counts = _router(tokens3, router_w, router_bias)
    pad_start, block_e, n_valid, cap = _moe_plan(counts[:, 0], t)
    pos = _positions(idx, rank, pad_start)
    xb = _dispatch(pos, tokens3, cap)
    return _moe_experts(xb, block_e, n_valid, w_gu, w_down), pos, w


def _rope_tables(n):
    rows = n // GRID_W
    row = jnp.repeat(jnp.arange(rows), GRID_W).astype(F32)
    col = jnp.tile(jnp.arange(GRID_W), rows).astype(F32)
    half = MLA_ROPE // 2
    inv = ROPE_THETA ** (-jnp.arange(0, half, 2, dtype=F32) / half)
    ang = jnp.concatenate([row[:, None] * inv, col[:, None] * inv], -1)
    return jnp.cos(ang), jnp.sin(ang)


def _rope(x, cos, sin):
    x1, x2 = x[..., 0::2], x[..., 1::2]
    return jnp.stack([x1 * cos - x2 * sin, x1 * sin + x2 * cos], -1).reshape(x.shape)


def _dwconv(x, w, b):
    pad = w.shape[0] // 2
    y = lax.conv_general_dilated(x, w[:, None, :], window_strides=(1,), padding=[(pad, pad)],
                                 dimension_numbers=('NWC', 'WIO', 'NWC'), feature_group_count=x.shape[-1])
    return y + b


def _hyena_filters(n, w1, b1, f1, w2, b2, f2, w3):
    t = jnp.linspace(0.0, 1.0, n, dtype=F32)[:, None]
    w = 2.0 * math.pi * jnp.arange(n, dtype=F32)[:, None] / n
    fr = jnp.linspace(1e-4, HY_BANDS - 1, HY_BANDS, dtype=F32)
    z = jnp.concatenate([t, jnp.cos(fr * w), -jnp.sin(fr * w)], -1)
    hp = lax.Precision.HIGHEST
    h = jnp.sin(f1 * (jnp.dot(z, w1, precision=hp) + b1))
    h = jnp.sin(f2 * (jnp.dot(h, w2, precision=hp) + b2))
    h = jnp.dot(h, w3, precision=hp)
    deltas = jnp.abs(jnp.linspace(HY_DECAY_MIN, HY_DECAY_MAX, HY_WIDTH, dtype=F32))
    decay = jnp.exp(-t * deltas)
    h_f = h[:, :HY_WIDTH] * decay
    h_b = h[:, HY_WIDTH:] * decay
    k = jnp.concatenate([h_f, jnp.zeros((1, HY_WIDTH), F32), h_b[:0:-1]], 0)
    return k / jnp.sum(jnp.abs(k), 0, keepdims=True)


def _hyena_sequence(proj, conv_w, conv_b, filt, skip):
    n = proj.shape[1]
    u = _dwconv(proj, conv_w, conv_b)
    x0, x1, v = jnp.split(u, 3, axis=-1)
    z = v * x1
    k = _hyena_filters(n, *filt)
    y = jnp.fft.irfft(jnp.fft.rfft(z, n=2 * n, axis=1) * jnp.fft.rfft(k, n=2 * n, axis=0)[None],
                      n=2 * n, axis=1)[:, :n]
    return (y + z * skip) * x0


def _mixer_hyena_mla(x, ctx, sh_l, sc_l, sh_c, sc_c, p):
    bsz, n, _ = x.shape
    nc = ctx.shape[1]
    w_in = p['w_in'].astype(BF16)
    w_parts = [w_in[:, :OFF_Q], w_in[:, OFF_Q:OFF_KV], w_in[:, OFF_KV:OFF_KPE], w_in[:, OFF_KPE:]]
    w_qb = p['w_qb'].astype(BF16)
    w_kvb = p['w_kvb'].astype(BF16)
    qg = p['q_norm'].reshape(1, 1, -1)
    kvg = p['kv_norm'].reshape(1, 1, -1)

    def project(h, shift, scale):
        hy, ql, kvl, kpe = _linear(h, w_parts, pre="mod", pre_args=(shift, scale), name="in_proj")
        q, = _linear(ql, [w_qb], pre="rms", pre_args=(qg,), name="q_proj")
        kv, = _linear(kvl, [w_kvb], pre="rms", pre_args=(kvg,), name="kv_proj")
        m = h.shape[1]
        q = q.reshape(bsz, m, MLA_HEADS, MLA_QK).transpose(0, 2, 1, 3)
        kv = kv.reshape(bsz, m, MLA_HEADS, MLA_NOPE + MLA_V).transpose(0, 2, 1, 3)
        return hy, q[..., :MLA_NOPE], q[..., MLA_NOPE:], kv[..., :MLA_NOPE], kv[..., MLA_NOPE:], kpe

    hy_l, qn_l, qp_l, kn_l, v_l, kp_l = project(x, sh_l, sc_l)
    hy_c, qn_c, qp_c, kn_c, v_c, kp_c = project(ctx, sh_c, sc_c)
    cos, sin = _rope_tables(n)
    qp_l = _rope(qp_l, cos, sin)
    kp_l = _rope(kp_l, cos, sin)

    def heads(kp):
        return jnp.broadcast_to(kp[:, None], (bsz, MLA_HEADS) + kp.shape[1:])

    def values_t(v):
        ones = jnp.ones((bsz, MLA_HEADS, SUBLANES, v.shape[2]), F32)
        return jnp.concatenate([v.transpose(0, 1, 3, 2), ones], 2).astype(BF16)

    q_l = (jnp.concatenate([qn_l, qp_l], -1) * (MLA_SCALE * LOG2E)).astype(BF16)
    q_c = (jnp.concatenate([qn_c, qp_c], -1) * (MLA_SCALE * LOG2E)).astype(BF16)
    k_c = jnp.concatenate([kn_c, heads(kp_c)], -1).astype(BF16)
    k_l = jnp.concatenate([kn_l, heads(kp_l)], -1).astype(BF16)
    k_all = jnp.concatenate([k_c, k_l], 2)
    v_all = jnp.concatenate([v_c, v_l], 2)
    att_l = _attention(q_l, k_all, values_t(v_all), tq=min(ATT_Q_TILE, n), tk=_key_tile(nc + n))
    att_c = _attention(q_c, k_c, values_t(v_c), tq=nc, tk=_key_tile(nc))

    filt = (p['filt_w1'], p['filt_b1'], p['filt_freq1'], p['filt_w2'], p['filt_b2'], p['filt_freq2'], p['filt_w3'])
    hyo_l = _hyena_sequence(hy_l, p['conv_w'], p['conv_b'], filt, p['skip'])
    hyo_c = _hyena_sequence(hy_c, p['conv_w'], p['conv_b'], filt, p['skip'])
    return (hyo_l, att_l), (hyo_c, att_c)


def _halo_specs(n, tm, k):
    nb = n // SUBLANES
    per = tm // SUBLANES
    return [pl.BlockSpec((1, tm, k), lambda b, i: (b, i, 0)),
            pl.BlockSpec((1, SUBLANES, k), lambda b, i: (b, jnp.maximum(i * per - 1, 0), 0)),
            pl.BlockSpec((1, SUBLANES, k), lambda b, i: (b, jnp.minimum((i + 1) * per, nb - 1), 0))]


def _conv3(u, cw_ref, cb_ref, tm):
    i = pl.program_id(1)
    rows = lax.broadcasted_iota(I32, (u.shape[0], 1), 0)
    inside = jnp.logical_and(jnp.logical_or(rows >= SUBLANES, i > 0),
                             jnp.logical_or(rows < tm + SUBLANES, i < pl.num_programs(1) - 1))
    u = jnp.where(inside, u, 0.0)
    prev = pltpu.roll(u, 1, 0)[SUBLANES:SUBLANES + tm]
    nxt = pltpu.roll(u, u.shape[0] - 1, 0)[SUBLANES:SUBLANES + tm]
    return cw_ref[0:1, :] * prev + cw_ref[1:2, :] * u[SUBLANES:SUBLANES + tm] + cw_ref[2:3, :] * nxt + cb_ref[...]


def _ssd_in_kernel(xc_ref, xp_ref, xn_ref, sh_ref, sc_ref, wz_ref, wx_ref, wdt_ref, cw_ref, cb_ref, dtb_ref,
                   z_ref, xs_ref, b_ref, c_ref, dt_ref):
    tm = xc_ref.shape[1]
    rows = jnp.concatenate([xp_ref[0], xc_ref[0], xn_ref[0]], axis=0)
    h = (rows * (1.0 + sc_ref[0]) + sh_ref[0]).astype(BF16)
    hc = h[SUBLANES:SUBLANES + tm]
    z_ref[0] = jnp.dot(hc, wz_ref[...], preferred_element_type=F32)
    y = _conv3(jnp.dot(h, wx_ref[...], preferred_element_type=F32), cw_ref, cb_ref, tm)
    xbc = y * jax.nn.sigmoid(y)
    xs_ref[0] = xbc[:, :SSD_INNER]
    b_ref[0] = xbc[:, SSD_INNER:SSD_INNER + SSD_BC]
    c_ref[0] = xbc[:, SSD_INNER + SSD_BC:]
    dt = jnp.dot(hc, wdt_ref[...], preferred_element_type=F32) + dtb_ref[...]
    dt_ref[0] = (jnp.maximum(dt, 0.0) + jnp.log1p(jnp.exp(-jnp.abs(dt)))).T


def _ssd_in_proj(x, shift, scale, w_z, w_xbc, w_dt, conv_w, conv_b, dt_bias):
    bsz, n, k = x.shape
    tm = min(ROW_TILE, n)
    assert n % tm == 0
    full = lambda a: pl.BlockSpec(a.shape, lambda b, i: (0,) * a.ndim)
    row = lambda c: pl.BlockSpec((1, tm, c), lambda b, i: (b, i, 0))
    nh2 = w_dt.shape[1]
    consts = [w_z, w_xbc, w_dt, conv_w, conv_b, dt_bias]
    widths = [SSD_INNER, SSD_INNER, SSD_BC, SSD_BC]
    vmem = 2 * (tm * k * 4 + sum(a.size * a.dtype.itemsize for a in consts) + tm * (sum(widths) + nh2) * 4) \
        + 6 * (tm + 2 * SUBLANES) * SSD_XBC * 4
    return pl.pallas_call(
        _ssd_in_kernel, grid=(bsz, n // tm),
        in_specs=_halo_specs(n, tm, k) + [_bcast_spec(shift), _bcast_spec(scale)] + [full(a) for a in consts],
        out_specs=[row(c) for c in widths] + [pl.BlockSpec((1, nh2, tm), lambda b, i: (b, 0, i))],
        out_shape=[jax.ShapeDtypeStruct((bsz, n, c), F32) for c in widths]
        + [jax.ShapeDtypeStruct((bsz, nh2, n), F32)],
        compiler_params=_params(("parallel", "parallel"), vmem), name="ssd_in_proj",
    )(x, x, x, shift, scale, *consts)


def _ssd_scan_kernel(x_ref, b_ref, c_ref, dt_ref, a_ref, s0_ref, y_ref, sfin_ref, st_sc, *, reverse):
    ci = pl.program_id(2)
    q = SSD_CHUNK

    @pl.when(ci == 0)
    def _():
        st_sc[...] = s0_ref[0, 0]

    dt = dt_ref[0]
    a = dt * a_ref[0]
    si = lax.broadcasted_iota(I32, (q, q), 0)
    li = lax.broadcasted_iota(I32, (q, q), 1)
    incl = jnp.where((si >= li) if reverse else (si <= li), 1.0, 0.0)
    hp = lax.Precision.HIGHEST
    acs = jnp.dot(a, incl, precision=hp, preferred_element_type=F32)
    tot = jnp.dot(a, jnp.ones((q, LANES), F32), precision=hp, preferred_element_type=F32)
    e_in = jnp.exp(acs)
    w_end = jnp.exp(tot - acs) * dt
    e_tot = jnp.exp(tot)
    acs_t = acs.T
    e_in_t = e_in.T
    mask = (li >= si) if reverse else (li <= si)
    cmat = c_ref[0]
    cb = lax.dot_general(cmat.astype(BF16), b_ref[0].astype(BF16), (((1,), (1,)), ((), ())),
                         preferred_element_type=F32)
    bt = b_ref[0].T
    first = lax.broadcasted_iota(I32, (q, LANES), 1) < SSD_HEADDIM
    ys = []
    for pr in range(SSD_HPG // 2):
        lo, hi = pr * LANES, (pr + 1) * LANES
        x_pair = x_ref[0, :, lo:hi].astype(BF16)
        st_pair = st_sc[:, lo:hi]
        rhs = jnp.concatenate([x_pair, st_pair.astype(BF16)], axis=0)
        y2, s2 = [], []
        for r in (2 * pr, 2 * pr + 1):
            seg = acs_t[:, r:r + 1] - acs[r:r + 1, :]
            m = cb * jnp.exp(jnp.where(mask, seg, -jnp.inf)) * dt[r:r + 1, :]
            lhs = jnp.concatenate([m.astype(BF16), (cmat * e_in_t[:, r:r + 1]).astype(BF16)], axis=1)
            y2.append(jnp.dot(lhs, rhs, preferred_element_type=F32))
            btr = (bt * w_end[r:r + 1, :]).astype(BF16)
            s2.append(e_tot[r:r + 1, :] * st_pair + jnp.dot(btr, x_pair, preferred_element_type=F32))
        ys.append(jnp.where(first, y2[0], y2[1]))
        st_sc[:, lo:hi] = jnp.where(first, s2[0], s2[1])
    y_ref[0] = jnp.concatenate(ys, axis=1)

    @pl.when(ci == pl.num_programs(2) - 1)
    def _():
        sfin_ref[0, 0] = st_sc[...]


def _ssd_scan(xs, bm, cm, dt_t, a_rep, s0, reverse):
    bsz, n, _ = xs.shape
    q = SSD_CHUNK
    nc = n // q
    assert n % q == 0 and q == LANES
    d = 1 if reverse else 0
    cc = (lambda c: nc - 1 - c) if reverse else (lambda c: c)
    gw = SSD_HPG * SSD_HEADDIM
    state = pl.BlockSpec((1, 1, SSD_STATE, gw), lambda b, g, c: (b, g, 0, 0))
    vmem = 2 * (2 * q * gw * 4 + 2 * q * SSD_STATE * 4 + 2 * SSD_STATE * gw * 4) + SSD_STATE * gw * 4 + 64 * q * q * 4
    return pl.pallas_call(
        functools.partial(_ssd_scan_kernel, reverse=reverse), grid=(bsz, SSD_GROUPS, nc),
        in_specs=[pl.BlockSpec((1, q, gw), lambda b, g, c: (b, cc(c), g)),
                  pl.BlockSpec((1, q, SSD_STATE), lambda b, g, c: (b, cc(c), g)),
                  pl.BlockSpec((1, q, SSD_STATE), lambda b, g, c: (b, cc(c), g)),
                  pl.BlockSpec((1, SSD_HPG, q), lambda b, g, c: (b, d * SSD_GROUPS + g, cc(c))),
                  pl.BlockSpec((1, SSD_HPG, LANES), lambda b, g, c: (d * SSD_GROUPS + g, 0, 0)),
                  state],
        out_specs=[pl.BlockSpec((1, q, gw), lambda b, g, c: (b, cc(c), g)), state],
        out_shape=[jax.ShapeDtypeStruct((bsz, n, SSD_INNER), F32),
                   jax.ShapeDtypeStruct((bsz, SSD_GROUPS, SSD_STATE, gw), F32)],
        scratch_shapes=[pltpu.VMEM((SSD_STATE, gw), F32)],
        compiler_params=_params(("parallel", "parallel", "arbitrary"), vmem),
        name="ssd_scan_rev" if reverse else "ssd_scan_fwd",
    )(xs, bm, cm, dt_t, a_rep, s0)


def _ssd_out_kernel(x_ref, yf_ref, yb_ref, xs_ref, z_ref, d_ref, ng_ref, w_ref, gate_ref, g_ref, b_ref, sh_ref,
                    sc_ref, xo_ref, ff_ref):
    z = z_ref[0]
    y = (yf_ref[0] + yb_ref[0] + xs_ref[0] * d_ref[0]) * (z * jax.nn.sigmoid(z))
    gw = SSD_INNER // SSD_GROUPS
    parts = []
    for g in range(SSD_GROUPS):
        yg = y[:, g * gw:(g + 1) * gw]
        parts.append(yg * lax.rsqrt(jnp.mean(yg * yg, -1, keepdims=True) + RMS_EPS))
    yn = (jnp.concatenate(parts, axis=1) * ng_ref[0]).astype(BF16)
    mix = jnp.dot(yn, w_ref[...], preferred_element_type=F32)
    xn = _layer_norm(DN_ALPHA * x_ref[0] + gate_ref[0] * mix, g_ref[0], b_ref[0])
    xo_ref[0] = xn
    _store_rows(ff_ref.at[0], xn * (1.0 + sc_ref[0]) + sh_ref[0])


def _ssd_out(x, y_f, y_b, xs, z, d_rep, norm_g, w_out, gate, ln_g, ln_b, shift2, scale2):
    bsz, n, d = x.shape
    tm = min(ROW_TILE, n)
    assert n % tm == 0
    row = lambda c: pl.BlockSpec((1, tm, c), lambda b, i: (b, i, 0))
    vecs = [gate, ln_g, ln_b, shift2, scale2]
    in_specs = [row(d)] + [row(SSD_INNER)] * 4 + [_bcast_spec(d_rep), _bcast_spec(norm_g),
                                                 pl.BlockSpec(w_out.shape, lambda b, i: (0, 0))]
    in_specs += [_bcast_spec(a) for a in vecs]
    out_specs = [row(d), pl.BlockSpec((1, tm, d // LANES, LANES), lambda b, i: (b, i, 0, 0))]
    vmem = 2 * (3 * tm * d * 4 + 4 * tm * SSD_INNER * 4 + w_out.size * 2) + 4 * tm * SSD_INNER * 4
    return pl.pallas_call(
        _ssd_out_kernel, grid=(bsz, n // tm), in_specs=in_specs, out_specs=out_specs,
        out_shape=[jax.ShapeDtypeStruct((bsz, n, d), F32), jax.ShapeDtypeStruct((bsz, n, d // LANES, LANES), F32)],
        compiler_params=_params(("parallel", "parallel"), vmem), name="ssd_out",
    )(x, y_f, y_b, xs, z, d_rep, norm_g, w_out, *vecs)


def _mixer_ssd(x, ctx, sh_l, sc_l, sh_c, sc_c, p):
    a_all = -jnp.exp(jnp.concatenate([p['a_log_f'], p['a_log_b']]))
    a_rep = jnp.broadcast_to(a_all.reshape(2 * SSD_GROUPS, SSD_HPG, 1), (2 * SSD_GROUPS, SSD_HPG, LANES))
    w_in = p['w_in'].astype(BF16)
    consts = (w_in[:, :SSD_INNER], w_in[:, SSD_INNER:SSD_INNER + SSD_XBC], w_in[:, SSD_INNER + SSD_XBC:],
              p['conv_w'], p['conv_b'][None], jnp.concatenate([p['dt_bias_f'], p['dt_bias_b']])[None])
    _, xc, bc, cc, dtc = _ssd_in_proj(ctx, sh_c, sc_c, *consts)
    zl, xl, bl, cl, dtl = _ssd_in_proj(x, sh_l, sc_l, *consts)
    s0 = jnp.zeros((ctx.shape[0], SSD_GROUPS, SSD_STATE, SSD_HPG * SSD_HEADDIM), F32)
    _, sc_f = _ssd_scan(xc, bc, cc, dtc, a_rep, s0, reverse=False)
    _, sc_b = _ssd_scan(xc, bc, cc, dtc, a_rep, s0, reverse=True)
    y_f, _ = _ssd_scan(xl, bl, cl, dtl, a_rep, sc_f, reverse=False)
    y_b, _ = _ssd_scan(xl, bl, cl, dtl, a_rep, sc_b, reverse=True)
    return y_f, y_b, xl, zl


def kernel(x, c, ctx, c_ctx, mod_w, mod_b, ln_mix_g, ln_mix_b, ln_ffn_g, ln_ffn_b, a_w_in, hy_conv_w, hy_conv_b, hy_filt_w1, hy_filt_b1, hy_filt_freq1, hy_filt_w2, hy_filt_b2, hy_filt_freq2, hy_filt_w3, hy_skip, mla_q_norm, mla_w_qb, mla_kv_norm, mla_w_kvb, a_w_out, ssd_w_in, ssd_conv_w, ssd_conv_b, ssd_dt_bias_f, ssd_dt_bias_b, ssd_a_log_f, ssd_a_log_b, ssd_d, ssd_norm_g, ssd_w_out, router_w, router_bias, exp_w_gu, exp_w_down, sh_w_gu, sh_w_down):
    bsz, n_lat, d = x.shape
    n_ctx = ctx.shape[1]
    dc = d // LANES
    hp = lax.Precision.HIGHEST
    for l in range(DEPTH):
        last = l == DEPTH - 1
        i = l // 2
        mod = (jnp.dot(jax.nn.silu(c), mod_w[l], precision=hp) + mod_b[l]).reshape(bsz, N_MOD, 1, d)
        mod_c = (jnp.dot(jax.nn.silu(c_ctx), mod_w[l], precision=hp) + mod_b[l]).reshape(1, N_MOD, 1, d)
        sh1, sc1, g1, sh2, sc2, g2 = [mod[:, j] for j in range(N_MOD)]
        csh1, csc1, cg1, csh2, csc2, cg2 = [mod_c[:, j] for j in range(N_MOD)]
        vec = lambda a: a.reshape(1, 1, d)
        if l % 2 == 0:
            p = {"w_in": a_w_in[i], "conv_w": hy_conv_w[i], "conv_b": hy_conv_b[i],
                 "filt_w1": hy_filt_w1[i], "filt_b1": hy_filt_b1[i], "filt_freq1": hy_filt_freq1[i],
                 "filt_w2": hy_filt_w2[i], "filt_b2": hy_filt_b2[i], "filt_freq2": hy_filt_freq2[i],
                 "filt_w3": hy_filt_w3[i], "skip": hy_skip[i], "q_norm": mla_q_norm[i], "w_qb": mla_w_qb[i],
                 "kv_norm": mla_kv_norm[i], "w_kvb": mla_w_kvb[i]}
            ys_l, ys_c = _mixer_hyena_mla(x, ctx, sh1, sc1, csh1, csc1, p)
            w_out = a_w_out[i].astype(BF16)
            ws = [w_out[:HY_WIDTH], w_out[HY_WIDTH:]]
            x, ff_x = _mix_out(x, ys_l, ws, g1, vec(ln_mix_g[l]), vec(ln_mix_b[l]), sh2, sc2)
        else:
            p = {"w_in": ssd_w_in[i], "conv_w": ssd_conv_w[i], "conv_b": ssd_conv_b[i],
                 "dt_bias_f": ssd_dt_bias_f[i], "dt_bias_b": ssd_dt_bias_b[i],
                 "a_log_f": ssd_a_log_f[i], "a_log_b": ssd_a_log_b[i]}
            assert last
            y_f, y_b, xs, z = _mixer_ssd(x, ctx, sh1, sc1, csh1, csc1, p)
            d_rep = jnp.repeat(ssd_d[i], SSD_HEADDIM).reshape(1, 1, SSD_INNER)
            x, ff_x = _ssd_out(x, y_f, y_b, xs, z, d_rep, ssd_norm_g[i].reshape(1, 1, SSD_INNER),
                               ssd_w_out[i].astype(BF16), g1, vec(ln_mix_g[l]), vec(ln_mix_b[l]), sh2, sc2)
        sh_gu = sh_w_gu[l].astype(BF16)
        sh_down = sh_w_down[l].astype(BF16)
        ln_g, ln_b = vec(ln_ffn_g[l]), vec(ln_ffn_b[l])
        moe_w = (router_w[l], router_bias[l], exp_w_gu[l], exp_w_down[l])
        if last:
            yb, pos, w = _moe_dispatch_experts(ff_x.reshape(-1, dc, LANES), *moe_w)
            x = _ffn_out(x, ff_x, yb, pos, w, 0, sh_gu, sh_down, g2, ln_g, ln_b)
        else:
            ctx, ff_c = _mix_out(ctx, ys_c, ws, cg1, vec(ln_mix_g[l]), vec(ln_mix_b[l]), csh2, csc2)
            tokens = jnp.concatenate([ff_c.reshape(-1, dc, LANES), ff_x.reshape(-1, dc, LANES)], 0)
            yb, pos, w = _moe_dispatch_experts(tokens, *moe_w)
            assert (bsz * n_ctx) % ROUTE_TOKENS == 0
            ctx = _ffn_out(ctx, ff_c, yb, pos, w, 0, sh_gu, sh_down, cg2, ln_g, ln_b)
            x = _ffn_out(x, ff_x, yb, pos, w, bsz * n_ctx // ROUTE_TOKENS, sh_gu, sh_down, g2, ln_g, ln_b)
    return x
```

```python
import functools
import math

import jax
import jax.numpy as jnp
from jax import lax
from jax.experimental import pallas as pl
from jax.experimental.pallas import tpu as pltpu

F32 = jnp.float32
BF16 = jnp.bfloat16
I32 = jnp.int32

D_MODEL = 1024
DEPTH = 2
GRID_W = 64
N_MOD = 6

HY_WIDTH = 512
HY_EMB = 33
HY_BANDS = (HY_EMB - 1) // 2
HY_TARGET = 1e-2
HY_FAST_DECAY = 0.3
HY_SLOW_DECAY = 1.5
HY_DECAY_MIN = math.log(HY_TARGET) / HY_SLOW_DECAY
HY_DECAY_MAX = math.log(HY_TARGET) / HY_FAST_DECAY

MLA_HEADS = 8
MLA_NOPE = 64
MLA_ROPE = 32
MLA_V = 64
MLA_Q_RANK = 256
MLA_KV_RANK = 128
MLA_QK = MLA_NOPE + MLA_ROPE
MLA_SCALE = MLA_QK ** -0.5
ROPE_THETA = 10000.0
LOG2E = math.log2(math.e)

OFF_Q = 3 * HY_WIDTH
OFF_KV = OFF_Q + MLA_Q_RANK
OFF_KPE = OFF_KV + MLA_KV_RANK

SSD_INNER = 2 * D_MODEL
SSD_HEADDIM = 64
SSD_HEADS = SSD_INNER // SSD_HEADDIM
SSD_GROUPS = 4
SSD_STATE = 128
SSD_CHUNK = 128
SSD_BC = SSD_GROUPS * SSD_STATE
SSD_XBC = SSD_INNER + 2 * SSD_BC
SSD_HPG = SSD_HEADS // SSD_GROUPS

N_EXPERTS = 256
TOP_K = 8
N_EXPERT_GROUPS = 8
TOPK_GROUPS = 4
EXPERT_DIM = 256
ROUTED_SCALE = 2.5

DN_ALPHA = (2 * DEPTH) ** 0.25
LN_EPS = 1e-5
RMS_EPS = 1e-6

LANES = 128
SUBLANES = 8
V7X_VMEM_CAP = 56 * 1024 * 1024

MOE_ROWS = 256
ROUTE_TOKENS = 256
ROW_TILE = 256
ATT_Q_TILE = 512
ATT_HEADS_PER_STEP = 2
DFT_INNER = 128
DFT_COLS = 2048
HY_DIRECT_MAX = 512


def _params(semantics, vmem_bytes, **kw):
    limit = int(min(max(vmem_bytes * 5 // 4, 32 * 1024 * 1024), V7X_VMEM_CAP))
    return pltpu.CompilerParams(dimension_semantics=semantics, vmem_limit_bytes=limit, **kw)


def _bcast_spec(a):
    if a.shape[0] == 1:
        return pl.BlockSpec((1, 1, a.shape[2]), lambda b, i: (0, 0, 0))
    return pl.BlockSpec((1, 1, a.shape[2]), lambda b, i: (b, 0, 0))


def _rows_2d(ref3):
    return jnp.concatenate([ref3[:, c, :] for c in range(ref3.shape[1])], axis=1)


def _store_rows(ref3, val):
    for c in range(ref3.shape[1]):
        ref3[:, c, :] = val[:, c * LANES:(c + 1) * LANES]


def _layer_norm(r, g, b):
    mu = jnp.mean(r, -1, keepdims=True)
    c = r - mu
    var = jnp.mean(c * c, -1, keepdims=True)
    return c * lax.rsqrt(var + LN_EPS) * g + b


def _swiglu_rows(xb, w_gu, w_down):
    h = jnp.dot(xb, w_gu, preferred_element_type=F32)
    half = h.shape[1] // 2
    g, u = h[:, :half], h[:, half:]
    a = (g * jax.nn.sigmoid(g) * u).astype(BF16)
    return jnp.dot(a, w_down, preferred_element_type=F32)


def _linear_kernel(*refs, pre, n_pre, n_w):
    x = refs[0][0]
    pre_refs = refs[1:1 + n_pre]
    w_refs = refs[1 + n_pre:1 + n_pre + n_w]
    o_refs = refs[1 + n_pre + n_w:]
    if pre == "mod":
        h = x * (1.0 + pre_refs[1][0]) + pre_refs[0][0]
    elif pre == "rms":
        h = x * lax.rsqrt(jnp.mean(x * x, -1, keepdims=True) + RMS_EPS) * pre_refs[0][0]
    else:
        h = x
    hb = h.astype(BF16)
    for w_ref, o_ref in zip(w_refs, o_refs):
        o_ref[0] = jnp.dot(hb, w_ref[...], preferred_element_type=F32).astype(o_ref.dtype)


def _linear(x, ws, pre=None, pre_args=(), name="linear"):
    bsz, n, k = x.shape
    tm = min(ROW_TILE, n)
    assert n % tm == 0
    in_specs = [pl.BlockSpec((1, tm, k), lambda b, i: (b, i, 0))]
    in_specs += [_bcast_spec(a) for a in pre_args]
    in_specs += [pl.BlockSpec(w.shape, lambda b, i: (0, 0)) for w in ws]
    out_specs = [pl.BlockSpec((1, tm, w.shape[1]), lambda b, i: (b, i, 0)) for w in ws]
    out_shape = [jax.ShapeDtypeStruct((bsz, n, w.shape[1]), F32) for w in ws]
    vmem = 2 * (tm * k * 4 + sum(w.size * 2 + tm * w.shape[1] * 4 for w in ws))
    return pl.pallas_call(
        functools.partial(_linear_kernel, pre=pre, n_pre=len(pre_args), n_w=len(ws)),
        grid=(bsz, n // tm), in_specs=in_specs, out_specs=out_specs, out_shape=out_shape,
        compiler_params=_params(("parallel", "parallel"), vmem), name=name,
    )(x, *pre_args, *ws)


def _mix_out_kernel(*refs, n_y):
    x_ref = refs[0]
    y_refs = refs[1:1 + n_y]
    w_refs = refs[1 + n_y:1 + 2 * n_y]
    gate_ref, g_ref, b_ref, sh_ref, sc_ref, xo_ref, ff_ref = refs[1 + 2 * n_y:]
    y = None
    for y_ref, w_ref in zip(y_refs, w_refs):
        t = jnp.dot(y_ref[0].astype(BF16), w_ref[...], preferred_element_type=F32)
        y = t if y is None else y + t
    xn = _layer_norm(DN_ALPHA * x_ref[0] + gate_ref[0] * y, g_ref[0], b_ref[0])
    xo_ref[0] = xn
    _store_rows(ff_ref.at[0], xn * (1.0 + sc_ref[0]) + sh_ref[0])


def _mix_out(x, ys, ws, gate, ln_g, ln_b, shift2, scale2):
    bsz, n, d = x.shape
    tm = min(ROW_TILE, n)
    assert n % tm == 0
    row = lambda c: pl.BlockSpec((1, tm, c), lambda b, i: (b, i, 0))
    vecs = [gate, ln_g, ln_b, shift2, scale2]
    in_specs = [row(d)] + [row(y.shape[2]) for y in ys]
    in_specs += [pl.BlockSpec(w.shape, lambda b, i: (0, 0)) for w in ws]
    in_specs += [_bcast_spec(a) for a in vecs]
    out_specs = [row(d), pl.BlockSpec((1, tm, d // LANES, LANES), lambda b, i: (b, i, 0, 0))]
    vmem = 2 * (3 * tm * d * 4 + sum(tm * y.shape[2] * 4 + w.size * 2 for y, w in zip(ys, ws)))
    return pl.pallas_call(
        functools.partial(_mix_out_kernel, n_y=len(ys)),
        grid=(bsz, n // tm), in_specs=in_specs, out_specs=out_specs,
        out_shape=[jax.ShapeDtypeStruct((bsz, n, d), F32), jax.ShapeDtypeStruct((bsz, n, d // LANES, LANES), F32)],
        compiler_params=_params(("parallel", "parallel"), vmem), name="mix_out",
    )(x, *ys, *ws, *vecs)


def _attn_kernel(q_ref, k_ref, vt_ref, o_ref, *, tk, hp):
    nk = k_ref.shape[2]
    tq = q_ref.shape[2]
    dva = vt_ref.shape[2]
    dv = dva - SUBLANES

    def body(j, carry):
        off = pl.multiple_of(j * tk, tk)
        new = []
        for h in range(hp):
            m_prev, acc = carry[h]
            st = lax.dot_general(k_ref[0, h, pl.ds(off, tk), :], q_ref[0, h], (((1,), (1,)), ((), ())),
                                 preferred_element_type=F32)
            m_new = jnp.maximum(m_prev, jnp.max(st, 0, keepdims=True))
            p = jnp.exp2(st - m_new).astype(BF16)
            alpha = jnp.exp2(m_prev - m_new)
            acc = alpha * acc + jnp.dot(vt_ref[0, h, :, pl.ds(off, tk)], p, preferred_element_type=F32)
            new.append((m_new, acc))
        return tuple(new)

    init = tuple((jnp.full((1, tq), -jnp.inf, F32), jnp.zeros((dva, tq), F32)) for _ in range(hp))
    fin = lax.fori_loop(0, nk // tk, body, init)
    outs = [acc[:dv] / acc[dv:dv + 1] for _, acc in fin]
    o_ref[0] = jnp.concatenate(outs, 0).T


def _attention(q, k, vt, tq, tk):
    bsz, h, nq, dk = q.shape
    nk, dva = k.shape[2], vt.shape[2]
    dv = dva - SUBLANES
    hp = ATT_HEADS_PER_STEP
    assert nq % tq == 0 and nk % tk == 0 and h % hp == 0
    vmem = 2 * hp * (tq * LANES * 2 + nk * LANES * 2 + dva * nk * 2) + 2 * tq * hp * dv * 4 + 6 * hp * tk * tq * 4
    return pl.pallas_call(
        functools.partial(_attn_kernel, tk=tk, hp=hp), grid=(bsz, h // hp, nq // tq),
        in_specs=[pl.BlockSpec((1, hp, tq, dk), lambda b, g, i: (b, g, i, 0)),
                  pl.BlockSpec((1, hp, nk, dk), lambda b, g, i: (b, g, 0, 0)),
                  pl.BlockSpec((1, hp, dva, nk), lambda b, g, i: (b, g, 0, 0))],
        out_specs=pl.BlockSpec((1, tq, hp * dv), lambda b, g, i: (b, i, g)),
        out_shape=jax.ShapeDtypeStruct((bsz, nq, h * dv), F32),
        compiler_params=_params(("parallel", "parallel", "arbitrary"), vmem), name="mla_attention",
    )(q, k, vt)


def _key_tile(nk):
    for t in (768, 512, 384, 256, 128):
        if nk % t == 0:
            return t
    return nk


def _router_kernel(x_ref, wt_ref, bias_ref, upper_ref, idx_ref, w_ref, rank_ref, cnt_ref, run_sc):
    i = pl.program_id(0)
    tm = x_ref.shape[0]

    @pl.when(i == 0)
    def _():
        run_sc[...] = jnp.zeros(run_sc.shape, F32)

    logits = lax.dot_general(wt_ref[...], _rows_2d(x_ref), (((1,), (1,)), ((), ())),
                             precision=lax.Precision.HIGHEST, preferred_element_type=F32)
    sc = jax.nn.sigmoid(logits)
    ch = sc + bias_ref[:, :1]
    neg = -jnp.inf
    chg = ch.reshape(N_EXPERT_GROUPS, N_EXPERTS // N_EXPERT_GROUPS, tm)
    m1 = jnp.max(chg, axis=1)
    eq = chg == m1[:, None, :]
    cnt = jnp.sum(eq.astype(F32), axis=1)
    m2 = jnp.max(jnp.where(eq, neg, chg), axis=1)
    g2 = m1 + jnp.where(cnt >= 2.0, m1, m2)
    gi = lax.broadcasted_iota(I32, g2.shape, 0)
    beaten = jnp.zeros(g2.shape, F32)
    for g in range(N_EXPERT_GROUPS):
        row = g2[g:g + 1, :]
        beaten = beaten + jnp.where(row > g2, 1.0, jnp.where(row == g2, jnp.where(gi > g, 1.0, 0.0), 0.0))
    keep = beaten < float(TOPK_GROUPS)
    cur = jnp.where(keep[:, None, :], chg, neg).reshape(N_EXPERTS, tm)
    eidx = lax.broadcasted_iota(I32, (N_EXPERTS, tm), 0)
    multi = jnp.zeros((N_EXPERTS, tm), F32)
    hits, idx_rows, w_rows = [], [], []
    for _ in range(TOP_K):
        m = jnp.max(cur, axis=0, keepdims=True)
        sel = jnp.min(jnp.where(cur == m, eidx, N_EXPERTS), axis=0, keepdims=True)
        hit = eidx == sel
        idx_rows.append(sel)
        w_rows.append(jnp.sum(jnp.where(hit, sc, 0.0), axis=0, keepdims=True))
        cur = jnp.where(hit, neg, cur)
        multi = multi + jnp.where(hit, 1.0, 0.0)
        hits.append(hit)
    base = jnp.concatenate([run_sc[...]] * (tm // LANES), axis=1)
    before = jnp.dot(multi.astype(BF16), upper_ref[...], preferred_element_type=F32) + base
    rank_rows = [jnp.sum(jnp.where(hit, before, 0.0), axis=0, keepdims=True) for hit in hits]
    w = jnp.concatenate(w_rows, axis=0)
    idx_ref[...] = jnp.concatenate(idx_rows, axis=0)
    w_ref[...] = w / jnp.sum(w, axis=0, keepdims=True) * ROUTED_SCALE
    rank_ref[...] = jnp.concatenate(rank_rows, axis=0).astype(I32)
    run_sc[...] = run_sc[...] + jnp.dot(multi.astype(BF16), jnp.ones((tm, LANES), BF16), preferred_element_type=F32)
    cnt_ref[...] = run_sc[...]


def _router(tokens3, router_w, router_bias):
    t, dc, _ = tokens3.shape
    tm = ROUTE_TOKENS
    assert t % tm == 0
    wt = router_w.T
    bias = jnp.broadcast_to(router_bias.astype(F32)[:, None], (N_EXPERTS, LANES))
    r = jnp.arange(tm)
    upper = (r[:, None] < r[None, :]).astype(BF16)
    col = pl.BlockSpec((TOP_K, tm), lambda i: (0, i))
    full = lambda a: pl.BlockSpec(a.shape, lambda i: (0,) * a.ndim)
    vmem = 2 * (tm * dc * LANES * 4 + wt.size * 4) + 40 * N_EXPERTS * tm * 4
    return pl.pallas_call(
        _router_kernel, grid=(t // tm,),
        in_specs=[pl.BlockSpec((tm, dc, LANES), lambda i: (i, 0, 0)), full(wt), full(bias), full(upper)],
        out_specs=[col, col, col, pl.BlockSpec((N_EXPERTS, LANES), lambda i: (0, 0))],
        out_shape=[jax.ShapeDtypeStruct((TOP_K, t), I32), jax.ShapeDtypeStruct((TOP_K, t), F32),
                   jax.ShapeDtypeStruct((TOP_K, t), I32), jax.ShapeDtypeStruct((N_EXPERTS, LANES), F32)],
        scratch_shapes=[pltpu.VMEM((N_EXPERTS, LANES), F32)],
        compiler_params=_params(("arbitrary",), vmem), name="moe_router",
    )(tokens3, wt, bias, upper)


def _positions_kernel(idx_ref, rank_ref, start_ref, pos_ref):
    tm = idx_ref.shape[1]
    eidx = lax.broadcasted_iota(I32, (N_EXPERTS, tm), 0)
    start = jnp.concatenate([start_ref[...]] * (tm // LANES), axis=1)
    rows = [jnp.sum(jnp.where(eidx == idx_ref[k:k + 1, :], start, 0), axis=0, keepdims=True) for k in range(TOP_K)]
    pos_ref[...] = jnp.concatenate(rows, axis=0) + rank_ref[...]


def _positions(idx, rank, pad_start):
    t = idx.shape[1]
    tm = ROUTE_TOKENS
    start = jnp.broadcast_to(pad_start.astype(I32)[:, None], (N_EXPERTS, LANES))
    col = pl.BlockSpec((TOP_K, tm), lambda i: (0, i))
    return pl.pallas_call(
        _positions_kernel, grid=(t // tm,),
        in_specs=[col, col, pl.BlockSpec((N_EXPERTS, LANES), lambda i: (0, 0))], out_specs=col,
        out_shape=jax.ShapeDtypeStruct((TOP_K, t), I32),
        compiler_params=_params(("parallel",), 8 * N_EXPERTS * tm * 4), name="moe_positions",
    )(idx, rank, start)


def _moe_plan(counts, t):
    n_blocks = -(-(t * TOP_K) // MOE_ROWS) + N_EXPERTS
    c = counts.astype(I32)
    padded = (c + MOE_ROWS - 1) // MOE_ROWS * MOE_ROWS
    pad_end = jnp.cumsum(padded)
    pad_start = pad_end - padded
    block_e = jnp.minimum(jnp.searchsorted(pad_end, jnp.arange(n_blocks, dtype=I32) * MOE_ROWS, side='right'),
                          N_EXPERTS - 1).astype(I32)
    n_valid = (pad_end[-1:] // MOE_ROWS).astype(I32)
    return pad_start, block_e, n_valid, n_blocks * MOE_ROWS


def _row_copy(src, dst, sem):
    return pltpu.make_async_copy(src, dst, sem)


def _dispatch_kernel(pos_ref, x_ref, buf_in_ref, buf_ref, sem):
    del buf_in_ref
    tm = x_ref.shape[0]

    def issue(t, carry):
        for k in range(TOP_K):
            _row_copy(x_ref.at[t], buf_ref.at[pos_ref[k, t]], sem).start()
        return carry

    lax.fori_loop(0, tm, issue, 0)
    for k in range(TOP_K):
        _row_copy(x_ref, buf_ref.at[pl.ds(0, tm)], sem).wait()


def _dispatch(pos, tokens3, cap):
    t, dc, _ = tokens3.shape
    tm = ROUTE_TOKENS
    zeros = jnp.zeros((cap, dc, LANES), tokens3.dtype)
    return pl.pallas_call(
        _dispatch_kernel, grid=(t // tm,),
        in_specs=[pl.BlockSpec((TOP_K, tm), lambda i: (0, i), memory_space=pltpu.SMEM),
                  pl.BlockSpec((tm, dc, LANES), lambda i: (i, 0, 0)),
                  pl.BlockSpec(memory_space=pl.ANY)],
        out_specs=pl.BlockSpec(memory_space=pl.ANY),
        out_shape=jax.ShapeDtypeStruct((cap, dc, LANES), tokens3.dtype),
        scratch_shapes=[pltpu.SemaphoreType.DMA(())],
        input_output_aliases={2: 0},
        compiler_params=_params(("arbitrary",), 2 * tm * dc * LANES * 4, has_side_effects=True),
        name="moe_dispatch",
    )(pos, tokens3, zeros)


def _experts_kernel(be_ref, nv_ref, x_ref, wgu_ref, wdn_ref, o_ref, wgu_sc, wdn_sc):
    i = pl.program_id(0)

    @pl.when(i < nv_ref[0])
    def _():
        @pl.when(jnp.logical_or(i == 0, be_ref[i] != be_ref[jnp.maximum(i - 1, 0)]))
        def _():
            wgu_sc[...] = wgu_ref[0].astype(BF16)
            wdn_sc[...] = wdn_ref[0].astype(BF16)

        _store_rows(o_ref, _swiglu_rows(_rows_2d(x_ref).astype(BF16), wgu_sc[...], wdn_sc[...]))

    @pl.when(i >= nv_ref[0])
    def _():
        o_ref[...] = jnp.zeros(o_ref.shape, F32)


def _moe_experts(xb, block_e, n_valid, w_gu, w_down):
    cap, dc, _ = xb.shape
    d = dc * LANES
    gu = w_gu.shape[2]
    ed = w_down.shape[1]
    live = lambda i, be, nv: jnp.maximum(jnp.minimum(i, nv[0] - 1), 0)
    grid_spec = pltpu.PrefetchScalarGridSpec(
        num_scalar_prefetch=2, grid=(cap // MOE_ROWS,),
        in_specs=[pl.BlockSpec((MOE_ROWS, dc, LANES), lambda i, be, nv: (live(i, be, nv), 0, 0)),
                  pl.BlockSpec((1, d, gu), lambda i, be, nv: (be[i], 0, 0)),
                  pl.BlockSpec((1, ed, d), lambda i, be, nv: (be[i], 0, 0))],
        out_specs=pl.BlockSpec((MOE_ROWS, dc, LANES), lambda i, be, nv: (i, 0, 0)),
        scratch_shapes=[pltpu.VMEM((d, gu), BF16), pltpu.VMEM((ed, d), BF16)])
    vmem = 2 * (2 * MOE_ROWS * d * 4 + d * gu * 4 + ed * d * 4) + (d * gu + ed * d) * 2 + 4 * MOE_ROWS * d * 4
    return pl.pallas_call(
        _experts_kernel, grid_spec=grid_spec, out_shape=jax.ShapeDtypeStruct((cap, dc, LANES), F32),
        compiler_params=_params(("arbitrary",), vmem), name="moe_experts",
    )(block_e, n_valid, xb, w_gu, w_down)


def _ffn_out_kernel(pos_ref, w_ref, x_ref, ff_ref, yb_ref, wgu_ref, wdn_ref, gate_ref, g_ref, b_ref, xo_ref,
                    rows_sc, sum_sc, sem):
    tm = x_ref.shape[1]

    def issue(t, carry):
        for k in range(TOP_K):
            _row_copy(yb_ref.at[pos_ref[k, t]], rows_sc.at[k, t], sem).start()
        return carry

    lax.fori_loop(0, tm, issue, 0)
    shared = _swiglu_rows(_rows_2d(ff_ref.at[0]).astype(BF16), wgu_ref[...], wdn_ref[...])
    for k in range(TOP_K):
        _row_copy(yb_ref.at[pl.ds(0, tm)], rows_sc.at[k], sem).wait()

    def combine(t, carry):
        acc = w_ref[0, t] * rows_sc[0, t]
        for k in range(1, TOP_K):
            acc = acc + w_ref[k, t] * rows_sc[k, t]
        sum_sc[t] = acc
        return carry

    lax.fori_loop(0, tm, combine, 0)
    out = _rows_2d(sum_sc) + shared
    xo_ref[0] = _layer_norm(DN_ALPHA * x_ref[0] + gate_ref[0] * out, g_ref[0], b_ref[0])


def _ffn_out(x, ff3, yb, pos, w, tile0, sh_gu, sh_down, gate, ln_g, ln_b):
    bsz, n, d = x.shape
    dc = d // LANES
    tm = min(ROUTE_TOKENS, n)
    nt = n // tm
    row = pl.BlockSpec((1, tm, d), lambda b, i: (b, i, 0))
    col = lambda b, i: (0, tile0 + b * nt + i)
    vecs = [gate, ln_g, ln_b]
    in_specs = [pl.BlockSpec((TOP_K, tm), col, memory_space=pltpu.SMEM),
                pl.BlockSpec((TOP_K, tm), col, memory_space=pltpu.SMEM),
                row, pl.BlockSpec((1, tm, dc, LANES), lambda b, i: (b, i, 0, 0)),
                pl.BlockSpec(memory_space=pl.ANY),
                pl.BlockSpec(sh_gu.shape, lambda b, i: (0, 0)), pl.BlockSpec(sh_down.shape, lambda b, i: (0, 0))]
    in_specs += [_bcast_spec(a) for a in vecs]
    vmem = (TOP_K + 1) * tm * d * 4 + 2 * (3 * tm * d * 4 + sh_gu.size * 2 + sh_down.size * 2) + 4 * tm * d * 4
    return pl.pallas_call(
        _ffn_out_kernel, grid=(bsz, nt), in_specs=in_specs, out_specs=row,
        out_shape=jax.ShapeDtypeStruct((bsz, n, d), F32),
        scratch_shapes=[pltpu.VMEM((TOP_K, tm, dc, LANES), F32), pltpu.VMEM((tm, dc, LANES), F32),
                        pltpu.SemaphoreType.DMA(())],
        compiler_params=_params(("arbitrary", "arbitrary"), vmem), name="moe_combine_ffn_out",
    )(pos, w, x, ff3, yb, sh_gu, sh_down, *vecs)


def _moe_dispatch_experts(tokens3, router_w, router_bias, w_gu, w_down):
    t = tokens3.shape[0]
    idx, w, rank, counts = _router(tokens3, router_w, router_bias)
    pad_start, block_e, n_valid, cap = _moe_plan(counts[:, 0], t)
    pos = _positions(idx, rank, pad_start)
    xb = _dispatch(pos, tokens3, cap)
    return _moe_experts(xb, block_e, n_valid, w_gu, w_down), pos, w


def _rope_tables(n):
    rows = n // GRID_W
    row = jnp.repeat(jnp.arange(rows), GRID_W).astype(F32)
    col = jnp.tile(jnp.arange(GRID_W), rows).astype(F32)
    half = MLA_ROPE // 2
    inv = ROPE_THETA ** (-jnp.arange(0, half, 2, dtype=F32) / half)
    ang = jnp.concatenate([row[:, None] * inv, col[:, None] * inv], -1)
    return jnp.cos(ang), jnp.sin(ang)


def _rope(x, cos, sin):
    x1, x2 = x[..., 0::2], x[..., 1::2]
    return jnp.stack([x1 * cos - x2 * sin, x1 * sin + x2 * cos], -1).reshape(x.shape)


def _dwconv(x, w, b):
    pad = w.shape[0] // 2
    y = lax.conv_general_dilated(x, w[:, None, :], window_strides=(1,), padding=[(pad, pad)],
                                 dimension_numbers=('NWC', 'WIO', 'NWC'), feature_group_count=x.shape[-1])
    return y + b


def _hyena_filters(n, w1, b1, f1, w2, b2, f2, w3):
    t = jnp.linspace(0.0, 1.0, n, dtype=F32)[:, None]
    w = 2.0 * math.pi * jnp.arange(n, dtype=F32)[:, None] / n
    fr = jnp.linspace(1e-4, HY_BANDS - 1, HY_BANDS, dtype=F32)
    z = jnp.concatenate([t, jnp.cos(fr * w), -jnp.sin(fr * w)], -1)
    hp = lax.Precision.HIGHEST
    h = jnp.sin(f1 * (jnp.dot(z, w1, precision=hp) + b1))
    h = jnp.sin(f2 * (jnp.dot(h, w2, precision=hp) + b2))
    h = jnp.dot(h, w3, precision=hp)
    deltas = jnp.abs(jnp.linspace(HY_DECAY_MIN, HY_DECAY_MAX, HY_WIDTH, dtype=F32))
    decay = jnp.exp(-t * deltas)
    h_f = h[:, :HY_WIDTH] * decay
    h_b = h[:, HY_WIDTH:] * decay
    k = jnp.concatenate([h_f, jnp.zeros((1, HY_WIDTH), F32), h_b[:0:-1]], 0)
    return k / jnp.sum(jnp.abs(k), 0, keepdims=True)


def _split(a):
    hi = a.astype(BF16)
    return hi, (a - hi.astype(F32)).astype(BF16)


def _dot3(a_hi, a_lo, x):
    x_hi, x_lo = _split(x)
    return (jnp.dot(a_hi, x_hi, preferred_element_type=F32) + jnp.dot(a_lo, x_hi, preferred_element_type=F32)
            + jnp.dot(a_hi, x_lo, preferred_element_type=F32))


def _cis(num, den):
    ang = (2.0 * math.pi / den) * (num % den).astype(F32)
    return jnp.cos(ang), jnp.sin(ang)


def _dft_outer_kernel(f_hi_ref, f_lo_ref, x_ref, o_ref):
    o_ref[0] = _dot3(f_hi_ref[...], f_lo_ref[...], x_ref[0])


def _dft_outer(x, f_hi, f_lo):
    bsz, k, m = x.shape
    r = f_hi.shape[0]
    tn = min(DFT_COLS, m)
    assert m % tn == 0
    vmem = 2 * (2 * f_hi.size * 2 + k * tn * 4 + r * tn * 4) + 3 * (k + r) * tn * 4
    return pl.pallas_call(
        _dft_outer_kernel, grid=(bsz, m // tn),
        in_specs=[pl.BlockSpec(f_hi.shape, lambda b, j: (0, 0)), pl.BlockSpec(f_lo.shape, lambda b, j: (0, 0)),
                  pl.BlockSpec((1, k, tn), lambda b, j: (b, 0, j))],
        out_specs=pl.BlockSpec((1, r, tn), lambda b, j: (b, 0, j)),
        out_shape=jax.ShapeDtypeStruct((bsz, r, m), F32),
        compiler_params=_params(("parallel", "parallel"), vmem), name="hyena_dft_outer",
    )(f_hi, f_lo, x)


def _dft_inner_kernel(m_hi_ref, m_lo_ref, mt_hi_ref, mt_lo_ref, a_ref, h_ref, o_ref, *, conv):
    n2 = a_ref.shape[3]
    x = a_ref[0, :, 0].reshape(2 * n2, a_ref.shape[4])
    y = _dot3(m_hi_ref[0], m_lo_ref[0], x)
    if conv:
        h = h_ref[0, :, 0].reshape(2 * n2, h_ref.shape[4])
        yr, yi, hr, hi = y[:n2], y[n2:], h[:n2], h[n2:]
        prod = jnp.concatenate([yr * hr - yi * hi, yr * hi + yi * hr], axis=0)
        y = _dot3(mt_hi_ref[0], mt_lo_ref[0], prod)
    o_ref[0, :, 0] = y.reshape(2, n2, y.shape[1])


def _dft_inner(a, h, mats, conv):
    bsz, _, n1, n2, c = a.shape
    blk = lambda sel: pl.BlockSpec((1, 2, 1, n2, c), sel)
    mat = pl.BlockSpec((1, 2 * n2, 2 * n2), lambda k, b: (k, 0, 0))
    vmem = 2 * (4 * 4 * n2 * n2 * 2 + 3 * 2 * n2 * c * 4) + 8 * 2 * n2 * c * 4
    return pl.pallas_call(
        functools.partial(_dft_inner_kernel, conv=conv), grid=(n1, bsz),
        in_specs=[mat, mat, mat, mat, blk(lambda k, b: (b, 0, k, 0, 0)), blk(lambda k, b: (0, 0, k, 0, 0))],
        out_specs=blk(lambda k, b: (b, 0, k, 0, 0)), out_shape=jax.ShapeDtypeStruct(a.shape, F32),
        compiler_params=_params(("parallel", "arbitrary"), vmem),
        name="hyena_dft_inner_conv" if conv else "hyena_dft_inner",
    )(*mats, a, h)


def _dft_final_kernel(fd_hi_ref, fd_lo_ref, a_ref, z_ref, x0_ref, skip_ref, o_ref):
    y = _dot3(fd_hi_ref[...], fd_lo_ref[...], a_ref[0])
    o_ref[0] = (y + z_ref[0] * skip_ref[...]) * x0_ref[0]


def _dft_final(a, z, x0, skip_t, fd_hi, fd_lo):
    bsz, r2, m = a.shape
    k = fd_hi.shape[0]
    tn = min(DFT_COLS, m)
    row = pl.BlockSpec((1, k, tn), lambda b, j: (b, 0, j))
    vmem = 2 * (2 * fd_hi.size * 2 + r2 * tn * 4 + 3 * k * tn * 4) + 3 * (k + r2) * tn * 4
    return pl.pallas_call(
        _dft_final_kernel, grid=(bsz, m // tn),
        in_specs=[pl.BlockSpec(fd_hi.shape, lambda b, j: (0, 0)), pl.BlockSpec(fd_lo.shape, lambda b, j: (0, 0)),
                  pl.BlockSpec((1, r2, tn), lambda b, j: (b, 0, j)), row, row,
                  pl.BlockSpec((1, tn), lambda b, j: (0, 0))],
        out_specs=row, out_shape=jax.ShapeDtypeStruct((bsz, k, m), F32),
        compiler_params=_params(("parallel", "parallel"), vmem), name="hyena_dft_final",
    )(fd_hi, fd_lo, a, z, x0, skip_t)


def _hyena_long_conv(z, x0, k, skip):
    bsz, n, c = z.shape
    n2 = DFT_INNER
    nn = 2 * n
    n1 = nn // n2
    assert nn == n1 * n2 and n1 % 2 == 0
    half = n1 // 2
    m = n2 * c
    j1 = jnp.arange(n1)
    ca, sa = _cis(j1[:, None] * j1[None, :], n1)
    fa = jnp.concatenate([ca, -sa], axis=0)
    fd = jnp.concatenate([ca, -sa], axis=1)[:half] / nn
    j2 = jnp.arange(n2)
    cb, sb = _cis(j2[None, None, :] * (n1 * j2[None, :, None] + j1[:, None, None]), nn)
    mb = jnp.concatenate([jnp.concatenate([cb, sb], 2), jnp.concatenate([-sb, cb], 2)], 1)
    mats = _split(mb) + _split(jnp.swapaxes(mb, 1, 2))
    fa_hi, fa_lo = _split(fa)
    hk = _dft_outer(k.reshape(1, n1, m), fa_hi, fa_lo).reshape(1, 2, n1, n2, c)
    hk = _dft_inner(hk, hk, mats, conv=False)
    a = _dft_outer(z.reshape(bsz, half, m), fa_hi[:, :half], fa_lo[:, :half]).reshape(bsz, 2, n1, n2, c)
    a = _dft_inner(a, hk, mats, conv=True).reshape(bsz, 2 * n1, m)
    skip_t = jnp.tile(skip.reshape(1, c), (1, min(DFT_COLS, m) // c))
    out = _dft_final(a, z.reshape(bsz, half, m), x0.reshape(bsz, half, m), skip_t, *_split(fd))
    return out.reshape(bsz, n, c)


def _short_conv_kernel(f_hi_ref, f_lo_ref, fi_hi_ref, fi_lo_ref, z_ref, x0_ref, k_ref, skip_ref, o_ref):
    n = z_ref.shape[1]
    nn = 2 * n
    z = z_ref[0]
    hk = _dot3(f_hi_ref[...], f_lo_ref[...], k_ref[...])
    zs = _dot3(f_hi_ref[:, :n], f_lo_ref[:, :n], z)
    zr, zi, hr, hi = zs[:nn], zs[nn:], hk[:nn], hk[nn:]
    prod = jnp.concatenate([zr * hr - zi * hi, zr * hi + zi * hr], axis=0)
    o_ref[0] = (_dot3(fi_hi_ref[...], fi_lo_ref[...], prod) + z * skip_ref[...]) * x0_ref[0]


def _hyena_short_conv(z, x0, k, skip):
    bsz, n, c = z.shape
    nn = 2 * n
    idx = jnp.arange(nn)
    cf, sf = _cis(idx[:, None] * idx[None, :], nn)
    f = jnp.concatenate([cf, -sf], axis=0)
    fi = jnp.concatenate([cf, -sf], axis=1)[:n] / nn
    full = lambda a: pl.BlockSpec(a.shape, lambda b: (0,) * a.ndim)
    row = pl.BlockSpec((1, n, c), lambda b: (b, 0, 0))
    ops = _split(f) + _split(fi)
    vmem = 2 * (sum(a.size * 2 for a in ops) + 3 * n * c * 4 + nn * c * 4) + 12 * 2 * nn * c * 4
    return pl.pallas_call(
        _short_conv_kernel, grid=(bsz,),
        in_specs=[full(a) for a in ops] + [row, row, full(k), pl.BlockSpec((1, c), lambda b: (0, 0))],
        out_specs=row, out_shape=jax.ShapeDtypeStruct((bsz, n, c), F32),
        compiler_params=_params(("parallel",), vmem), name="hyena_short_conv",
    )(*ops, z, x0, k, skip.reshape(1, c))


def _hyena_sequence(proj, conv_w, conv_b, filt, skip):
    n = proj.shape[1]
    u = _dwconv(proj, conv_w, conv_b)
    x0, x1, v = jnp.split(u, 3, axis=-1)
    z = v * x1
    k = _hyena_filters(n, *filt)
    conv = _hyena_short_conv if n <= HY_DIRECT_MAX else _hyena_long_conv
    return conv(z, x0, k, skip)


def _mixer_hyena_mla(x, ctx, sh_l, sc_l, sh_c, sc_c, p):
    bsz, n, _ = x.shape
    nc = ctx.shape[1]
    w_in = p['w_in'].astype(BF16)
    w_parts = [w_in[:, :OFF_Q], w_in[:, OFF_Q:OFF_KV], w_in[:, OFF_KV:OFF_KPE], w_in[:, OFF_KPE:]]
    w_qb = p['w_qb'].astype(BF16)
    w_kvb = p['w_kvb'].astype(BF16)
    qg = p['q_norm'].reshape(1, 1, -1)
    kvg = p['kv_norm'].reshape(1, 1, -1)

    def project(h, shift, scale):
        hy, ql, kvl, kpe = _linear(h, w_parts, pre="mod", pre_args=(shift, scale), name="in_proj")
        q, = _linear(ql, [w_qb], pre="rms", pre_args=(qg,), name="q_proj")
        kv, = _linear(kvl, [w_kvb], pre="rms", pre_args=(kvg,), name="kv_proj")
        m = h.shape[1]
        q = q.reshape(bsz, m, MLA_HEADS, MLA_QK).transpose(0, 2, 1, 3)
        kv = kv.reshape(bsz, m, MLA_HEADS, MLA_NOPE + MLA_V).transpose(0, 2, 1, 3)
        return hy, q[..., :MLA_NOPE], q[..., MLA_NOPE:], kv[..., :MLA_NOPE], kv[..., MLA_NOPE:], kpe

    hy_l, qn_l, qp_l, kn_l, v_l, kp_l = project(x, sh_l, sc_l)
    hy_c, qn_c, qp_c, kn_c, v_c, kp_c = project(ctx, sh_c, sc_c)
    cos, sin = _rope_tables(n)
    qp_l = _rope(qp_l, cos, sin)
    kp_l = _rope(kp_l, cos, sin)

    def heads(kp):
        return jnp.broadcast_to(kp[:, None], (bsz, MLA_HEADS) + kp.shape[1:])

    def values_t(v):
        ones = jnp.ones((bsz, MLA_HEADS, SUBLANES, v.shape[2]), F32)
        return jnp.concatenate([v.transpose(0, 1, 3, 2), ones], 2).astype(BF16)

    q_l = (jnp.concatenate([qn_l, qp_l], -1) * (MLA_SCALE * LOG2E)).astype(BF16)
    q_c = (jnp.concatenate([qn_c, qp_c], -1) * (MLA_SCALE * LOG2E)).astype(BF16)
    k_c = jnp.concatenate([kn_c, heads(kp_c)], -1).astype(BF16)
    k_l = jnp.concatenate([kn_l, heads(kp_l)], -1).astype(BF16)
    k_all = jnp.concatenate([k_c, k_l], 2)
    v_all = jnp.concatenate([v_c, v_l], 2)
    att_l = _attention(q_l, k_all, values_t(v_all), tq=min(ATT_Q_TILE, n), tk=_key_tile(nc + n))
    att_c = _attention(q_c, k_c, values_t(v_c), tq=nc, tk=_key_tile(nc))

    filt = (p['filt_w1'], p['filt_b1'], p['filt_freq1'], p['filt_w2'], p['filt_b2'], p['filt_freq2'], p['filt_w3'])
    hyo_l = _hyena_sequence(hy_l, p['conv_w'], p['conv_b'], filt, p['skip'])
    hyo_c = _hyena_sequence(hy_c, p['conv_w'], p['conv_b'], filt, p['skip'])
    return (hyo_l, att_l), (hyo_c, att_c)


def _halo_specs(n, tm, k):
    nb = n // SUBLANES
    per = tm // SUBLANES
    return [pl.BlockSpec((1, tm, k), lambda b, i: (b, i, 0)),
            pl.BlockSpec((1, SUBLANES, k), lambda b, i: (b, jnp.maximum(i * per - 1, 0), 0)),
            pl.BlockSpec((1, SUBLANES, k), lambda b, i: (b, jnp.minimum((i + 1) * per, nb - 1), 0))]


def _conv3(u, cw_ref, cb_ref, tm):
    i = pl.program_id(1)
    rows = lax.broadcasted_iota(I32, (u.shape[0], 1), 0)
    inside = jnp.logical_and(jnp.logical_or(rows >= SUBLANES, i > 0),
                             jnp.logical_or(rows < tm + SUBLANES, i < pl.num_programs(1) - 1))
    u = jnp.where(inside, u, 0.0)
    prev = pltpu.roll(u, 1, 0)[SUBLANES:SUBLANES + tm]
    nxt = pltpu.roll(u, u.shape[0] - 1, 0)[SUBLANES:SUBLANES + tm]
    return cw_ref[0:1, :] * prev + cw_ref[1:2, :] * u[SUBLANES:SUBLANES + tm] + cw_ref[2:3, :] * nxt + cb_ref[...]


def _ssd_in_kernel(xc_ref, xp_ref, xn_ref, sh_ref, sc_ref, wz_ref, wx_ref, wdt_ref, cw_ref, cb_ref, dtb_ref,
                   z_ref, xs_ref, b_ref, c_ref, dt_ref):
    tm = xc_ref.shape[1]
    rows = jnp.concatenate([xp_ref[0], xc_ref[0], xn_ref[0]], axis=0)
    h = (rows * (1.0 + sc_ref[0]) + sh_ref[0]).astype(BF16)
    hc = h[SUBLANES:SUBLANES + tm]
    z_ref[0] = jnp.dot(hc, wz_ref[...], preferred_element_type=F32)
    y = _conv3(jnp.dot(h, wx_ref[...], preferred_element_type=F32), cw_ref, cb_ref, tm)
    xbc = y * jax.nn.sigmoid(y)
    xs_ref[0] = xbc[:, :SSD_INNER]
    b_ref[0] = xbc[:, SSD_INNER:SSD_INNER + SSD_BC]
    c_ref[0] = xbc[:, SSD_INNER + SSD_BC:]
    dt = jnp.dot(hc, wdt_ref[...], preferred_element_type=F32) + dtb_ref[...]
    dt_ref[0] = (jnp.maximum(dt, 0.0) + jnp.log1p(jnp.exp(-jnp.abs(dt)))).T


def _ssd_in_proj(x, shift, scale, w_z, w_xbc, w_dt, conv_w, conv_b, dt_bias):
    bsz, n, k = x.shape
    tm = min(ROW_TILE, n)
    assert n % tm == 0
    full = lambda a: pl.BlockSpec(a.shape, lambda b, i: (0,) * a.ndim)
    row = lambda c: pl.BlockSpec((1, tm, c), lambda b, i: (b, i, 0))
    nh2 = w_dt.shape[1]
    consts = [w_z, w_xbc, w_dt, conv_w, conv_b, dt_bias]
    widths = [SSD_INNER, SSD_INNER, SSD_BC, SSD_BC]
    vmem = 2 * (tm * k * 4 + sum(a.size * a.dtype.itemsize for a in consts) + tm * (sum(widths) + nh2) * 4) \
        + 6 * (tm + 2 * SUBLANES) * SSD_XBC * 4
    return pl.pallas_call(
        _ssd_in_kernel, grid=(bsz, n // tm),
        in_specs=_halo_specs(n, tm, k) + [_bcast_spec(shift), _bcast_spec(scale)] + [full(a) for a in consts],
        out_specs=[row(c) for c in widths] + [pl.BlockSpec((1, nh2, tm), lambda b, i: (b, 0, i))],
        out_shape=[jax.ShapeDtypeStruct((bsz, n, c), F32) for c in widths]
        + [jax.ShapeDtypeStruct((bsz, nh2, n), F32)],
        compiler_params=_params(("parallel", "parallel"), vmem), name="ssd_in_proj",
    )(x, x, x, shift, scale, *consts)


def _ssd_scan_kernel(x_ref, b_ref, c_ref, dt_ref, a_ref, s0_ref, y_ref, sfin_ref, st_sc, *, reverse):
    ci = pl.program_id(2)
    q = SSD_CHUNK

    @pl.when(ci == 0)
    def _():
        st_sc[...] = s0_ref[0, 0]

    dt = dt_ref[0]
    a = dt * a_ref[0]
    si = lax.broadcasted_iota(I32, (q, q), 0)
    li = lax.broadcasted_iota(I32, (q, q), 1)
    incl = jnp.where((si >= li) if reverse else (si <= li), 1.0, 0.0)
    hp = lax.Precision.HIGHEST
    acs = jnp.dot(a, incl, precision=hp, preferred_element_type=F32)
    tot = jnp.dot(a, jnp.ones((q, LANES), F32), precision=hp, preferred_element_type=F32)
    e_in = jnp.exp(acs)
    w_end = jnp.exp(tot - acs) * dt
    e_tot = jnp.exp(tot)
    acs_t = acs.T
    e_in_t = e_in.T
    mask = (li >= si) if reverse else (li <= si)
    cmat = c_ref[0]
    cb = lax.dot_general(cmat.astype(BF16), b_ref[0].astype(BF16), (((1,), (1,)), ((), ())),
                         preferred_element_type=F32)
    bt = b_ref[0].T
    first = lax.broadcasted_iota(I32, (q, LANES), 1) < SSD_HEADDIM
    ys = []
    for pr in range(SSD_HPG // 2):
        lo, hi = pr * LANES, (pr + 1) * LANES
        x_pair = x_ref[0, :, lo:hi].astype(BF16)
        st_pair = st_sc[:, lo:hi]
        rhs = jnp.concatenate([x_pair, st_pair.astype(BF16)], axis=0)
        y2, s2 = [], []
        for r in (2 * pr, 2 * pr + 1):
            seg = acs_t[:, r:r + 1] - acs[r:r + 1, :]
            m = cb * jnp.exp(jnp.where(mask, seg, -jnp.inf)) * dt[r:r + 1, :]
            lhs = jnp.concatenate([m.astype(BF16), (cmat * e_in_t[:, r:r + 1]).astype(BF16)], axis=1)
            y2.append(jnp.dot(lhs, rhs, preferred_element_type=F32))
            btr = (bt * w_end[r:r + 1, :]).astype(BF16)
            s2.append(e_tot[r:r + 1, :] * st_pair + jnp.dot(btr, x_pair, preferred_element_type=F32))
        ys.append(jnp.where(first, y2[0], y2[1]))
        st_sc[:, lo:hi] = jnp.where(first, s2[0], s2[1])
    y_ref[0] = jnp.concatenate(ys, axis=1)

    @pl.when(ci == pl.num_programs(2) - 1)
    def _():
        sfin_ref[0, 0] = st_sc[...]


def _ssd_scan(xs, bm, cm, dt_t, a_rep, s0, reverse):
    bsz, n, _ = xs.shape
    q = SSD_CHUNK
    nc = n // q
    assert n % q == 0 and q == LANES
    d = 1 if reverse else 0
    cc = (lambda c: nc - 1 - c) if reverse else (lambda c: c)
    gw = SSD_HPG * SSD_HEADDIM
    state = pl.BlockSpec((1, 1, SSD_STATE, gw), lambda b, g, c: (b, g, 0, 0))
    vmem = 2 * (2 * q * gw * 4 + 2 * q * SSD_STATE * 4 + 2 * SSD_STATE * gw * 4) + SSD_STATE * gw * 4 + 64 * q * q * 4
    return pl.pallas_call(
        functools.partial(_ssd_scan_kernel, reverse=reverse), grid=(bsz, SSD_GROUPS, nc),
        in_specs=[pl.BlockSpec((1, q, gw), lambda b, g, c: (b, cc(c), g)),
                  pl.BlockSpec((1, q, SSD_STATE), lambda b, g, c: (b, cc(c), g)),
                  pl.BlockSpec((1, q, SSD_STATE), lambda b, g, c: (b, cc(c), g)),
                  pl.BlockSpec((1, SSD_HPG, q), lambda b, g, c: (b, d * SSD_GROUPS + g, cc(c))),
                  pl.BlockSpec((1, SSD_HPG, LANES), lambda b, g, c: (d * SSD_GROUPS + g, 0, 0)),
                  state],
        out_specs=[pl.BlockSpec((1, q, gw), lambda b, g, c: (b, cc(c), g)), state],
        out_shape=[jax.ShapeDtypeStruct((bsz, n, SSD_INNER), F32),
                   jax.ShapeDtypeStruct((bsz, SSD_GROUPS, SSD_STATE, gw), F32)],
        scratch_shapes=[pltpu.VMEM((SSD_STATE, gw), F32)],
        compiler_params=_params(("parallel", "parallel", "arbitrary"), vmem),
        name="ssd_scan_rev" if reverse else "ssd_scan_fwd",
    )(xs, bm, cm, dt_t, a_rep, s0)


def _ssd_out_kernel(x_ref, yf_ref, yb_ref, xs_ref, z_ref, d_ref, ng_ref, w_ref, gate_ref, g_ref, b_ref, sh_ref,
                    sc_ref, xo_ref, ff_ref):
    z = z_ref[0]
    y = (yf_ref[0] + yb_ref[0] + xs_ref[0] * d_ref[0]) * (z * jax.nn.sigmoid(z))
    gw = SSD_INNER // SSD_GROUPS
    parts = []
    for g in range(SSD_GROUPS):
        yg = y[:, g * gw:(g + 1) * gw]
        parts.append(yg * lax.rsqrt(jnp.mean(yg * yg, -1, keepdims=True) + RMS_EPS))
    yn = (jnp.concatenate(parts, axis=1) * ng_ref[0]).astype(BF16)
    mix = jnp.dot(yn, w_ref[...], preferred_element_type=F32)
    xn = _layer_norm(DN_ALPHA * x_ref[0] + gate_ref[0] * mix, g_ref[0], b_ref[0])
    xo_ref[0] = xn
    _store_rows(ff_ref.at[0], xn * (1.0 + sc_ref[0]) + sh_ref[0])


def _ssd_out(x, y_f, y_b, xs, z, d_rep, norm_g, w_out, gate, ln_g, ln_b, shift2, scale2):
    bsz, n, d = x.shape
    tm = min(ROW_TILE, n)
    assert n % tm == 0
    row = lambda c: pl.BlockSpec((1, tm, c), lambda b, i: (b, i, 0))
    vecs = [gate, ln_g, ln_b, shift2, scale2]
    in_specs = [row(d)] + [row(SSD_INNER)] * 4 + [_bcast_spec(d_rep), _bcast_spec(norm_g),
                                                 pl.BlockSpec(w_out.shape, lambda b, i: (0, 0))]
    in_specs += [_bcast_spec(a) for a in vecs]
    out_specs = [row(d), pl.BlockSpec((1, tm, d // LANES, LANES), lambda b, i: (b, i, 0, 0))]
    vmem = 2 * (3 * tm * d * 4 + 4 * tm * SSD_INNER * 4 + w_out.size * 2) + 4 * tm * SSD_INNER * 4
    return pl.pallas_call(
        _ssd_out_kernel, grid=(bsz, n // tm), in_specs=in_specs, out_specs=out_specs,
        out_shape=[jax.ShapeDtypeStruct((bsz, n, d), F32), jax.ShapeDtypeStruct((bsz, n, d // LANES, LANES), F32)],
        compiler_params=_params(("parallel", "parallel"), vmem), name="ssd_out",
    )(x, y_f, y_b, xs, z, d_rep, norm_g, w_out, *vecs)


def _mixer_ssd(x, ctx, sh_l, sc_l, sh_c, sc_c, p):
    a_all = -jnp.exp(jnp.concatenate([p['a_log_f'], p['a_log_b']]))
    a_rep = jnp.broadcast_to(a_all.reshape(2 * SSD_GROUPS, SSD_HPG, 1), (2 * SSD_GROUPS, SSD_HPG, LANES))
    w_in = p['w_in'].astype(BF16)
    consts = (w_in[:, :SSD_INNER], w_in[:, SSD_INNER:SSD_INNER + SSD_XBC], w_in[:, SSD_INNER + SSD_XBC:],
              p['conv_w'], p['conv_b'][None], jnp.concatenate([p['dt_bias_f'], p['dt_bias_b']])[None])
    _, xc, bc, cc, dtc = _ssd_in_proj(ctx, sh_c, sc_c, *consts)
    zl, xl, bl, cl, dtl = _ssd_in_proj(x, sh_l, sc_l, *consts)
    s0 = jnp.zeros((ctx.shape[0], SSD_GROUPS, SSD_STATE, SSD_HPG * SSD_HEADDIM), F32)
    _, sc_f = _ssd_scan(xc, bc, cc, dtc, a_rep, s0, reverse=False)
    _, sc_b = _ssd_scan(xc, bc, cc, dtc, a_rep, s0, reverse=True)
    y_f, _ = _ssd_scan(xl, bl, cl, dtl, a_rep, sc_f, reverse=False)
    y_b, _ = _ssd_scan(xl, bl, cl, dtl, a_rep, sc_b, reverse=True)
    return y_f, y_b, xl, zl


def kernel(x, c, ctx, c_ctx, mod_w, mod_b, ln_mix_g, ln_mix_b, ln_ffn_g, ln_ffn_b, a_w_in, hy_conv_w, hy_conv_b, hy_filt_w1, hy_filt_b1, hy_filt_freq1, hy_filt_w2, hy_filt_b2, hy_filt_freq2, hy_filt_w3, hy_skip, mla_q_norm, mla_w_qb, mla_kv_norm, mla_w_kvb, a_w_out, ssd_w_in, ssd_conv_w, ssd_conv_b, ssd_dt_bias_f, ssd_dt_bias_b, ssd_a_log_f, ssd_a_log_b, ssd_d, ssd_norm_g, ssd_w_out, router_w, router_bias, exp_w_gu, exp_w_down, sh_w_gu, sh_w_down):
    bsz, n_lat, d = x.shape
    n_ctx = ctx.shape[1]
    dc = d // LANES
    hp = lax.Precision.HIGHEST
    for l in range(DEPTH):
        last = l == DEPTH - 1
        i = l // 2
        mod = (jnp.dot(jax.nn.silu(c), mod_w[l], precision=hp) + mod_b[l]).reshape(bsz, N_MOD, 1, d)
        mod_c = (jnp.dot(jax.nn.silu(c_ctx), mod_w[l], precision=hp) + mod_b[l]).reshape(1, N_MOD, 1, d)
        sh1, sc1, g1, sh2, sc2, g2 = [mod[:, j] for j in range(N_MOD)]
        csh1, csc1, cg1, csh2, csc2, cg2 = [mod_c[:, j] for j in range(N_MOD)]
        vec = lambda a: a.reshape(1, 1, d)
        if l % 2 == 0:
            p = {"w_in": a_w_in[i], "conv_w": hy_conv_w[i], "conv_b": hy_conv_b[i],
                 "filt_w1": hy_filt_w1[i], "filt_b1": hy_filt_b1[i], "filt_freq1": hy_filt_freq1[i],
                 "filt_w2": hy_filt_w2[i], "filt_b2": hy_filt_b2[i], "filt_freq2": hy_filt_freq2[i],
                 "filt_w3": hy_filt_w3[i], "skip": hy_skip[i], "q_norm": mla_q_norm[i], "w_qb": mla_w_qb[i],
                 "kv_norm": mla_kv_norm[i], "w_kvb": mla_w_kvb[i]}
            ys_l, ys_c = _mixer_hyena_mla(x, ctx, sh1, sc1, csh1, csc1, p)
            w_out = a_w_out[i].astype(BF16)
            ws = [w_out[:HY_WIDTH], w_out[HY_WIDTH:]]
            x, ff_x = _mix_out(x, ys_l, ws, g1, vec(ln_mix_g[l]), vec(ln_mix_b[l]), sh2, sc2)
        else:
            p = {"w_in": ssd_w_in[i], "conv_w": ssd_conv_w[i], "conv_b": ssd_conv_b[i],
                 "dt_bias_f": ssd_dt_bias_f[i], "dt_bias_b": ssd_dt_bias_b[i],
                 "a_log_f": ssd_a_log_f[i], "a_log_b": ssd_a_log_b[i]}
            assert last
            y_f, y_b, xs, z = _mixer_ssd(x, ctx, sh1, sc1, csh1, csc1, p)
            d_rep = jnp.repeat(ssd_d[i], SSD_HEADDIM).reshape(1, 1, SSD_INNER)
            x, ff_x = _ssd_out(x, y_f, y_b, xs, z, d_rep, ssd_norm_g[i].reshape(1, 1, SSD_INNER),
                               ssd_w_out[i].astype(BF16), g1, vec(ln_mix_g[l]), vec(ln_mix_b[l]), sh2, sc2)
        sh_gu = sh_w_gu[l].astype(BF16)
        sh_down = sh_w_down[l].astype(BF16)
        ln_g, ln_b = vec(ln_ffn_g[l]), vec(ln_ffn_b[l])
        moe_w = (router_w[l], router_bias[l], exp_w_gu[l], exp_w_down[l])
        if last:
            yb, pos, w = _moe_dispatch_experts(ff_x.reshape(-1, dc, LANES), *moe_w)
            x = _ffn_out(x, ff_x, yb, pos, w, 0, sh_gu, sh_down, g2, ln_g, ln_b)
        else:
            ctx, ff_c = _mix_out(ctx, ys_c, ws, cg1, vec(ln_mix_g[l]), vec(ln_mix_b[l]), csh2, csc2)
            tokens = jnp.concatenate([ff_c.reshape(-1, dc, LANES), ff_x.reshape(-1, dc, LANES)], 0)
            yb, pos, w = _moe_dispatch_experts(tokens, *moe_w)
            assert (bsz * n_ctx) % ROUTE_TOKENS == 0
            ctx = _ffn_out(ctx, ff_c, yb, pos, w, 0, sh_gu, sh_down, cg2, ln_g, ln_b)
            x = _ffn_out(x, ff_x, yb, pos, w, bsz * n_ctx // ROUTE_TOKENS, sh_gu, sh_down, g2, ln_g, ln_b)
    return x
```

```python
import functools
import math

import jax
import jax.numpy as jnp
from jax import lax
from jax.experimental import pallas as pl
from jax.experimental.pallas import tpu as pltpu

F32 = jnp.float32
BF16 = jnp.bfloat16
I32 = jnp.int32

D_MODEL = 1024
DEPTH = 2
GRID_W = 64
N_MOD = 6

HY_WIDTH = 512
HY_EMB = 33
HY_BANDS = (HY_EMB - 1) // 2
HY_TARGET = 1e-2
HY_FAST_DECAY = 0.3
HY_SLOW_DECAY = 1.5
HY_DECAY_MIN = math.log(HY_TARGET) / HY_SLOW_DECAY
HY_DECAY_MAX = math.log(HY_TARGET) / HY_FAST_DECAY

MLA_HEADS = 8
MLA_NOPE = 64
MLA_ROPE = 32
MLA_V = 64
MLA_Q_RANK = 256
MLA_KV_RANK = 128
MLA_QK = MLA_NOPE + MLA_ROPE
MLA_SCALE = MLA_QK ** -0.5
ROPE_THETA = 10000.0
LOG2E = math.log2(math.e)

OFF_Q = 3 * HY_WIDTH
OFF_KV = OFF_Q + MLA_Q_RANK
OFF_KPE = OFF_KV + MLA_KV_RANK

SSD_INNER = 2 * D_MODEL
SSD_HEADDIM = 64
SSD_HEADS = SSD_INNER // SSD_HEADDIM
SSD_GROUPS = 4
SSD_STATE = 128
SSD_CHUNK = 128
SSD_BC = SSD_GROUPS * SSD_STATE
SSD_XBC = SSD_INNER + 2 * SSD_BC
SSD_HPG = SSD_HEADS // SSD_GROUPS

N_EXPERTS = 256
TOP_K = 8
N_EXPERT_GROUPS = 8
TOPK_GROUPS = 4
EXPERT_DIM = 256
ROUTED_SCALE = 2.5

DN_ALPHA = (2 * DEPTH) ** 0.25
LN_EPS = 1e-5
RMS_EPS = 1e-6

LANES = 128
SUBLANES = 8
V7X_VMEM_CAP = 56 * 1024 * 1024

MOE_ROWS = 256
ROUTE_TOKENS = 256
ROW_TILE = 256
ATT_Q_TILE = 2048
ATT_HEADS_PER_STEP = 2
DFT_INNER = 128
DFT_COLS = 2048
HY_DIRECT_MAX = 512


def _params(semantics, vmem_bytes, **kw):
    limit = int(min(max(vmem_bytes * 5 // 4, 32 * 1024 * 1024), V7X_VMEM_CAP))
    return pltpu.CompilerParams(dimension_semantics=semantics, vmem_limit_bytes=limit, **kw)


def _bcast_spec(a):
    if a.shape[0] == 1:
        return pl.BlockSpec((1, 1, a.shape[2]), lambda b, i: (0, 0, 0))
    return pl.BlockSpec((1, 1, a.shape[2]), lambda b, i: (b, 0, 0))


def _layer_norm(r, g, b):
    mu = jnp.mean(r, -1, keepdims=True)
    c = r - mu
    var = jnp.mean(c * c, -1, keepdims=True)
    return c * lax.rsqrt(var + LN_EPS) * g + b


def _swiglu_rows(xb, w_gu, w_down):
    h = jnp.dot(xb, w_gu, preferred_element_type=F32)
    half = h.shape[1] // 2
    g, u = h[:, :half], h[:, half:]
    a = (g * jax.nn.sigmoid(g) * u).astype(BF16)
    return jnp.dot(a, w_down, preferred_element_type=F32)


def _linear_kernel(*refs, pre, n_pre, n_w):
    x = refs[0][0]
    pre_refs = refs[1:1 + n_pre]
    w_refs = refs[1 + n_pre:1 + n_pre + n_w]
    o_refs = refs[1 + n_pre + n_w:]
    if pre == "mod":
        h = x * (1.0 + pre_refs[1][0]) + pre_refs[0][0]
    elif pre == "rms":
        h = x * lax.rsqrt(jnp.mean(x * x, -1, keepdims=True) + RMS_EPS) * pre_refs[0][0]
    else:
        h = x
    hb = h.astype(BF16)
    for w_ref, o_ref in zip(w_refs, o_refs):
        o_ref[0] = jnp.dot(hb, w_ref[...], preferred_element_type=F32).astype(o_ref.dtype)


def _linear(x, ws, pre=None, pre_args=(), name="linear"):
    bsz, n, k = x.shape
    tm = min(ROW_TILE, n)
    assert n % tm == 0
    in_specs = [pl.BlockSpec((1, tm, k), lambda b, i: (b, i, 0))]
    in_specs += [_bcast_spec(a) for a in pre_args]
    in_specs += [pl.BlockSpec(w.shape, lambda b, i: (0, 0)) for w in ws]
    out_specs = [pl.BlockSpec((1, tm, w.shape[1]), lambda b, i: (b, i, 0)) for w in ws]
    out_shape = [jax.ShapeDtypeStruct((bsz, n, w.shape[1]), F32) for w in ws]
    vmem = 2 * (tm * k * 4 + sum(w.size * 2 + tm * w.shape[1] * 4 for w in ws))
    return pl.pallas_call(
        functools.partial(_linear_kernel, pre=pre, n_pre=len(pre_args), n_w=len(ws)),
        grid=(bsz, n // tm), in_specs=in_specs, out_specs=out_specs, out_shape=out_shape,
        compiler_params=_params(("parallel", "parallel"), vmem), name=name,
    )(x, *pre_args, *ws)


def _mix_out_kernel(*refs, n_y):
    x_ref = refs[0]
    y_refs = refs[1:1 + n_y]
    w_refs = refs[1 + n_y:1 + 2 * n_y]
    gate_ref, g_ref, b_ref, sh_ref, sc_ref, xo_ref, ff_ref = refs[1 + 2 * n_y:]
    y = None
    for y_ref, w_ref in zip(y_refs, w_refs):
        t = jnp.dot(y_ref[0].astype(BF16), w_ref[...], preferred_element_type=F32)
        y = t if y is None else y + t
    xn = _layer_norm(DN_ALPHA * x_ref[0] + gate_ref[0] * y, g_ref[0], b_ref[0])
    xo_ref[0] = xn
    ff_ref[0] = xn * (1.0 + sc_ref[0]) + sh_ref[0]


def _mix_out(x, ys, ws, gate, ln_g, ln_b, shift2, scale2):
    bsz, n, d = x.shape
    tm = min(ROW_TILE, n)
    assert n % tm == 0
    row = lambda c: pl.BlockSpec((1, tm, c), lambda b, i: (b, i, 0))
    vecs = [gate, ln_g, ln_b, shift2, scale2]
    in_specs = [row(d)] + [row(y.shape[2]) for y in ys]
    in_specs += [pl.BlockSpec(w.shape, lambda b, i: (0, 0)) for w in ws]
    in_specs += [_bcast_spec(a) for a in vecs]
    vmem = 2 * (3 * tm * d * 4 + sum(tm * y.shape[2] * 4 + w.size * 2 for y, w in zip(ys, ws)))
    return pl.pallas_call(
        functools.partial(_mix_out_kernel, n_y=len(ys)),
        grid=(bsz, n // tm), in_specs=in_specs, out_specs=[row(d), row(d)],
        out_shape=[jax.ShapeDtypeStruct((bsz, n, d), F32)] * 2,
        compiler_params=_params(("parallel", "parallel"), vmem), name="mix_out",
    )(x, *ys, *ws, *vecs)


def _attn_kernel(q_ref, k_ref, vt_ref, o_ref, *, tk, hp):
    nk = k_ref.shape[2]
    tq = q_ref.shape[2]
    dva = vt_ref.shape[2]
    dv = dva - SUBLANES

    def body(j, carry):
        off = pl.multiple_of(j * tk, tk)
        new = []
        for h in range(hp):
            m_prev, acc = carry[h]
            st = lax.dot_general(k_ref[0, h, pl.ds(off, tk), :], q_ref[0, h], (((1,), (1,)), ((), ())),
                                 preferred_element_type=F32)
            m_new = jnp.maximum(m_prev, jnp.max(st, 0, keepdims=True))
            p = jnp.exp2(st - m_new).astype(BF16)
            alpha = jnp.exp2(m_prev - m_new)
            acc = alpha * acc + jnp.dot(vt_ref[0, h, :, pl.ds(off, tk)], p, preferred_element_type=F32)
            new.append((m_new, acc))
        return tuple(new)

    init = tuple((jnp.full((1, tq), -jnp.inf, F32), jnp.zeros((dva, tq), F32)) for _ in range(hp))
    fin = lax.fori_loop(0, nk // tk, body, init)
    outs = [acc[:dv] / acc[dv:dv + 1] for _, acc in fin]
    o_ref[0] = jnp.concatenate(outs, 0).T


def _attention(q, k, vt, tq, tk):
    bsz, h, nq, dk = q.shape
    nk, dva = k.shape[2], vt.shape[2]
    dv = dva - SUBLANES
    hp = ATT_HEADS_PER_STEP
    assert nq % tq == 0 and nk % tk == 0 and h % hp == 0
    vmem = 2 * hp * (tq * LANES * 2 + nk * LANES * 2 + dva * nk * 2) + 2 * tq * hp * dv * 4 + 6 * hp * tk * tq * 4
    return pl.pallas_call(
        functools.partial(_attn_kernel, tk=tk, hp=hp), grid=(bsz, h // hp, nq // tq),
        in_specs=[pl.BlockSpec((1, hp, tq, dk), lambda b, g, i: (b, g, i, 0)),
                  pl.BlockSpec((1, hp, nk, dk), lambda b, g, i: (b, g, 0, 0)),
                  pl.BlockSpec((1, hp, dva, nk), lambda b, g, i: (b, g, 0, 0))],
        out_specs=pl.BlockSpec((1, tq, hp * dv), lambda b, g, i: (b, i, g)),
        out_shape=jax.ShapeDtypeStruct((bsz, nq, h * dv), F32),
        compiler_params=_params(("parallel", "parallel", "arbitrary"), vmem), name="mla_attention",
    )(q, k, vt)


def _key_tile(nk):
    for t in (768, 512, 384, 256, 128):
        if nk % t == 0:
            return t
    return nk


def _router_kernel(x_ref, wt_ref, bias_ref, upper_ref, idx_ref, w_ref, rank_ref, cnt_ref, run_sc):
    i = pl.program_id(0)
    tm = x_ref.shape[0]

    @pl.when(i == 0)
    def _():
        run_sc[...] = jnp.zeros(run_sc.shape, F32)

    logits = lax.dot_general(wt_ref[...], x_ref[...], (((1,), (1,)), ((), ())),
                             precision=lax.Precision.HIGHEST, preferred_element_type=F32)
    sc = jax.nn.sigmoid(logits)
    ch = sc + bias_ref[:, :1]
    neg = -jnp.inf
    chg = ch.reshape(N_EXPERT_GROUPS, N_EXPERTS // N_EXPERT_GROUPS, tm)
    m1 = jnp.max(chg, axis=1)
    eq = chg == m1[:, None, :]
    cnt = jnp.sum(eq.astype(F32), axis=1)
    m2 = jnp.max(jnp.where(eq, neg, chg), axis=1)
    g2 = m1 + jnp.where(cnt >= 2.0, m1, m2)
    gi = lax.broadcasted_iota(I32, g2.shape, 0)
    beaten = jnp.zeros(g2.shape, F32)
    for g in range(N_EXPERT_GROUPS):
        row = g2[g:g + 1, :]
        beaten = beaten + jnp.where(row > g2, 1.0, jnp.where(row == g2, jnp.where(gi > g, 1.0, 0.0), 0.0))
    keep = beaten < float(TOPK_GROUPS)
    cur = jnp.where(keep[:, None, :], chg, neg).reshape(N_EXPERTS, tm)
    eidx = lax.broadcasted_iota(I32, (N_EXPERTS, tm), 0)
    multi = jnp.zeros((N_EXPERTS, tm), F32)
    hits, idx_rows, w_rows = [], [], []
    for _ in range(TOP_K):
        m = jnp.max(cur, axis=0, keepdims=True)
        sel = jnp.min(jnp.where(cur == m, eidx, N_EXPERTS), axis=0, keepdims=True)
        hit = eidx == sel
        idx_rows.append(sel)
        w_rows.append(jnp.sum(jnp.where(hit, sc, 0.0), axis=0, keepdims=True))
        cur = jnp.where(hit, neg, cur)
        multi = multi + jnp.where(hit, 1.0, 0.0)
        hits.append(hit)
    base = jnp.concatenate([run_sc[...]] * (tm // LANES), axis=1)
    before = jnp.dot(multi.astype(BF16), upper_ref[...], preferred_element_type=F32) + base
    rank_rows = [jnp.sum(jnp.where(hit, before, 0.0), axis=0, keepdims=True) for hit in hits]
    w = jnp.concatenate(w_rows, axis=0)
    idx_ref[...] = jnp.concatenate(idx_rows, axis=0)
    w_ref[...] = (w / jnp.sum(w, axis=0, keepdims=True) * ROUTED_SCALE).T
    rank_ref[...] = jnp.concatenate(rank_rows, axis=0).astype(I32)
    run_sc[...] = run_sc[...] + jnp.dot(multi.astype(BF16), jnp.ones((tm, LANES), BF16), preferred_element_type=F32)
    cnt_ref[...] = run_sc[...]


def _router(tokens, router_w, router_bias):
    t, d = tokens.shape
    tm = ROUTE_TOKENS
    assert t % tm == 0
    wt = router_w.T
    bias = jnp.broadcast_to(router_bias.astype(F32)[:, None], (N_EXPERTS, LANES))
    r = jnp.arange(tm)
    upper = (r[:, None] < r[None, :]).astype(BF16)
    col = pl.BlockSpec((TOP_K, tm), lambda i: (0, i))
    full = lambda a: pl.BlockSpec(a.shape, lambda i: (0,) * a.ndim)
    vmem = 2 * (tm * d * 4 + wt.size * 4) + 40 * N_EXPERTS * tm * 4
    return pl.pallas_call(
        _router_kernel, grid=(t // tm,),
        in_specs=[pl.BlockSpec((tm, d), lambda i: (i, 0)), full(wt), full(bias), full(upper)],
        out_specs=[col, pl.BlockSpec((tm, TOP_K), lambda i: (i, 0)), col,
                   pl.BlockSpec((N_EXPERTS, LANES), lambda i: (0, 0))],
        out_shape=[jax.ShapeDtypeStruct((TOP_K, t), I32), jax.ShapeDtypeStruct((t, TOP_K), F32),
                   jax.ShapeDtypeStruct((TOP_K, t), I32), jax.ShapeDtypeStruct((N_EXPERTS, LANES), F32)],
        scratch_shapes=[pltpu.VMEM((N_EXPERTS, LANES), F32)],
        compiler_params=_params(("arbitrary",), vmem), name="moe_router",
    )(tokens, wt, bias, upper)


def _positions_kernel(idx_ref, rank_ref, start_ref, pos_ref):
    tm = idx_ref.shape[1]
    eidx = lax.broadcasted_iota(I32, (N_EXPERTS, tm), 0)
    start = jnp.concatenate([start_ref[...]] * (tm // LANES), axis=1)
    rows = [jnp.sum(jnp.where(eidx == idx_ref[k:k + 1, :], start, 0), axis=0, keepdims=True) for k in range(TOP_K)]
    pos_ref[...] = jnp.concatenate(rows, axis=0) + rank_ref[...]


def _positions(idx, rank, pad_start):
    t = idx.shape[1]
    tm = ROUTE_TOKENS
    start = jnp.broadcast_to(pad_start.astype(I32)[:, None], (N_EXPERTS, LANES))
    col = pl.BlockSpec((TOP_K, tm), lambda i: (0, i))
    return pl.pallas_call(
        _positions_kernel, grid=(t // tm,),
        in_specs=[col, col, pl.BlockSpec((N_EXPERTS, LANES), lambda i: (0, 0))], out_specs=col,
        out_shape=jax.ShapeDtypeStruct((TOP_K, t), I32),
        compiler_params=_params(("parallel",), 8 * N_EXPERTS * tm * 4), name="moe_positions",
    )(idx, rank, start)


def _moe_plan(counts, t):
    n_blocks = -(-(t * TOP_K) // MOE_ROWS) + N_EXPERTS
    c = counts.astype(I32)
    padded = (c + MOE_ROWS - 1) // MOE_ROWS * MOE_ROWS
    pad_end = jnp.cumsum(padded)
    pad_start = pad_end - padded
    block_e = jnp.minimum(jnp.searchsorted(pad_end, jnp.arange(n_blocks, dtype=I32) * MOE_ROWS, side='right'),
                          N_EXPERTS - 1).astype(I32)
    n_valid = (pad_end[-1:] // MOE_ROWS).astype(I32)
    return pad_start, block_e, n_valid, n_blocks * MOE_ROWS


def _row_copy(src, dst, sem):
    return pltpu.make_async_copy(src, dst, sem)


def _dispatch_kernel(pos_ref, x_ref, buf_in_ref, buf_ref, sem):
    del buf_in_ref
    tm = x_ref.shape[0]

    def issue(t, carry):
        for k in range(TOP_K):
            _row_copy(x_ref.at[pl.ds(t, 1)], buf_ref.at[pl.ds(pos_ref[k, t], 1)], sem).start()
        return carry

    lax.fori_loop(0, tm, issue, 0)
    for k in range(TOP_K):
        _row_copy(x_ref, buf_ref.at[pl.ds(0, tm)], sem).wait()


def _dispatch(pos, tokens, cap):
    t, d = tokens.shape
    tm = ROUTE_TOKENS
    zeros = jnp.zeros((cap, d), tokens.dtype)
    return pl.pallas_call(
        _dispatch_kernel, grid=(t // tm,),
        in_specs=[pl.BlockSpec((TOP_K, tm), lambda i: (0, i), memory_space=pltpu.SMEM),
                  pl.BlockSpec((tm, d), lambda i: (i, 0)),
                  pl.BlockSpec(memory_space=pl.ANY)],
        out_specs=pl.BlockSpec(memory_space=pl.ANY),
        out_shape=jax.ShapeDtypeStruct((cap, d), tokens.dtype),
        scratch_shapes=[pltpu.SemaphoreType.DMA(())],
        input_output_aliases={2: 0},
        compiler_params=_params(("arbitrary",), 2 * tm * d * 4, has_side_effects=True),
        name="moe_dispatch",
    )(pos, tokens, zeros)


def _experts_kernel(be_ref, nv_ref, x_ref, wgu_ref, wdn_ref, o_ref, wgu_sc, wdn_sc):
    i = pl.program_id(0)

    @pl.when(i < nv_ref[0])
    def _():
        @pl.when(jnp.logical_or(i == 0, be_ref[i] != be_ref[jnp.maximum(i - 1, 0)]))
        def _():
            wgu_sc[...] = wgu_ref[0, 0].astype(BF16)
            wdn_sc[...] = wdn_ref[0, 0].astype(BF16)

        o_ref[...] = _swiglu_rows(x_ref[...].astype(BF16), wgu_sc[...], wdn_sc[...])

    @pl.when(i >= nv_ref[0])
    def _():
        o_ref[...] = jnp.zeros(o_ref.shape, F32)


def _moe_experts(xb, block_e, n_valid, w_gu, w_down, layer):
    cap, d = xb.shape
    gu = w_gu.shape[3]
    ed = w_down.shape[2]
    live = lambda i, be, nv: jnp.maximum(jnp.minimum(i, nv[0] - 1), 0)
    grid_spec = pltpu.PrefetchScalarGridSpec(
        num_scalar_prefetch=2, grid=(cap // MOE_ROWS,),
        in_specs=[pl.BlockSpec((MOE_ROWS, d), lambda i, be, nv: (live(i, be, nv), 0)),
                  pl.BlockSpec((1, 1, d, gu), lambda i, be, nv: (layer, be[i], 0, 0)),
                  pl.BlockSpec((1, 1, ed, d), lambda i, be, nv: (layer, be[i], 0, 0))],
        out_specs=pl.BlockSpec((MOE_ROWS, d), lambda i, be, nv: (i, 0)),
        scratch_shapes=[pltpu.VMEM((d, gu), BF16), pltpu.VMEM((ed, d), BF16)])
    vmem = 2 * (2 * MOE_ROWS * d * 4 + d * gu * 4 + ed * d * 4) + (d * gu + ed * d) * 2 + 4 * MOE_ROWS * d * 4
    return pl.pallas_call(
        _experts_kernel, grid_spec=grid_spec, out_shape=jax.ShapeDtypeStruct((cap, d), F32),
        compiler_params=_params(("arbitrary",), vmem), name="moe_experts",
    )(block_e, n_valid, xb, w_gu, w_down)


def _ffn_out_kernel(pos_ref, w_ref, x_ref, ff_ref, yb_ref, wgu_ref, wdn_ref, gate_ref, g_ref, b_ref, xo_ref,
                    rows_sc, sem):
    tm = x_ref.shape[1]

    def issue(t, carry):
        for k in range(TOP_K):
            _row_copy(yb_ref.at[pl.ds(pos_ref[k, t], 1)], rows_sc.at[k, pl.ds(t, 1)], sem).start()
        return carry

    lax.fori_loop(0, tm, issue, 0)
    out = _swiglu_rows(ff_ref[0].astype(BF16), wgu_ref[...], wdn_ref[...])
    for k in range(TOP_K):
        _row_copy(yb_ref.at[pl.ds(0, tm)], rows_sc.at[k], sem).wait()
    for k in range(TOP_K):
        out = out + w_ref[:, k:k + 1] * rows_sc[k]
    xo_ref[0] = _layer_norm(DN_ALPHA * x_ref[0] + gate_ref[0] * out, g_ref[0], b_ref[0])


def _ffn_out(x, ff, yb, pos, w, tile0, sh_gu, sh_down, gate, ln_g, ln_b):
    bsz, n, d = x.shape
    tm = min(ROUTE_TOKENS, n)
    nt = n // tm
    row = pl.BlockSpec((1, tm, d), lambda b, i: (b, i, 0))
    vecs = [gate, ln_g, ln_b]
    in_specs = [pl.BlockSpec((TOP_K, tm), lambda b, i: (0, tile0 + b * nt + i), memory_space=pltpu.SMEM),
                pl.BlockSpec((tm, TOP_K), lambda b, i: (tile0 + b * nt + i, 0)),
                row, row, pl.BlockSpec(memory_space=pl.ANY),
                pl.BlockSpec(sh_gu.shape, lambda b, i: (0, 0)), pl.BlockSpec(sh_down.shape, lambda b, i: (0, 0))]
    in_specs += [_bcast_spec(a) for a in vecs]
    vmem = TOP_K * tm * d * 4 + 2 * (3 * tm * d * 4 + sh_gu.size * 2 + sh_down.size * 2) + 6 * tm * d * 4
    return pl.pallas_call(
        _ffn_out_kernel, grid=(bsz, nt), in_specs=in_specs, out_specs=row,
        out_shape=jax.ShapeDtypeStruct((bsz, n, d), F32),
        scratch_shapes=[pltpu.VMEM((TOP_K, tm, d), F32), pltpu.SemaphoreType.DMA(())],
        compiler_params=_params(("arbitrary", "arbitrary"), vmem), name="moe_combine_ffn_out",
    )(pos, w, x, ff, yb, sh_gu, sh_down, *vecs)


def _moe_dispatch_experts(tokens, router_w, router_bias, w_gu, w_down, layer):
    t = tokens.shape[0]
    idx, w, rank, counts = _router(tokens, router_w, router_bias)
    pad_start, block_e, n_valid, cap = _moe_plan(counts[:, 0], t)
    pos = _positions(idx, rank, pad_start)
    xb = _dispatch(pos, tokens, cap)
    return _moe_experts(xb, block_e, n_valid, w_gu, w_down, layer), pos, w


def _rope_tables(n):
    rows = n // GRID_W
    row = jnp.repeat(jnp.arange(rows), GRID_W).astype(F32)
    col = jnp.tile(jnp.arange(GRID_W), rows).astype(F32)
    half = MLA_ROPE // 2
    inv = ROPE_THETA ** (-jnp.arange(0, half, 2, dtype=F32) / half)
    ang = jnp.concatenate([row[:, None] * inv, col[:, None] * inv], -1)
    return jnp.cos(ang), jnp.sin(ang)


def _rope(x, cos, sin):
    x1, x2 = x[..., 0::2], x[..., 1::2]
    return jnp.stack([x1 * cos - x2 * sin, x1 * sin + x2 * cos], -1).reshape(x.shape)


def _dwconv(x, w, b):
    pad = w.shape[0] // 2
    y = lax.conv_general_dilated(x, w[:, None, :], window_strides=(1,), padding=[(pad, pad)],
                                 dimension_numbers=('NWC', 'WIO', 'NWC'), feature_group_count=x.shape[-1])
    return y + b


def _hyena_filters(n, w1, b1, f1, w2, b2, f2, w3):
    t = jnp.linspace(0.0, 1.0, n, dtype=F32)[:, None]
    w = 2.0 * math.pi * jnp.arange(n, dtype=F32)[:, None] / n
    fr = jnp.linspace(1e-4, HY_BANDS - 1, HY_BANDS, dtype=F32)
    z = jnp.concatenate([t, jnp.cos(fr * w), -jnp.sin(fr * w)], -1)
    hp = lax.Precision.HIGHEST
    h = jnp.sin(f1 * (jnp.dot(z, w1, precision=hp) + b1))
    h = jnp.sin(f2 * (jnp.dot(h, w2, precision=hp) + b2))
    h = jnp.dot(h, w3, precision=hp)
    deltas = jnp.abs(jnp.linspace(HY_DECAY_MIN, HY_DECAY_MAX, HY_WIDTH, dtype=F32))
    decay = jnp.exp(-t * deltas)
    h_f = h[:, :HY_WIDTH] * decay
    h_b = h[:, HY_WIDTH:] * decay
    k = jnp.concatenate([h_f, jnp.zeros((1, HY_WIDTH), F32), h_b[:0:-1]], 0)
    return k / jnp.sum(jnp.abs(k), 0, keepdims=True)


def _split(a):
    hi = a.astype(BF16)
    return hi, (a - hi.astype(F32)).astype(BF16)


def _dot3(a_hi, a_lo, x):
    x_hi, x_lo = _split(x)
    return (jnp.dot(a_hi, x_hi, preferred_element_type=F32) + jnp.dot(a_lo, x_hi, preferred_element_type=F32)
            + jnp.dot(a_hi, x_lo, preferred_element_type=F32))


def _cis(num, den):
    ang = (2.0 * math.pi / den) * (num % den).astype(F32)
    return jnp.cos(ang), jnp.sin(ang)


def _dft_outer_kernel(f_hi_ref, f_lo_ref, x_ref, o_ref):
    o_ref[0] = _dot3(f_hi_ref[...], f_lo_ref[...], x_ref[0])


def _dft_outer(x, f_hi, f_lo):
    bsz, k, m = x.shape
    r = f_hi.shape[0]
    tn = min(DFT_COLS, m)
    assert m % tn == 0
    vmem = 2 * (2 * f_hi.size * 2 + k * tn * 4 + r * tn * 4) + 3 * (k + r) * tn * 4
    return pl.pallas_call(
        _dft_outer_kernel, grid=(bsz, m // tn),
        in_specs=[pl.BlockSpec(f_hi.shape, lambda b, j: (0, 0)), pl.BlockSpec(f_lo.shape, lambda b, j: (0, 0)),
                  pl.BlockSpec((1, k, tn), lambda b, j: (b, 0, j))],
        out_specs=pl.BlockSpec((1, r, tn), lambda b, j: (b, 0, j)),
        out_shape=jax.ShapeDtypeStruct((bsz, r, m), F32),
        compiler_params=_params(("parallel", "parallel"), vmem), name="hyena_dft_outer",
    )(f_hi, f_lo, x)


def _dft_inner_kernel(m_hi_ref, m_lo_ref, mt_hi_ref, mt_lo_ref, a_ref, h_ref, o_ref, *, conv):
    n2 = a_ref.shape[3]
    x = a_ref[0, :, 0].reshape(2 * n2, a_ref.shape[4])
    y = _dot3(m_hi_ref[0], m_lo_ref[0], x)
    if conv:
        h = h_ref[0, :, 0].reshape(2 * n2, h_ref.shape[4])
        yr, yi, hr, hi = y[:n2], y[n2:], h[:n2], h[n2:]
        prod = jnp.concatenate([yr * hr - yi * hi, yr * hi + yi * hr], axis=0)
        y = _dot3(mt_hi_ref[0], mt_lo_ref[0], prod)
    o_ref[0, :, 0] = y.reshape(2, n2, y.shape[1])


def _dft_inner(a, h, mats, conv):
    bsz, _, n1, n2, c = a.shape
    blk = lambda sel: pl.BlockSpec((1, 2, 1, n2, c), sel)
    mat = pl.BlockSpec((1, 2 * n2, 2 * n2), lambda k, b: (k, 0, 0))
    vmem = 2 * (4 * 4 * n2 * n2 * 2 + 3 * 2 * n2 * c * 4) + 8 * 2 * n2 * c * 4
    return pl.pallas_call(
        functools.partial(_dft_inner_kernel, conv=conv), grid=(n1, bsz),
        in_specs=[mat, mat, mat, mat, blk(lambda k, b: (b, 0, k, 0, 0)), blk(lambda k, b: (0, 0, k, 0, 0))],
        out_specs=blk(lambda k, b: (b, 0, k, 0, 0)), out_shape=jax.ShapeDtypeStruct(a.shape, F32),
        compiler_params=_params(("parallel", "arbitrary"), vmem),
        name="hyena_dft_inner_conv" if conv else "hyena_dft_inner",
    )(*mats, a, h)


def _dft_final_kernel(fd_hi_ref, fd_lo_ref, a_ref, z_ref, x0_ref, skip_ref, o_ref):
    y = _dot3(fd_hi_ref[...], fd_lo_ref[...], a_ref[0])
    o_ref[0] = (y + z_ref[0] * skip_ref[...]) * x0_ref[0]


def _dft_final(a, z, x0, skip_t, fd_hi, fd_lo):
    bsz, r2, m = a.shape
    k = fd_hi.shape[0]
    tn = min(DFT_COLS, m)
    row = pl.BlockSpec((1, k, tn), lambda b, j: (b, 0, j))
    vmem = 2 * (2 * fd_hi.size * 2 + r2 * tn * 4 + 3 * k * tn * 4) + 3 * (k + r2) * tn * 4
    return pl.pallas_call(
        _dft_final_kernel, grid=(bsz, m // tn),
        in_specs=[pl.BlockSpec(fd_hi.shape, lambda b, j: (0, 0)), pl.BlockSpec(fd_lo.shape, lambda b, j: (0, 0)),
                  pl.BlockSpec((1, r2, tn), lambda b, j: (b, 0, j)), row, row,
                  pl.BlockSpec((1, tn), lambda b, j: (0, 0))],
        out_specs=row, out_shape=jax.ShapeDtypeStruct((bsz, k, m), F32),
        compiler_params=_params(("parallel", "parallel"), vmem), name="hyena_dft_final",
    )(fd_hi, fd_lo, a, z, x0, skip_t)


def _hyena_long_conv(z, x0, k, skip):
    bsz, n, c = z.shape
    n2 = DFT_INNER
    nn = 2 * n
    n1 = nn // n2
    assert nn == n1 * n2 and n1 % 2 == 0
    half = n1 // 2
    m = n2 * c
    j1 = jnp.arange(n1)
    ca, sa = _cis(j1[:, None] * j1[None, :], n1)
    fa = jnp.concatenate([ca, -sa], axis=0)
    fd = jnp.concatenate([ca, -sa], axis=1)[:half] / nn
    j2 = jnp.arange(n2)
    cb, sb = _cis(j2[None, None, :] * (n1 * j2[None, :, None] + j1[:, None, None]), nn)
    mb = jnp.concatenate([jnp.concatenate([cb, sb], 2), jnp.concatenate([-sb, cb], 2)], 1)
    mats = _split(mb) + _split(jnp.swapaxes(mb, 1, 2))
    fa_hi, fa_lo = _split(fa)
    hk = _dft_outer(k.reshape(1, n1, m), fa_hi, fa_lo).reshape(1, 2, n1, n2, c)
    hk = _dft_inner(hk, hk, mats, conv=False)
    a = _dft_outer(z.reshape(bsz, half, m), fa_hi[:, :half], fa_lo[:, :half]).reshape(bsz, 2, n1, n2, c)
    a = _dft_inner(a, hk, mats, conv=True).reshape(bsz, 2 * n1, m)
    skip_t = jnp.tile(skip.reshape(1, c), (1, min(DFT_COLS, m) // c))
    out = _dft_final(a, z.reshape(bsz, half, m), x0.reshape(bsz, half, m), skip_t, *_split(fd))
    return out.reshape(bsz, n, c)


def _short_conv_kernel(f_hi_ref, f_lo_ref, fi_hi_ref, fi_lo_ref, z_ref, x0_ref, k_ref, skip_ref, o_ref):
    n = z_ref.shape[1]
    nn = 2 * n
    z = z_ref[0]
    hk = _dot3(f_hi_ref[...], f_lo_ref[...], k_ref[...])
    zs = _dot3(f_hi_ref[:, :n], f_lo_ref[:, :n], z)
    zr, zi, hr, hi = zs[:nn], zs[nn:], hk[:nn], hk[nn:]
    prod = jnp.concatenate([zr * hr - zi * hi, zr * hi + zi * hr], axis=0)
    o_ref[0] = (_dot3(fi_hi_ref[...], fi_lo_ref[...], prod) + z * skip_ref[...]) * x0_ref[0]


def _hyena_short_conv(z, x0, k, skip):
    bsz, n, c = z.shape
    nn = 2 * n
    idx = jnp.arange(nn)
    cf, sf = _cis(idx[:, None] * idx[None, :], nn)
    f = jnp.concatenate([cf, -sf], axis=0)
    fi = jnp.concatenate([cf, -sf], axis=1)[:n] / nn
    full = lambda a: pl.BlockSpec(a.shape, lambda b: (0,) * a.ndim)
    row = pl.BlockSpec((1, n, c), lambda b: (b, 0, 0))
    ops = _split(f) + _split(fi)
    vmem = 2 * (sum(a.size * 2 for a in ops) + 3 * n * c * 4 + nn * c * 4) + 12 * 2 * nn * c * 4
    return pl.pallas_call(
        _short_conv_kernel, grid=(bsz,),
        in_specs=[full(a) for a in ops] + [row, row, full(k), pl.BlockSpec((1, c), lambda b: (0, 0))],
        out_specs=row, out_shape=jax.ShapeDtypeStruct((bsz, n, c), F32),
        compiler_params=_params(("parallel",), vmem), name="hyena_short_conv",
    )(*ops, z, x0, k, skip.reshape(1, c))


def _hyena_sequence(proj, conv_w, conv_b, filt, skip):
    n = proj.shape[1]
    u = _dwconv(proj, conv_w, conv_b)
    x0, x1, v = jnp.split(u, 3, axis=-1)
    z = v * x1
    k = _hyena_filters(n, *filt)
    conv = _hyena_short_conv if n <= HY_DIRECT_MAX else _hyena_long_conv
    return conv(z, x0, k, skip)


def _mixer_hyena_mla(x, ctx, sh_l, sc_l, sh_c, sc_c, p):
    bsz, n, _ = x.shape
    nc = ctx.shape[1]
    w_in = p['w_in'].astype(BF16)
    w_parts = [w_in[:, :OFF_Q], w_in[:, OFF_Q:OFF_KV], w_in[:, OFF_KV:OFF_KPE], w_in[:, OFF_KPE:]]
    w_qb = p['w_qb'].astype(BF16)
    w_kvb = p['w_kvb'].astype(BF16)
    qg = p['q_norm'].reshape(1, 1, -1)
    kvg = p['kv_norm'].reshape(1, 1, -1)

    def project(h, shift, scale):
        hy, ql, kvl, kpe = _linear(h, w_parts, pre="mod", pre_args=(shift, scale), name="in_proj")
        q, = _linear(ql, [w_qb], pre="rms", pre_args=(qg,), name="q_proj")
        kv, = _linear(kvl, [w_kvb], pre="rms", pre_args=(kvg,), name="kv_proj")
        m = h.shape[1]
        q = q.reshape(bsz, m, MLA_HEADS, MLA_QK).transpose(0, 2, 1, 3)
        kv = kv.reshape(bsz, m, MLA_HEADS, MLA_NOPE + MLA_V).transpose(0, 2, 1, 3)
        return hy, q[..., :MLA_NOPE], q[..., MLA_NOPE:], kv[..., :MLA_NOPE], kv[..., MLA_NOPE:], kpe

    hy_l, qn_l, qp_l, kn_l, v_l, kp_l = project(x, sh_l, sc_l)
    hy_c, qn_c, qp_c, kn_c, v_c, kp_c = project(ctx, sh_c, sc_c)
    cos, sin = _rope_tables(n)
    qp_l = _rope(qp_l, cos, sin)
    kp_l = _rope(kp_l, cos, sin)

    def heads(kp):
        return jnp.broadcast_to(kp[:, None], (bsz, MLA_HEADS) + kp.shape[1:])

    def values_t(v):
        ones = jnp.ones((bsz, MLA_HEADS, SUBLANES, v.shape[2]), F32)
        return jnp.concatenate([v.transpose(0, 1, 3, 2), ones], 2).astype(BF16)

    q_l = (jnp.concatenate([qn_l, qp_l], -1) * (MLA_SCALE * LOG2E)).astype(BF16)
    q_c = (jnp.concatenate([qn_c, qp_c], -1) * (MLA_SCALE * LOG2E)).astype(BF16)
    k_c = jnp.concatenate([kn_c, heads(kp_c)], -1).astype(BF16)
    k_l = jnp.concatenate([kn_l, heads(kp_l)], -1).astype(BF16)
    k_all = jnp.concatenate([k_c, k_l], 2)
    v_all = jnp.concatenate([v_c, v_l], 2)
    att_l = _attention(q_l, k_all, values_t(v_all), tq=min(ATT_Q_TILE, n), tk=_key_tile(nc + n))
    att_c = _attention(q_c, k_c, values_t(v_c), tq=nc, tk=_key_tile(nc))

    filt = (p['filt_w1'], p['filt_b1'], p['filt_freq1'], p['filt_w2'], p['filt_b2'], p['filt_freq2'], p['filt_w3'])
    hyo_l = _hyena_sequence(hy_l, p['conv_w'], p['conv_b'], filt, p['skip'])
    hyo_c = _hyena_sequence(hy_c, p['conv_w'], p['conv_b'], filt, p['skip'])
    return (hyo_l, att_l), (hyo_c, att_c)


def _halo_specs(n, tm, k):
    nb = n // SUBLANES
    per = tm // SUBLANES
    return [pl.BlockSpec((1, tm, k), lambda b, i: (b, i, 0)),
            pl.BlockSpec((1, SUBLANES, k), lambda b, i: (b, jnp.maximum(i * per - 1, 0), 0)),
            pl.BlockSpec((1, SUBLANES, k), lambda b, i: (b, jnp.minimum((i + 1) * per, nb - 1), 0))]


def _conv3(u, cw_ref, cb_ref, tm):
    i = pl.program_id(1)
    rows = lax.broadcasted_iota(I32, (u.shape[0], 1), 0)
    inside = jnp.logical_and(jnp.logical_or(rows >= SUBLANES, i > 0),
                             jnp.logical_or(rows < tm + SUBLANES, i < pl.num_programs(1) - 1))
    u = jnp.where(inside, u, 0.0)
    prev = pltpu.roll(u, 1, 0)[SUBLANES:SUBLANES + tm]
    nxt = pltpu.roll(u, u.shape[0] - 1, 0)[SUBLANES:SUBLANES + tm]
    return cw_ref[0:1, :] * prev + cw_ref[1:2, :] * u[SUBLANES:SUBLANES + tm] + cw_ref[2:3, :] * nxt + cb_ref[...]


def _ssd_in_kernel(xc_ref, xp_ref, xn_ref, sh_ref, sc_ref, wz_ref, wx_ref, wdt_ref, cw_ref, cb_ref, dtb_ref,
                   z_ref, xs_ref, b_ref, c_ref, dt_ref):
    tm = xc_ref.shape[1]
    rows = jnp.concatenate([xp_ref[0], xc_ref[0], xn_ref[0]], axis=0)
    h = (rows * (1.0 + sc_ref[0]) + sh_ref[0]).astype(BF16)
    hc = h[SUBLANES:SUBLANES + tm]
    z_ref[0] = jnp.dot(hc, wz_ref[...], preferred_element_type=F32)
    y = _conv3(jnp.dot(h, wx_ref[...], preferred_element_type=F32), cw_ref, cb_ref, tm)
    xbc = y * jax.nn.sigmoid(y)
    xs_ref[0] = xbc[:, :SSD_INNER]
    b_ref[0] = xbc[:, SSD_INNER:SSD_INNER + SSD_BC]
    c_ref[0] = xbc[:, SSD_INNER + SSD_BC:]
    dt = jnp.dot(hc, wdt_ref[...], preferred_element_type=F32) + dtb_ref[...]
    dt_ref[0] = (jnp.maximum(dt, 0.0) + jnp.log1p(jnp.exp(-jnp.abs(dt)))).T


def _ssd_in_proj(x, shift, scale, w_z, w_xbc, w_dt, conv_w, conv_b, dt_bias):
    bsz, n, k = x.shape
    tm = min(ROW_TILE, n)
    assert n % tm == 0
    full = lambda a: pl.BlockSpec(a.shape, lambda b, i: (0,) * a.ndim)
    row = lambda c: pl.BlockSpec((1, tm, c), lambda b, i: (b, i, 0))
    nh2 = w_dt.shape[1]
    consts = [w_z, w_xbc, w_dt, conv_w, conv_b, dt_bias]
    widths = [SSD_INNER, SSD_INNER, SSD_BC, SSD_BC]
    vmem = 2 * (tm * k * 4 + sum(a.size * a.dtype.itemsize for a in consts) + tm * (sum(widths) + nh2) * 4) \
        + 6 * (tm + 2 * SUBLANES) * SSD_XBC * 4
    return pl.pallas_call(
        _ssd_in_kernel, grid=(bsz, n // tm),
        in_specs=_halo_specs(n, tm, k) + [_bcast_spec(shift), _bcast_spec(scale)] + [full(a) for a in consts],
        out_specs=[row(c) for c in widths] + [pl.BlockSpec((1, nh2, tm), lambda b, i: (b, 0, i))],
        out_shape=[jax.ShapeDtypeStruct((bsz, n, c), F32) for c in widths]
        + [jax.ShapeDtypeStruct((bsz, nh2, n), F32)],
        compiler_params=_params(("parallel", "parallel"), vmem), name="ssd_in_proj",
    )(x, x, x, shift, scale, *consts)


def _ssd_scan_kernel(x_ref, b_ref, c_ref, dt_ref, a_ref, s0_ref, y_ref, sfin_ref, st_sc, *, reverse):
    ci = pl.program_id(2)
    q = SSD_CHUNK

    @pl.when(ci == 0)
    def _():
        st_sc[...] = s0_ref[0, 0]

    dt = dt_ref[0]
    a = dt * a_ref[0]
    si = lax.broadcasted_iota(I32, (q, q), 0)
    li = lax.broadcasted_iota(I32, (q, q), 1)
    incl = jnp.where((si >= li) if reverse else (si <= li), 1.0, 0.0)
    hp = lax.Precision.HIGHEST
    acs = jnp.dot(a, incl, precision=hp, preferred_element_type=F32)
    tot = jnp.dot(a, jnp.ones((q, LANES), F32), precision=hp, preferred_element_type=F32)
    e_in = jnp.exp(acs)
    w_end = jnp.exp(tot - acs) * dt
    e_tot = jnp.exp(tot)
    acs_t = acs.T
    e_in_t = e_in.T
    mask = (li >= si) if reverse else (li <= si)
    cmat = c_ref[0]
    cb = lax.dot_general(cmat.astype(BF16), b_ref[0].astype(BF16), (((1,), (1,)), ((), ())),
                         preferred_element_type=F32)
    bt = b_ref[0].T
    first = lax.broadcasted_iota(I32, (q, LANES), 1) < SSD_HEADDIM
    ys = []
    for pr in range(SSD_HPG // 2):
        lo, hi = pr * LANES, (pr + 1) * LANES
        x_pair = x_ref[0, :, lo:hi].astype(BF16)
        st_pair = st_sc[:, lo:hi]
        rhs = jnp.concatenate([x_pair, st_pair.astype(BF16)], axis=0)
        y2, s2 = [], []
        for r in (2 * pr, 2 * pr + 1):
            seg = acs_t[:, r:r + 1] - acs[r:r + 1, :]
            m = cb * jnp.exp(jnp.where(mask, seg, -jnp.inf)) * dt[r:r + 1, :]
            lhs = jnp.concatenate([m.astype(BF16), (cmat * e_in_t[:, r:r + 1]).astype(BF16)], axis=1)
            y2.append(jnp.dot(lhs, rhs, preferred_element_type=F32))
            btr = (bt * w_end[r:r + 1, :]).astype(BF16)
            s2.append(e_tot[r:r + 1, :] * st_pair + jnp.dot(btr, x_pair, preferred_element_type=F32))
        ys.append(jnp.where(first, y2[0], y2[1]))
        st_sc[:, lo:hi] = jnp.where(first, s2[0], s2[1])
    y_ref[0] = jnp.concatenate(ys, axis=1)

    @pl.when(ci == pl.num_programs(2) - 1)
    def _():
        sfin_ref[0, 0] = st_sc[...]


def _ssd_scan(xs, bm, cm, dt_t, a_rep, s0, reverse):
    bsz, n, _ = xs.shape
    q = SSD_CHUNK
    nc = n // q
    assert n % q == 0 and q == LANES
    d = 1 if reverse else 0
    cc = (lambda c: nc - 1 - c) if reverse else (lambda c: c)
    gw = SSD_HPG * SSD_HEADDIM
    state = pl.BlockSpec((1, 1, SSD_STATE, gw), lambda b, g, c: (b, g, 0, 0))
    vmem = 2 * (2 * q * gw * 4 + 2 * q * SSD_STATE * 4 + 2 * SSD_STATE * gw * 4) + SSD_STATE * gw * 4 + 64 * q * q * 4
    return pl.pallas_call(
        functools.partial(_ssd_scan_kernel, reverse=reverse), grid=(bsz, SSD_GROUPS, nc),
        in_specs=[pl.BlockSpec((1, q, gw), lambda b, g, c: (b, cc(c), g)),
                  pl.BlockSpec((1, q, SSD_STATE), lambda b, g, c: (b, cc(c), g)),
                  pl.BlockSpec((1, q, SSD_STATE), lambda b, g, c: (b, cc(c), g)),
                  pl.BlockSpec((1, SSD_HPG, q), lambda b, g, c: (b, d * SSD_GROUPS + g, cc(c))),
                  pl.BlockSpec((1, SSD_HPG, LANES), lambda b, g, c: (d * SSD_GROUPS + g, 0, 0)),
                  state],
        out_specs=[pl.BlockSpec((1, q, gw), lambda b, g, c: (b, cc(c), g)), state],
        out_shape=[jax.ShapeDtypeStruct((bsz, n, SSD_INNER), F32),
                   jax.ShapeDtypeStruct((bsz, SSD_GROUPS, SSD_STATE, gw), F32)],
        scratch_shapes=[pltpu.VMEM((SSD_STATE, gw), F32)],
        compiler_params=_params(("parallel", "parallel", "arbitrary"), vmem),
        name="ssd_scan_rev" if reverse else "ssd_scan_fwd",
    )(xs, bm, cm, dt_t, a_rep, s0)


def _ssd_out_kernel(x_ref, yf_ref, yb_ref, xs_ref, z_ref, d_ref, ng_ref, w_ref, gate_ref, g_ref, b_ref, sh_ref,
                    sc_ref, xo_ref, ff_ref):
    z = z_ref[0]
    y = (yf_ref[0] + yb_ref[0] + xs_ref[0] * d_ref[0]) * (z * jax.nn.sigmoid(z))
    gw = SSD_INNER // SSD_GROUPS
    parts = []
    for g in range(SSD_GROUPS):
        yg = y[:, g * gw:(g + 1) * gw]
        parts.append(yg * lax.rsqrt(jnp.mean(yg * yg, -1, keepdims=True) + RMS_EPS))
    yn = (jnp.concatenate(parts, axis=1) * ng_ref[0]).astype(BF16)
    mix = jnp.dot(yn, w_ref[...], preferred_element_type=F32)
    xn = _layer_norm(DN_ALPHA * x_ref[0] + gate_ref[0] * mix, g_ref[0], b_ref[0])
    xo_ref[0] = xn
    ff_ref[0] = xn * (1.0 + sc_ref[0]) + sh_ref[0]


def _ssd_out(x, y_f, y_b, xs, z, d_rep, norm_g, w_out, gate, ln_g, ln_b, shift2, scale2):
    bsz, n, d = x.shape
    tm = min(ROW_TILE, n)
    assert n % tm == 0
    row = lambda c: pl.BlockSpec((1, tm, c), lambda b, i: (b, i, 0))
    vecs = [gate, ln_g, ln_b, shift2, scale2]
    in_specs = [row(d)] + [row(SSD_INNER)] * 4 + [_bcast_spec(d_rep), _bcast_spec(norm_g),
                                                 pl.BlockSpec(w_out.shape, lambda b, i: (0, 0))]
    in_specs += [_bcast_spec(a) for a in vecs]
    vmem = 2 * (3 * tm * d * 4 + 4 * tm * SSD_INNER * 4 + w_out.size * 2) + 4 * tm * SSD_INNER * 4
    return pl.pallas_call(
        _ssd_out_kernel, grid=(bsz, n // tm), in_specs=in_specs, out_specs=[row(d), row(d)],
        out_shape=[jax.ShapeDtypeStruct((bsz, n, d), F32)] * 2,
        compiler_params=_params(("parallel", "parallel"), vmem), name="ssd_out",
    )(x, y_f, y_b, xs, z, d_rep, norm_g, w_out, *vecs)


def _mixer_ssd(x, ctx, sh_l, sc_l, sh_c, sc_c, p):
    a_all = -jnp.exp(jnp.concatenate([p['a_log_f'], p['a_log_b']]))
    a_rep = jnp.broadcast_to(a_all.reshape(2 * SSD_GROUPS, SSD_HPG, 1), (2 * SSD_GROUPS, SSD_HPG, LANES))
    w_in = p['w_in'].astype(BF16)
    consts = (w_in[:, :SSD_INNER], w_in[:, SSD_INNER:SSD_INNER + SSD_XBC], w_in[:, SSD_INNER + SSD_XBC:],
              p['conv_w'], p['conv_b'][None], jnp.concatenate([p['dt_bias_f'], p['dt_bias_b']])[None])
    _, xc, bc, cc, dtc = _ssd_in_proj(ctx, sh_c, sc_c, *consts)
    zl, xl, bl, cl, dtl = _ssd_in_proj(x, sh_l, sc_l, *consts)
    s0 = jnp.zeros((ctx.shape[0], SSD_GROUPS, SSD_STATE, SSD_HPG * SSD_HEADDIM), F32)
    _, sc_f = _ssd_scan(xc, bc, cc, dtc, a_rep, s0, reverse=False)
    _, sc_b = _ssd_scan(xc, bc, cc, dtc, a_rep, s0, reverse=True)
    y_f, _ = _ssd_scan(xl, bl, cl, dtl, a_rep, sc_f, reverse=False)
    y_b, _ = _ssd_scan(xl, bl, cl, dtl, a_rep, sc_b, reverse=True)
    return y_f, y_b, xl, zl


def kernel(x, c, ctx, c_ctx, mod_w, mod_b, ln_mix_g, ln_mix_b, ln_ffn_g, ln_ffn_b, a_w_in, hy_conv_w, hy_conv_b, hy_filt_w1, hy_filt_b1, hy_filt_freq1, hy_filt_w2, hy_filt_b2, hy_filt_freq2, hy_filt_w3, hy_skip, mla_q_norm, mla_w_qb, mla_kv_norm, mla_w_kvb, a_w_out, ssd_w_in, ssd_conv_w, ssd_conv_b, ssd_dt_bias_f, ssd_dt_bias_b, ssd_a_log_f, ssd_a_log_b, ssd_d, ssd_norm_g, ssd_w_out, router_w, router_bias, exp_w_gu, exp_w_down, sh_w_gu, sh_w_down):
    bsz, n_lat, d = x.shape
    n_ctx = ctx.shape[1]
    hp = lax.Precision.HIGHEST
    for l in range(DEPTH):
        last = l == DEPTH - 1
        i = l // 2
        mod = (jnp.dot(jax.nn.silu(c), mod_w[l], precision=hp) + mod_b[l]).reshape(bsz, N_MOD, 1, d)
        mod_c = (jnp.dot(jax.nn.silu(c_ctx), mod_w[l], precision=hp) + mod_b[l]).reshape(1, N_MOD, 1, d)
        sh1, sc1, g1, sh2, sc2, g2 = [mod[:, j] for j in range(N_MOD)]
        csh1, csc1, cg1, csh2, csc2, cg2 = [mod_c[:, j] for j in range(N_MOD)]
        vec = lambda a: a.reshape(1, 1, d)
        if l % 2 == 0:
            p = {"w_in": a_w_in[i], "conv_w": hy_conv_w[i], "conv_b": hy_conv_b[i],
                 "filt_w1": hy_filt_w1[i], "filt_b1": hy_filt_b1[i], "filt_freq1": hy_filt_freq1[i],
                 "filt_w2": hy_filt_w2[i], "filt_b2": hy_filt_b2[i], "filt_freq2": hy_filt_freq2[i],
                 "filt_w3": hy_filt_w3[i], "skip": hy_skip[i], "q_norm": mla_q_norm[i], "w_qb": mla_w_qb[i],
                 "kv_norm": mla_kv_norm[i], "w_kvb": mla_w_kvb[i]}
            ys_l, ys_c = _mixer_hyena_mla(x, ctx, sh1, sc1, csh1, csc1, p)
            w_out = a_w_out[i].astype(BF16)
            ws = [w_out[:HY_WIDTH], w_out[HY_WIDTH:]]
            x, ff_x = _mix_out(x, ys_l, ws, g1, vec(ln_mix_g[l]), vec(ln_mix_b[l]), sh2, sc2)
        else:
            p = {"w_in": ssd_w_in[i], "conv_w": ssd_conv_w[i], "conv_b": ssd_conv_b[i],
                 "dt_bias_f": ssd_dt_bias_f[i], "dt_bias_b": ssd_dt_bias_b[i],
                 "a_log_f": ssd_a_log_f[i], "a_log_b": ssd_a_log_b[i]}
            assert last
            y_f, y_b, xs, z = _mixer_ssd(x, ctx, sh1, sc1, csh1, csc1, p)
            d_rep = jnp.repeat(ssd_d[i], SSD_HEADDIM).reshape(1, 1, SSD_INNER)
            x, ff_x = _ssd_out(x, y_f, y_b, xs, z, d_rep, ssd_norm_g[i].reshape(1, 1, SSD_INNER),
                               ssd_w_out[i].astype(BF16), g1, vec(ln_mix_g[l]), vec(ln_mix_b[l]), sh2, sc2)
        sh_gu = sh_w_gu[l].astype(BF16)
        sh_down = sh_w_down[l].astype(BF16)
        ln_g, ln_b = vec(ln_ffn_g[l]), vec(ln_ffn_b[l])
        moe_w = (router_w[l], router_bias[l], exp_w_gu, exp_w_down, l)
        if last:
            yb, pos, w = _moe_dispatch_experts(ff_x.reshape(-1, d), *moe_w)
            x = _ffn_out(x, ff_x, yb, pos, w, 0, sh_gu, sh_down, g2, ln_g, ln_b)
        else:
            ctx, ff_c = _mix_out(ctx, ys_c, ws, cg1, vec(ln_mix_g[l]), vec(ln_mix_b[l]), csh2, csc2)
            tokens = jnp.concatenate([ff_c.reshape(-1, d), ff_x.reshape(-1, d)], 0)
            yb, pos, w = _moe_dispatch_experts(tokens, *moe_w)
            assert (bsz * n_ctx) % ROUTE_TOKENS == 0
            ctx = _ffn_out(ctx, ff_c, yb, pos, w, 0, sh_gu, sh_down, cg2, ln_g, ln_b)
            x = _ffn_out(x, ff_x, yb, pos, w, bsz * n_ctx // ROUTE_TOKENS, sh_gu, sh_down, g2, ln_g, ln_b)
    return x
```

```python
import functools
import math

import jax
import jax.numpy as jnp
from jax import lax
from jax.experimental import pallas as pl
from jax.experimental.pallas import tpu as pltpu

F32 = jnp.float32
BF16 = jnp.bfloat16
I32 = jnp.int32

D_MODEL = 1024
DEPTH = 2
GRID_W = 64
N_MOD = 6

HY_WIDTH = 512
HY_EMB = 33
HY_BANDS = (HY_EMB - 1) // 2
HY_TARGET = 1e-2
HY_FAST_DECAY = 0.3
HY_SLOW_DECAY = 1.5
HY_DECAY_MIN = math.log(HY_TARGET) / HY_SLOW_DECAY
HY_DECAY_MAX = math.log(HY_TARGET) / HY_FAST_DECAY

MLA_HEADS = 8
MLA_NOPE = 64
MLA_ROPE = 32
MLA_V = 64
MLA_Q_RANK = 256
MLA_KV_RANK = 128
MLA_QK = MLA_NOPE + MLA_ROPE
MLA_SCALE = MLA_QK ** -0.5
ROPE_THETA = 10000.0
LOG2E = math.log2(math.e)

OFF_Q = 3 * HY_WIDTH
OFF_KV = OFF_Q + MLA_Q_RANK
OFF_KPE = OFF_KV + MLA_KV_RANK

SSD_INNER = 2 * D_MODEL
SSD_HEADDIM = 64
SSD_HEADS = SSD_INNER // SSD_HEADDIM
SSD_GROUPS = 4
SSD_STATE = 128
SSD_CHUNK = 128
SSD_BC = SSD_GROUPS * SSD_STATE
SSD_XBC = SSD_INNER + 2 * SSD_BC
SSD_HPG = SSD_HEADS // SSD_GROUPS

N_EXPERTS = 256
TOP_K = 8
N_EXPERT_GROUPS = 8
TOPK_GROUPS = 4
EXPERT_DIM = 256
ROUTED_SCALE = 2.5

DN_ALPHA = (2 * DEPTH) ** 0.25
LN_EPS = 1e-5
RMS_EPS = 1e-6

LANES = 128
SUBLANES = 8
V7X_VMEM_CAP = 56 * 1024 * 1024

MOE_ROWS = 256
ROUTE_TOKENS = 256
ROW_TILE = 256
ATT_Q_TILE = 2048
ATT_HEADS_PER_STEP = 2
ATT_ONES_ROWS = 16
MLA_SLAB = LANES
DFT_INNER = 128
DFT_COLS = 2048
HY_DIRECT_MAX = 512


def _params(semantics, vmem_bytes, **kw):
    limit = int(min(max(vmem_bytes * 5 // 4, 32 * 1024 * 1024), V7X_VMEM_CAP))
    return pltpu.CompilerParams(dimension_semantics=semantics, vmem_limit_bytes=limit, **kw)


def _bcast_spec(a):
    if a.shape[0] == 1:
        return pl.BlockSpec((1, 1, a.shape[2]), lambda b, i: (0, 0, 0))
    return pl.BlockSpec((1, 1, a.shape[2]), lambda b, i: (b, 0, 0))


def _layer_norm(r, g, b):
    mu = jnp.mean(r, -1, keepdims=True)
    c = r - mu
    var = jnp.mean(c * c, -1, keepdims=True)
    return c * lax.rsqrt(var + LN_EPS) * g + b


def _swiglu_rows(xb, w_gu, w_down):
    h = jnp.dot(xb, w_gu, preferred_element_type=F32)
    half = h.shape[1] // 2
    g, u = h[:, :half], h[:, half:]
    a = (g * jax.nn.sigmoid(g) * u).astype(BF16)
    return jnp.dot(a, w_down, preferred_element_type=F32)


def _mix_out_kernel(*refs, n_y):
    x_ref = refs[0]
    y_refs = refs[1:1 + n_y]
    w_refs = refs[1 + n_y:1 + 2 * n_y]
    gate_ref, g_ref, b_ref, sh_ref, sc_ref, xo_ref, ff_ref = refs[1 + 2 * n_y:]
    y = None
    for y_ref, w_ref in zip(y_refs, w_refs):
        t = jnp.dot(y_ref[0].astype(BF16), w_ref[...], preferred_element_type=F32)
        y = t if y is None else y + t
    xn = _layer_norm(DN_ALPHA * x_ref[0] + gate_ref[0] * y, g_ref[0], b_ref[0])
    xo_ref[0] = xn
    ff_ref[0] = xn * (1.0 + sc_ref[0]) + sh_ref[0]


def _mix_out(x, ys, ws, gate, ln_g, ln_b, shift2, scale2):
    bsz, n, d = x.shape
    tm = min(ROW_TILE, n)
    assert n % tm == 0
    row = lambda c: pl.BlockSpec((1, tm, c), lambda b, i: (b, i, 0))
    vecs = [gate, ln_g, ln_b, shift2, scale2]
    in_specs = [row(d)] + [row(y.shape[2]) for y in ys]
    in_specs += [pl.BlockSpec(w.shape, lambda b, i: (0, 0)) for w in ws]
    in_specs += [_bcast_spec(a) for a in vecs]
    vmem = 2 * (3 * tm * d * 4 + sum(tm * y.shape[2] * 4 + w.size * 2 for y, w in zip(ys, ws)))
    return pl.pallas_call(
        functools.partial(_mix_out_kernel, n_y=len(ys)),
        grid=(bsz, n // tm), in_specs=in_specs, out_specs=[row(d), row(d)],
        out_shape=[jax.ShapeDtypeStruct((bsz, n, d), F32)] * 2,
        compiler_params=_params(("parallel", "parallel"), vmem), name="mix_out",
    )(x, *ys, *ws, *vecs)


def _attn_kernel(q_ref, k_ref, vt_ref, o_ref, *, tk, hp):
    nk = k_ref.shape[2]
    tq = q_ref.shape[2]
    dva = vt_ref.shape[2]
    dv = dva - ATT_ONES_ROWS

    def body(j, carry):
        off = pl.multiple_of(j * tk, tk)
        new = []
        for h in range(hp):
            m_prev, acc = carry[h]
            st = lax.dot_general(k_ref[0, h, pl.ds(off, tk), :], q_ref[0, h], (((1,), (1,)), ((), ())),
                                 preferred_element_type=F32)
            m_new = jnp.maximum(m_prev, jnp.max(st, 0, keepdims=True))
            p = jnp.exp2(st - m_new).astype(BF16)
            alpha = jnp.exp2(m_prev - m_new)
            acc = alpha * acc + jnp.dot(vt_ref[0, h, :, pl.ds(off, tk)], p, preferred_element_type=F32)
            new.append((m_new, acc))
        return tuple(new)

    init = tuple((jnp.full((1, tq), -jnp.inf, F32), jnp.zeros((dva, tq), F32)) for _ in range(hp))
    fin = lax.fori_loop(0, nk // tk, body, init)
    outs = [acc[:dv] / acc[dv:dv + 1] for _, acc in fin]
    o_ref[0] = jnp.concatenate(outs, 0).T


def _attention(q, k, vt, tq, tk):
    bsz, h, nq, dk = q.shape
    nk, dva = k.shape[2], vt.shape[2]
    dv = dva - ATT_ONES_ROWS
    hp = ATT_HEADS_PER_STEP
    assert nq % tq == 0 and nk % tk == 0 and h % hp == 0
    vmem = 2 * hp * (tq * LANES * 2 + nk * LANES * 2 + dva * nk * 2) + 2 * tq * hp * dv * 4 + 6 * hp * tk * tq * 4
    return pl.pallas_call(
        functools.partial(_attn_kernel, tk=tk, hp=hp), grid=(bsz, h // hp, nq // tq),
        in_specs=[pl.BlockSpec((1, hp, tq, dk), lambda b, g, i: (b, g, i, 0)),
                  pl.BlockSpec((1, hp, nk, dk), lambda b, g, i: (b, g, 0, 0)),
                  pl.BlockSpec((1, hp, dva, nk), lambda b, g, i: (b, g, 0, 0))],
        out_specs=pl.BlockSpec((1, tq, hp * dv), lambda b, g, i: (b, i, g)),
        out_shape=jax.ShapeDtypeStruct((bsz, nq, h * dv), F32),
        compiler_params=_params(("parallel", "parallel", "arbitrary"), vmem), name="mla_attention",
    )(q, k, vt)


def _key_tile(nk):
    for t in (768, 512, 384, 256, 128):
        if nk % t == 0:
            return t
    return nk


def _router_kernel(x_ref, wt_ref, bias_ref, upper_ref, idx_ref, w_ref, rank_ref, cnt_ref, run_sc):
    i = pl.program_id(0)
    tm = x_ref.shape[0]

    @pl.when(i == 0)
    def _():
        run_sc[...] = jnp.zeros(run_sc.shape, F32)

    logits = lax.dot_general(wt_ref[...], x_ref[...], (((1,), (1,)), ((), ())),
                             precision=lax.Precision.HIGHEST, preferred_element_type=F32)
    sc = jax.nn.sigmoid(logits)
    ch = sc + bias_ref[:, :1]
    neg = -jnp.inf
    chg = ch.reshape(N_EXPERT_GROUPS, N_EXPERTS // N_EXPERT_GROUPS, tm)
    m1 = jnp.max(chg, axis=1)
    eq = chg == m1[:, None, :]
    cnt = jnp.sum(eq.astype(F32), axis=1)
    m2 = jnp.max(jnp.where(eq, neg, chg), axis=1)
    g2 = m1 + jnp.where(cnt >= 2.0, m1, m2)
    gi = lax.broadcasted_iota(I32, g2.shape, 0)
    beaten = jnp.zeros(g2.shape, F32)
    for g in range(N_EXPERT_GROUPS):
        row = g2[g:g + 1, :]
        beaten = beaten + jnp.where(row > g2, 1.0, jnp.where(row == g2, jnp.where(gi > g, 1.0, 0.0), 0.0))
    keep = beaten < float(TOPK_GROUPS)
    cur = jnp.where(keep[:, None, :], chg, neg).reshape(N_EXPERTS, tm)
    eidx = lax.broadcasted_iota(I32, (N_EXPERTS, tm), 0)
    multi = jnp.zeros((N_EXPERTS, tm), F32)
    hits, idx_rows, w_rows = [], [], []
    for _ in range(TOP_K):
        m = jnp.max(cur, axis=0, keepdims=True)
        sel = jnp.min(jnp.where(cur == m, eidx, N_EXPERTS), axis=0, keepdims=True)
        hit = eidx == sel
        idx_rows.append(sel)
        w_rows.append(jnp.sum(jnp.where(hit, sc, 0.0), axis=0, keepdims=True))
        cur = jnp.where(hit, neg, cur)
        multi = multi + jnp.where(hit, 1.0, 0.0)
        hits.append(hit)
    base = jnp.concatenate([run_sc[...]] * (tm // LANES), axis=1)
    before = jnp.dot(multi.astype(BF16), upper_ref[...], preferred_element_type=F32) + base
    rank_rows = [jnp.sum(jnp.where(hit, before, 0.0), axis=0, keepdims=True) for hit in hits]
    w = jnp.concatenate(w_rows, axis=0)
    idx_ref[...] = jnp.concatenate(idx_rows, axis=0)
    w_ref[...] = (w / jnp.sum(w, axis=0, keepdims=True) * ROUTED_SCALE).T
    rank_ref[...] = jnp.concatenate(rank_rows, axis=0).astype(I32)
    run_sc[...] = run_sc[...] + jnp.dot(multi.astype(BF16), jnp.ones((tm, LANES), BF16), preferred_element_type=F32)
    cnt_ref[...] = run_sc[...]


def _router(tokens, router_w, router_bias):
    t, d = tokens.shape
    tm = ROUTE_TOKENS
    assert t % tm == 0
    wt = router_w.T
    bias = jnp.broadcast_to(router_bias.astype(F32)[:, None], (N_EXPERTS, LANES))
    r = jnp.arange(tm)
    upper = (r[:, None] < r[None, :]).astype(BF16)
    col = pl.BlockSpec((TOP_K, tm), lambda i: (0, i))
    full = lambda a: pl.BlockSpec(a.shape, lambda i: (0,) * a.ndim)
    vmem = 2 * (tm * d * 4 + wt.size * 4) + 40 * N_EXPERTS * tm * 4
    return pl.pallas_call(
        _router_kernel, grid=(t // tm,),
        in_specs=[pl.BlockSpec((tm, d), lambda i: (i, 0)), full(wt), full(bias), full(upper)],
        out_specs=[col, pl.BlockSpec((tm, TOP_K), lambda i: (i, 0)), col,
                   pl.BlockSpec((N_EXPERTS, LANES), lambda i: (0, 0))],
        out_shape=[jax.ShapeDtypeStruct((TOP_K, t), I32), jax.ShapeDtypeStruct((t, TOP_K), F32),
                   jax.ShapeDtypeStruct((TOP_K, t), I32), jax.ShapeDtypeStruct((N_EXPERTS, LANES), F32)],
        scratch_shapes=[pltpu.VMEM((N_EXPERTS, LANES), F32)],
        compiler_params=_params(("arbitrary",), vmem), name="moe_router",
    )(tokens, wt, bias, upper)


def _positions_kernel(idx_ref, rank_ref, start_ref, pos_ref):
    tm = idx_ref.shape[1]
    eidx = lax.broadcasted_iota(I32, (N_EXPERTS, tm), 0)
    start = jnp.concatenate([start_ref[...]] * (tm // LANES), axis=1)
    rows = [jnp.sum(jnp.where(eidx == idx_ref[k:k + 1, :], start, 0), axis=0, keepdims=True) for k in range(TOP_K)]
    pos_ref[...] = jnp.concatenate(rows, axis=0) + rank_ref[...]


def _positions(idx, rank, pad_start):
    t = idx.shape[1]
    tm = ROUTE_TOKENS
    start = jnp.broadcast_to(pad_start.astype(I32)[:, None], (N_EXPERTS, LANES))
    col = pl.BlockSpec((TOP_K, tm), lambda i: (0, i))
    return pl.pallas_call(
        _positions_kernel, grid=(t // tm,),
        in_specs=[col, col, pl.BlockSpec((N_EXPERTS, LANES), lambda i: (0, 0))], out_specs=col,
        out_shape=jax.ShapeDtypeStruct((TOP_K, t), I32),
        compiler_params=_params(("parallel",), 8 * N_EXPERTS * tm * 4), name="moe_positions",
    )(idx, rank, start)


def _moe_plan(counts, t):
    n_blocks = -(-(t * TOP_K) // MOE_ROWS) + N_EXPERTS
    c = counts.astype(I32)
    padded = (c + MOE_ROWS - 1) // MOE_ROWS * MOE_ROWS
    pad_end = jnp.cumsum(padded)
    pad_start = pad_end - padded
    block_e = jnp.minimum(jnp.searchsorted(pad_end, jnp.arange(n_blocks, dtype=I32) * MOE_ROWS, side='right'),
                          N_EXPERTS - 1).astype(I32)
    n_valid = (pad_end[-1:] // MOE_ROWS).astype(I32)
    return pad_start, block_e, n_valid, n_blocks * MOE_ROWS


def _row_copy(src, dst, sem):
    return pltpu.make_async_copy(src, dst, sem)


def _dispatch_kernel(pos_ref, x_ref, buf_in_ref, buf_ref, sem):
    del buf_in_ref
    tm = x_ref.shape[0]

    def issue(t, carry):
        for k in range(TOP_K):
            _row_copy(x_ref.at[pl.ds(t, 1)], buf_ref.at[pl.ds(pos_ref[k, t], 1)], sem).start()
        return carry

    lax.fori_loop(0, tm, issue, 0)
    for k in range(TOP_K):
        _row_copy(x_ref, buf_ref.at[pl.ds(0, tm)], sem).wait()


def _dispatch(pos, tokens, cap):
    t, d = tokens.shape
    tm = ROUTE_TOKENS
    zeros = jnp.zeros((cap, d), tokens.dtype)
    return pl.pallas_call(
        _dispatch_kernel, grid=(t // tm,),
        in_specs=[pl.BlockSpec((TOP_K, tm), lambda i: (0, i), memory_space=pltpu.SMEM),
                  pl.BlockSpec((tm, d), lambda i: (i, 0)),
                  pl.BlockSpec(memory_space=pl.ANY)],
        out_specs=pl.BlockSpec(memory_space=pl.ANY),
        out_shape=jax.ShapeDtypeStruct((cap, d), tokens.dtype),
        scratch_shapes=[pltpu.SemaphoreType.DMA(())],
        input_output_aliases={2: 0},
        compiler_params=_params(("arbitrary",), 2 * tm * d * 4, has_side_effects=True),
        name="moe_dispatch",
    )(pos, tokens, zeros)


def _experts_kernel(be_ref, nv_ref, x_ref, wgu_ref, wdn_ref, o_ref, wgu_sc, wdn_sc):
    i = pl.program_id(0)

    @pl.when(i < nv_ref[0])
    def _():
        @pl.when(jnp.logical_or(i == 0, be_ref[i] != be_ref[jnp.maximum(i - 1, 0)]))
        def _():
            wgu_sc[...] = wgu_ref[0, 0].astype(BF16)
            wdn_sc[...] = wdn_ref[0, 0].astype(BF16)

        o_ref[...] = _swiglu_rows(x_ref[...].astype(BF16), wgu_sc[...], wdn_sc[...])

    @pl.when(i >= nv_ref[0])
    def _():
        o_ref[...] = jnp.zeros(o_ref.shape, F32)


def _moe_experts(xb, block_e, n_valid, w_gu, w_down, layer):
    cap, d = xb.shape
    gu = w_gu.shape[3]
    ed = w_down.shape[2]
    live = lambda i, be, nv: jnp.maximum(jnp.minimum(i, nv[0] - 1), 0)
    grid_spec = pltpu.PrefetchScalarGridSpec(
        num_scalar_prefetch=2, grid=(cap // MOE_ROWS,),
        in_specs=[pl.BlockSpec((MOE_ROWS, d), lambda i, be, nv: (live(i, be, nv), 0)),
                  pl.BlockSpec((1, 1, d, gu), lambda i, be, nv: (layer, be[i], 0, 0)),
                  pl.BlockSpec((1, 1, ed, d), lambda i, be, nv: (layer, be[i], 0, 0))],
        out_specs=pl.BlockSpec((MOE_ROWS, d), lambda i, be, nv: (i, 0)),
        scratch_shapes=[pltpu.VMEM((d, gu), BF16), pltpu.VMEM((ed, d), BF16)])
    vmem = 2 * (2 * MOE_ROWS * d * 4 + d * gu * 4 + ed * d * 4) + (d * gu + ed * d) * 2 + 4 * MOE_ROWS * d * 4
    return pl.pallas_call(
        _experts_kernel, grid_spec=grid_spec, out_shape=jax.ShapeDtypeStruct((cap, d), F32),
        compiler_params=_params(("arbitrary",), vmem), name="moe_experts",
    )(block_e, n_valid, xb, w_gu, w_down)


def _ffn_out_kernel(pos_ref, w_ref, x_ref, ff_ref, yb_ref, wgu_ref, wdn_ref, gate_ref, g_ref, b_ref, xo_ref,
                    rows_sc, sem):
    tm = x_ref.shape[1]

    def issue(t, carry):
        for k in range(TOP_K):
            _row_copy(yb_ref.at[pl.ds(pos_ref[k, t], 1)], rows_sc.at[k, pl.ds(t, 1)], sem).start()
        return carry

    lax.fori_loop(0, tm, issue, 0)
    out = _swiglu_rows(ff_ref[0].astype(BF16), wgu_ref[...], wdn_ref[...])
    for k in range(TOP_K):
        _row_copy(yb_ref.at[pl.ds(0, tm)], rows_sc.at[k], sem).wait()
    for k in range(TOP_K):
        out = out + w_ref[:, k:k + 1] * rows_sc[k]
    xo_ref[0] = _layer_norm(DN_ALPHA * x_ref[0] + gate_ref[0] * out, g_ref[0], b_ref[0])


def _ffn_out(x, ff, yb, pos, w, tile0, sh_gu, sh_down, gate, ln_g, ln_b):
    bsz, n, d = x.shape
    tm = min(ROUTE_TOKENS, n)
    nt = n // tm
    row = pl.BlockSpec((1, tm, d), lambda b, i: (b, i, 0))
    vecs = [gate, ln_g, ln_b]
    in_specs = [pl.BlockSpec((TOP_K, tm), lambda b, i: (0, tile0 + b * nt + i), memory_space=pltpu.SMEM),
                pl.BlockSpec((tm, TOP_K), lambda b, i: (tile0 + b * nt + i, 0)),
                row, row, pl.BlockSpec(memory_space=pl.ANY),
                pl.BlockSpec(sh_gu.shape, lambda b, i: (0, 0)), pl.BlockSpec(sh_down.shape, lambda b, i: (0, 0))]
    in_specs += [_bcast_spec(a) for a in vecs]
    vmem = TOP_K * tm * d * 4 + 2 * (3 * tm * d * 4 + sh_gu.size * 2 + sh_down.size * 2) + 6 * tm * d * 4
    return pl.pallas_call(
        _ffn_out_kernel, grid=(bsz, nt), in_specs=in_specs, out_specs=row,
        out_shape=jax.ShapeDtypeStruct((bsz, n, d), F32),
        scratch_shapes=[pltpu.VMEM((TOP_K, tm, d), F32), pltpu.SemaphoreType.DMA(())],
        compiler_params=_params(("arbitrary", "arbitrary"), vmem), name="moe_combine_ffn_out",
    )(pos, w, x, ff, yb, sh_gu, sh_down, *vecs)


def _moe_dispatch_experts(tokens, router_w, router_bias, w_gu, w_down, layer):
    t = tokens.shape[0]
    idx, w, rank, counts = _router(tokens, router_w, router_bias)
    pad_start, block_e, n_valid, cap = _moe_plan(counts[:, 0], t)
    pos = _positions(idx, rank, pad_start)
    xb = _dispatch(pos, tokens, cap)
    return _moe_experts(xb, block_e, n_valid, w_gu, w_down, layer), pos, w


def _rope_tables(n):
    rows = n // GRID_W
    row = jnp.repeat(jnp.arange(rows), GRID_W).astype(F32)
    col = jnp.tile(jnp.arange(GRID_W), rows).astype(F32)
    half = MLA_ROPE // 2
    inv = ROPE_THETA ** (-jnp.arange(0, half, 2, dtype=F32) / half)
    ang = jnp.concatenate([row[:, None] * inv, col[:, None] * inv], -1)
    return jnp.cos(ang), jnp.sin(ang)


def _hyena_filters(n, w1, b1, f1, w2, b2, f2, w3):
    t = jnp.linspace(0.0, 1.0, n, dtype=F32)[:, None]
    w = 2.0 * math.pi * jnp.arange(n, dtype=F32)[:, None] / n
    fr = jnp.linspace(1e-4, HY_BANDS - 1, HY_BANDS, dtype=F32)
    z = jnp.concatenate([t, jnp.cos(fr * w), -jnp.sin(fr * w)], -1)
    hp = lax.Precision.HIGHEST
    h = jnp.sin(f1 * (jnp.dot(z, w1, precision=hp) + b1))
    h = jnp.sin(f2 * (jnp.dot(h, w2, precision=hp) + b2))
    h = jnp.dot(h, w3, precision=hp)
    deltas = jnp.abs(jnp.linspace(HY_DECAY_MIN, HY_DECAY_MAX, HY_WIDTH, dtype=F32))
    decay = jnp.exp(-t * deltas)
    h_f = h[:, :HY_WIDTH] * decay
    h_b = h[:, HY_WIDTH:] * decay
    k = jnp.concatenate([h_f, jnp.zeros((1, HY_WIDTH), F32), h_b[:0:-1]], 0)
    return k / jnp.sum(jnp.abs(k), 0, keepdims=True)


def _split(a):
    hi = a.astype(BF16)
    return hi, (a - hi.astype(F32)).astype(BF16)


def _dot3(a_hi, a_lo, x):
    x_hi, x_lo = _split(x)
    return (jnp.dot(a_hi, x_hi, preferred_element_type=F32) + jnp.dot(a_lo, x_hi, preferred_element_type=F32)
            + jnp.dot(a_hi, x_lo, preferred_element_type=F32))


def _cis(num, den):
    ang = (2.0 * math.pi / den) * (num % den).astype(F32)
    return jnp.cos(ang), jnp.sin(ang)


def _dft_outer_kernel(f_hi_ref, f_lo_ref, x_ref, o_ref):
    o_ref[0] = _dot3(f_hi_ref[...], f_lo_ref[...], x_ref[0])


def _dft_outer(x, f_hi, f_lo):
    bsz, k, m = x.shape
    r = f_hi.shape[0]
    tn = min(DFT_COLS, m)
    assert m % tn == 0
    vmem = 2 * (2 * f_hi.size * 2 + k * tn * 4 + r * tn * 4) + 3 * (k + r) * tn * 4
    return pl.pallas_call(
        _dft_outer_kernel, grid=(bsz, m // tn),
        in_specs=[pl.BlockSpec(f_hi.shape, lambda b, j: (0, 0)), pl.BlockSpec(f_lo.shape, lambda b, j: (0, 0)),
                  pl.BlockSpec((1, k, tn), lambda b, j: (b, 0, j))],
        out_specs=pl.BlockSpec((1, r, tn), lambda b, j: (b, 0, j)),
        out_shape=jax.ShapeDtypeStruct((bsz, r, m), F32),
        compiler_params=_params(("parallel", "parallel"), vmem), name="hyena_dft_outer",
    )(f_hi, f_lo, x)


def _dft_inner_kernel(m_hi_ref, m_lo_ref, mt_hi_ref, mt_lo_ref, a_ref, h_ref, o_ref, *, conv):
    n2 = a_ref.shape[3]
    x = a_ref[0, :, 0].reshape(2 * n2, a_ref.shape[4])
    y = _dot3(m_hi_ref[0], m_lo_ref[0], x)
    if conv:
        h = h_ref[0, :, 0].reshape(2 * n2, h_ref.shape[4])
        yr, yi, hr, hi = y[:n2], y[n2:], h[:n2], h[n2:]
        prod = jnp.concatenate([yr * hr - yi * hi, yr * hi + yi * hr], axis=0)
        y = _dot3(mt_hi_ref[0], mt_lo_ref[0], prod)
    o_ref[0, :, 0] = y.reshape(2, n2, y.shape[1])


def _dft_inner(a, h, mats, conv):
    bsz, _, n1, n2, c = a.shape
    blk = lambda sel: pl.BlockSpec((1, 2, 1, n2, c), sel)
    mat = pl.BlockSpec((1, 2 * n2, 2 * n2), lambda k, b: (k, 0, 0))
    vmem = 2 * (4 * 4 * n2 * n2 * 2 + 3 * 2 * n2 * c * 4) + 8 * 2 * n2 * c * 4
    return pl.pallas_call(
        functools.partial(_dft_inner_kernel, conv=conv), grid=(n1, bsz),
        in_specs=[mat, mat, mat, mat, blk(lambda k, b: (b, 0, k, 0, 0)), blk(lambda k, b: (0, 0, k, 0, 0))],
        out_specs=blk(lambda k, b: (b, 0, k, 0, 0)), out_shape=jax.ShapeDtypeStruct(a.shape, F32),
        compiler_params=_params(("parallel", "arbitrary"), vmem),
        name="hyena_dft_inner_conv" if conv else "hyena_dft_inner",
    )(*mats, a, h)


def _dft_final_kernel(fd_hi_ref, fd_lo_ref, a_ref, z_ref, x0_ref, skip_ref, o_ref):
    y = _dot3(fd_hi_ref[...], fd_lo_ref[...], a_ref[0])
    o_ref[0] = (y + z_ref[0] * skip_ref[...]) * x0_ref[0]


def _dft_final(a, z, x0, skip_t, fd_hi, fd_lo):
    bsz, r2, m = a.shape
    k = fd_hi.shape[0]
    tn = min(DFT_COLS, m)
    row = pl.BlockSpec((1, k, tn), lambda b, j: (b, 0, j))
    vmem = 2 * (2 * fd_hi.size * 2 + r2 * tn * 4 + 3 * k * tn * 4) + 3 * (k + r2) * tn * 4
    return pl.pallas_call(
        _dft_final_kernel, grid=(bsz, m // tn),
        in_specs=[pl.BlockSpec(fd_hi.shape, lambda b, j: (0, 0)), pl.BlockSpec(fd_lo.shape, lambda b, j: (0, 0)),
                  pl.BlockSpec((1, r2, tn), lambda b, j: (b, 0, j)), row, row,
                  pl.BlockSpec((1, tn), lambda b, j: (0, 0))],
        out_specs=row, out_shape=jax.ShapeDtypeStruct((bsz, k, m), F32),
        compiler_params=_params(("parallel", "parallel"), vmem), name="hyena_dft_final",
    )(fd_hi, fd_lo, a, z, x0, skip_t)


def _hyena_long_conv(z, x0, k, skip):
    bsz, n, c = z.shape
    n2 = DFT_INNER
    nn = 2 * n
    n1 = nn // n2
    assert nn == n1 * n2 and n1 % 2 == 0
    half = n1 // 2
    m = n2 * c
    j1 = jnp.arange(n1)
    ca, sa = _cis(j1[:, None] * j1[None, :], n1)
    fa = jnp.concatenate([ca, -sa], axis=0)
    fd = jnp.concatenate([ca, -sa], axis=1)[:half] / nn
    j2 = jnp.arange(n2)
    cb, sb = _cis(j2[None, None, :] * (n1 * j2[None, :, None] + j1[:, None, None]), nn)
    mb = jnp.concatenate([jnp.concatenate([cb, sb], 2), jnp.concatenate([-sb, cb], 2)], 1)
    mats = _split(mb) + _split(jnp.swapaxes(mb, 1, 2))
    fa_hi, fa_lo = _split(fa)
    hk = _dft_outer(k.reshape(1, n1, m), fa_hi, fa_lo).reshape(1, 2, n1, n2, c)
    hk = _dft_inner(hk, hk, mats, conv=False)
    a = _dft_outer(z.reshape(bsz, half, m), fa_hi[:, :half], fa_lo[:, :half]).reshape(bsz, 2, n1, n2, c)
    a = _dft_inner(a, hk, mats, conv=True).reshape(bsz, 2 * n1, m)
    skip_t = jnp.tile(skip.reshape(1, c), (1, min(DFT_COLS, m) // c))
    out = _dft_final(a, z.reshape(bsz, half, m), x0.reshape(bsz, half, m), skip_t, *_split(fd))
    return out.reshape(bsz, n, c)


def _short_conv_kernel(f_hi_ref, f_lo_ref, fi_hi_ref, fi_lo_ref, z_ref, x0_ref, k_ref, skip_ref, o_ref):
    n = z_ref.shape[1]
    nn = 2 * n
    z = z_ref[0]
    hk = _dot3(f_hi_ref[...], f_lo_ref[...], k_ref[...])
    zs = _dot3(f_hi_ref[:, :n], f_lo_ref[:, :n], z)
    zr, zi, hr, hi = zs[:nn], zs[nn:], hk[:nn], hk[nn:]
    prod = jnp.concatenate([zr * hr - zi * hi, zr * hi + zi * hr], axis=0)
    o_ref[0] = (_dot3(fi_hi_ref[...], fi_lo_ref[...], prod) + z * skip_ref[...]) * x0_ref[0]


def _hyena_short_conv(z, x0, k, skip):
    bsz, n, c = z.shape
    nn = 2 * n
    idx = jnp.arange(nn)
    cf, sf = _cis(idx[:, None] * idx[None, :], nn)
    f = jnp.concatenate([cf, -sf], axis=0)
    fi = jnp.concatenate([cf, -sf], axis=1)[:n] / nn
    full = lambda a: pl.BlockSpec(a.shape, lambda b: (0,) * a.ndim)
    row = pl.BlockSpec((1, n, c), lambda b: (b, 0, 0))
    ops = _split(f) + _split(fi)
    vmem = 2 * (sum(a.size * 2 for a in ops) + 3 * n * c * 4 + nn * c * 4) + 12 * 2 * nn * c * 4
    return pl.pallas_call(
        _short_conv_kernel, grid=(bsz,),
        in_specs=[full(a) for a in ops] + [row, row, full(k), pl.BlockSpec((1, c), lambda b: (0, 0))],
        out_specs=row, out_shape=jax.ShapeDtypeStruct((bsz, n, c), F32),
        compiler_params=_params(("parallel",), vmem), name="hyena_short_conv",
    )(*ops, z, x0, k, skip.reshape(1, c))


def _hyena_sequence(z, x0, filt, skip):
    n = z.shape[1]
    k = _hyena_filters(n, *filt)
    conv = _hyena_short_conv if n <= HY_DIRECT_MAX else _hyena_long_conv
    return conv(z, x0, k, skip)


def _rms(x, g):
    return x * lax.rsqrt(jnp.mean(x * x, -1, keepdims=True) + RMS_EPS) * g


def _mla_in_kernel(xc_ref, xp_ref, xn_ref, sh_ref, sc_ref, why_ref, cw_ref, cb_ref, wql_ref, qg_ref, wqa_ref, wqb_ref,
                   wkvl_ref, kvg_ref, wk_ref, wv_ref, wpa_ref, wpb_ref, qa_ref, qb_ref, ka_ref, kb_ref,
                   x0_ref, z_ref, q_ref, k_ref, vt_ref):
    tm = xc_ref.shape[1]
    rows = jnp.concatenate([xp_ref[0], xc_ref[0], xn_ref[0]], axis=0)
    h = (rows * (1.0 + sc_ref[0]) + sh_ref[0]).astype(BF16)
    hc = h[SUBLANES:SUBLANES + tm]
    y = _conv3(jnp.dot(h, why_ref[...], preferred_element_type=F32), cw_ref, cb_ref, tm)
    x0_ref[0] = y[:, :HY_WIDTH]
    z_ref[0] = y[:, 2 * HY_WIDTH:] * y[:, HY_WIDTH:2 * HY_WIDTH]
    ql = _rms(jnp.dot(hc, wql_ref[...], preferred_element_type=F32), qg_ref[...]).astype(BF16)
    nh = q_ref.shape[1]
    qa = jnp.concatenate([qa_ref[...]] * nh, axis=1)
    qb = jnp.concatenate([qb_ref[...]] * nh, axis=1)
    q = (jnp.dot(ql, wqa_ref[...], preferred_element_type=F32) * qa
         + jnp.dot(ql, wqb_ref[...], preferred_element_type=F32) * qb).astype(BF16)
    kvl = _rms(jnp.dot(hc, wkvl_ref[...], preferred_element_type=F32), kvg_ref[...]).astype(BF16)
    kn = jnp.dot(kvl, wk_ref[...], preferred_element_type=F32)
    v_t = jnp.dot(kvl, wv_ref[...], preferred_element_type=F32).T
    kpe = (jnp.dot(hc, wpa_ref[...], preferred_element_type=F32) * ka_ref[...]
           + jnp.dot(hc, wpb_ref[...], preferred_element_type=F32) * kb_ref[...])
    ones = jnp.ones((ATT_ONES_ROWS, tm), BF16)
    for hd in range(nh):
        q_ref[0, hd] = q[:, hd * MLA_SLAB:(hd + 1) * MLA_SLAB]
        k_ref[0, hd] = (kn[:, hd * MLA_SLAB:(hd + 1) * MLA_SLAB] + kpe).astype(BF16)
        vt_ref[0, hd, :MLA_V, :] = v_t[hd * MLA_V:(hd + 1) * MLA_V].astype(BF16)
        vt_ref[0, hd, MLA_V:, :] = ones


def _rot_cols(w_pe):
    ev, od = w_pe[..., 0::2], w_pe[..., 1::2]
    return jnp.concatenate([ev, od], -1), jnp.concatenate([-od, ev], -1)


def _mla_weights(p):
    w_in = p['w_in']
    kin = w_in.shape[0]
    wq = p['w_qb'].reshape(MLA_Q_RANK, MLA_HEADS, MLA_QK)
    qa_pe, qb_pe = _rot_cols(wq[..., MLA_NOPE:])
    zq = jnp.zeros((MLA_Q_RANK, MLA_HEADS, MLA_SLAB - MLA_QK), F32)
    w_qa = jnp.concatenate([wq[..., :MLA_NOPE], qa_pe, zq], -1).reshape(MLA_Q_RANK, MLA_HEADS * MLA_SLAB)
    w_qb = jnp.concatenate([jnp.zeros_like(wq[..., :MLA_NOPE]), qb_pe, zq], -1)
    w_qb = w_qb.reshape(MLA_Q_RANK, MLA_HEADS * MLA_SLAB)
    wkv = p['w_kvb'].reshape(MLA_KV_RANK, MLA_HEADS, MLA_NOPE + MLA_V)
    w_k = jnp.concatenate([wkv[..., :MLA_NOPE], jnp.zeros((MLA_KV_RANK, MLA_HEADS, MLA_SLAB - MLA_NOPE), F32)], -1)
    w_k = w_k.reshape(MLA_KV_RANK, MLA_HEADS * MLA_SLAB)
    w_v = wkv[..., MLA_NOPE:].reshape(MLA_KV_RANK, MLA_HEADS * MLA_V)
    pa, pb = _rot_cols(w_in[:, OFF_KPE:])
    left, right = jnp.zeros((kin, MLA_NOPE), F32), jnp.zeros((kin, MLA_SLAB - MLA_QK), F32)
    w_pa = jnp.concatenate([left, pa, right], -1)
    w_pb = jnp.concatenate([left, pb, right], -1)
    bf = lambda a: a.astype(BF16)
    return dict(w_hy=bf(w_in[:, :OFF_Q]), conv_w=p['conv_w'], conv_b=p['conv_b'][None],
                w_ql=bf(w_in[:, OFF_Q:OFF_KV]), q_g=p['q_norm'][None], w_qa=bf(w_qa), w_qb=bf(w_qb),
                w_kvl=bf(w_in[:, OFF_KV:OFF_KPE]), kv_g=p['kv_norm'][None], w_k=bf(w_k), w_v=bf(w_v),
                w_pa=bf(w_pa), w_pb=bf(w_pb))


def _rope_slabs(n, rotate):
    one = jnp.ones((n, MLA_NOPE), F32)
    zero = jnp.zeros((n, MLA_SLAB - MLA_QK), F32)
    if rotate:
        cos, sin = _rope_tables(n)
    else:
        cos, sin = jnp.ones((n, MLA_ROPE // 2), F32), jnp.zeros((n, MLA_ROPE // 2), F32)
    return (jnp.concatenate([one, cos, cos, zero], -1), jnp.concatenate([jnp.zeros_like(one), sin, sin, zero], -1))


def _mla_in_proj(x, shift, scale, w, rotate):
    bsz, n, kin = x.shape
    tm = min(ROW_TILE, n)
    assert n % tm == 0
    ca, sb = _rope_slabs(n, rotate)
    s = MLA_SCALE * LOG2E
    tabs = [ca * s, sb * s, ca, sb]
    consts = [w['w_hy'], w['conv_w'], w['conv_b'], w['w_ql'], w['q_g'], w['w_qa'], w['w_qb'], w['w_kvl'], w['kv_g'],
              w['w_k'], w['w_v'], w['w_pa'], w['w_pb']]
    full = lambda a: pl.BlockSpec(a.shape, lambda b, i: (0,) * a.ndim)
    row = lambda c: pl.BlockSpec((1, tm, c), lambda b, i: (b, i, 0))
    head = pl.BlockSpec((1, MLA_HEADS, tm, MLA_SLAB), lambda b, i: (b, 0, i, 0))
    dva = MLA_V + ATT_ONES_ROWS
    vmem = 2 * (tm * kin * 4 + sum(a.size * a.dtype.itemsize for a in consts) + 2 * tm * HY_WIDTH * 4
                + 3 * MLA_HEADS * tm * MLA_SLAB * 2 + 4 * tm * MLA_SLAB * 4) + 8 * (tm + 16) * 3 * HY_WIDTH * 4
    return pl.pallas_call(
        _mla_in_kernel, grid=(bsz, n // tm),
        in_specs=_halo_specs(n, tm, kin) + [_bcast_spec(shift), _bcast_spec(scale)] + [full(a) for a in consts]
        + [pl.BlockSpec((tm, MLA_SLAB), lambda b, i: (i, 0))] * 4,
        out_specs=[row(HY_WIDTH), row(HY_WIDTH), head, head,
                   pl.BlockSpec((1, MLA_HEADS, dva, tm), lambda b, i: (b, 0, 0, i))],
        out_shape=[jax.ShapeDtypeStruct((bsz, n, HY_WIDTH), F32)] * 2
        + [jax.ShapeDtypeStruct((bsz, MLA_HEADS, n, MLA_SLAB), BF16)] * 2
        + [jax.ShapeDtypeStruct((bsz, MLA_HEADS, dva, n), BF16)],
        compiler_params=_params(("parallel", "parallel"), vmem), name="mla_in_proj",
    )(x, x, x, shift, scale, *consts, *tabs)


def _mixer_hyena_mla(x, ctx, sh_l, sc_l, sh_c, sc_c, p):
    n, nc = x.shape[1], ctx.shape[1]
    w = _mla_weights(p)
    x0_l, z_l, q_l, k_l, vt_l = _mla_in_proj(x, sh_l, sc_l, w, rotate=True)
    x0_c, z_c, q_c, k_c, vt_c = _mla_in_proj(ctx, sh_c, sc_c, w, rotate=False)
    k_all = jnp.concatenate([k_c, k_l], 2)
    vt_all = jnp.concatenate([vt_c, vt_l], 3)
    att_l = _attention(q_l, k_all, vt_all, tq=min(ATT_Q_TILE, n), tk=_key_tile(nc + n))
    att_c = _attention(q_c, k_c, vt_c, tq=nc, tk=_key_tile(nc))
    filt = (p['filt_w1'], p['filt_b1'], p['filt_freq1'], p['filt_w2'], p['filt_b2'], p['filt_freq2'], p['filt_w3'])
    hyo_l = _hyena_sequence(z_l, x0_l, filt, p['skip'])
    hyo_c = _hyena_sequence(z_c, x0_c, filt, p['skip'])
    return (hyo_l, att_l), (hyo_c, att_c)


def _halo_specs(n, tm, k):
    nb = n // SUBLANES
    per = tm // SUBLANES
    return [pl.BlockSpec((1, tm, k), lambda b, i: (b, i, 0)),
            pl.BlockSpec((1, SUBLANES, k), lambda b, i: (b, jnp.maximum(i * per - 1, 0), 0)),
            pl.BlockSpec((1, SUBLANES, k), lambda b, i: (b, jnp.minimum((i + 1) * per, nb - 1), 0))]


def _conv3(u, cw_ref, cb_ref, tm):
    i = pl.program_id(1)
    rows = lax.broadcasted_iota(I32, (u.shape[0], 1), 0)
    inside = jnp.logical_and(jnp.logical_or(rows >= SUBLANES, i > 0),
                             jnp.logical_or(rows < tm + SUBLANES, i < pl.num_programs(1) - 1))
    u = jnp.where(inside, u, 0.0)
    prev = pltpu.roll(u, 1, 0)[SUBLANES:SUBLANES + tm]
    nxt = pltpu.roll(u, u.shape[0] - 1, 0)[SUBLANES:SUBLANES + tm]
    return cw_ref[0:1, :] * prev + cw_ref[1:2, :] * u[SUBLANES:SUBLANES + tm] + cw_ref[2:3, :] * nxt + cb_ref[...]


def _ssd_in_kernel(xc_ref, xp_ref, xn_ref, sh_ref, sc_ref, wz_ref, wx_ref, wdt_ref, cw_ref, cb_ref, dtb_ref,
                   z_ref, xs_ref, b_ref, c_ref, dt_ref):
    tm = xc_ref.shape[1]
    rows = jnp.concatenate([xp_ref[0], xc_ref[0], xn_ref[0]], axis=0)
    h = (rows * (1.0 + sc_ref[0]) + sh_ref[0]).astype(BF16)
    hc = h[SUBLANES:SUBLANES + tm]
    z_ref[0] = jnp.dot(hc, wz_ref[...], preferred_element_type=F32)
    y = _conv3(jnp.dot(h, wx_ref[...], preferred_element_type=F32), cw_ref, cb_ref, tm)
    xbc = y * jax.nn.sigmoid(y)
    xs_ref[0] = xbc[:, :SSD_INNER]
    b_ref[0] = xbc[:, SSD_INNER:SSD_INNER + SSD_BC]
    c_ref[0] = xbc[:, SSD_INNER + SSD_BC:]
    dt = jnp.dot(hc, wdt_ref[...], preferred_element_type=F32) + dtb_ref[...]
    dt_ref[0] = (jnp.maximum(dt, 0.0) + jnp.log1p(jnp.exp(-jnp.abs(dt)))).T


def _ssd_in_proj(x, shift, scale, w_z, w_xbc, w_dt, conv_w, conv_b, dt_bias):
    bsz, n, k = x.shape
    tm = min(ROW_TILE, n)
    assert n % tm == 0
    full = lambda a: pl.BlockSpec(a.shape, lambda b, i: (0,) * a.ndim)
    row = lambda c: pl.BlockSpec((1, tm, c), lambda b, i: (b, i, 0))
    nh2 = w_dt.shape[1]
    consts = [w_z, w_xbc, w_dt, conv_w, conv_b, dt_bias]
    widths = [SSD_INNER, SSD_INNER, SSD_BC, SSD_BC]
    vmem = 2 * (tm * k * 4 + sum(a.size * a.dtype.itemsize for a in consts) + tm * (sum(widths) + nh2) * 4) \
        + 6 * (tm + 2 * SUBLANES) * SSD_XBC * 4
    return pl.pallas_call(
        _ssd_in_kernel, grid=(bsz, n // tm),
        in_specs=_halo_specs(n, tm, k) + [_bcast_spec(shift), _bcast_spec(scale)] + [full(a) for a in consts],
        out_specs=[row(c) for c in widths] + [pl.BlockSpec((1, nh2, tm), lambda b, i: (b, 0, i))],
        out_shape=[jax.ShapeDtypeStruct((bsz, n, c), F32) for c in widths]
        + [jax.ShapeDtypeStruct((bsz, nh2, n), F32)],
        compiler_params=_params(("parallel", "parallel"), vmem), name="ssd_in_proj",
    )(x, x, x, shift, scale, *consts)


def _ssd_scan_kernel(x_ref, b_ref, c_ref, dt_ref, a_ref, s0_ref, y_ref, sfin_ref, st_sc, *, reverse):
    ci = pl.program_id(2)
    q = SSD_CHUNK

    @pl.when(ci == 0)
    def _():
        st_sc[...] = s0_ref[0, 0]

    dt = dt_ref[0]
    a = dt * a_ref[0]
    si = lax.broadcasted_iota(I32, (q, q), 0)
    li = lax.broadcasted_iota(I32, (q, q), 1)
    incl = jnp.where((si >= li) if reverse else (si <= li), 1.0, 0.0)
    hp = lax.Precision.HIGHEST
    acs = jnp.dot(a, incl, precision=hp, preferred_element_type=F32)
    tot = jnp.dot(a, jnp.ones((q, LANES), F32), precision=hp, preferred_element_type=F32)
    e_in = jnp.exp(acs)
    w_end = jnp.exp(tot - acs) * dt
    e_tot = jnp.exp(tot)
    acs_t = acs.T
    e_in_t = e_in.T
    mask = (li >= si) if reverse else (li <= si)
    cmat = c_ref[0]
    cb = lax.dot_general(cmat.astype(BF16), b_ref[0].astype(BF16), (((1,), (1,)), ((), ())),
                         preferred_element_type=F32)
    bt = b_ref[0].T
    first = lax.broadcasted_iota(I32, (q, LANES), 1) < SSD_HEADDIM
    ys = []
    for pr in range(SSD_HPG // 2):
        lo, hi = pr * LANES, (pr + 1) * LANES
        x_pair = x_ref[0, :, lo:hi].astype(BF16)
        st_pair = st_sc[:, lo:hi]
        rhs = jnp.concatenate([x_pair, st_pair.astype(BF16)], axis=0)
        y2, s2 = [], []
        for r in (2 * pr, 2 * pr + 1):
            seg = acs_t[:, r:r + 1] - acs[r:r + 1, :]
            m = cb * jnp.exp(jnp.where(mask, seg, -jnp.inf)) * dt[r:r + 1, :]
            lhs = jnp.concatenate([m.astype(BF16), (cmat * e_in_t[:, r:r + 1]).astype(BF16)], axis=1)
            y2.append(jnp.dot(lhs, rhs, preferred_element_type=F32))
            btr = (bt * w_end[r:r + 1, :]).astype(BF16)
            s2.append(e_tot[r:r + 1, :] * st_pair + jnp.dot(btr, x_pair, preferred_element_type=F32))
        ys.append(jnp.where(first, y2[0], y2[1]))
        st_sc[:, lo:hi] = jnp.where(first, s2[0], s2[1])
    y_ref[0] = jnp.concatenate(ys, axis=1)

    @pl.when(ci == pl.num_programs(2) - 1)
    def _():
        sfin_ref[0, 0] = st_sc[...]


def _ssd_scan(xs, bm, cm, dt_t, a_rep, s0, reverse):
    bsz, n, _ = xs.shape
    q = SSD_CHUNK
    nc = n // q
    assert n % q == 0 and q == LANES
    d = 1 if reverse else 0
    cc = (lambda c: nc - 1 - c) if reverse else (lambda c: c)
    gw = SSD_HPG * SSD_HEADDIM
    state = pl.BlockSpec((1, 1, SSD_STATE, gw), lambda b, g, c: (b, g, 0, 0))
    vmem = 2 * (2 * q * gw * 4 + 2 * q * SSD_STATE * 4 + 2 * SSD_STATE * gw * 4) + SSD_STATE * gw * 4 + 64 * q * q * 4
    return pl.pallas_call(
        functools.partial(_ssd_scan_kernel, reverse=reverse), grid=(bsz, SSD_GROUPS, nc),
        in_specs=[pl.BlockSpec((1, q, gw), lambda b, g, c: (b, cc(c), g)),
                  pl.BlockSpec((1, q, SSD_STATE), lambda b, g, c: (b, cc(c), g)),
                  pl.BlockSpec((1, q, SSD_STATE), lambda b, g, c: (b, cc(c), g)),
                  pl.BlockSpec((1, SSD_HPG, q), lambda b, g, c: (b, d * SSD_GROUPS + g, cc(c))),
                  pl.BlockSpec((1, SSD_HPG, LANES), lambda b, g, c: (d * SSD_GROUPS + g, 0, 0)),
                  state],
        out_specs=[pl.BlockSpec((1, q, gw), lambda b, g, c: (b, cc(c), g)), state],
        out_shape=[jax.ShapeDtypeStruct((bsz, n, SSD_INNER), F32),
                   jax.ShapeDtypeStruct((bsz, SSD_GROUPS, SSD_STATE, gw), F32)],
        scratch_shapes=[pltpu.VMEM((SSD_STATE, gw), F32)],
        compiler_params=_params(("parallel", "parallel", "arbitrary"), vmem),
        name="ssd_scan_rev" if reverse else "ssd_scan_fwd",
    )(xs, bm, cm, dt_t, a_rep, s0)


def _ssd_out_kernel(x_ref, yf_ref, yb_ref, xs_ref, z_ref, d_ref, ng_ref, w_ref, gate_ref, g_ref, b_ref, sh_ref,
                    sc_ref, xo_ref, ff_ref):
    z = z_ref[0]
    y = (yf_ref[0] + yb_ref[0] + xs_ref[0] * d_ref[0]) * (z * jax.nn.sigmoid(z))
    gw = SSD_INNER // SSD_GROUPS
    parts = []
    for g in range(SSD_GROUPS):
        yg = y[:, g * gw:(g + 1) * gw]
        parts.append(yg * lax.rsqrt(jnp.mean(yg * yg, -1, keepdims=True) + RMS_EPS))
    yn = (jnp.concatenate(parts, axis=1) * ng_ref[0]).astype(BF16)
    mix = jnp.dot(yn, w_ref[...], preferred_element_type=F32)
    xn = _layer_norm(DN_ALPHA * x_ref[0] + gate_ref[0] * mix, g_ref[0], b_ref[0])
    xo_ref[0] = xn
    ff_ref[0] = xn * (1.0 + sc_ref[0]) + sh_ref[0]


def _ssd_out(x, y_f, y_b, xs, z, d_rep, norm_g, w_out, gate, ln_g, ln_b, shift2, scale2):
    bsz, n, d = x.shape
    tm = min(ROW_TILE, n)
    assert n % tm == 0
    row = lambda c: pl.BlockSpec((1, tm, c), lambda b, i: (b, i, 0))
    vecs = [gate, ln_g, ln_b, shift2, scale2]
    in_specs = [row(d)] + [row(SSD_INNER)] * 4 + [_bcast_spec(d_rep), _bcast_spec(norm_g),
                                                 pl.BlockSpec(w_out.shape, lambda b, i: (0, 0))]
    in_specs += [_bcast_spec(a) for a in vecs]
    vmem = 2 * (3 * tm * d * 4 + 4 * tm * SSD_INNER * 4 + w_out.size * 2) + 4 * tm * SSD_INNER * 4
    return pl.pallas_call(
        _ssd_out_kernel, grid=(bsz, n // tm), in_specs=in_specs, out_specs=[row(d), row(d)],
        out_shape=[jax.ShapeDtypeStruct((bsz, n, d), F32)] * 2,
        compiler_params=_params(("parallel", "parallel"), vmem), name="ssd_out",
    )(x, y_f, y_b, xs, z, d_rep, norm_g, w_out, *vecs)


def _mixer_ssd(x, ctx, sh_l, sc_l, sh_c, sc_c, p):
    a_all = -jnp.exp(jnp.concatenate([p['a_log_f'], p['a_log_b']]))
    a_rep = jnp.broadcast_to(a_all.reshape(2 * SSD_GROUPS, SSD_HPG, 1), (2 * SSD_GROUPS, SSD_HPG, LANES))
    w_in = p['w_in'].astype(BF16)
    consts = (w_in[:, :SSD_INNER], w_in[:, SSD_INNER:SSD_INNER + SSD_XBC], w_in[:, SSD_INNER + SSD_XBC:],
              p['conv_w'], p['conv_b'][None], jnp.concatenate([p['dt_bias_f'], p['dt_bias_b']])[None])
    _, xc, bc, cc, dtc = _ssd_in_proj(ctx, sh_c, sc_c, *consts)
    zl, xl, bl, cl, dtl = _ssd_in_proj(x, sh_l, sc_l, *consts)
    s0 = jnp.zeros((ctx.shape[0], SSD_GROUPS, SSD_STATE, SSD_HPG * SSD_HEADDIM), F32)
    _, sc_f = _ssd_scan(xc, bc, cc, dtc, a_rep, s0, reverse=False)
    _, sc_b = _ssd_scan(xc, bc, cc, dtc, a_rep, s0, reverse=True)
    y_f, _ = _ssd_scan(xl, bl, cl, dtl, a_rep, sc_f, reverse=False)
    y_b, _ = _ssd_scan(xl, bl, cl, dtl, a_rep, sc_b, reverse=True)
    return y_f, y_b, xl, zl


def kernel(x, c, ctx, c_ctx, mod_w, mod_b, ln_mix_g, ln_mix_b, ln_ffn_g, ln_ffn_b, a_w_in, hy_conv_w, hy_conv_b, hy_filt_w1, hy_filt_b1, hy_filt_freq1, hy_filt_w2, hy_filt_b2, hy_filt_freq2, hy_filt_w3, hy_skip, mla_q_norm, mla_w_qb, mla_kv_norm, mla_w_kvb, a_w_out, ssd_w_in, ssd_conv_w, ssd_conv_b, ssd_dt_bias_f, ssd_dt_bias_b, ssd_a_log_f, ssd_a_log_b, ssd_d, ssd_norm_g, ssd_w_out, router_w, router_bias, exp_w_gu, exp_w_down, sh_w_gu, sh_w_down):
    bsz, n_lat, d = x.shape
    n_ctx = ctx.shape[1]
    hp = lax.Precision.HIGHEST
    for l in range(DEPTH):
        last = l == DEPTH - 1
        i = l // 2
        mod = (jnp.dot(jax.nn.silu(c), mod_w[l], precision=hp) + mod_b[l]).reshape(bsz, N_MOD, 1, d)
        mod_c = (jnp.dot(jax.nn.silu(c_ctx), mod_w[l], precision=hp) + mod_b[l]).reshape(1, N_MOD, 1, d)
        sh1, sc1, g1, sh2, sc2, g2 = [mod[:, j] for j in range(N_MOD)]
        csh1, csc1, cg1, csh2, csc2, cg2 = [mod_c[:, j] for j in range(N_MOD)]
        vec = lambda a: a.reshape(1, 1, d)
        if l % 2 == 0:
            p = {"w_in": a_w_in[i], "conv_w": hy_conv_w[i], "conv_b": hy_conv_b[i],
                 "filt_w1": hy_filt_w1[i], "filt_b1": hy_filt_b1[i], "filt_freq1": hy_filt_freq1[i],
                 "filt_w2": hy_filt_w2[i], "filt_b2": hy_filt_b2[i], "filt_freq2": hy_filt_freq2[i],
                 "filt_w3": hy_filt_w3[i], "skip": hy_skip[i], "q_norm": mla_q_norm[i], "w_qb": mla_w_qb[i],
                 "kv_norm": mla_kv_norm[i], "w_kvb": mla_w_kvb[i]}
            ys_l, ys_c = _mixer_hyena_mla(x, ctx, sh1, sc1, csh1, csc1, p)
            w_out = a_w_out[i].astype(BF16)
            ws = [w_out[:HY_WIDTH], w_out[HY_WIDTH:]]
            x, ff_x = _mix_out(x, ys_l, ws, g1, vec(ln_mix_g[l]), vec(ln_mix_b[l]), sh2, sc2)
        else:
            p = {"w_in": ssd_w_in[i], "conv_w": ssd_conv_w[i], "conv_b": ssd_conv_b[i],
                 "dt_bias_f": ssd_dt_bias_f[i], "dt_bias_b": ssd_dt_bias_b[i],
                 "a_log_f": ssd_a_log_f[i], "a_log_b": ssd_a_log_b[i]}
            assert last
            y_f, y_b, xs, z = _mixer_ssd(x, ctx, sh1, sc1, csh1, csc1, p)
            d_rep = jnp.repeat(ssd_d[i], SSD_HEADDIM).reshape(1, 1, SSD_INNER)
            x, ff_x = _ssd_out(x, y_f, y_b, xs, z, d_rep, ssd_norm_g[i].reshape(1, 1, SSD_INNER),
                               ssd_w_out[i].astype(BF16), g1, vec(ln_mix_g[l]), vec(ln_mix_b[l]), sh2, sc2)
        sh_gu = sh_w_gu[l].astype(BF16)
        sh_down = sh_w_down[l].astype(BF16)
        ln_g, ln_b = vec(ln_ffn_g[l]), vec(ln_ffn_b[l])
        moe_w = (router_w[l], router_bias[l], exp_w_gu, exp_w_down, l)
        if last:
            yb, pos, w = _moe_dispatch_experts(ff_x.reshape(-1, d), *moe_w)
            x = _ffn_out(x, ff_x, yb, pos, w, 0, sh_gu, sh_down, g2, ln_g, ln_b)
        else:
            ctx, ff_c = _mix_out(ctx, ys_c, ws, cg1, vec(ln_mix_g[l]), vec(ln_mix_b[l]), csh2, csc2)
            tokens = jnp.concatenate([ff_c.reshape(-1, d), ff_x.reshape(-1, d)], 0)
            yb, pos, w = _moe_dispatch_experts(tokens, *moe_w)
            assert (bsz * n_ctx) % ROUTE_TOKENS == 0
            ctx = _ffn_out(ctx, ff_c, yb, pos, w, 0, sh_gu, sh_down, cg2, ln_g, ln_b)
            x = _ffn_out(x, ff_x, yb, pos, w, bsz * n_ctx // ROUTE_TOKENS, sh_gu, sh_down, g2, ln_g, ln_b)
    return x
```

```python
import functools
import math

import jax
import jax.numpy as jnp
from jax import lax
from jax.experimental import pallas as pl
from jax.experimental.pallas import tpu as pltpu

F32 = jnp.float32
BF16 = jnp.bfloat16
I32 = jnp.int32
U32 = jnp.uint32

D_MODEL = 1024
DEPTH = 2
GRID_W = 64
N_MOD = 6

HY_WIDTH = 512
HY_EMB = 33
HY_BANDS = (HY_EMB - 1) // 2
HY_TARGET = 1e-2
HY_FAST_DECAY = 0.3
HY_SLOW_DECAY = 1.5
HY_DECAY_MIN = math.log(HY_TARGET) / HY_SLOW_DECAY
HY_DECAY_MAX = math.log(HY_TARGET) / HY_FAST_DECAY

MLA_HEADS = 8
MLA_NOPE = 64
MLA_ROPE = 32
MLA_V = 64
MLA_Q_RANK = 256
MLA_KV_RANK = 128
MLA_QK = MLA_NOPE + MLA_ROPE
MLA_SCALE = MLA_QK ** -0.5
ROPE_THETA = 10000.0
LOG2E = math.log2(math.e)

OFF_Q = 3 * HY_WIDTH
OFF_KV = OFF_Q + MLA_Q_RANK
OFF_KPE = OFF_KV + MLA_KV_RANK

SSD_INNER = 2 * D_MODEL
SSD_HEADDIM = 64
SSD_HEADS = SSD_INNER // SSD_HEADDIM
SSD_GROUPS = 4
SSD_STATE = 128
SSD_CHUNK = 128
SSD_BC = SSD_GROUPS * SSD_STATE
SSD_XBC = SSD_INNER + 2 * SSD_BC
SSD_HPG = SSD_HEADS // SSD_GROUPS

N_EXPERTS = 256
TOP_K = 8
N_EXPERT_GROUPS = 8
TOPK_GROUPS = 4
EXPERT_DIM = 256
ROUTED_SCALE = 2.5

DN_ALPHA = (2 * DEPTH) ** 0.25
LN_EPS = 1e-5
RMS_EPS = 1e-6

LANES = 128
SUBLANES = 8
V7X_VMEM_CAP = 56 * 1024 * 1024

MOE_ROWS = 256
ROUTE_TOKENS = 256
ROW_TILE = 256
ATT_Q_TILE = 2048
ATT_HEADS_PER_STEP = 2
ATT_ONES_ROWS = 16
MLA_SLAB = LANES
DFT_INNER = 128
DFT_COLS = 2048
HY_DIRECT_MAX = 512


def _params(semantics, vmem_bytes, **kw):
    limit = int(min(max(vmem_bytes * 5 // 4, 32 * 1024 * 1024), V7X_VMEM_CAP))
    return pltpu.CompilerParams(dimension_semantics=semantics, vmem_limit_bytes=limit, **kw)


def _bcast_spec(a):
    if a.shape[0] == 1:
        return pl.BlockSpec((1, 1, a.shape[2]), lambda b, i: (0, 0, 0))
    return pl.BlockSpec((1, 1, a.shape[2]), lambda b, i: (b, 0, 0))


def _layer_norm(r, g, b):
    mu = jnp.mean(r, -1, keepdims=True)
    c = r - mu
    var = jnp.mean(c * c, -1, keepdims=True)
    return c * lax.rsqrt(var + LN_EPS) * g + b


def _swiglu_rows(xb, w_gu, w_down):
    h = jnp.dot(xb, w_gu, preferred_element_type=F32)
    half = h.shape[1] // 2
    g, u = h[:, :half], h[:, half:]
    a = (g * jax.nn.sigmoid(g) * u).astype(BF16)
    return jnp.dot(a, w_down, preferred_element_type=F32)


def _pack_pairs(x):
    half = x.shape[1] // 2
    hi = pltpu.bitcast(x[:, :half].astype(BF16).astype(F32), U32)
    lo = pltpu.bitcast(x[:, half:].astype(BF16).astype(F32), U32)
    return hi | (lo >> 16)


def _unpack_pairs(u):
    hi = pltpu.bitcast(u & jnp.uint32(0xFFFF0000), F32)
    lo = pltpu.bitcast(u << 16, F32)
    return jnp.concatenate([hi, lo], axis=1)


def _mix_epilogue(x, mix, gate_ref, g_ref, b_ref, sh_ref, sc_ref, xo_ref, ff_ref, ffp_ref):
    xn = _layer_norm(DN_ALPHA * x + gate_ref[0] * mix, g_ref[0], b_ref[0])
    xo_ref[0] = xn
    ff = xn * (1.0 + sc_ref[0]) + sh_ref[0]
    ff_ref[0] = ff
    ffp_ref[0] = _pack_pairs(ff)


def _mix_out_kernel(*refs, n_y):
    x_ref = refs[0]
    y_refs = refs[1:1 + n_y]
    w_refs = refs[1 + n_y:1 + 2 * n_y]
    y = None
    for y_ref, w_ref in zip(y_refs, w_refs):
        t = jnp.dot(y_ref[0].astype(BF16), w_ref[...], preferred_element_type=F32)
        y = t if y is None else y + t
    _mix_epilogue(x_ref[0], y, *refs[1 + 2 * n_y:])


def _mix_out(x, ys, ws, gate, ln_g, ln_b, shift2, scale2):
    bsz, n, d = x.shape
    tm = min(ROW_TILE, n)
    assert n % tm == 0
    row = lambda c: pl.BlockSpec((1, tm, c), lambda b, i: (b, i, 0))
    vecs = [gate, ln_g, ln_b, shift2, scale2]
    in_specs = [row(d)] + [row(y.shape[2]) for y in ys]
    in_specs += [pl.BlockSpec(w.shape, lambda b, i: (0, 0)) for w in ws]
    in_specs += [_bcast_spec(a) for a in vecs]
    vmem = 2 * (3 * tm * d * 4 + sum(tm * y.shape[2] * 4 + w.size * 2 for y, w in zip(ys, ws)))
    return pl.pallas_call(
        functools.partial(_mix_out_kernel, n_y=len(ys)),
        grid=(bsz, n // tm), in_specs=in_specs, out_specs=[row(d), row(d), row(d // 2)],
        out_shape=[jax.ShapeDtypeStruct((bsz, n, d), F32)] * 2 + [jax.ShapeDtypeStruct((bsz, n, d // 2), U32)],
        compiler_params=_params(("parallel", "parallel"), vmem), name="mix_out",
    )(x, *ys, *ws, *vecs)


def _attn_kernel(q_ref, k_ref, vt_ref, o_ref, *, tk, hp):
    nk = k_ref.shape[2]
    tq = q_ref.shape[2]
    dva = vt_ref.shape[2]
    dv = dva - ATT_ONES_ROWS

    def body(j, carry):
        off = pl.multiple_of(j * tk, tk)
        new = []
        for h in range(hp):
            m_prev, acc = carry[h]
            st = lax.dot_general(k_ref[0, h, pl.ds(off, tk), :], q_ref[0, h], (((1,), (1,)), ((), ())),
                                 preferred_element_type=F32)
            m_new = jnp.maximum(m_prev, jnp.max(st, 0, keepdims=True))
            p = jnp.exp2(st - m_new).astype(BF16)
            alpha = jnp.exp2(m_prev - m_new)
            acc = alpha * acc + jnp.dot(vt_ref[0, h, :, pl.ds(off, tk)], p, preferred_element_type=F32)
            new.append((m_new, acc))
        return tuple(new)

    init = tuple((jnp.full((1, tq), -jnp.inf, F32), jnp.zeros((dva, tq), F32)) for _ in range(hp))
    fin = lax.fori_loop(0, nk // tk, body, init)
    outs = [acc[:dv] / acc[dv:dv + 1] for _, acc in fin]
    o_ref[0] = jnp.concatenate(outs, 0).T


def _attention(q, k, vt, tq, tk):
    bsz, h, nq, dk = q.shape
    nk, dva = k.shape[2], vt.shape[2]
    dv = dva - ATT_ONES_ROWS
    hp = ATT_HEADS_PER_STEP
    assert nq % tq == 0 and nk % tk == 0 and h % hp == 0
    vmem = 2 * hp * (tq * LANES * 2 + nk * LANES * 2 + dva * nk * 2) + 2 * tq * hp * dv * 4 + 6 * hp * tk * tq * 4
    return pl.pallas_call(
        functools.partial(_attn_kernel, tk=tk, hp=hp), grid=(bsz, h // hp, nq // tq),
        in_specs=[pl.BlockSpec((1, hp, tq, dk), lambda b, g, i: (b, g, i, 0)),
                  pl.BlockSpec((1, hp, nk, dk), lambda b, g, i: (b, g, 0, 0)),
                  pl.BlockSpec((1, hp, dva, nk), lambda b, g, i: (b, g, 0, 0))],
        out_specs=pl.BlockSpec((1, tq, hp * dv), lambda b, g, i: (b, i, g)),
        out_shape=jax.ShapeDtypeStruct((bsz, nq, h * dv), F32),
        compiler_params=_params(("parallel", "parallel", "arbitrary"), vmem), name="mla_attention",
    )(q, k, vt)


def _key_tile(nk):
    for t in (768, 512, 384, 256, 128):
        if nk % t == 0:
            return t
    return nk


def _router_kernel(x_ref, wt_ref, bias_ref, upper_ref, idx_ref, w_ref, rank_ref, cnt_ref, run_sc):
    i = pl.program_id(0)
    tm = x_ref.shape[0]

    @pl.when(i == 0)
    def _():
        run_sc[...] = jnp.zeros(run_sc.shape, F32)

    logits = lax.dot_general(wt_ref[...], x_ref[...], (((1,), (1,)), ((), ())),
                             precision=lax.Precision.HIGHEST, preferred_element_type=F32)
    sc = jax.nn.sigmoid(logits)
    ch = sc + bias_ref[:, :1]
    neg = -jnp.inf
    chg = ch.reshape(N_EXPERT_GROUPS, N_EXPERTS // N_EXPERT_GROUPS, tm)
    m1 = jnp.max(chg, axis=1)
    eq = chg == m1[:, None, :]
    cnt = jnp.sum(eq.astype(F32), axis=1)
    m2 = jnp.max(jnp.where(eq, neg, chg), axis=1)
    g2 = m1 + jnp.where(cnt >= 2.0, m1, m2)
    gi = lax.broadcasted_iota(I32, g2.shape, 0)
    beaten = jnp.zeros(g2.shape, F32)
    for g in range(N_EXPERT_GROUPS):
        row = g2[g:g + 1, :]
        beaten = beaten + jnp.where(row > g2, 1.0, jnp.where(row == g2, jnp.where(gi > g, 1.0, 0.0), 0.0))
    keep = beaten < float(TOPK_GROUPS)
    cur = jnp.where(keep[:, None, :], chg, neg).reshape(N_EXPERTS, tm)
    eidx = lax.broadcasted_iota(I32, (N_EXPERTS, tm), 0)
    multi = jnp.zeros((N_EXPERTS, tm), F32)
    hits, idx_rows, w_rows = [], [], []
    for _ in range(TOP_K):
        m = jnp.max(cur, axis=0, keepdims=True)
        sel = jnp.min(jnp.where(cur == m, eidx, N_EXPERTS), axis=0, keepdims=True)
        hit = eidx == sel
        idx_rows.append(sel)
        w_rows.append(jnp.sum(jnp.where(hit, sc, 0.0), axis=0, keepdims=True))
        cur = jnp.where(hit, neg, cur)
        multi = multi + jnp.where(hit, 1.0, 0.0)
        hits.append(hit)
    base = jnp.concatenate([run_sc[...]] * (tm // LANES), axis=1)
    before = jnp.dot(multi.astype(BF16), upper_ref[...], preferred_element_type=F32) + base
    rank_rows = [jnp.sum(jnp.where(hit, before, 0.0), axis=0, keepdims=True) for hit in hits]
    w = jnp.concatenate(w_rows, axis=0)
    idx_ref[...] = jnp.concatenate(idx_rows, axis=0)
    w_ref[...] = (w / jnp.sum(w, axis=0, keepdims=True) * ROUTED_SCALE).T
    rank_ref[...] = jnp.concatenate(rank_rows, axis=0).astype(I32)
    run_sc[...] = run_sc[...] + jnp.dot(multi.astype(BF16), jnp.ones((tm, LANES), BF16), preferred_element_type=F32)
    cnt_ref[...] = run_sc[...]


def _router(tokens, router_w, router_bias):
    t, d = tokens.shape
    tm = ROUTE_TOKENS
    assert t % tm == 0
    wt = router_w.T
    bias = jnp.broadcast_to(router_bias.astype(F32)[:, None], (N_EXPERTS, LANES))
    r = jnp.arange(tm)
    upper = (r[:, None] < r[None, :]).astype(BF16)
    col = pl.BlockSpec((TOP_K, tm), lambda i: (0, i))
    full = lambda a: pl.BlockSpec(a.shape, lambda i: (0,) * a.ndim)
    vmem = 2 * (tm * d * 4 + wt.size * 4) + 40 * N_EXPERTS * tm * 4
    return pl.pallas_call(
        _router_kernel, grid=(t // tm,),
        in_specs=[pl.BlockSpec((tm, d), lambda i: (i, 0)), full(wt), full(bias), full(upper)],
        out_specs=[col, pl.BlockSpec((tm, TOP_K), lambda i: (i, 0)), col,
                   pl.BlockSpec((N_EXPERTS, LANES), lambda i: (0, 0))],
        out_shape=[jax.ShapeDtypeStruct((TOP_K, t), I32), jax.ShapeDtypeStruct((t, TOP_K), F32),
                   jax.ShapeDtypeStruct((TOP_K, t), I32), jax.ShapeDtypeStruct((N_EXPERTS, LANES), F32)],
        scratch_shapes=[pltpu.VMEM((N_EXPERTS, LANES), F32)],
        compiler_params=_params(("arbitrary",), vmem), name="moe_router",
    )(tokens, wt, bias, upper)


def _positions_kernel(idx_ref, rank_ref, start_ref, pos_ref):
    tm = idx_ref.shape[1]
    eidx = lax.broadcasted_iota(I32, (N_EXPERTS, tm), 0)
    start = jnp.concatenate([start_ref[...]] * (tm // LANES), axis=1)
    rows = [jnp.sum(jnp.where(eidx == idx_ref[k:k + 1, :], start, 0), axis=0, keepdims=True) for k in range(TOP_K)]
    pos_ref[...] = jnp.concatenate(rows, axis=0) + rank_ref[...]


def _positions(idx, rank, pad_start):
    t = idx.shape[1]
    tm = ROUTE_TOKENS
    start = jnp.broadcast_to(pad_start.astype(I32)[:, None], (N_EXPERTS, LANES))
    col = pl.BlockSpec((TOP_K, tm), lambda i: (0, i))
    return pl.pallas_call(
        _positions_kernel, grid=(t // tm,),
        in_specs=[col, col, pl.BlockSpec((N_EXPERTS, LANES), lambda i: (0, 0))], out_specs=col,
        out_shape=jax.ShapeDtypeStruct((TOP_K, t), I32),
        compiler_params=_params(("parallel",), 8 * N_EXPERTS * tm * 4), name="moe_positions",
    )(idx, rank, start)


def _moe_plan(counts, t):
    n_blocks = -(-(t * TOP_K) // MOE_ROWS) + N_EXPERTS
    c = counts.astype(I32)
    padded = (c + MOE_ROWS - 1) // MOE_ROWS * MOE_ROWS
    pad_end = jnp.cumsum(padded)
    pad_start = pad_end - padded
    block_e = jnp.minimum(jnp.searchsorted(pad_end, jnp.arange(n_blocks, dtype=I32) * MOE_ROWS, side='right'),
                          N_EXPERTS - 1).astype(I32)
    n_valid = (pad_end[-1:] // MOE_ROWS).astype(I32)
    return pad_start, block_e, n_valid, n_blocks * MOE_ROWS


def _row_copy(src, dst, sem):
    return pltpu.make_async_copy(src, dst, sem)


def _dispatch_kernel(pos_ref, x_ref, buf_in_ref, buf_ref, sem):
    del buf_in_ref
    tm = x_ref.shape[0]

    def issue(t, carry):
        for k in range(TOP_K):
            _row_copy(x_ref.at[pl.ds(t, 1)], buf_ref.at[pl.ds(pos_ref[k, t], 1)], sem).start()
        return carry

    lax.fori_loop(0, tm, issue, 0)
    for k in range(TOP_K):
        _row_copy(x_ref, buf_ref.at[pl.ds(0, tm)], sem).wait()


def _dispatch(pos, tokens, cap):
    t, d = tokens.shape
    tm = ROUTE_TOKENS
    zeros = jnp.zeros((cap, d), tokens.dtype)
    return pl.pallas_call(
        _dispatch_kernel, grid=(t // tm,),
        in_specs=[pl.BlockSpec((TOP_K, tm), lambda i: (0, i), memory_space=pltpu.SMEM),
                  pl.BlockSpec((tm, d), lambda i: (i, 0)),
                  pl.BlockSpec(memory_space=pl.ANY)],
        out_specs=pl.BlockSpec(memory_space=pl.ANY),
        out_shape=jax.ShapeDtypeStruct((cap, d), tokens.dtype),
        scratch_shapes=[pltpu.SemaphoreType.DMA(())],
        input_output_aliases={2: 0},
        compiler_params=_params(("arbitrary",), 2 * tm * d * 4, has_side_effects=True),
        name="moe_dispatch",
    )(pos, tokens, zeros)


def _experts_kernel(be_ref, nv_ref, x_ref, wgu_ref, wdn_ref, o_ref, wgu_sc, wdn_sc):
    i = pl.program_id(0)

    @pl.when(i < nv_ref[0])
    def _():
        @pl.when(jnp.logical_or(i == 0, be_ref[i] != be_ref[jnp.maximum(i - 1, 0)]))
        def _():
            wgu_sc[...] = wgu_ref[0, 0].astype(BF16)
            wdn_sc[...] = wdn_ref[0, 0].astype(BF16)

        x = _unpack_pairs(x_ref[...]).astype(BF16)
        o_ref[...] = _pack_pairs(_swiglu_rows(x, wgu_sc[...], wdn_sc[...]))

    @pl.when(i >= nv_ref[0])
    def _():
        o_ref[...] = jnp.zeros(o_ref.shape, U32)


def _moe_experts(xb, block_e, n_valid, w_gu, w_down, layer):
    cap, dp = xb.shape
    d = 2 * dp
    gu = w_gu.shape[3]
    ed = w_down.shape[2]
    live = lambda i, be, nv: jnp.maximum(jnp.minimum(i, nv[0] - 1), 0)
    grid_spec = pltpu.PrefetchScalarGridSpec(
        num_scalar_prefetch=2, grid=(cap // MOE_ROWS,),
        in_specs=[pl.BlockSpec((MOE_ROWS, dp), lambda i, be, nv: (live(i, be, nv), 0)),
                  pl.BlockSpec((1, 1, d, gu), lambda i, be, nv: (layer, be[i], 0, 0)),
                  pl.BlockSpec((1, 1, ed, d), lambda i, be, nv: (layer, be[i], 0, 0))],
        out_specs=pl.BlockSpec((MOE_ROWS, dp), lambda i, be, nv: (i, 0)),
        scratch_shapes=[pltpu.VMEM((d, gu), BF16), pltpu.VMEM((ed, d), BF16)])
    vmem = 2 * (2 * MOE_ROWS * dp * 4 + d * gu * 4 + ed * d * 4) + (d * gu + ed * d) * 2 + 6 * MOE_ROWS * d * 4
    return pl.pallas_call(
        _experts_kernel, grid_spec=grid_spec, out_shape=jax.ShapeDtypeStruct((cap, dp), U32),
        compiler_params=_params(("arbitrary",), vmem), name="moe_experts",
    )(block_e, n_valid, xb, w_gu, w_down)


def _ffn_out_kernel(pos_ref, w_ref, x_ref, ff_ref, yb_ref, wgu_ref, wdn_ref, gate_ref, g_ref, b_ref, xo_ref,
                    rows_sc, sem):
    tm = x_ref.shape[1]

    def issue(t, carry):
        for k in range(TOP_K):
            _row_copy(yb_ref.at[pl.ds(pos_ref[k, t], 1)], rows_sc.at[k, pl.ds(t, 1)], sem).start()
        return carry

    lax.fori_loop(0, tm, issue, 0)
    out = _swiglu_rows(ff_ref[0].astype(BF16), wgu_ref[...], wdn_ref[...])
    for k in range(TOP_K):
        _row_copy(yb_ref.at[pl.ds(0, tm)], rows_sc.at[k], sem).wait()
    for k in range(TOP_K):
        out = out + w_ref[:, k:k + 1] * _unpack_pairs(rows_sc[k])
    xo_ref[0] = _layer_norm(DN_ALPHA * x_ref[0] + gate_ref[0] * out, g_ref[0], b_ref[0])


def _ffn_out(x, ff, yb, pos, w, tile0, sh_gu, sh_down, gate, ln_g, ln_b):
    bsz, n, d = x.shape
    tm = min(ROUTE_TOKENS, n)
    nt = n // tm
    row = pl.BlockSpec((1, tm, d), lambda b, i: (b, i, 0))
    vecs = [gate, ln_g, ln_b]
    in_specs = [pl.BlockSpec((TOP_K, tm), lambda b, i: (0, tile0 + b * nt + i), memory_space=pltpu.SMEM),
                pl.BlockSpec((tm, TOP_K), lambda b, i: (tile0 + b * nt + i, 0)),
                row, row, pl.BlockSpec(memory_space=pl.ANY),
                pl.BlockSpec(sh_gu.shape, lambda b, i: (0, 0)), pl.BlockSpec(sh_down.shape, lambda b, i: (0, 0))]
    in_specs += [_bcast_spec(a) for a in vecs]
    vmem = TOP_K * tm * d * 2 + 2 * (3 * tm * d * 4 + sh_gu.size * 2 + sh_down.size * 2) + 6 * tm * d * 4
    return pl.pallas_call(
        _ffn_out_kernel, grid=(bsz, nt), in_specs=in_specs, out_specs=row,
        out_shape=jax.ShapeDtypeStruct((bsz, n, d), F32),
        scratch_shapes=[pltpu.VMEM((TOP_K, tm, d // 2), U32), pltpu.SemaphoreType.DMA(())],
        compiler_params=_params(("arbitrary", "arbitrary"), vmem), name="moe_combine_ffn_out",
    )(pos, w, x, ff, yb, sh_gu, sh_down, *vecs)


def _moe_dispatch_experts(tokens, packed, router_w, router_bias, w_gu, w_down, layer):
    t = tokens.shape[0]
    idx, w, rank, counts = _router(tokens, router_w, router_bias)
    pad_start, block_e, n_valid, cap = _moe_plan(counts[:, 0], t)
    pos = _positions(idx, rank, pad_start)
    xb = _dispatch(pos, packed, cap)
    return _moe_experts(xb, block_e, n_valid, w_gu, w_down, layer), pos, w


def _rope_tables(n):
    rows = n // GRID_W
    row = jnp.repeat(jnp.arange(rows), GRID_W).astype(F32)
    col = jnp.tile(jnp.arange(GRID_W), rows).astype(F32)
    half = MLA_ROPE // 2
    inv = ROPE_THETA ** (-jnp.arange(0, half, 2, dtype=F32) / half)
    ang = jnp.concatenate([row[:, None] * inv, col[:, None] * inv], -1)
    return jnp.cos(ang), jnp.sin(ang)


def _hyena_filters(n, w1, b1, f1, w2, b2, f2, w3):
    t = jnp.linspace(0.0, 1.0, n, dtype=F32)[:, None]
    w = 2.0 * math.pi * jnp.arange(n, dtype=F32)[:, None] / n
    fr = jnp.linspace(1e-4, HY_BANDS - 1, HY_BANDS, dtype=F32)
    z = jnp.concatenate([t, jnp.cos(fr * w), -jnp.sin(fr * w)], -1)
    hp = lax.Precision.HIGHEST
    h = jnp.sin(f1 * (jnp.dot(z, w1, precision=hp) + b1))
    h = jnp.sin(f2 * (jnp.dot(h, w2, precision=hp) + b2))
    h = jnp.dot(h, w3, precision=hp)
    deltas = jnp.abs(jnp.linspace(HY_DECAY_MIN, HY_DECAY_MAX, HY_WIDTH, dtype=F32))
    decay = jnp.exp(-t * deltas)
    h_f = h[:, :HY_WIDTH] * decay
    h_b = h[:, HY_WIDTH:] * decay
    k = jnp.concatenate([h_f, jnp.zeros((1, HY_WIDTH), F32), h_b[:0:-1]], 0)
    return k / jnp.sum(jnp.abs(k), 0, keepdims=True)


def _split(a):
    hi = a.astype(BF16)
    return hi, (a - hi.astype(F32)).astype(BF16)


def _dot3(a_hi, a_lo, x):
    x_hi, x_lo = _split(x)
    return (jnp.dot(a_hi, x_hi, preferred_element_type=F32) + jnp.dot(a_lo, x_hi, preferred_element_type=F32)
            + jnp.dot(a_hi, x_lo, preferred_element_type=F32))


def _cis(num, den):
    ang = (2.0 * math.pi / den) * (num % den).astype(F32)
    return jnp.cos(ang), jnp.sin(ang)


def _dft_outer_kernel(f_hi_ref, f_lo_ref, x_ref, o_ref):
    o_ref[0] = _dot3(f_hi_ref[...], f_lo_ref[...], x_ref[0])


def _dft_outer(x, f_hi, f_lo):
    bsz, k, m = x.shape
    r = f_hi.shape[0]
    tn = min(DFT_COLS, m)
    assert m % tn == 0
    vmem = 2 * (2 * f_hi.size * 2 + k * tn * 4 + r * tn * 4) + 3 * (k + r) * tn * 4
    return pl.pallas_call(
        _dft_outer_kernel, grid=(bsz, m // tn),
        in_specs=[pl.BlockSpec(f_hi.shape, lambda b, j: (0, 0)), pl.BlockSpec(f_lo.shape, lambda b, j: (0, 0)),
                  pl.BlockSpec((1, k, tn), lambda b, j: (b, 0, j))],
        out_specs=pl.BlockSpec((1, r, tn), lambda b, j: (b, 0, j)),
        out_shape=jax.ShapeDtypeStruct((bsz, r, m), F32),
        compiler_params=_params(("parallel", "parallel"), vmem), name="hyena_dft_outer",
    )(f_hi, f_lo, x)


def _dft_inner_kernel(m_hi_ref, m_lo_ref, mt_hi_ref, mt_lo_ref, a_ref, h_ref, o_ref, *, conv):
    n2 = a_ref.shape[3]
    x = a_ref[0, :, 0].reshape(2 * n2, a_ref.shape[4])
    y = _dot3(m_hi_ref[0], m_lo_ref[0], x)
    if conv:
        h = h_ref[0, :, 0].reshape(2 * n2, h_ref.shape[4])
        yr, yi, hr, hi = y[:n2], y[n2:], h[:n2], h[n2:]
        prod = jnp.concatenate([yr * hr - yi * hi, yr * hi + yi * hr], axis=0)
        y = _dot3(mt_hi_ref[0], mt_lo_ref[0], prod)
    o_ref[0, :, 0] = y.reshape(2, n2, y.shape[1])


def _dft_inner(a, h, mats, conv):
    bsz, _, n1, n2, c = a.shape
    blk = lambda sel: pl.BlockSpec((1, 2, 1, n2, c), sel)
    mat = pl.BlockSpec((1, 2 * n2, 2 * n2), lambda k, b: (k, 0, 0))
    vmem = 2 * (4 * 4 * n2 * n2 * 2 + 3 * 2 * n2 * c * 4) + 8 * 2 * n2 * c * 4
    return pl.pallas_call(
        functools.partial(_dft_inner_kernel, conv=conv), grid=(n1, bsz),
        in_specs=[mat, mat, mat, mat, blk(lambda k, b: (b, 0, k, 0, 0)), blk(lambda k, b: (0, 0, k, 0, 0))],
        out_specs=blk(lambda k, b: (b, 0, k, 0, 0)), out_shape=jax.ShapeDtypeStruct(a.shape, F32),
        compiler_params=_params(("parallel", "arbitrary"), vmem),
        name="hyena_dft_inner_conv" if conv else "hyena_dft_inner",
    )(*mats, a, h)


def _dft_final_kernel(fd_hi_ref, fd_lo_ref, a_ref, z_ref, x0_ref, skip_ref, o_ref):
    y = _dot3(fd_hi_ref[...], fd_lo_ref[...], a_ref[0])
    o_ref[0] = (y + z_ref[0] * skip_ref[...]) * x0_ref[0]


def _dft_final(a, z, x0, skip_t, fd_hi, fd_lo):
    bsz, r2, m = a.shape
    k = fd_hi.shape[0]
    tn = min(DFT_COLS, m)
    row = pl.BlockSpec((1, k, tn), lambda b, j: (b, 0, j))
    vmem = 2 * (2 * fd_hi.size * 2 + r2 * tn * 4 + 3 * k * tn * 4) + 3 * (k + r2) * tn * 4
    return pl.pallas_call(
        _dft_final_kernel, grid=(bsz, m // tn),
        in_specs=[pl.BlockSpec(fd_hi.shape, lambda b, j: (0, 0)), pl.BlockSpec(fd_lo.shape, lambda b, j: (0, 0)),
                  pl.BlockSpec((1, r2, tn), lambda b, j: (b, 0, j)), row, row,
                  pl.BlockSpec((1, tn), lambda b, j: (0, 0))],
        out_specs=row, out_shape=jax.ShapeDtypeStruct((bsz, k, m), F32),
        compiler_params=_params(("parallel", "parallel"), vmem), name="hyena_dft_final",
    )(fd_hi, fd_lo, a, z, x0, skip_t)


def _hyena_long_conv(z, x0, k, skip):
    bsz, n, c = z.shape
    n2 = DFT_INNER
    nn = 2 * n
    n1 = nn // n2
    assert nn == n1 * n2 and n1 % 2 == 0
    half = n1 // 2
    m = n2 * c
    j1 = jnp.arange(n1)
    ca, sa = _cis(j1[:, None] * j1[None, :], n1)
    fa = jnp.concatenate([ca, -sa], axis=0)
    fd = jnp.concatenate([ca, -sa], axis=1)[:half] / nn
    j2 = jnp.arange(n2)
    cb, sb = _cis(j2[None, None, :] * (n1 * j2[None, :, None] + j1[:, None, None]), nn)
    mb = jnp.concatenate([jnp.concatenate([cb, sb], 2), jnp.concatenate([-sb, cb], 2)], 1)
    mats = _split(mb) + _split(jnp.swapaxes(mb, 1, 2))
    fa_hi, fa_lo = _split(fa)
    hk = _dft_outer(k.reshape(1, n1, m), fa_hi, fa_lo).reshape(1, 2, n1, n2, c)
    hk = _dft_inner(hk, hk, mats, conv=False)
    a = _dft_outer(z.reshape(bsz, half, m), fa_hi[:, :half], fa_lo[:, :half]).reshape(bsz, 2, n1, n2, c)
    a = _dft_inner(a, hk, mats, conv=True).reshape(bsz, 2 * n1, m)
    skip_t = jnp.tile(skip.reshape(1, c), (1, min(DFT_COLS, m) // c))
    out = _dft_final(a, z.reshape(bsz, half, m), x0.reshape(bsz, half, m), skip_t, *_split(fd))
    return out.reshape(bsz, n, c)


def _short_conv_kernel(f_hi_ref, f_lo_ref, fi_hi_ref, fi_lo_ref, z_ref, x0_ref, k_ref, skip_ref, o_ref):
    n = z_ref.shape[1]
    nn = 2 * n
    z = z_ref[0]
    hk = _dot3(f_hi_ref[...], f_lo_ref[...], k_ref[...])
    zs = _dot3(f_hi_ref[:, :n], f_lo_ref[:, :n], z)
    zr, zi, hr, hi = zs[:nn], zs[nn:], hk[:nn], hk[nn:]
    prod = jnp.concatenate([zr * hr - zi * hi, zr * hi + zi * hr], axis=0)
    o_ref[0] = (_dot3(fi_hi_ref[...], fi_lo_ref[...], prod) + z * skip_ref[...]) * x0_ref[0]


def _hyena_short_conv(z, x0, k, skip):
    bsz, n, c = z.shape
    nn = 2 * n
    idx = jnp.arange(nn)
    cf, sf = _cis(idx[:, None] * idx[None, :], nn)
    f = jnp.concatenate([cf, -sf], axis=0)
    fi = jnp.concatenate([cf, -sf], axis=1)[:n] / nn
    full = lambda a: pl.BlockSpec(a.shape, lambda b: (0,) * a.ndim)
    row = pl.BlockSpec((1, n, c), lambda b: (b, 0, 0))
    ops = _split(f) + _split(fi)
    vmem = 2 * (sum(a.size * 2 for a in ops) + 3 * n * c * 4 + nn * c * 4) + 12 * 2 * nn * c * 4
    return pl.pallas_call(
        _short_conv_kernel, grid=(bsz,),
        in_specs=[full(a) for a in ops] + [row, row, full(k), pl.BlockSpec((1, c), lambda b: (0, 0))],
        out_specs=row, out_shape=jax.ShapeDtypeStruct((bsz, n, c), F32),
        compiler_params=_params(("parallel",), vmem), name="hyena_short_conv",
    )(*ops, z, x0, k, skip.reshape(1, c))


def _hyena_sequence(z, x0, filt, skip):
    n = z.shape[1]
    k = _hyena_filters(n, *filt)
    conv = _hyena_short_conv if n <= HY_DIRECT_MAX else _hyena_long_conv
    return conv(z, x0, k, skip)


def _rms(x, g):
    return x * lax.rsqrt(jnp.mean(x * x, -1, keepdims=True) + RMS_EPS) * g


def _mla_in_kernel(xc_ref, xp_ref, xn_ref, sh_ref, sc_ref, why_ref, cw_ref, cb_ref, wql_ref, qg_ref, wqa_ref, wqb_ref,
                   wkvl_ref, kvg_ref, wk_ref, wv_ref, wpa_ref, wpb_ref, qa_ref, qb_ref, ka_ref, kb_ref,
                   x0_ref, z_ref, q_ref, k_ref, vt_ref):
    tm = xc_ref.shape[1]
    rows = jnp.concatenate([xp_ref[0], xc_ref[0], xn_ref[0]], axis=0)
    h = (rows * (1.0 + sc_ref[0]) + sh_ref[0]).astype(BF16)
    hc = h[SUBLANES:SUBLANES + tm]
    y = _conv3(jnp.dot(h, why_ref[...], preferred_element_type=F32), cw_ref, cb_ref, tm)
    x0_ref[0] = y[:, :HY_WIDTH]
    z_ref[0] = y[:, 2 * HY_WIDTH:] * y[:, HY_WIDTH:2 * HY_WIDTH]
    ql = _rms(jnp.dot(hc, wql_ref[...], preferred_element_type=F32), qg_ref[...]).astype(BF16)
    nh = q_ref.shape[1]
    qa = jnp.concatenate([qa_ref[...]] * nh, axis=1)
    qb = jnp.concatenate([qb_ref[...]] * nh, axis=1)
    q = (jnp.dot(ql, wqa_ref[...], preferred_element_type=F32) * qa
         + jnp.dot(ql, wqb_ref[...], preferred_element_type=F32) * qb).astype(BF16)
    kvl = _rms(jnp.dot(hc, wkvl_ref[...], preferred_element_type=F32), kvg_ref[...]).astype(BF16)
    kn = jnp.dot(kvl, wk_ref[...], preferred_element_type=F32)
    v_t = jnp.dot(kvl, wv_ref[...], preferred_element_type=F32).T
    kpe = (jnp.dot(hc, wpa_ref[...], preferred_element_type=F32) * ka_ref[...]
           + jnp.dot(hc, wpb_ref[...], preferred_element_type=F32) * kb_ref[...])
    ones = jnp.ones((ATT_ONES_ROWS, tm), BF16)
    for hd in range(nh):
        q_ref[0, hd] = q[:, hd * MLA_SLAB:(hd + 1) * MLA_SLAB]
        k_ref[0, hd] = (kn[:, hd * MLA_SLAB:(hd + 1) * MLA_SLAB] + kpe).astype(BF16)
        vt_ref[0, hd, :MLA_V, :] = v_t[hd * MLA_V:(hd + 1) * MLA_V].astype(BF16)
        vt_ref[0, hd, MLA_V:, :] = ones


def _rot_cols(w_pe):
    ev, od = w_pe[..., 0::2], w_pe[..., 1::2]
    return jnp.concatenate([ev, od], -1), jnp.concatenate([-od, ev], -1)


def _mla_weights(p):
    w_in = p['w_in']
    kin = w_in.shape[0]
    wq = p['w_qb'].reshape(MLA_Q_RANK, MLA_HEADS, MLA_QK)
    qa_pe, qb_pe = _rot_cols(wq[..., MLA_NOPE:])
    zq = jnp.zeros((MLA_Q_RANK, MLA_HEADS, MLA_SLAB - MLA_QK), F32)
    w_qa = jnp.concatenate([wq[..., :MLA_NOPE], qa_pe, zq], -1).reshape(MLA_Q_RANK, MLA_HEADS * MLA_SLAB)
    w_qb = jnp.concatenate([jnp.zeros_like(wq[..., :MLA_NOPE]), qb_pe, zq], -1)
    w_qb = w_qb.reshape(MLA_Q_RANK, MLA_HEADS * MLA_SLAB)
    wkv = p['w_kvb'].reshape(MLA_KV_RANK, MLA_HEADS, MLA_NOPE + MLA_V)
    w_k = jnp.concatenate([wkv[..., :MLA_NOPE], jnp.zeros((MLA_KV_RANK, MLA_HEADS, MLA_SLAB - MLA_NOPE), F32)], -1)
    w_k = w_k.reshape(MLA_KV_RANK, MLA_HEADS * MLA_SLAB)
    w_v = wkv[..., MLA_NOPE:].reshape(MLA_KV_RANK, MLA_HEADS * MLA_V)
    pa, pb = _rot_cols(w_in[:, OFF_KPE:])
    left, right = jnp.zeros((kin, MLA_NOPE), F32), jnp.zeros((kin, MLA_SLAB - MLA_QK), F32)
    w_pa = jnp.concatenate([left, pa, right], -1)
    w_pb = jnp.concatenate([left, pb, right], -1)
    bf = lambda a: a.astype(BF16)
    return dict(w_hy=bf(w_in[:, :OFF_Q]), conv_w=p['conv_w'], conv_b=p['conv_b'][None],
                w_ql=bf(w_in[:, OFF_Q:OFF_KV]), q_g=p['q_norm'][None], w_qa=bf(w_qa), w_qb=bf(w_qb),
                w_kvl=bf(w_in[:, OFF_KV:OFF_KPE]), kv_g=p['kv_norm'][None], w_k=bf(w_k), w_v=bf(w_v),
                w_pa=bf(w_pa), w_pb=bf(w_pb))


def _rope_slabs(n, rotate):
    one = jnp.ones((n, MLA_NOPE), F32)
    zero = jnp.zeros((n, MLA_SLAB - MLA_QK), F32)
    if rotate:
        cos, sin = _rope_tables(n)
    else:
        cos, sin = jnp.ones((n, MLA_ROPE // 2), F32), jnp.zeros((n, MLA_ROPE // 2), F32)
    return (jnp.concatenate([one, cos, cos, zero], -1), jnp.concatenate([jnp.zeros_like(one), sin, sin, zero], -1))


def _mla_in_proj(x, shift, scale, w, rotate):
    bsz, n, kin = x.shape
    tm = min(ROW_TILE, n)
    assert n % tm == 0
    ca, sb = _rope_slabs(n, rotate)
    s = MLA_SCALE * LOG2E
    tabs = [ca * s, sb * s, ca, sb]
    consts = [w['w_hy'], w['conv_w'], w['conv_b'], w['w_ql'], w['q_g'], w['w_qa'], w['w_qb'], w['w_kvl'], w['kv_g'],
              w['w_k'], w['w_v'], w['w_pa'], w['w_pb']]
    full = lambda a: pl.BlockSpec(a.shape, lambda b, i: (0,) * a.ndim)
    row = lambda c: pl.BlockSpec((1, tm, c), lambda b, i: (b, i, 0))
    head = pl.BlockSpec((1, MLA_HEADS, tm, MLA_SLAB), lambda b, i: (b, 0, i, 0))
    dva = MLA_V + ATT_ONES_ROWS
    vmem = 2 * (tm * kin * 4 + sum(a.size * a.dtype.itemsize for a in consts) + 2 * tm * HY_WIDTH * 4
                + 3 * MLA_HEADS * tm * MLA_SLAB * 2 + 4 * tm * MLA_SLAB * 4) + 8 * (tm + 16) * 3 * HY_WIDTH * 4
    return pl.pallas_call(
        _mla_in_kernel, grid=(bsz, n // tm),
        in_specs=_halo_specs(n, tm, kin) + [_bcast_spec(shift), _bcast_spec(scale)] + [full(a) for a in consts]
        + [pl.BlockSpec((tm, MLA_SLAB), lambda b, i: (i, 0))] * 4,
        out_specs=[row(HY_WIDTH), row(HY_WIDTH), head, head,
                   pl.BlockSpec((1, MLA_HEADS, dva, tm), lambda b, i: (b, 0, 0, i))],
        out_shape=[jax.ShapeDtypeStruct((bsz, n, HY_WIDTH), F32)] * 2
        + [jax.ShapeDtypeStruct((bsz, MLA_HEADS, n, MLA_SLAB), BF16)] * 2
        + [jax.ShapeDtypeStruct((bsz, MLA_HEADS, dva, n), BF16)],
        compiler_params=_params(("parallel", "parallel"), vmem), name="mla_in_proj",
    )(x, x, x, shift, scale, *consts, *tabs)


def _mixer_hyena_mla(x, ctx, sh_l, sc_l, sh_c, sc_c, p):
    n, nc = x.shape[1], ctx.shape[1]
    w = _mla_weights(p)
    x0_l, z_l, q_l, k_l, vt_l = _mla_in_proj(x, sh_l, sc_l, w, rotate=True)
    x0_c, z_c, q_c, k_c, vt_c = _mla_in_proj(ctx, sh_c, sc_c, w, rotate=False)
    k_all = jnp.concatenate([k_c, k_l], 2)
    vt_all = jnp.concatenate([vt_c, vt_l], 3)
    att_l = _attention(q_l, k_all, vt_all, tq=min(ATT_Q_TILE, n), tk=_key_tile(nc + n))
    att_c = _attention(q_c, k_c, vt_c, tq=nc, tk=_key_tile(nc))
    filt = (p['filt_w1'], p['filt_b1'], p['filt_freq1'], p['filt_w2'], p['filt_b2'], p['filt_freq2'], p['filt_w3'])
    hyo_l = _hyena_sequence(z_l, x0_l, filt, p['skip'])
    hyo_c = _hyena_sequence(z_c, x0_c, filt, p['skip'])
    return (hyo_l, att_l), (hyo_c, att_c)


def _halo_specs(n, tm, k):
    nb = n // SUBLANES
    per = tm // SUBLANES
    return [pl.BlockSpec((1, tm, k), lambda b, i: (b, i, 0)),
            pl.BlockSpec((1, SUBLANES, k), lambda b, i: (b, jnp.maximum(i * per - 1, 0), 0)),
            pl.BlockSpec((1, SUBLANES, k), lambda b, i: (b, jnp.minimum((i + 1) * per, nb - 1), 0))]


def _conv3(u, cw_ref, cb_ref, tm):
    i = pl.program_id(1)
    rows = lax.broadcasted_iota(I32, (u.shape[0], 1), 0)
    inside = jnp.logical_and(jnp.logical_or(rows >= SUBLANES, i > 0),
                             jnp.logical_or(rows < tm + SUBLANES, i < pl.num_programs(1) - 1))
    u = jnp.where(inside, u, 0.0)
    prev = pltpu.roll(u, 1, 0)[SUBLANES:SUBLANES + tm]
    nxt = pltpu.roll(u, u.shape[0] - 1, 0)[SUBLANES:SUBLANES + tm]
    return cw_ref[0:1, :] * prev + cw_ref[1:2, :] * u[SUBLANES:SUBLANES + tm] + cw_ref[2:3, :] * nxt + cb_ref[...]


def _ssd_in_kernel(xc_ref, xp_ref, xn_ref, sh_ref, sc_ref, wz_ref, wx_ref, wdt_ref, cw_ref, cb_ref, dtb_ref,
                   z_ref, xs_ref, b_ref, c_ref, dt_ref):
    tm = xc_ref.shape[1]
    rows = jnp.concatenate([xp_ref[0], xc_ref[0], xn_ref[0]], axis=0)
    h = (rows * (1.0 + sc_ref[0]) + sh_ref[0]).astype(BF16)
    hc = h[SUBLANES:SUBLANES + tm]
    z_ref[0] = jnp.dot(hc, wz_ref[...], preferred_element_type=F32)
    y = _conv3(jnp.dot(h, wx_ref[...], preferred_element_type=F32), cw_ref, cb_ref, tm)
    xbc = y * jax.nn.sigmoid(y)
    xs_ref[0] = xbc[:, :SSD_INNER]
    b_ref[0] = xbc[:, SSD_INNER:SSD_INNER + SSD_BC]
    c_ref[0] = xbc[:, SSD_INNER + SSD_BC:]
    dt = jnp.dot(hc, wdt_ref[...], preferred_element_type=F32) + dtb_ref[...]
    dt_ref[0] = (jnp.maximum(dt, 0.0) + jnp.log1p(jnp.exp(-jnp.abs(dt)))).T


def _ssd_in_proj(x, shift, scale, w_z, w_xbc, w_dt, conv_w, conv_b, dt_bias):
    bsz, n, k = x.shape
    tm = min(ROW_TILE, n)
    assert n % tm == 0
    full = lambda a: pl.BlockSpec(a.shape, lambda b, i: (0,) * a.ndim)
    row = lambda c: pl.BlockSpec((1, tm, c), lambda b, i: (b, i, 0))
    nh2 = w_dt.shape[1]
    consts = [w_z, w_xbc, w_dt, conv_w, conv_b, dt_bias]
    widths = [SSD_INNER, SSD_INNER, SSD_BC, SSD_BC]
    vmem = 2 * (tm * k * 4 + sum(a.size * a.dtype.itemsize for a in consts) + tm * (sum(widths) + nh2) * 4) \
        + 6 * (tm + 2 * SUBLANES) * SSD_XBC * 4
    return pl.pallas_call(
        _ssd_in_kernel, grid=(bsz, n // tm),
        in_specs=_halo_specs(n, tm, k) + [_bcast_spec(shift), _bcast_spec(scale)] + [full(a) for a in consts],
        out_specs=[row(c) for c in widths] + [pl.BlockSpec((1, nh2, tm), lambda b, i: (b, 0, i))],
        out_shape=[jax.ShapeDtypeStruct((bsz, n, c), F32) for c in widths]
        + [jax.ShapeDtypeStruct((bsz, nh2, n), F32)],
        compiler_params=_params(("parallel", "parallel"), vmem), name="ssd_in_proj",
    )(x, x, x, shift, scale, *consts)


def _ssd_scan_kernel(x_ref, b_ref, c_ref, dt_ref, a_ref, s0_ref, y_ref, sfin_ref, st_sc, *, reverse):
    ci = pl.program_id(2)
    q = SSD_CHUNK

    @pl.when(ci == 0)
    def _():
        st_sc[...] = s0_ref[0, 0]

    dt = dt_ref[0]
    a = dt * a_ref[0]
    si = lax.broadcasted_iota(I32, (q, q), 0)
    li = lax.broadcasted_iota(I32, (q, q), 1)
    incl = jnp.where((si >= li) if reverse else (si <= li), 1.0, 0.0)
    hp = lax.Precision.HIGHEST
    acs = jnp.dot(a, incl, precision=hp, preferred_element_type=F32)
    tot = jnp.dot(a, jnp.ones((q, LANES), F32), precision=hp, preferred_element_type=F32)
    e_in = jnp.exp(acs)
    w_end = jnp.exp(tot - acs) * dt
    e_tot = jnp.exp(tot)
    acs_t = acs.T
    e_in_t = e_in.T
    mask = (li >= si) if reverse else (li <= si)
    cmat = c_ref[0]
    cb = lax.dot_general(cmat.astype(BF16), b_ref[0].astype(BF16), (((1,), (1,)), ((), ())),
                         preferred_element_type=F32)
    bt = b_ref[0].T
    first = lax.broadcasted_iota(I32, (q, LANES), 1) < SSD_HEADDIM
    ys = []
    for pr in range(SSD_HPG // 2):
        lo, hi = pr * LANES, (pr + 1) * LANES
        x_pair = x_ref[0, :, lo:hi].astype(BF16)
        st_pair = st_sc[:, lo:hi]
        rhs = jnp.concatenate([x_pair, st_pair.astype(BF16)], axis=0)
        y2, s2 = [], []
        for r in (2 * pr, 2 * pr + 1):
            seg = acs_t[:, r:r + 1] - acs[r:r + 1, :]
            m = cb * jnp.exp(jnp.where(mask, seg, -jnp.inf)) * dt[r:r + 1, :]
            lhs = jnp.concatenate([m.astype(BF16), (cmat * e_in_t[:, r:r + 1]).astype(BF16)], axis=1)
            y2.append(jnp.dot(lhs, rhs, preferred_element_type=F32))
            btr = (bt * w_end[r:r + 1, :]).astype(BF16)
            s2.append(e_tot[r:r + 1, :] * st_pair + jnp.dot(btr, x_pair, preferred_element_type=F32))
        ys.append(jnp.where(first, y2[0], y2[1]))
        st_sc[:, lo:hi] = jnp.where(first, s2[0], s2[1])
    y_ref[0] = jnp.concatenate(ys, axis=1)

    @pl.when(ci == pl.num_programs(2) - 1)
    def _():
        sfin_ref[0, 0] = st_sc[...]


def _ssd_scan(xs, bm, cm, dt_t, a_rep, s0, reverse):
    bsz, n, _ = xs.shape
    q = SSD_CHUNK
    nc = n // q
    assert n % q == 0 and q == LANES
    d = 1 if reverse else 0
    cc = (lambda c: nc - 1 - c) if reverse else (lambda c: c)
    gw = SSD_HPG * SSD_HEADDIM
    state = pl.BlockSpec((1, 1, SSD_STATE, gw), lambda b, g, c: (b, g, 0, 0))
    vmem = 2 * (2 * q * gw * 4 + 2 * q * SSD_STATE * 4 + 2 * SSD_STATE * gw * 4) + SSD_STATE * gw * 4 + 64 * q * q * 4
    return pl.pallas_call(
        functools.partial(_ssd_scan_kernel, reverse=reverse), grid=(bsz, SSD_GROUPS, nc),
        in_specs=[pl.BlockSpec((1, q, gw), lambda b, g, c: (b, cc(c), g)),
                  pl.BlockSpec((1, q, SSD_STATE), lambda b, g, c: (b, cc(c), g)),
                  pl.BlockSpec((1, q, SSD_STATE), lambda b, g, c: (b, cc(c), g)),
                  pl.BlockSpec((1, SSD_HPG, q), lambda b, g, c: (b, d * SSD_GROUPS + g, cc(c))),
                  pl.BlockSpec((1, SSD_HPG, LANES), lambda b, g, c: (d * SSD_GROUPS + g, 0, 0)),
                  state],
        out_specs=[pl.BlockSpec((1, q, gw), lambda b, g, c: (b, cc(c), g)), state],
        out_shape=[jax.ShapeDtypeStruct((bsz, n, SSD_INNER), F32),
                   jax.ShapeDtypeStruct((bsz, SSD_GROUPS, SSD_STATE, gw), F32)],
        scratch_shapes=[pltpu.VMEM((SSD_STATE, gw), F32)],
        compiler_params=_params(("parallel", "parallel", "arbitrary"), vmem),
        name="ssd_scan_rev" if reverse else "ssd_scan_fwd",
    )(xs, bm, cm, dt_t, a_rep, s0)


def _ssd_out_kernel(x_ref, yf_ref, yb_ref, xs_ref, z_ref, d_ref, ng_ref, w_ref, *epilogue_refs):
    z = z_ref[0]
    y = (yf_ref[0] + yb_ref[0] + xs_ref[0] * d_ref[0]) * (z * jax.nn.sigmoid(z))
    gw = SSD_INNER // SSD_GROUPS
    parts = []
    for g in range(SSD_GROUPS):
        yg = y[:, g * gw:(g + 1) * gw]
        parts.append(yg * lax.rsqrt(jnp.mean(yg * yg, -1, keepdims=True) + RMS_EPS))
    yn = (jnp.concatenate(parts, axis=1) * ng_ref[0]).astype(BF16)
    _mix_epilogue(x_ref[0], jnp.dot(yn, w_ref[...], preferred_element_type=F32), *epilogue_refs)


def _ssd_out(x, y_f, y_b, xs, z, d_rep, norm_g, w_out, gate, ln_g, ln_b, shift2, scale2):
    bsz, n, d = x.shape
    tm = min(ROW_TILE, n)
    assert n % tm == 0
    row = lambda c: pl.BlockSpec((1, tm, c), lambda b, i: (b, i, 0))
    vecs = [gate, ln_g, ln_b, shift2, scale2]
    in_specs = [row(d)] + [row(SSD_INNER)] * 4 + [_bcast_spec(d_rep), _bcast_spec(norm_g),
                                                 pl.BlockSpec(w_out.shape, lambda b, i: (0, 0))]
    in_specs += [_bcast_spec(a) for a in vecs]
    vmem = 2 * (3 * tm * d * 4 + 4 * tm * SSD_INNER * 4 + w_out.size * 2) + 4 * tm * SSD_INNER * 4
    return pl.pallas_call(
        _ssd_out_kernel, grid=(bsz, n // tm), in_specs=in_specs, out_specs=[row(d), row(d), row(d // 2)],
        out_shape=[jax.ShapeDtypeStruct((bsz, n, d), F32)] * 2 + [jax.ShapeDtypeStruct((bsz, n, d // 2), U32)],
        compiler_params=_params(("parallel", "parallel"), vmem), name="ssd_out",
    )(x, y_f, y_b, xs, z, d_rep, norm_g, w_out, *vecs)


def _mixer_ssd(x, ctx, sh_l, sc_l, sh_c, sc_c, p):
    a_all = -jnp.exp(jnp.concatenate([p['a_log_f'], p['a_log_b']]))
    a_rep = jnp.broadcast_to(a_all.reshape(2 * SSD_GROUPS, SSD_HPG, 1), (2 * SSD_GROUPS, SSD_HPG, LANES))
    w_in = p['w_in'].astype(BF16)
    consts = (w_in[:, :SSD_INNER], w_in[:, SSD_INNER:SSD_INNER + SSD_XBC], w_in[:, SSD_INNER + SSD_XBC:],
              p['conv_w'], p['conv_b'][None], jnp.concatenate([p['dt_bias_f'], p['dt_bias_b']])[None])
    _, xc, bc, cc, dtc = _ssd_in_proj(ctx, sh_c, sc_c, *consts)
    zl, xl, bl, cl, dtl = _ssd_in_proj(x, sh_l, sc_l, *consts)
    s0 = jnp.zeros((ctx.shape[0], SSD_GROUPS, SSD_STATE, SSD_HPG * SSD_HEADDIM), F32)
    _, sc_f = _ssd_scan(xc, bc, cc, dtc, a_rep, s0, reverse=False)
    _, sc_b = _ssd_scan(xc, bc, cc, dtc, a_rep, s0, reverse=True)
    y_f, _ = _ssd_scan(xl, bl, cl, dtl, a_rep, sc_f, reverse=False)
    y_b, _ = _ssd_scan(xl, bl, cl, dtl, a_rep, sc_b, reverse=True)
    return y_f, y_b, xl, zl


def kernel(x, c, ctx, c_ctx, mod_w, mod_b, ln_mix_g, ln_mix_b, ln_ffn_g, ln_ffn_b, a_w_in, hy_conv_w, hy_conv_b, hy_filt_w1, hy_filt_b1, hy_filt_freq1, hy_filt_w2, hy_filt_b2, hy_filt_freq2, hy_filt_w3, hy_skip, mla_q_norm, mla_w_qb, mla_kv_norm, mla_w_kvb, a_w_out, ssd_w_in, ssd_conv_w, ssd_conv_b, ssd_dt_bias_f, ssd_dt_bias_b, ssd_a_log_f, ssd_a_log_b, ssd_d, ssd_norm_g, ssd_w_out, router_w, router_bias, exp_w_gu, exp_w_down, sh_w_gu, sh_w_down):
    bsz, n_lat, d = x.shape
    n_ctx = ctx.shape[1]
    hp = lax.Precision.HIGHEST
    for l in range(DEPTH):
        last = l == DEPTH - 1
        i = l // 2
        mod = (jnp.dot(jax.nn.silu(c), mod_w[l], precision=hp) + mod_b[l]).reshape(bsz, N_MOD, 1, d)
        mod_c = (jnp.dot(jax.nn.silu(c_ctx), mod_w[l], precision=hp) + mod_b[l]).reshape(1, N_MOD, 1, d)
        sh1, sc1, g1, sh2, sc2, g2 = [mod[:, j] for j in range(N_MOD)]
        csh1, csc1, cg1, csh2, csc2, cg2 = [mod_c[:, j] for j in range(N_MOD)]
        vec = lambda a: a.reshape(1, 1, d)
        if l % 2 == 0:
            p = {"w_in": a_w_in[i], "conv_w": hy_conv_w[i], "conv_b": hy_conv_b[i],
                 "filt_w1": hy_filt_w1[i], "filt_b1": hy_filt_b1[i], "filt_freq1": hy_filt_freq1[i],
                 "filt_w2": hy_filt_w2[i], "filt_b2": hy_filt_b2[i], "filt_freq2": hy_filt_freq2[i],
                 "filt_w3": hy_filt_w3[i], "skip": hy_skip[i], "q_norm": mla_q_norm[i], "w_qb": mla_w_qb[i],
                 "kv_norm": mla_kv_norm[i], "w_kvb": mla_w_kvb[i]}
            ys_l, ys_c = _mixer_hyena_mla(x, ctx, sh1, sc1, csh1, csc1, p)
            w_out = a_w_out[i].astype(BF16)
            ws = [w_out[:HY_WIDTH], w_out[HY_WIDTH:]]
            x, ff_x, fp_x = _mix_out(x, ys_l, ws, g1, vec(ln_mix_g[l]), vec(ln_mix_b[l]), sh2, sc2)
        else:
            p = {"w_in": ssd_w_in[i], "conv_w": ssd_conv_w[i], "conv_b": ssd_conv_b[i],
                 "dt_bias_f": ssd_dt_bias_f[i], "dt_bias_b": ssd_dt_bias_b[i],
                 "a_log_f": ssd_a_log_f[i], "a_log_b": ssd_a_log_b[i]}
            assert last
            y_f, y_b, xs, z = _mixer_ssd(x, ctx, sh1, sc1, csh1, csc1, p)
            d_rep = jnp.repeat(ssd_d[i], SSD_HEADDIM).reshape(1, 1, SSD_INNER)
            x, ff_x, fp_x = _ssd_out(x, y_f, y_b, xs, z, d_rep, ssd_norm_g[i].reshape(1, 1, SSD_INNER),
                                     ssd_w_out[i].astype(BF16), g1, vec(ln_mix_g[l]), vec(ln_mix_b[l]), sh2, sc2)
        sh_gu = sh_w_gu[l].astype(BF16)
        sh_down = sh_w_down[l].astype(BF16)
        ln_g, ln_b = vec(ln_ffn_g[l]), vec(ln_ffn_b[l])
        moe_w = (router_w[l], router_bias[l], exp_w_gu, exp_w_down, l)
        if last:
            yb, pos, w = _moe_dispatch_experts(ff_x.reshape(-1, d), fp_x.reshape(-1, d // 2), *moe_w)
            x = _ffn_out(x, ff_x, yb, pos, w, 0, sh_gu, sh_down, g2, ln_g, ln_b)
        else:
            ctx, ff_c, fp_c = _mix_out(ctx, ys_c, ws, cg1, vec(ln_mix_g[l]), vec(ln_mix_b[l]), csh2, csc2)
            tokens = jnp.concatenate([ff_c.reshape(-1, d), ff_x.reshape(-1, d)], 0)
            packed = jnp.concatenate([fp_c.reshape(-1, d // 2), fp_x.reshape(-1, d // 2)], 0)
            yb, pos, w = _moe_dispatch_experts(tokens, packed, *moe_w)
            assert (bsz * n_ctx) % ROUTE_TOKENS == 0
            ctx = _ffn_out(ctx, ff_c, yb, pos, w, 0, sh_gu, sh_down, cg2, ln_g, ln_b)
            x = _ffn_out(x, ff_x, yb, pos, w, bsz * n_ctx // ROUTE_TOKENS, sh_gu, sh_down, g2, ln_g, ln_b)
    return x
```

```python
import functools
import math

import jax
import jax.numpy as jnp
from jax import lax
from jax.experimental import pallas as pl
from jax.experimental.pallas import tpu as pltpu

F32 = jnp.float32
BF16 = jnp.bfloat16
I32 = jnp.int32
U32 = jnp.uint32

D_MODEL = 1024
DEPTH = 2
GRID_W = 64
N_MOD = 6

HY_WIDTH = 512
HY_EMB = 33
HY_BANDS = (HY_EMB - 1) // 2
HY_TARGET = 1e-2
HY_FAST_DECAY = 0.3
HY_SLOW_DECAY = 1.5
HY_DECAY_MIN = math.log(HY_TARGET) / HY_SLOW_DECAY
HY_DECAY_MAX = math.log(HY_TARGET) / HY_FAST_DECAY

MLA_HEADS = 8
MLA_NOPE = 64
MLA_ROPE = 32
MLA_V = 64
MLA_Q_RANK = 256
MLA_KV_RANK = 128
MLA_QK = MLA_NOPE + MLA_ROPE
MLA_SCALE = MLA_QK ** -0.5
ROPE_THETA = 10000.0
LOG2E = math.log2(math.e)

OFF_Q = 3 * HY_WIDTH
OFF_KV = OFF_Q + MLA_Q_RANK
OFF_KPE = OFF_KV + MLA_KV_RANK

SSD_INNER = 2 * D_MODEL
SSD_HEADDIM = 64
SSD_HEADS = SSD_INNER // SSD_HEADDIM
SSD_GROUPS = 4
SSD_STATE = 128
SSD_CHUNK = 128
SSD_BC = SSD_GROUPS * SSD_STATE
SSD_XBC = SSD_INNER + 2 * SSD_BC
SSD_HPG = SSD_HEADS // SSD_GROUPS

N_EXPERTS = 256
TOP_K = 8
N_EXPERT_GROUPS = 8
TOPK_GROUPS = 4
EXPERT_DIM = 256
ROUTED_SCALE = 2.5

DN_ALPHA = (2 * DEPTH) ** 0.25
LN_EPS = 1e-5
RMS_EPS = 1e-6

LANES = 128
SUBLANES = 8
V7X_VMEM_CAP = 56 * 1024 * 1024

MOE_ROWS = 512
MOE_SUB_ROWS = 256
ROUTE_TOKENS = 256
ROW_TILE = 256
ATT_Q_TILE = 2048
ATT_HEADS_PER_STEP = 2
ATT_ONES_ROWS = 16
MLA_SLAB = LANES
DFT_INNER = 128
DFT_COLS = 2048
HY_DIRECT_MAX = 512
FILT_ROWS = 512
MOD_COLS = 1024


def _params(semantics, vmem_bytes, **kw):
    limit = int(min(max(vmem_bytes * 5 // 4, 32 * 1024 * 1024), V7X_VMEM_CAP))
    return pltpu.CompilerParams(dimension_semantics=semantics, vmem_limit_bytes=limit, **kw)


def _bcast_spec(a):
    if a.shape[0] == 1:
        return pl.BlockSpec((1, 1, a.shape[2]), lambda b, i: (0, 0, 0))
    return pl.BlockSpec((1, 1, a.shape[2]), lambda b, i: (b, 0, 0))


def _layer_norm(r, g, b):
    mu = jnp.mean(r, -1, keepdims=True)
    c = r - mu
    var = jnp.mean(c * c, -1, keepdims=True)
    return c * lax.rsqrt(var + LN_EPS) * g + b


def _swiglu_rows(xb, w_gu, w_down):
    h = jnp.dot(xb, w_gu, preferred_element_type=F32)
    half = h.shape[1] // 2
    g, u = h[:, :half], h[:, half:]
    a = (g * jax.nn.sigmoid(g) * u).astype(BF16)
    return jnp.dot(a, w_down, preferred_element_type=F32)


def _mod_kernel(c_ref, w_ref, b_ref, o_ref):
    c = c_ref[...]
    o_ref[...] = jnp.dot(c * jax.nn.sigmoid(c), w_ref[0], precision=lax.Precision.HIGHEST,
                         preferred_element_type=F32) + b_ref[0]


def _modulation(cond, mod_w, mod_b, layer):
    r, d = cond.shape
    nout = mod_w.shape[2]
    assert nout % MOD_COLS == 0 and r % SUBLANES == 0
    return pl.pallas_call(
        _mod_kernel, grid=(nout // MOD_COLS,),
        in_specs=[pl.BlockSpec((r, d), lambda j: (0, 0)), pl.BlockSpec((1, d, MOD_COLS), lambda j: (layer, 0, j)),
                  pl.BlockSpec((1, 1, MOD_COLS), lambda j: (layer, 0, j))],
        out_specs=pl.BlockSpec((r, MOD_COLS), lambda j: (0, j)),
        out_shape=jax.ShapeDtypeStruct((r, nout), F32),
        compiler_params=_params(("parallel",), 2 * (d * MOD_COLS * 4 + 2 * r * MOD_COLS * 4)), name="modulation",
    )(cond, mod_w, mod_b.reshape(mod_b.shape[0], 1, nout))


def _pack_pairs(x):
    half = x.shape[1] // 2
    hi = pltpu.bitcast(x[:, :half].astype(BF16).astype(F32), U32)
    lo = pltpu.bitcast(x[:, half:].astype(BF16).astype(F32), U32)
    return hi | (lo >> 16)


def _unpack_pairs(u):
    hi = pltpu.bitcast(u & jnp.uint32(0xFFFF0000), F32)
    lo = pltpu.bitcast(u << 16, F32)
    return jnp.concatenate([hi, lo], axis=1)


def _mix_epilogue(x, mix, gate_ref, g_ref, b_ref, sh_ref, sc_ref, xo_ref, ff_ref, ffp_ref):
    xn = _layer_norm(DN_ALPHA * x + gate_ref[0] * mix, g_ref[0], b_ref[0])
    xo_ref[0] = xn
    ff = xn * (1.0 + sc_ref[0]) + sh_ref[0]
    ff_ref[0] = ff
    ffp_ref[0] = _pack_pairs(ff)


def _mix_out_kernel(*refs, n_y):
    x_ref = refs[0]
    y_refs = refs[1:1 + n_y]
    w_refs = refs[1 + n_y:1 + 2 * n_y]
    y = None
    for y_ref, w_ref in zip(y_refs, w_refs):
        t = jnp.dot(y_ref[0].astype(BF16), w_ref[...], preferred_element_type=F32)
        y = t if y is None else y + t
    _mix_epilogue(x_ref[0], y, *refs[1 + 2 * n_y:])


def _mix_out(x, ys, ws, gate, ln_g, ln_b, shift2, scale2):
    bsz, n, d = x.shape
    tm = min(ROW_TILE, n)
    assert n % tm == 0
    row = lambda c: pl.BlockSpec((1, tm, c), lambda b, i: (b, i, 0))
    vecs = [gate, ln_g, ln_b, shift2, scale2]
    in_specs = [row(d)] + [row(y.shape[2]) for y in ys]
    in_specs += [pl.BlockSpec(w.shape, lambda b, i: (0, 0)) for w in ws]
    in_specs += [_bcast_spec(a) for a in vecs]
    vmem = 2 * (3 * tm * d * 4 + sum(tm * y.shape[2] * 4 + w.size * 2 for y, w in zip(ys, ws)))
    return pl.pallas_call(
        functools.partial(_mix_out_kernel, n_y=len(ys)),
        grid=(bsz, n // tm), in_specs=in_specs, out_specs=[row(d), row(d), row(d // 2)],
        out_shape=[jax.ShapeDtypeStruct((bsz, n, d), F32)] * 2 + [jax.ShapeDtypeStruct((bsz, n, d // 2), U32)],
        compiler_params=_params(("parallel", "parallel"), vmem), name="mix_out",
    )(x, *ys, *ws, *vecs)


def _attn_kernel(q_ref, k_ref, vt_ref, o_ref, *, tk, hp):
    nk = k_ref.shape[2]
    tq = q_ref.shape[2]
    dva = vt_ref.shape[2]
    dv = dva - ATT_ONES_ROWS

    def body(j, carry):
        off = pl.multiple_of(j * tk, tk)
        new = []
        for h in range(hp):
            m_prev, acc = carry[h]
            st = lax.dot_general(k_ref[0, h, pl.ds(off, tk), :], q_ref[0, h], (((1,), (1,)), ((), ())),
                                 preferred_element_type=F32)
            m_new = jnp.maximum(m_prev, jnp.max(st, 0, keepdims=True))
            p = jnp.exp2(st - m_new).astype(BF16)
            alpha = jnp.exp2(m_prev - m_new)
            acc = alpha * acc + jnp.dot(vt_ref[0, h, :, pl.ds(off, tk)], p, preferred_element_type=F32)
            new.append((m_new, acc))
        return tuple(new)

    init = tuple((jnp.full((1, tq), -jnp.inf, F32), jnp.zeros((dva, tq), F32)) for _ in range(hp))
    fin = lax.fori_loop(0, nk // tk, body, init)
    outs = [acc[:dv] / acc[dv:dv + 1] for _, acc in fin]
    o_ref[0] = jnp.concatenate(outs, 0).T


def _attention(q, k, vt, tq, tk):
    bsz, h, nq, dk = q.shape
    nk, dva = k.shape[2], vt.shape[2]
    dv = dva - ATT_ONES_ROWS
    hp = ATT_HEADS_PER_STEP
    assert nq % tq == 0 and nk % tk == 0 and h % hp == 0
    vmem = 2 * hp * (tq * LANES * 2 + nk * LANES * 2 + dva * nk * 2) + 2 * tq * hp * dv * 4 + 6 * hp * tk * tq * 4
    return pl.pallas_call(
        functools.partial(_attn_kernel, tk=tk, hp=hp), grid=(bsz, h // hp, nq // tq),
        in_specs=[pl.BlockSpec((1, hp, tq, dk), lambda b, g, i: (b, g, i, 0)),
                  pl.BlockSpec((1, hp, nk, dk), lambda b, g, i: (b, g, 0, 0)),
                  pl.BlockSpec((1, hp, dva, nk), lambda b, g, i: (b, g, 0, 0))],
        out_specs=pl.BlockSpec((1, tq, hp * dv), lambda b, g, i: (b, i, g)),
        out_shape=jax.ShapeDtypeStruct((bsz, nq, h * dv), F32),
        compiler_params=_params(("parallel", "parallel", "arbitrary"), vmem), name="mla_attention",
    )(q, k, vt)


def _key_tile(nk):
    for t in (768, 512, 384, 256, 128):
        if nk % t == 0:
            return t
    return nk


def _router_kernel(x_ref, wt_ref, bias_ref, upper_ref, idx_ref, w_ref, rank_ref, cnt_ref, run_sc):
    i = pl.program_id(0)
    tm = x_ref.shape[0]

    @pl.when(i == 0)
    def _():
        run_sc[...] = jnp.zeros(run_sc.shape, F32)

    logits = lax.dot_general(wt_ref[...], x_ref[...], (((1,), (1,)), ((), ())),
                             precision=lax.Precision.HIGHEST, preferred_element_type=F32)
    sc = jax.nn.sigmoid(logits)
    ch = sc + bias_ref[:, :1]
    neg = -jnp.inf
    chg = ch.reshape(N_EXPERT_GROUPS, N_EXPERTS // N_EXPERT_GROUPS, tm)
    m1 = jnp.max(chg, axis=1)
    eq = chg == m1[:, None, :]
    cnt = jnp.sum(eq.astype(F32), axis=1)
    m2 = jnp.max(jnp.where(eq, neg, chg), axis=1)
    g2 = m1 + jnp.where(cnt >= 2.0, m1, m2)
    gi = lax.broadcasted_iota(I32, g2.shape, 0)
    beaten = jnp.zeros(g2.shape, F32)
    for g in range(N_EXPERT_GROUPS):
        row = g2[g:g + 1, :]
        beaten = beaten + jnp.where(row > g2, 1.0, jnp.where(row == g2, jnp.where(gi > g, 1.0, 0.0), 0.0))
    keep = beaten < float(TOPK_GROUPS)
    cur = jnp.where(keep[:, None, :], chg, neg).reshape(N_EXPERTS, tm)
    eidx = lax.broadcasted_iota(I32, (N_EXPERTS, tm), 0)
    multi = jnp.zeros((N_EXPERTS, tm), F32)
    hits, idx_rows, w_rows = [], [], []
    for _ in range(TOP_K):
        m = jnp.max(cur, axis=0, keepdims=True)
        sel = jnp.min(jnp.where(cur == m, eidx, N_EXPERTS), axis=0, keepdims=True)
        hit = eidx == sel
        idx_rows.append(sel)
        w_rows.append(jnp.sum(jnp.where(hit, sc, 0.0), axis=0, keepdims=True))
        cur = jnp.where(hit, neg, cur)
        multi = multi + jnp.where(hit, 1.0, 0.0)
        hits.append(hit)
    base = jnp.concatenate([run_sc[...]] * (tm // LANES), axis=1)
    before = jnp.dot(multi.astype(BF16), upper_ref[...], preferred_element_type=F32) + base
    rank_rows = [jnp.sum(jnp.where(hit, before, 0.0), axis=0, keepdims=True) for hit in hits]
    w = jnp.concatenate(w_rows, axis=0)
    idx_ref[...] = jnp.concatenate(idx_rows, axis=0)
    w_ref[...] = (w / jnp.sum(w, axis=0, keepdims=True) * ROUTED_SCALE).T
    rank_ref[...] = jnp.concatenate(rank_rows, axis=0).astype(I32)
    run_sc[...] = run_sc[...] + jnp.dot(multi.astype(BF16), jnp.ones((tm, LANES), BF16), preferred_element_type=F32)
    cnt_ref[...] = run_sc[...]


def _router(tokens, router_w, router_bias):
    t, d = tokens.shape
    tm = ROUTE_TOKENS
    assert t % tm == 0
    wt = router_w.T
    bias = jnp.broadcast_to(router_bias.astype(F32)[:, None], (N_EXPERTS, LANES))
    r = jnp.arange(tm)
    upper = (r[:, None] < r[None, :]).astype(BF16)
    col = pl.BlockSpec((TOP_K, tm), lambda i: (0, i))
    full = lambda a: pl.BlockSpec(a.shape, lambda i: (0,) * a.ndim)
    vmem = 2 * (tm * d * 4 + wt.size * 4) + 40 * N_EXPERTS * tm * 4
    return pl.pallas_call(
        _router_kernel, grid=(t // tm,),
        in_specs=[pl.BlockSpec((tm, d), lambda i: (i, 0)), full(wt), full(bias), full(upper)],
        out_specs=[col, pl.BlockSpec((tm, TOP_K), lambda i: (i, 0)), col,
                   pl.BlockSpec((N_EXPERTS, LANES), lambda i: (0, 0))],
        out_shape=[jax.ShapeDtypeStruct((TOP_K, t), I32), jax.ShapeDtypeStruct((t, TOP_K), F32),
                   jax.ShapeDtypeStruct((TOP_K, t), I32), jax.ShapeDtypeStruct((N_EXPERTS, LANES), F32)],
        scratch_shapes=[pltpu.VMEM((N_EXPERTS, LANES), F32)],
        compiler_params=_params(("arbitrary",), vmem), name="moe_router",
    )(tokens, wt, bias, upper)


def _positions_kernel(idx_ref, rank_ref, start_ref, pos_ref):
    tm = idx_ref.shape[1]
    eidx = lax.broadcasted_iota(I32, (N_EXPERTS, tm), 0)
    start = jnp.concatenate([start_ref[...]] * (tm // LANES), axis=1)
    rows = [jnp.sum(jnp.where(eidx == idx_ref[k:k + 1, :], start, 0), axis=0, keepdims=True) for k in range(TOP_K)]
    pos_ref[...] = jnp.concatenate(rows, axis=0) + rank_ref[...]


def _positions(idx, rank, pad_start):
    t = idx.shape[1]
    tm = ROUTE_TOKENS
    start = jnp.broadcast_to(pad_start.astype(I32)[:, None], (N_EXPERTS, LANES))
    col = pl.BlockSpec((TOP_K, tm), lambda i: (0, i))
    return pl.pallas_call(
        _positions_kernel, grid=(t // tm,),
        in_specs=[col, col, pl.BlockSpec((N_EXPERTS, LANES), lambda i: (0, 0))], out_specs=col,
        out_shape=jax.ShapeDtypeStruct((TOP_K, t), I32),
        compiler_params=_params(("parallel",), 8 * N_EXPERTS * tm * 4), name="moe_positions",
    )(idx, rank, start)


def _moe_plan(counts, t):
    n_blocks = -(-(t * TOP_K) // MOE_ROWS) + N_EXPERTS
    c = counts.astype(I32)
    padded = (c + MOE_ROWS - 1) // MOE_ROWS * MOE_ROWS
    pad_end = jnp.cumsum(padded)
    pad_start = pad_end - padded
    block_e = jnp.minimum(jnp.searchsorted(pad_end, jnp.arange(n_blocks, dtype=I32) * MOE_ROWS, side='right'),
                          N_EXPERTS - 1).astype(I32)
    n_valid = (pad_end[-1:] // MOE_ROWS).astype(I32)
    return pad_start, block_e, n_valid, n_blocks * MOE_ROWS


def _row_copy(src, dst, sem):
    return pltpu.make_async_copy(src, dst, sem)


def _dispatch_kernel(pos_ref, x_ref, buf_in_ref, buf_ref, sem):
    del buf_in_ref
    tm = x_ref.shape[0]

    def issue(t, carry):
        for k in range(TOP_K):
            _row_copy(x_ref.at[pl.ds(t, 1)], buf_ref.at[pl.ds(pos_ref[k, t], 1)], sem).start()
        return carry

    lax.fori_loop(0, tm, issue, 0)
    for k in range(TOP_K):
        _row_copy(x_ref, buf_ref.at[pl.ds(0, tm)], sem).wait()


def _dispatch(pos, tokens, cap):
    t, d = tokens.shape
    tm = ROUTE_TOKENS
    zeros = jnp.zeros((cap, d), tokens.dtype)
    return pl.pallas_call(
        _dispatch_kernel, grid=(t // tm,),
        in_specs=[pl.BlockSpec((TOP_K, tm), lambda i: (0, i), memory_space=pltpu.SMEM),
                  pl.BlockSpec((tm, d), lambda i: (i, 0)),
                  pl.BlockSpec(memory_space=pl.ANY)],
        out_specs=pl.BlockSpec(memory_space=pl.ANY),
        out_shape=jax.ShapeDtypeStruct((cap, d), tokens.dtype),
        scratch_shapes=[pltpu.SemaphoreType.DMA(())],
        input_output_aliases={2: 0},
        compiler_params=_params(("arbitrary",), 2 * tm * d * 4, has_side_effects=True),
        name="moe_dispatch",
    )(pos, tokens, zeros)


def _experts_kernel(be_ref, nv_ref, x_ref, wgu_ref, wdn_ref, o_ref, wgu_sc, wdn_sc):
    i = pl.program_id(0)

    @pl.when(i < nv_ref[0])
    def _():
        @pl.when(jnp.logical_or(i == 0, be_ref[i] != be_ref[jnp.maximum(i - 1, 0)]))
        def _():
            wgu_sc[...] = wgu_ref[0, 0].astype(BF16)
            wdn_sc[...] = wdn_ref[0, 0].astype(BF16)

        for part in range(MOE_ROWS // MOE_SUB_ROWS):
            rows = pl.ds(part * MOE_SUB_ROWS, MOE_SUB_ROWS)
            x = _unpack_pairs(x_ref[rows, :]).astype(BF16)
            o_ref[rows, :] = _pack_pairs(_swiglu_rows(x, wgu_sc[...], wdn_sc[...]))

    @pl.when(i >= nv_ref[0])
    def _():
        o_ref[...] = jnp.zeros(o_ref.shape, U32)


def _moe_experts(xb, block_e, n_valid, w_gu, w_down, layer):
    cap, dp = xb.shape
    d = 2 * dp
    gu = w_gu.shape[3]
    ed = w_down.shape[2]
    live = lambda i, be, nv: jnp.maximum(jnp.minimum(i, nv[0] - 1), 0)
    grid_spec = pltpu.PrefetchScalarGridSpec(
        num_scalar_prefetch=2, grid=(cap // MOE_ROWS,),
        in_specs=[pl.BlockSpec((MOE_ROWS, dp), lambda i, be, nv: (live(i, be, nv), 0)),
                  pl.BlockSpec((1, 1, d, gu), lambda i, be, nv: (layer, be[i], 0, 0)),
                  pl.BlockSpec((1, 1, ed, d), lambda i, be, nv: (layer, be[i], 0, 0))],
        out_specs=pl.BlockSpec((MOE_ROWS, dp), lambda i, be, nv: (i, 0)),
        scratch_shapes=[pltpu.VMEM((d, gu), BF16), pltpu.VMEM((ed, d), BF16)])
    vmem = 2 * (2 * MOE_ROWS * dp * 4 + d * gu * 4 + ed * d * 4) + (d * gu + ed * d) * 2 + 6 * MOE_ROWS * d * 4
    return pl.pallas_call(
        _experts_kernel, grid_spec=grid_spec, out_shape=jax.ShapeDtypeStruct((cap, dp), U32),
        compiler_params=_params(("arbitrary",), vmem), name="moe_experts",
    )(block_e, n_valid, xb, w_gu, w_down)


def _ffn_out_kernel(pos_ref, w_ref, x_ref, ff_ref, yb_ref, wgu_ref, wdn_ref, gate_ref, g_ref, b_ref, xo_ref,
                    rows_sc, sem):
    tm = x_ref.shape[1]

    def issue(t, carry):
        for k in range(TOP_K):
            _row_copy(yb_ref.at[pl.ds(pos_ref[k, t], 1)], rows_sc.at[k, pl.ds(t, 1)], sem).start()
        return carry

    lax.fori_loop(0, tm, issue, 0)
    out = _swiglu_rows(ff_ref[0].astype(BF16), wgu_ref[...], wdn_ref[...])
    for k in range(TOP_K):
        _row_copy(yb_ref.at[pl.ds(0, tm)], rows_sc.at[k], sem).wait()
    for k in range(TOP_K):
        out = out + w_ref[:, k:k + 1] * _unpack_pairs(rows_sc[k])
    xo_ref[0] = _layer_norm(DN_ALPHA * x_ref[0] + gate_ref[0] * out, g_ref[0], b_ref[0])


def _ffn_out(x, ff, yb, pos, w, tile0, sh_gu, sh_down, gate, ln_g, ln_b):
    bsz, n, d = x.shape
    tm = min(ROUTE_TOKENS, n)
    nt = n // tm
    row = pl.BlockSpec((1, tm, d), lambda b, i: (b, i, 0))
    vecs = [gate, ln_g, ln_b]
    in_specs = [pl.BlockSpec((TOP_K, tm), lambda b, i: (0, tile0 + b * nt + i), memory_space=pltpu.SMEM),
                pl.BlockSpec((tm, TOP_K), lambda b, i: (tile0 + b * nt + i, 0)),
                row, row, pl.BlockSpec(memory_space=pl.ANY),
                pl.BlockSpec(sh_gu.shape, lambda b, i: (0, 0)), pl.BlockSpec(sh_down.shape, lambda b, i: (0, 0))]
    in_specs += [_bcast_spec(a) for a in vecs]
    vmem = TOP_K * tm * d * 2 + 2 * (3 * tm * d * 4 + sh_gu.size * 2 + sh_down.size * 2) + 6 * tm * d * 4
    return pl.pallas_call(
        _ffn_out_kernel, grid=(bsz, nt), in_specs=in_specs, out_specs=row,
        out_shape=jax.ShapeDtypeStruct((bsz, n, d), F32),
        scratch_shapes=[pltpu.VMEM((TOP_K, tm, d // 2), U32), pltpu.SemaphoreType.DMA(())],
        compiler_params=_params(("arbitrary", "arbitrary"), vmem), name="moe_combine_ffn_out",
    )(pos, w, x, ff, yb, sh_gu, sh_down, *vecs)


def _moe_dispatch_experts(tokens, packed, router_w, router_bias, w_gu, w_down, layer):
    t = tokens.shape[0]
    idx, w, rank, counts = _router(tokens, router_w, router_bias)
    pad_start, block_e, n_valid, cap = _moe_plan(counts[:, 0], t)
    pos = _positions(idx, rank, pad_start)
    xb = _dispatch(pos, packed, cap)
    return _moe_experts(xb, block_e, n_valid, w_gu, w_down, layer), pos, w


def _rope_tables(n):
    rows = n // GRID_W
    row = jnp.repeat(jnp.arange(rows), GRID_W).astype(F32)
    col = jnp.tile(jnp.arange(GRID_W), rows).astype(F32)
    half = MLA_ROPE // 2
    inv = ROPE_THETA ** (-jnp.arange(0, half, 2, dtype=F32) / half)
    ang = jnp.concatenate([row[:, None] * inv, col[:, None] * inv], -1)
    return jnp.cos(ang), jnp.sin(ang)


def _filter_kernel(fr_ref, w1_ref, b1_ref, f1_ref, w2_ref, b2_ref, f2_ref, w3_ref, dl_ref, k_ref, l1_ref, *, n):
    i = pl.program_id(0)
    tr = k_ref.shape[0]
    hp = lax.Precision.HIGHEST
    tap = i * tr + lax.broadcasted_iota(I32, (tr, 1), 0)
    lag = jnp.where(tap < n, tap, 2 * n - tap).astype(F32)
    t = lag * (1.0 / (n - 1))
    ang = (2.0 * math.pi / n) * lag * fr_ref[...]
    lane = lax.broadcasted_iota(I32, (tr, LANES), 1)
    z = jnp.where(lane == 0, t, jnp.where(lane <= HY_BANDS, jnp.cos(ang),
                                          jnp.where(lane <= 2 * HY_BANDS, -jnp.sin(ang), 0.0)))
    h = jnp.sin(f1_ref[...] * (jnp.dot(z, w1_ref[...], precision=hp, preferred_element_type=F32) + b1_ref[...]))
    h = jnp.sin(f2_ref[...] * (jnp.dot(h, w2_ref[...], precision=hp, preferred_element_type=F32) + b2_ref[...]))
    h = jnp.dot(h, w3_ref[...], precision=hp, preferred_element_type=F32)
    hsel = jnp.where(tap < n, h[:, :HY_WIDTH], h[:, HY_WIDTH:])
    k = jnp.where(tap == n, 0.0, hsel * jnp.exp(-t * dl_ref[...]))
    k_ref[...] = k

    @pl.when(i == 0)
    def _():
        l1_ref[...] = jnp.zeros(l1_ref.shape, F32)

    l1_ref[...] = l1_ref[...] + jnp.sum(jnp.abs(k), axis=0, keepdims=True)


def _hyena_filter(n, w1, b1, f1, w2, b2, f2, w3):
    nn = 2 * n
    tr = min(FILT_ROWS, nn)
    assert nn % tr == 0
    emb, ffn = w1.shape
    fr = jnp.linspace(1e-4, HY_BANDS - 1, HY_BANDS, dtype=F32)
    fr_l = jnp.concatenate([jnp.zeros((1,), F32), fr, fr, jnp.zeros((LANES - emb,), F32)])[None]
    w1p = jnp.concatenate([w1, jnp.zeros((LANES - emb, ffn), F32)], 0)
    deltas = jnp.abs(jnp.linspace(HY_DECAY_MIN, HY_DECAY_MAX, HY_WIDTH, dtype=F32))[None]
    ops = [fr_l, w1p, b1[None], f1[None], w2, b2[None], f2[None], w3, deltas]
    full = lambda a: pl.BlockSpec(a.shape, lambda i: (0,) * a.ndim)
    return pl.pallas_call(
        functools.partial(_filter_kernel, n=n), grid=(nn // tr,), in_specs=[full(a) for a in ops],
        out_specs=[pl.BlockSpec((tr, HY_WIDTH), lambda i: (i, 0)), pl.BlockSpec((1, HY_WIDTH), lambda i: (0, 0))],
        out_shape=[jax.ShapeDtypeStruct((nn, HY_WIDTH), F32), jax.ShapeDtypeStruct((1, HY_WIDTH), F32)],
        compiler_params=_params(("arbitrary",), 16 * tr * 2 * HY_WIDTH * 4), name="hyena_filter",
    )(*ops)


def _split(a):
    hi = a.astype(BF16)
    return hi, (a - hi.astype(F32)).astype(BF16)


def _dot3(a_hi, a_lo, x):
    x_hi, x_lo = _split(x)
    return (jnp.dot(a_hi, x_hi, preferred_element_type=F32) + jnp.dot(a_lo, x_hi, preferred_element_type=F32)
            + jnp.dot(a_hi, x_lo, preferred_element_type=F32))


def _cis(num, den):
    ang = (2.0 * math.pi / den) * (num % den).astype(F32)
    return jnp.cos(ang), jnp.sin(ang)


def _dft_outer_kernel(f_hi_ref, f_lo_ref, x_ref, o_ref):
    o_ref[0] = _dot3(f_hi_ref[...], f_lo_ref[...], x_ref[0])


def _dft_outer(x, f_hi, f_lo):
    bsz, k, m = x.shape
    r = f_hi.shape[0]
    tn = min(DFT_COLS, m)
    assert m % tn == 0
    vmem = 2 * (2 * f_hi.size * 2 + k * tn * 4 + r * tn * 4) + 3 * (k + r) * tn * 4
    return pl.pallas_call(
        _dft_outer_kernel, grid=(bsz, m // tn),
        in_specs=[pl.BlockSpec(f_hi.shape, lambda b, j: (0, 0)), pl.BlockSpec(f_lo.shape, lambda b, j: (0, 0)),
                  pl.BlockSpec((1, k, tn), lambda b, j: (b, 0, j))],
        out_specs=pl.BlockSpec((1, r, tn), lambda b, j: (b, 0, j)),
        out_shape=jax.ShapeDtypeStruct((bsz, r, m), F32),
        compiler_params=_params(("parallel", "parallel"), vmem), name="hyena_dft_outer",
    )(f_hi, f_lo, x)


def _dft_inner_kernel(m_hi_ref, m_lo_ref, mt_hi_ref, mt_lo_ref, a_ref, h_ref, o_ref, *, conv):
    n2 = a_ref.shape[3]
    x = a_ref[0, :, 0].reshape(2 * n2, a_ref.shape[4])
    y = _dot3(m_hi_ref[0], m_lo_ref[0], x)
    if conv:
        h = h_ref[0, :, 0].reshape(2 * n2, h_ref.shape[4])
        yr, yi, hr, hi = y[:n2], y[n2:], h[:n2], h[n2:]
        prod = jnp.concatenate([yr * hr - yi * hi, yr * hi + yi * hr], axis=0)
        y = _dot3(mt_hi_ref[0], mt_lo_ref[0], prod)
    o_ref[0, :, 0] = y.reshape(2, n2, y.shape[1])


def _dft_inner(a, h, mats, conv):
    bsz, _, n1, n2, c = a.shape
    blk = lambda sel: pl.BlockSpec((1, 2, 1, n2, c), sel)
    mat = pl.BlockSpec((1, 2 * n2, 2 * n2), lambda k, b: (k, 0, 0))
    vmem = 2 * (4 * 4 * n2 * n2 * 2 + 3 * 2 * n2 * c * 4) + 8 * 2 * n2 * c * 4
    return pl.pallas_call(
        functools.partial(_dft_inner_kernel, conv=conv), grid=(n1, bsz),
        in_specs=[mat, mat, mat, mat, blk(lambda k, b: (b, 0, k, 0, 0)), blk(lambda k, b: (0, 0, k, 0, 0))],
        out_specs=blk(lambda k, b: (b, 0, k, 0, 0)), out_shape=jax.ShapeDtypeStruct(a.shape, F32),
        compiler_params=_params(("parallel", "arbitrary"), vmem),
        name="hyena_dft_inner_conv" if conv else "hyena_dft_inner",
    )(*mats, a, h)


def _dft_final_kernel(fd_hi_ref, fd_lo_ref, a_ref, z_ref, x0_ref, skip_ref, l1_ref, o_ref):
    y = _dot3(fd_hi_ref[...], fd_lo_ref[...], a_ref[0])
    o_ref[0] = (y / l1_ref[...] + z_ref[0] * skip_ref[...]) * x0_ref[0]


def _dft_final(a, z, x0, skip_t, l1_t, fd_hi, fd_lo):
    bsz, r2, m = a.shape
    k = fd_hi.shape[0]
    tn = min(DFT_COLS, m)
    row = pl.BlockSpec((1, k, tn), lambda b, j: (b, 0, j))
    vmem = 2 * (2 * fd_hi.size * 2 + r2 * tn * 4 + 3 * k * tn * 4) + 3 * (k + r2) * tn * 4
    return pl.pallas_call(
        _dft_final_kernel, grid=(bsz, m // tn),
        in_specs=[pl.BlockSpec(fd_hi.shape, lambda b, j: (0, 0)), pl.BlockSpec(fd_lo.shape, lambda b, j: (0, 0)),
                  pl.BlockSpec((1, r2, tn), lambda b, j: (b, 0, j)), row, row,
                  pl.BlockSpec((1, tn), lambda b, j: (0, 0)), pl.BlockSpec((1, tn), lambda b, j: (0, 0))],
        out_specs=row, out_shape=jax.ShapeDtypeStruct((bsz, k, m), F32),
        compiler_params=_params(("parallel", "parallel"), vmem), name="hyena_dft_final",
    )(fd_hi, fd_lo, a, z, x0, skip_t, l1_t)


def _hyena_long_conv(z, x0, k, l1, skip):
    bsz, n, c = z.shape
    n2 = DFT_INNER
    nn = 2 * n
    n1 = nn // n2
    assert nn == n1 * n2 and n1 % 2 == 0
    half = n1 // 2
    m = n2 * c
    j1 = jnp.arange(n1)
    ca, sa = _cis(j1[:, None] * j1[None, :], n1)
    fa = jnp.concatenate([ca, -sa], axis=0)
    fd = jnp.concatenate([ca, -sa], axis=1)[:half] / nn
    j2 = jnp.arange(n2)
    cb, sb = _cis(j2[None, None, :] * (n1 * j2[None, :, None] + j1[:, None, None]), nn)
    mb = jnp.concatenate([jnp.concatenate([cb, sb], 2), jnp.concatenate([-sb, cb], 2)], 1)
    mats = _split(mb) + _split(jnp.swapaxes(mb, 1, 2))
    fa_hi, fa_lo = _split(fa)
    hk = _dft_outer(k.reshape(1, n1, m), fa_hi, fa_lo).reshape(1, 2, n1, n2, c)
    hk = _dft_inner(hk, hk, mats, conv=False)
    a = _dft_outer(z.reshape(bsz, half, m), fa_hi[:, :half], fa_lo[:, :half]).reshape(bsz, 2, n1, n2, c)
    a = _dft_inner(a, hk, mats, conv=True).reshape(bsz, 2 * n1, m)
    reps = (1, min(DFT_COLS, m) // c)
    out = _dft_final(a, z.reshape(bsz, half, m), x0.reshape(bsz, half, m), jnp.tile(skip.reshape(1, c), reps),
                     jnp.tile(l1, reps), *_split(fd))
    return out.reshape(bsz, n, c)


def _short_conv_kernel(f_hi_ref, f_lo_ref, fi_hi_ref, fi_lo_ref, z_ref, x0_ref, k_ref, skip_ref, l1_ref, o_ref):
    n = z_ref.shape[1]
    nn = 2 * n
    z = z_ref[0]
    hk = _dot3(f_hi_ref[...], f_lo_ref[...], k_ref[...])
    zs = _dot3(f_hi_ref[:, :n], f_lo_ref[:, :n], z)
    zr, zi, hr, hi = zs[:nn], zs[nn:], hk[:nn], hk[nn:]
    prod = jnp.concatenate([zr * hr - zi * hi, zr * hi + zi * hr], axis=0)
    o_ref[0] = (_dot3(fi_hi_ref[...], fi_lo_ref[...], prod) / l1_ref[...] + z * skip_ref[...]) * x0_ref[0]


def _hyena_short_conv(z, x0, k, l1, skip):
    bsz, n, c = z.shape
    nn = 2 * n
    idx = jnp.arange(nn)
    cf, sf = _cis(idx[:, None] * idx[None, :], nn)
    f = jnp.concatenate([cf, -sf], axis=0)
    fi = jnp.concatenate([cf, -sf], axis=1)[:n] / nn
    full = lambda a: pl.BlockSpec(a.shape, lambda b: (0,) * a.ndim)
    row = pl.BlockSpec((1, n, c), lambda b: (b, 0, 0))
    ops = _split(f) + _split(fi)
    vmem = 2 * (sum(a.size * 2 for a in ops) + 3 * n * c * 4 + nn * c * 4) + 12 * 2 * nn * c * 4
    return pl.pallas_call(
        _short_conv_kernel, grid=(bsz,),
        in_specs=[full(a) for a in ops] + [row, row, full(k), pl.BlockSpec((1, c), lambda b: (0, 0)),
                                           pl.BlockSpec((1, c), lambda b: (0, 0))],
        out_specs=row, out_shape=jax.ShapeDtypeStruct((bsz, n, c), F32),
        compiler_params=_params(("parallel",), vmem), name="hyena_short_conv",
    )(*ops, z, x0, k, skip.reshape(1, c), l1)


def _hyena_sequence(z, x0, filt, skip):
    n = z.shape[1]
    k, l1 = _hyena_filter(n, *filt)
    conv = _hyena_short_conv if n <= HY_DIRECT_MAX else _hyena_long_conv
    return conv(z, x0, k, l1, skip)


def _rms(x, g):
    return x * lax.rsqrt(jnp.mean(x * x, -1, keepdims=True) + RMS_EPS) * g


def _mla_in_kernel(xc_ref, xp_ref, xn_ref, sh_ref, sc_ref, why_ref, cw_ref, cb_ref, wql_ref, qg_ref, wqa_ref, wqb_ref,
                   wkvl_ref, kvg_ref, wk_ref, wv_ref, wpa_ref, wpb_ref, qa_ref, qb_ref, ka_ref, kb_ref,
                   x0_ref, z_ref, q_ref, k_ref, vt_ref):
    tm = xc_ref.shape[1]
    rows = jnp.concatenate([xp_ref[0], xc_ref[0], xn_ref[0]], axis=0)
    h = (rows * (1.0 + sc_ref[0]) + sh_ref[0]).astype(BF16)
    hc = h[SUBLANES:SUBLANES + tm]
    y = _conv3(jnp.dot(h, why_ref[...], preferred_element_type=F32), cw_ref, cb_ref, tm)
    x0_ref[0] = y[:, :HY_WIDTH]
    z_ref[0] = y[:, 2 * HY_WIDTH:] * y[:, HY_WIDTH:2 * HY_WIDTH]
    ql = _rms(jnp.dot(hc, wql_ref[...], preferred_element_type=F32), qg_ref[...]).astype(BF16)
    nh = q_ref.shape[1]
    qa = jnp.concatenate([qa_ref[...]] * nh, axis=1)
    qb = jnp.concatenate([qb_ref[...]] * nh, axis=1)
    q = (jnp.dot(ql, wqa_ref[...], preferred_element_type=F32) * qa
         + jnp.dot(ql, wqb_ref[...], preferred_element_type=F32) * qb).astype(BF16)
    kvl = _rms(jnp.dot(hc, wkvl_ref[...], preferred_element_type=F32), kvg_ref[...]).astype(BF16)
    kn = jnp.dot(kvl, wk_ref[...], preferred_element_type=F32)
    v_t = jnp.dot(kvl, wv_ref[...], preferred_element_type=F32).T
    kpe = (jnp.dot(hc, wpa_ref[...], preferred_element_type=F32) * ka_ref[...]
           + jnp.dot(hc, wpb_ref[...], preferred_element_type=F32) * kb_ref[...])
    ones = jnp.ones((ATT_ONES_ROWS, tm), BF16)
    for hd in range(nh):
        q_ref[0, hd] = q[:, hd * MLA_SLAB:(hd + 1) * MLA_SLAB]
        k_ref[0, hd] = (kn[:, hd * MLA_SLAB:(hd + 1) * MLA_SLAB] + kpe).astype(BF16)
        vt_ref[0, hd, :MLA_V, :] = v_t[hd * MLA_V:(hd + 1) * MLA_V].astype(BF16)
        vt_ref[0, hd, MLA_V:, :] = ones


def _rot_cols(w_pe):
    ev, od = w_pe[..., 0::2], w_pe[..., 1::2]
    return jnp.concatenate([ev, od], -1), jnp.concatenate([-od, ev], -1)


def _mla_weights(p):
    w_in = p['w_in']
    kin = w_in.shape[0]
    wq = p['w_qb'].reshape(MLA_Q_RANK, MLA_HEADS, MLA_QK)
    qa_pe, qb_pe = _rot_cols(wq[..., MLA_NOPE:])
    zq = jnp.zeros((MLA_Q_RANK, MLA_HEADS, MLA_SLAB - MLA_QK), F32)
    w_qa = jnp.concatenate([wq[..., :MLA_NOPE], qa_pe, zq], -1).reshape(MLA_Q_RANK, MLA_HEADS * MLA_SLAB)
    w_qb = jnp.concatenate([jnp.zeros_like(wq[..., :MLA_NOPE]), qb_pe, zq], -1)
    w_qb = w_qb.reshape(MLA_Q_RANK, MLA_HEADS * MLA_SLAB)
    wkv = p['w_kvb'].reshape(MLA_KV_RANK, MLA_HEADS, MLA_NOPE + MLA_V)
    w_k = jnp.concatenate([wkv[..., :MLA_NOPE], jnp.zeros((MLA_KV_RANK, MLA_HEADS, MLA_SLAB - MLA_NOPE), F32)], -1)
    w_k = w_k.reshape(MLA_KV_RANK, MLA_HEADS * MLA_SLAB)
    w_v = wkv[..., MLA_NOPE:].reshape(MLA_KV_RANK, MLA_HEADS * MLA_V)
    pa, pb = _rot_cols(w_in[:, OFF_KPE:])
    left, right = jnp.zeros((kin, MLA_NOPE), F32), jnp.zeros((kin, MLA_SLAB - MLA_QK), F32)
    w_pa = jnp.concatenate([left, pa, right], -1)
    w_pb = jnp.concatenate([left, pb, right], -1)
    bf = lambda a: a.astype(BF16)
    return dict(w_hy=bf(w_in[:, :OFF_Q]), conv_w=p['conv_w'], conv_b=p['conv_b'][None],
                w_ql=bf(w_in[:, OFF_Q:OFF_KV]), q_g=p['q_norm'][None], w_qa=bf(w_qa), w_qb=bf(w_qb),
                w_kvl=bf(w_in[:, OFF_KV:OFF_KPE]), kv_g=p['kv_norm'][None], w_k=bf(w_k), w_v=bf(w_v),
                w_pa=bf(w_pa), w_pb=bf(w_pb))


def _rope_slabs(n, rotate):
    one = jnp.ones((n, MLA_NOPE), F32)
    zero = jnp.zeros((n, MLA_SLAB - MLA_QK), F32)
    if rotate:
        cos, sin = _rope_tables(n)
    else:
        cos, sin = jnp.ones((n, MLA_ROPE // 2), F32), jnp.zeros((n, MLA_ROPE // 2), F32)
    return (jnp.concatenate([one, cos, cos, zero], -1), jnp.concatenate([jnp.zeros_like(one), sin, sin, zero], -1))


def _mla_in_proj(x, shift, scale, w, rotate):
    bsz, n, kin = x.shape
    tm = min(ROW_TILE, n)
    assert n % tm == 0
    ca, sb = _rope_slabs(n, rotate)
    s = MLA_SCALE * LOG2E
    tabs = [ca * s, sb * s, ca, sb]
    consts = [w['w_hy'], w['conv_w'], w['conv_b'], w['w_ql'], w['q_g'], w['w_qa'], w['w_qb'], w['w_kvl'], w['kv_g'],
              w['w_k'], w['w_v'], w['w_pa'], w['w_pb']]
    full = lambda a: pl.BlockSpec(a.shape, lambda b, i: (0,) * a.ndim)
    row = lambda c: pl.BlockSpec((1, tm, c), lambda b, i: (b, i, 0))
    head = pl.BlockSpec((1, MLA_HEADS, tm, MLA_SLAB), lambda b, i: (b, 0, i, 0))
    dva = MLA_V + ATT_ONES_ROWS
    vmem = 2 * (tm * kin * 4 + sum(a.size * a.dtype.itemsize for a in consts) + 2 * tm * HY_WIDTH * 4
                + 3 * MLA_HEADS * tm * MLA_SLAB * 2 + 4 * tm * MLA_SLAB * 4) + 8 * (tm + 16) * 3 * HY_WIDTH * 4
    return pl.pallas_call(
        _mla_in_kernel, grid=(bsz, n // tm),
        in_specs=_halo_specs(n, tm, kin) + [_bcast_spec(shift), _bcast_spec(scale)] + [full(a) for a in consts]
        + [pl.BlockSpec((tm, MLA_SLAB), lambda b, i: (i, 0))] * 4,
        out_specs=[row(HY_WIDTH), row(HY_WIDTH), head, head,
                   pl.BlockSpec((1, MLA_HEADS, dva, tm), lambda b, i: (b, 0, 0, i))],
        out_shape=[jax.ShapeDtypeStruct((bsz, n, HY_WIDTH), F32)] * 2
        + [jax.ShapeDtypeStruct((bsz, MLA_HEADS, n, MLA_SLAB), BF16)] * 2
        + [jax.ShapeDtypeStruct((bsz, MLA_HEADS, dva, n), BF16)],
        compiler_params=_params(("parallel", "parallel"), vmem), name="mla_in_proj",
    )(x, x, x, shift, scale, *consts, *tabs)


def _mixer_hyena_mla(x, ctx, sh_l, sc_l, sh_c, sc_c, p):
    n, nc = x.shape[1], ctx.shape[1]
    w = _mla_weights(p)
    x0_l, z_l, q_l, k_l, vt_l = _mla_in_proj(x, sh_l, sc_l, w, rotate=True)
    x0_c, z_c, q_c, k_c, vt_c = _mla_in_proj(ctx, sh_c, sc_c, w, rotate=False)
    k_all = jnp.concatenate([k_c, k_l], 2)
    vt_all = jnp.concatenate([vt_c, vt_l], 3)
    att_l = _attention(q_l, k_all, vt_all, tq=min(ATT_Q_TILE, n), tk=_key_tile(nc + n))
    att_c = _attention(q_c, k_c, vt_c, tq=nc, tk=_key_tile(nc))
    filt = (p['filt_w1'], p['filt_b1'], p['filt_freq1'], p['filt_w2'], p['filt_b2'], p['filt_freq2'], p['filt_w3'])
    hyo_l = _hyena_sequence(z_l, x0_l, filt, p['skip'])
    hyo_c = _hyena_sequence(z_c, x0_c, filt, p['skip'])
    return (hyo_l, att_l), (hyo_c, att_c)


def _halo_specs(n, tm, k):
    nb = n // SUBLANES
    per = tm // SUBLANES
    return [pl.BlockSpec((1, tm, k), lambda b, i: (b, i, 0)),
            pl.BlockSpec((1, SUBLANES, k), lambda b, i: (b, jnp.maximum(i * per - 1, 0), 0)),
            pl.BlockSpec((1, SUBLANES, k), lambda b, i: (b, jnp.minimum((i + 1) * per, nb - 1), 0))]


def _conv3(u, cw_ref, cb_ref, tm):
    i = pl.program_id(1)
    rows = lax.broadcasted_iota(I32, (u.shape[0], 1), 0)
    inside = jnp.logical_and(jnp.logical_or(rows >= SUBLANES, i > 0),
                             jnp.logical_or(rows < tm + SUBLANES, i < pl.num_programs(1) - 1))
    u = jnp.where(inside, u, 0.0)
    prev = pltpu.roll(u, 1, 0)[SUBLANES:SUBLANES + tm]
    nxt = pltpu.roll(u, u.shape[0] - 1, 0)[SUBLANES:SUBLANES + tm]
    return cw_ref[0:1, :] * prev + cw_ref[1:2, :] * u[SUBLANES:SUBLANES + tm] + cw_ref[2:3, :] * nxt + cb_ref[...]


def _ssd_in_kernel(xc_ref, xp_ref, xn_ref, sh_ref, sc_ref, wz_ref, wx_ref, wdt_ref, cw_ref, cb_ref, dtb_ref,
                   z_ref, xs_ref, b_ref, c_ref, dt_ref):
    tm = xc_ref.shape[1]
    rows = jnp.concatenate([xp_ref[0], xc_ref[0], xn_ref[0]], axis=0)
    h = (rows * (1.0 + sc_ref[0]) + sh_ref[0]).astype(BF16)
    hc = h[SUBLANES:SUBLANES + tm]
    z_ref[0] = jnp.dot(hc, wz_ref[...], preferred_element_type=F32)
    y = _conv3(jnp.dot(h, wx_ref[...], preferred_element_type=F32), cw_ref, cb_ref, tm)
    xbc = y * jax.nn.sigmoid(y)
    xs_ref[0] = xbc[:, :SSD_INNER]
    b_ref[0] = xbc[:, SSD_INNER:SSD_INNER + SSD_BC]
    c_ref[0] = xbc[:, SSD_INNER + SSD_BC:]
    dt = jnp.dot(hc, wdt_ref[...], preferred_element_type=F32) + dtb_ref[...]
    dt_ref[0] = (jnp.maximum(dt, 0.0) + jnp.log1p(jnp.exp(-jnp.abs(dt)))).T


def _ssd_in_proj(x, shift, scale, w_z, w_xbc, w_dt, conv_w, conv_b, dt_bias):
    bsz, n, k = x.shape
    tm = min(ROW_TILE, n)
    assert n % tm == 0
    full = lambda a: pl.BlockSpec(a.shape, lambda b, i: (0,) * a.ndim)
    row = lambda c: pl.BlockSpec((1, tm, c), lambda b, i: (b, i, 0))
    nh2 = w_dt.shape[1]
    consts = [w_z, w_xbc, w_dt, conv_w, conv_b, dt_bias]
    widths = [SSD_INNER, SSD_INNER, SSD_BC, SSD_BC]
    vmem = 2 * (tm * k * 4 + sum(a.size * a.dtype.itemsize for a in consts) + tm * (sum(widths) + nh2) * 4) \
        + 6 * (tm + 2 * SUBLANES) * SSD_XBC * 4
    return pl.pallas_call(
        _ssd_in_kernel, grid=(bsz, n // tm),
        in_specs=_halo_specs(n, tm, k) + [_bcast_spec(shift), _bcast_spec(scale)] + [full(a) for a in consts],
        out_specs=[row(c) for c in widths] + [pl.BlockSpec((1, nh2, tm), lambda b, i: (b, 0, i))],
        out_shape=[jax.ShapeDtypeStruct((bsz, n, c), F32) for c in widths]
        + [jax.ShapeDtypeStruct((bsz, nh2, n), F32)],
        compiler_params=_params(("parallel", "parallel"), vmem), name="ssd_in_proj",
    )(x, x, x, shift, scale, *consts)


def _ssd_scan_kernel(x_ref, b_ref, c_ref, dt_ref, a_ref, s0_ref, y_ref, sfin_ref, st_sc, *, reverse):
    ci = pl.program_id(2)
    q = SSD_CHUNK

    @pl.when(ci == 0)
    def _():
        st_sc[...] = s0_ref[0, 0]

    dt = dt_ref[0]
    a = dt * a_ref[0]
    si = lax.broadcasted_iota(I32, (q, q), 0)
    li = lax.broadcasted_iota(I32, (q, q), 1)
    incl = jnp.where((si >= li) if reverse else (si <= li), 1.0, 0.0)
    hp = lax.Precision.HIGHEST
    acs = jnp.dot(a, incl, precision=hp, preferred_element_type=F32)
    tot = jnp.dot(a, jnp.ones((q, LANES), F32), precision=hp, preferred_element_type=F32)
    e_in = jnp.exp(acs)
    w_end = jnp.exp(tot - acs) * dt
    e_tot = jnp.exp(tot)
    acs_t = acs.T
    e_in_t = e_in.T
    mask = (li >= si) if reverse else (li <= si)
    cmat = c_ref[0]
    cb = lax.dot_general(cmat.astype(BF16), b_ref[0].astype(BF16), (((1,), (1,)), ((), ())),
                         preferred_element_type=F32)
    bt = b_ref[0].T
    first = lax.broadcasted_iota(I32, (q, LANES), 1) < SSD_HEADDIM
    ys = []
    for pr in range(SSD_HPG // 2):
        lo, hi = pr * LANES, (pr + 1) * LANES
        x_pair = x_ref[0, :, lo:hi].astype(BF16)
        st_pair = st_sc[:, lo:hi]
        rhs = jnp.concatenate([x_pair, st_pair.astype(BF16)], axis=0)
        y2, s2 = [], []
        for r in (2 * pr, 2 * pr + 1):
            seg = acs_t[:, r:r + 1] - acs[r:r + 1, :]
            m = cb * jnp.exp(jnp.where(mask, seg, -jnp.inf)) * dt[r:r + 1, :]
            lhs = jnp.concatenate([m.astype(BF16), (cmat * e_in_t[:, r:r + 1]).astype(BF16)], axis=1)
            y2.append(jnp.dot(lhs, rhs, preferred_element_type=F32))
            btr = (bt * w_end[r:r + 1, :]).astype(BF16)
            s2.append(e_tot[r:r + 1, :] * st_pair + jnp.dot(btr, x_pair, preferred_element_type=F32))
        ys.append(jnp.where(first, y2[0], y2[1]))
        st_sc[:, lo:hi] = jnp.where(first, s2[0], s2[1])
    y_ref[0] = jnp.concatenate(ys, axis=1)

    @pl.when(ci == pl.num_programs(2) - 1)
    def _():
        sfin_ref[0, 0] = st_sc[...]


def _ssd_scan(xs, bm, cm, dt_t, a_rep, s0, reverse):
    bsz, n, _ = xs.shape
    q = SSD_CHUNK
    nc = n // q
    assert n % q == 0 and q == LANES
    d = 1 if reverse else 0
    cc = (lambda c: nc - 1 - c) if reverse else (lambda c: c)
    gw = SSD_HPG * SSD_HEADDIM
    state = pl.BlockSpec((1, 1, SSD_STATE, gw), lambda b, g, c: (b, g, 0, 0))
    vmem = 2 * (2 * q * gw * 4 + 2 * q * SSD_STATE * 4 + 2 * SSD_STATE * gw * 4) + SSD_STATE * gw * 4 + 64 * q * q * 4
    return pl.pallas_call(
        functools.partial(_ssd_scan_kernel, reverse=reverse), grid=(bsz, SSD_GROUPS, nc),
        in_specs=[pl.BlockSpec((1, q, gw), lambda b, g, c: (b, cc(c), g)),
                  pl.BlockSpec((1, q, SSD_STATE), lambda b, g, c: (b, cc(c), g)),
                  pl.BlockSpec((1, q, SSD_STATE), lambda b, g, c: (b, cc(c), g)),
                  pl.BlockSpec((1, SSD_HPG, q), lambda b, g, c: (b, d * SSD_GROUPS + g, cc(c))),
                  pl.BlockSpec((1, SSD_HPG, LANES), lambda b, g, c: (d * SSD_GROUPS + g, 0, 0)),
                  state],
        out_specs=[pl.BlockSpec((1, q, gw), lambda b, g, c: (b, cc(c), g)), state],
        out_shape=[jax.ShapeDtypeStruct((bsz, n, SSD_INNER), F32),
                   jax.ShapeDtypeStruct((bsz, SSD_GROUPS, SSD_STATE, gw), F32)],
        scratch_shapes=[pltpu.VMEM((SSD_STATE, gw), F32)],
        compiler_params=_params(("parallel", "parallel", "arbitrary"), vmem),
        name="ssd_scan_rev" if reverse else "ssd_scan_fwd",
    )(xs, bm, cm, dt_t, a_rep, s0)


def _ssd_out_kernel(x_ref, yf_ref, yb_ref, xs_ref, z_ref, d_ref, ng_ref, w_ref, *epilogue_refs):
    z = z_ref[0]
    y = (yf_ref[0] + yb_ref[0] + xs_ref[0] * d_ref[0]) * (z * jax.nn.sigmoid(z))
    gw = SSD_INNER // SSD_GROUPS
    parts = []
    for g in range(SSD_GROUPS):
        yg = y[:, g * gw:(g + 1) * gw]
        parts.append(yg * lax.rsqrt(jnp.mean(yg * yg, -1, keepdims=True) + RMS_EPS))
    yn = (jnp.concatenate(parts, axis=1) * ng_ref[0]).astype(BF16)
    _mix_epilogue(x_ref[0], jnp.dot(yn, w_ref[...], preferred_element_type=F32), *epilogue_refs)


def _ssd_out(x, y_f, y_b, xs, z, d_rep, norm_g, w_out, gate, ln_g, ln_b, shift2, scale2):
    bsz, n, d = x.shape
    tm = min(ROW_TILE, n)
    assert n % tm == 0
    row = lambda c: pl.BlockSpec((1, tm, c), lambda b, i: (b, i, 0))
    vecs = [gate, ln_g, ln_b, shift2, scale2]
    in_specs = [row(d)] + [row(SSD_INNER)] * 4 + [_bcast_spec(d_rep), _bcast_spec(norm_g),
                                                 pl.BlockSpec(w_out.shape, lambda b, i: (0, 0))]
    in_specs += [_bcast_spec(a) for a in vecs]
    vmem = 2 * (3 * tm * d * 4 + 4 * tm * SSD_INNER * 4 + w_out.size * 2) + 4 * tm * SSD_INNER * 4
    return pl.pallas_call(
        _ssd_out_kernel, grid=(bsz, n // tm), in_specs=in_specs, out_specs=[row(d), row(d), row(d // 2)],
        out_shape=[jax.ShapeDtypeStruct((bsz, n, d), F32)] * 2 + [jax.ShapeDtypeStruct((bsz, n, d // 2), U32)],
        compiler_params=_params(("parallel", "parallel"), vmem), name="ssd_out",
    )(x, y_f, y_b, xs, z, d_rep, norm_g, w_out, *vecs)


def _mixer_ssd(x, ctx, sh_l, sc_l, sh_c, sc_c, p):
    a_all = -jnp.exp(jnp.concatenate([p['a_log_f'], p['a_log_b']]))
    a_rep = jnp.broadcast_to(a_all.reshape(2 * SSD_GROUPS, SSD_HPG, 1), (2 * SSD_GROUPS, SSD_HPG, LANES))
    w_in = p['w_in'].astype(BF16)
    consts = (w_in[:, :SSD_INNER], w_in[:, SSD_INNER:SSD_INNER + SSD_XBC], w_in[:, SSD_INNER + SSD_XBC:],
              p['conv_w'], p['conv_b'][None], jnp.concatenate([p['dt_bias_f'], p['dt_bias_b']])[None])
    _, xc, bc, cc, dtc = _ssd_in_proj(ctx, sh_c, sc_c, *consts)
    zl, xl, bl, cl, dtl = _ssd_in_proj(x, sh_l, sc_l, *consts)
    s0 = jnp.zeros((ctx.shape[0], SSD_GROUPS, SSD_STATE, SSD_HPG * SSD_HEADDIM), F32)
    _, sc_f = _ssd_scan(xc, bc, cc, dtc, a_rep, s0, reverse=False)
    _, sc_b = _ssd_scan(xc, bc, cc, dtc, a_rep, s0, reverse=True)
    y_f, _ = _ssd_scan(xl, bl, cl, dtl, a_rep, sc_f, reverse=False)
    y_b, _ = _ssd_scan(xl, bl, cl, dtl, a_rep, sc_b, reverse=True)
    return y_f, y_b, xl, zl


def kernel(x, c, ctx, c_ctx, mod_w, mod_b, ln_mix_g, ln_mix_b, ln_ffn_g, ln_ffn_b, a_w_in, hy_conv_w, hy_conv_b, hy_filt_w1, hy_filt_b1, hy_filt_freq1, hy_filt_w2, hy_filt_b2, hy_filt_freq2, hy_filt_w3, hy_skip, mla_q_norm, mla_w_qb, mla_kv_norm, mla_w_kvb, a_w_out, ssd_w_in, ssd_conv_w, ssd_conv_b, ssd_dt_bias_f, ssd_dt_bias_b, ssd_a_log_f, ssd_a_log_b, ssd_d, ssd_norm_g, ssd_w_out, router_w, router_bias, exp_w_gu, exp_w_down, sh_w_gu, sh_w_down):
    bsz, n_lat, d = x.shape
    n_ctx = ctx.shape[1]
    pad = -(bsz + 1) % SUBLANES
    cond = jnp.concatenate([c, c_ctx[None], jnp.zeros((pad, d), F32)], 0)
    for l in range(DEPTH):
        last = l == DEPTH - 1
        i = l // 2
        mods = _modulation(cond, mod_w, mod_b, l)
        mod = mods[:bsz].reshape(bsz, N_MOD, 1, d)
        mod_c = mods[bsz:bsz + 1].reshape(1, N_MOD, 1, d)
        sh1, sc1, g1, sh2, sc2, g2 = [mod[:, j] for j in range(N_MOD)]
        csh1, csc1, cg1, csh2, csc2, cg2 = [mod_c[:, j] for j in range(N_MOD)]
        vec = lambda a: a.reshape(1, 1, d)
        if l % 2 == 0:
            p = {"w_in": a_w_in[i], "conv_w": hy_conv_w[i], "conv_b": hy_conv_b[i],
                 "filt_w1": hy_filt_w1[i], "filt_b1": hy_filt_b1[i], "filt_freq1": hy_filt_freq1[i],
                 "filt_w2": hy_filt_w2[i], "filt_b2": hy_filt_b2[i], "filt_freq2": hy_filt_freq2[i],
                 "filt_w3": hy_filt_w3[i], "skip": hy_skip[i], "q_norm": mla_q_norm[i], "w_qb": mla_w_qb[i],
                 "kv_norm": mla_kv_norm[i], "w_kvb": mla_w_kvb[i]}
            ys_l, ys_c = _mixer_hyena_mla(x, ctx, sh1, sc1, csh1, csc1, p)
            w_out = a_w_out[i].astype(BF16)
            ws = [w_out[:HY_WIDTH], w_out[HY_WIDTH:]]
            x, ff_x, fp_x = _mix_out(x, ys_l, ws, g1, vec(ln_mix_g[l]), vec(ln_mix_b[l]), sh2, sc2)
        else:
            p = {"w_in": ssd_w_in[i], "conv_w": ssd_conv_w[i], "conv_b": ssd_conv_b[i],
                 "dt_bias_f": ssd_dt_bias_f[i], "dt_bias_b": ssd_dt_bias_b[i],
                 "a_log_f": ssd_a_log_f[i], "a_log_b": ssd_a_log_b[i]}
            assert last
            y_f, y_b, xs, z = _mixer_ssd(x, ctx, sh1, sc1, csh1, csc1, p)
            d_rep = jnp.repeat(ssd_d[i], SSD_HEADDIM).reshape(1, 1, SSD_INNER)
            x, ff_x, fp_x = _ssd_out(x, y_f, y_b, xs, z, d_rep, ssd_norm_g[i].reshape(1, 1, SSD_INNER),
                                     ssd_w_out[i].astype(BF16), g1, vec(ln_mix_g[l]), vec(ln_mix_b[l]), sh2, sc2)
        sh_gu = sh_w_gu[l].astype(BF16)
        sh_down = sh_w_down[l].astype(BF16)
        ln_g, ln_b = vec(ln_ffn_g[l]), vec(ln_ffn_b[l])
        moe_w = (router_w[l], router_bias[l], exp_w_gu, exp_w_down, l)
        if last:
            yb, pos, w = _moe_dispatch_experts(ff_x.reshape(-1, d), fp_x.reshape(-1, d // 2), *moe_w)
            x = _ffn_out(x, ff_x, yb, pos, w, 0, sh_gu, sh_down, g2, ln_g, ln_b)
        else:
            ctx, ff_c, fp_c = _mix_out(ctx, ys_c, ws, cg1, vec(ln_mix_g[l]), vec(ln_mix_b[l]), csh2, csc2)
            tokens = jnp.concatenate([ff_c.reshape(-1, d), ff_x.reshape(-1, d)], 0)
            packed = jnp.concatenate([fp_c.reshape(-1, d // 2), fp_x.reshape(-1, d // 2)], 0)
            yb, pos, w = _moe_dispatch_experts(tokens, packed, *moe_w)
            assert (bsz * n_ctx) % ROUTE_TOKENS == 0
            ctx = _ffn_out(ctx, ff_c, yb, pos, w, 0, sh_gu, sh_down, cg2, ln_g, ln_b)
            x = _ffn_out(x, ff_x, yb, pos, w, bsz * n_ctx // ROUTE_TOKENS, sh_gu, sh_down, g2, ln_g, ln_b)
    return x
```

```python
import functools
import math

import jax
import jax.numpy as jnp
from jax import lax
from jax.experimental import pallas as pl
from jax.experimental.pallas import tpu as pltpu

F32 = jnp.float32
BF16 = jnp.bfloat16
I32 = jnp.int32
U32 = jnp.uint32

D_MODEL = 1024
DEPTH = 2
GRID_W = 64
N_MOD = 6

HY_WIDTH = 512
HY_EMB = 33
HY_BANDS = (HY_EMB - 1) // 2
HY_TARGET = 1e-2
HY_FAST_DECAY = 0.3
HY_SLOW_DECAY = 1.5
HY_DECAY_MIN = math.log(HY_TARGET) / HY_SLOW_DECAY
HY_DECAY_MAX = math.log(HY_TARGET) / HY_FAST_DECAY

MLA_HEADS = 8
MLA_NOPE = 64
MLA_ROPE = 32
MLA_V = 64
MLA_Q_RANK = 256
MLA_KV_RANK = 128
MLA_QK = MLA_NOPE + MLA_ROPE
MLA_SCALE = MLA_QK ** -0.5
ROPE_THETA = 10000.0
LOG2E = math.log2(math.e)

OFF_Q = 3 * HY_WIDTH
OFF_KV = OFF_Q + MLA_Q_RANK
OFF_KPE = OFF_KV + MLA_KV_RANK

SSD_INNER = 2 * D_MODEL
SSD_HEADDIM = 64
SSD_HEADS = SSD_INNER // SSD_HEADDIM
SSD_GROUPS = 4
SSD_STATE = 128
SSD_CHUNK = 128
SSD_BC = SSD_GROUPS * SSD_STATE
SSD_XBC = SSD_INNER + 2 * SSD_BC
SSD_HPG = SSD_HEADS // SSD_GROUPS

N_EXPERTS = 256
TOP_K = 8
N_EXPERT_GROUPS = 8
TOPK_GROUPS = 4
EXPERT_DIM = 256
ROUTED_SCALE = 2.5

DN_ALPHA = (2 * DEPTH) ** 0.25
LN_EPS = 1e-5
RMS_EPS = 1e-6

LANES = 128
SUBLANES = 8
V7X_VMEM_CAP = 56 * 1024 * 1024

MOE_ROWS = 512
MOE_SUB_ROWS = 256
ROUTE_TOKENS = 256
ROW_TILE = 256
ATT_Q_TILE = 2048
ATT_HEADS_PER_STEP = 2
ATT_ONES_ROWS = 16
MLA_SLAB = LANES
DFT_INNER = 128
DFT_COLS = 2048
HY_DIRECT_MAX = 512
FILT_ROWS = 512
MOD_COLS = 1024


def _params(semantics, vmem_bytes, **kw):
    limit = int(min(max(vmem_bytes * 5 // 4, 32 * 1024 * 1024), V7X_VMEM_CAP))
    return pltpu.CompilerParams(dimension_semantics=semantics, vmem_limit_bytes=limit, **kw)


def _bcast_spec(a):
    if a.shape[0] == 1:
        return pl.BlockSpec((1, 1, a.shape[2]), lambda b, i: (0, 0, 0))
    return pl.BlockSpec((1, 1, a.shape[2]), lambda b, i: (b, 0, 0))


def _layer_norm(r, g, b):
    mu = jnp.mean(r, -1, keepdims=True)
    c = r - mu
    var = jnp.mean(c * c, -1, keepdims=True)
    return c * lax.rsqrt(var + LN_EPS) * g + b


def _swiglu_rows(xb, w_gu, w_down):
    h = jnp.dot(xb, w_gu, preferred_element_type=F32)
    half = h.shape[1] // 2
    g, u = h[:, :half], h[:, half:]
    a = (g * jax.nn.sigmoid(g) * u).astype(BF16)
    return jnp.dot(a, w_down, preferred_element_type=F32)


def _mod_kernel(c_ref, w_ref, b_ref, o_ref):
    c = c_ref[...]
    o_ref[...] = jnp.dot(c * jax.nn.sigmoid(c), w_ref[0], precision=lax.Precision.HIGHEST,
                         preferred_element_type=F32) + b_ref[0]


def _modulation(cond, mod_w, mod_b, layer):
    r, d = cond.shape
    nout = mod_w.shape[2]
    assert nout % MOD_COLS == 0 and r % SUBLANES == 0
    return pl.pallas_call(
        _mod_kernel, grid=(nout // MOD_COLS,),
        in_specs=[pl.BlockSpec((r, d), lambda j: (0, 0)), pl.BlockSpec((1, d, MOD_COLS), lambda j: (layer, 0, j)),
                  pl.BlockSpec((1, 1, MOD_COLS), lambda j: (layer, 0, j))],
        out_specs=pl.BlockSpec((r, MOD_COLS), lambda j: (0, j)),
        out_shape=jax.ShapeDtypeStruct((r, nout), F32),
        compiler_params=_params(("parallel",), 2 * (d * MOD_COLS * 4 + 2 * r * MOD_COLS * 4)), name="modulation",
    )(cond, mod_w, mod_b.reshape(mod_b.shape[0], 1, nout))


def _pack_pairs(x):
    half = x.shape[1] // 2
    hi = pltpu.bitcast(x[:, :half].astype(BF16).astype(F32), U32)
    lo = pltpu.bitcast(x[:, half:].astype(BF16).astype(F32), U32)
    return hi | (lo >> 16)


def _unpack_pairs(u):
    hi = pltpu.bitcast(u & jnp.uint32(0xFFFF0000), F32)
    lo = pltpu.bitcast(u << 16, F32)
    return jnp.concatenate([hi, lo], axis=1)


def _mix_epilogue(x, mix, gate_ref, g_ref, b_ref, sh_ref, sc_ref, xo_ref, ff_ref, ffp_ref):
    xn = _layer_norm(DN_ALPHA * x + gate_ref[0] * mix, g_ref[0], b_ref[0])
    xo_ref[0] = xn
    ff = xn * (1.0 + sc_ref[0]) + sh_ref[0]
    ff_ref[0] = ff
    ffp_ref[0] = _pack_pairs(ff)


def _mix_out_kernel(*refs, n_y):
    x_ref = refs[0]
    y_refs = refs[1:1 + n_y]
    w_refs = refs[1 + n_y:1 + 2 * n_y]
    y = None
    for y_ref, w_ref in zip(y_refs, w_refs):
        t = jnp.dot(y_ref[0].astype(BF16), w_ref[...], preferred_element_type=F32)
        y = t if y is None else y + t
    _mix_epilogue(x_ref[0], y, *refs[1 + 2 * n_y:])


def _mix_out(x, ys, ws, gate, ln_g, ln_b, shift2, scale2):
    bsz, n, d = x.shape
    tm = min(ROW_TILE, n)
    assert n % tm == 0
    row = lambda c: pl.BlockSpec((1, tm, c), lambda b, i: (b, i, 0))
    vecs = [gate, ln_g, ln_b, shift2, scale2]
    in_specs = [row(d)] + [row(y.shape[2]) for y in ys]
    in_specs += [pl.BlockSpec(w.shape, lambda b, i: (0, 0)) for w in ws]
    in_specs += [_bcast_spec(a) for a in vecs]
    vmem = 2 * (3 * tm * d * 4 + sum(tm * y.shape[2] * 4 + w.size * 2 for y, w in zip(ys, ws)))
    return pl.pallas_call(
        functools.partial(_mix_out_kernel, n_y=len(ys)),
        grid=(bsz, n // tm), in_specs=in_specs, out_specs=[row(d), row(d), row(d // 2)],
        out_shape=[jax.ShapeDtypeStruct((bsz, n, d), F32)] * 2 + [jax.ShapeDtypeStruct((bsz, n, d // 2), U32)],
        compiler_params=_params(("parallel", "parallel"), vmem), name="mix_out",
    )(x, *ys, *ws, *vecs)


def _attn_kernel(q_ref, k_ref, vt_ref, o_ref, *, tk, hp):
    nk = k_ref.shape[2]
    tq = q_ref.shape[2]
    dva = vt_ref.shape[2]
    dv = dva - ATT_ONES_ROWS

    def body(j, carry):
        off = pl.multiple_of(j * tk, tk)
        new = []
        for h in range(hp):
            m_prev, acc = carry[h]
            st = lax.dot_general(k_ref[0, h, pl.ds(off, tk), :], q_ref[0, h], (((1,), (1,)), ((), ())),
                                 preferred_element_type=F32)
            m_new = jnp.maximum(m_prev, jnp.max(st, 0, keepdims=True))
            p = jnp.exp2(st - m_new).astype(BF16)
            alpha = jnp.exp2(m_prev - m_new)
            acc = alpha * acc + jnp.dot(vt_ref[0, h, :, pl.ds(off, tk)], p, preferred_element_type=F32)
            new.append((m_new, acc))
        return tuple(new)

    init = tuple((jnp.full((1, tq), -jnp.inf, F32), jnp.zeros((dva, tq), F32)) for _ in range(hp))
    fin = lax.fori_loop(0, nk // tk, body, init)
    outs = [acc[:dv] / acc[dv:dv + 1] for _, acc in fin]
    o_ref[0] = jnp.concatenate(outs, 0).T


def _attention(q, k, vt, tq, tk):
    bsz, h, nq, dk = q.shape
    nk, dva = k.shape[2], vt.shape[2]
    dv = dva - ATT_ONES_ROWS
    hp = ATT_HEADS_PER_STEP
    assert nq % tq == 0 and nk % tk == 0 and h % hp == 0
    vmem = 2 * hp * (tq * LANES * 2 + nk * LANES * 2 + dva * nk * 2) + 2 * tq * hp * dv * 4 + 6 * hp * tk * tq * 4
    return pl.pallas_call(
        functools.partial(_attn_kernel, tk=tk, hp=hp), grid=(bsz, h // hp, nq // tq),
        in_specs=[pl.BlockSpec((1, hp, tq, dk), lambda b, g, i: (b, g, i, 0)),
                  pl.BlockSpec((1, hp, nk, dk), lambda b, g, i: (b, g, 0, 0)),
                  pl.BlockSpec((1, hp, dva, nk), lambda b, g, i: (b, g, 0, 0))],
        out_specs=pl.BlockSpec((1, tq, hp * dv), lambda b, g, i: (b, i, g)),
        out_shape=jax.ShapeDtypeStruct((bsz, nq, h * dv), F32),
        compiler_params=_params(("parallel", "parallel", "arbitrary"), vmem), name="mla_attention",
    )(q, k, vt)


def _key_tile(nk):
    for t in (768, 512, 384, 256, 128):
        if nk % t == 0:
            return t
    return nk


def _router_kernel(x_ref, wt_ref, bias_ref, upper_ref, idx_ref, w_ref, rank_ref, cnt_ref, run_sc):
    i = pl.program_id(0)
    tm = x_ref.shape[0]

    @pl.when(i == 0)
    def _():
        run_sc[...] = jnp.zeros(run_sc.shape, F32)

    logits = lax.dot_general(wt_ref[...], x_ref[...], (((1,), (1,)), ((), ())),
                             precision=lax.Precision.HIGHEST, preferred_element_type=F32)
    sc = jax.nn.sigmoid(logits)
    ch = sc + bias_ref[:, :1]
    neg = -jnp.inf
    chg = ch.reshape(N_EXPERT_GROUPS, N_EXPERTS // N_EXPERT_GROUPS, tm)
    m1 = jnp.max(chg, axis=1)
    eq = chg == m1[:, None, :]
    cnt = jnp.sum(eq.astype(F32), axis=1)
    m2 = jnp.max(jnp.where(eq, neg, chg), axis=1)
    g2 = m1 + jnp.where(cnt >= 2.0, m1, m2)
    gi = lax.broadcasted_iota(I32, g2.shape, 0)
    beaten = jnp.zeros(g2.shape, F32)
    for g in range(N_EXPERT_GROUPS):
        row = g2[g:g + 1, :]
        beaten = beaten + jnp.where(row > g2, 1.0, jnp.where(row == g2, jnp.where(gi > g, 1.0, 0.0), 0.0))
    keep = beaten < float(TOPK_GROUPS)
    cur = jnp.where(keep[:, None, :], chg, neg).reshape(N_EXPERTS, tm)
    eidx = lax.broadcasted_iota(I32, (N_EXPERTS, tm), 0)
    multi = jnp.zeros((N_EXPERTS, tm), F32)
    hits, idx_rows, w_rows = [], [], []
    for _ in range(TOP_K):
        m = jnp.max(cur, axis=0, keepdims=True)
        sel = jnp.min(jnp.where(cur == m, eidx, N_EXPERTS), axis=0, keepdims=True)
        hit = eidx == sel
        idx_rows.append(sel)
        w_rows.append(jnp.sum(jnp.where(hit, sc, 0.0), axis=0, keepdims=True))
        cur = jnp.where(hit, neg, cur)
        multi = multi + jnp.where(hit, 1.0, 0.0)
        hits.append(hit)
    base = jnp.concatenate([run_sc[...]] * (tm // LANES), axis=1)
    before = jnp.dot(multi.astype(BF16), upper_ref[...], preferred_element_type=F32) + base
    rank_rows = [jnp.sum(jnp.where(hit, before, 0.0), axis=0, keepdims=True) for hit in hits]
    w = jnp.concatenate(w_rows, axis=0)
    idx_ref[...] = jnp.concatenate(idx_rows, axis=0)
    w_ref[...] = (w / jnp.sum(w, axis=0, keepdims=True) * ROUTED_SCALE).T
    rank_ref[...] = jnp.concatenate(rank_rows, axis=0).astype(I32)
    run_sc[...] = run_sc[...] + jnp.dot(multi.astype(BF16), jnp.ones((tm, LANES), BF16), preferred_element_type=F32)
    cnt_ref[...] = run_sc[...]


def _router(tokens, router_w, router_bias):
    t, d = tokens.shape
    tm = ROUTE_TOKENS
    assert t % tm == 0
    wt = router_w.T
    bias = jnp.broadcast_to(router_bias.astype(F32)[:, None], (N_EXPERTS, LANES))
    r = jnp.arange(tm)
    upper = (r[:, None] < r[None, :]).astype(BF16)
    col = pl.BlockSpec((TOP_K, tm), lambda i: (0, i))
    full = lambda a: pl.BlockSpec(a.shape, lambda i: (0,) * a.ndim)
    vmem = 2 * (tm * d * 4 + wt.size * 4) + 40 * N_EXPERTS * tm * 4
    return pl.pallas_call(
        _router_kernel, grid=(t // tm,),
        in_specs=[pl.BlockSpec((tm, d), lambda i: (i, 0)), full(wt), full(bias), full(upper)],
        out_specs=[col, pl.BlockSpec((tm, TOP_K), lambda i: (i, 0)), col,
                   pl.BlockSpec((N_EXPERTS, LANES), lambda i: (0, 0))],
        out_shape=[jax.ShapeDtypeStruct((TOP_K, t), I32), jax.ShapeDtypeStruct((t, TOP_K), F32),
                   jax.ShapeDtypeStruct((TOP_K, t), I32), jax.ShapeDtypeStruct((N_EXPERTS, LANES), F32)],
        scratch_shapes=[pltpu.VMEM((N_EXPERTS, LANES), F32)],
        compiler_params=_params(("arbitrary",), vmem), name="moe_router",
    )(tokens, wt, bias, upper)


def _positions_kernel(idx_ref, rank_ref, start_ref, pos_ref):
    tm = idx_ref.shape[1]
    eidx = lax.broadcasted_iota(I32, (N_EXPERTS, tm), 0)
    start = jnp.concatenate([start_ref[...]] * (tm // LANES), axis=1)
    rows = [jnp.sum(jnp.where(eidx == idx_ref[k:k + 1, :], start, 0), axis=0, keepdims=True) for k in range(TOP_K)]
    pos_ref[...] = jnp.concatenate(rows, axis=0) + rank_ref[...]


def _positions(idx, rank, pad_start):
    t = idx.shape[1]
    tm = ROUTE_TOKENS
    start = jnp.broadcast_to(pad_start.astype(I32)[:, None], (N_EXPERTS, LANES))
    col = pl.BlockSpec((TOP_K, tm), lambda i: (0, i))
    return pl.pallas_call(
        _positions_kernel, grid=(t // tm,),
        in_specs=[col, col, pl.BlockSpec((N_EXPERTS, LANES), lambda i: (0, 0))], out_specs=col,
        out_shape=jax.ShapeDtypeStruct((TOP_K, t), I32),
        compiler_params=_params(("parallel",), 8 * N_EXPERTS * tm * 4), name="moe_positions",
    )(idx, rank, start)


def _moe_plan(counts, t):
    n_blocks = -(-(t * TOP_K) // MOE_ROWS) + N_EXPERTS
    c = counts.astype(I32)
    padded = (c + MOE_ROWS - 1) // MOE_ROWS * MOE_ROWS
    pad_end = jnp.cumsum(padded)
    pad_start = pad_end - padded
    block_e = jnp.minimum(jnp.searchsorted(pad_end, jnp.arange(n_blocks, dtype=I32) * MOE_ROWS, side='right'),
                          N_EXPERTS - 1).astype(I32)
    n_valid = (pad_end[-1:] // MOE_ROWS).astype(I32)
    return pad_start, block_e, n_valid, n_blocks * MOE_ROWS


def _row_copy(src, dst, sem):
    return pltpu.make_async_copy(src, dst, sem)


def _dispatch_kernel(pos_ref, x_ref, buf_in_ref, buf_ref, sem):
    del buf_in_ref
    tm = x_ref.shape[0]

    def issue(t, carry):
        for k in range(TOP_K):
            _row_copy(x_ref.at[pl.ds(t, 1)], buf_ref.at[pl.ds(pos_ref[k, t], 1)], sem).start()
        return carry

    lax.fori_loop(0, tm, issue, 0)
    for k in range(TOP_K):
        _row_copy(x_ref, buf_ref.at[pl.ds(0, tm)], sem).wait()


def _dispatch(pos, tokens, cap):
    t, d = tokens.shape
    tm = ROUTE_TOKENS
    zeros = jnp.zeros((cap, d), tokens.dtype)
    return pl.pallas_call(
        _dispatch_kernel, grid=(t // tm,),
        in_specs=[pl.BlockSpec((TOP_K, tm), lambda i: (0, i), memory_space=pltpu.SMEM),
                  pl.BlockSpec((tm, d), lambda i: (i, 0)),
                  pl.BlockSpec(memory_space=pl.ANY)],
        out_specs=pl.BlockSpec(memory_space=pl.ANY),
        out_shape=jax.ShapeDtypeStruct((cap, d), tokens.dtype),
        scratch_shapes=[pltpu.SemaphoreType.DMA(())],
        input_output_aliases={2: 0},
        compiler_params=_params(("arbitrary",), 2 * tm * d * 4, has_side_effects=True),
        name="moe_dispatch",
    )(pos, tokens, zeros)


def _experts_kernel(be_ref, nv_ref, x_ref, wgu_ref, wdn_ref, o_ref, wgu_sc, wdn_sc):
    i = pl.program_id(0)

    @pl.when(i < nv_ref[0])
    def _():
        @pl.when(jnp.logical_or(i == 0, be_ref[i] != be_ref[jnp.maximum(i - 1, 0)]))
        def _():
            wgu_sc[...] = wgu_ref[0, 0].astype(BF16)
            wdn_sc[...] = wdn_ref[0, 0].astype(BF16)

        for part in range(MOE_ROWS // MOE_SUB_ROWS):
            rows = pl.ds(part * MOE_SUB_ROWS, MOE_SUB_ROWS)
            x = _unpack_pairs(x_ref[rows, :]).astype(BF16)
            o_ref[rows, :] = _pack_pairs(_swiglu_rows(x, wgu_sc[...], wdn_sc[...]))

    @pl.when(i >= nv_ref[0])
    def _():
        o_ref[...] = jnp.zeros(o_ref.shape, U32)


def _moe_experts(xb, block_e, n_valid, w_gu, w_down, layer):
    cap, dp = xb.shape
    d = 2 * dp
    gu = w_gu.shape[3]
    ed = w_down.shape[2]
    live = lambda i, be, nv: jnp.maximum(jnp.minimum(i, nv[0] - 1), 0)
    grid_spec = pltpu.PrefetchScalarGridSpec(
        num_scalar_prefetch=2, grid=(cap // MOE_ROWS,),
        in_specs=[pl.BlockSpec((MOE_ROWS, dp), lambda i, be, nv: (live(i, be, nv), 0)),
                  pl.BlockSpec((1, 1, d, gu), lambda i, be, nv: (layer, be[i], 0, 0)),
                  pl.BlockSpec((1, 1, ed, d), lambda i, be, nv: (layer, be[i], 0, 0))],
        out_specs=pl.BlockSpec((MOE_ROWS, dp), lambda i, be, nv: (i, 0)),
        scratch_shapes=[pltpu.VMEM((d, gu), BF16), pltpu.VMEM((ed, d), BF16)])
    vmem = 2 * (2 * MOE_ROWS * dp * 4 + d * gu * 4 + ed * d * 4) + (d * gu + ed * d) * 2 + 6 * MOE_ROWS * d * 4
    return pl.pallas_call(
        _experts_kernel, grid_spec=grid_spec, out_shape=jax.ShapeDtypeStruct((cap, dp), U32),
        compiler_params=_params(("arbitrary",), vmem), name="moe_experts",
    )(block_e, n_valid, xb, w_gu, w_down)


def _ffn_out_kernel(pos_ref, w_ref, x_ref, ff_ref, yb_ref, wgu_ref, wdn_ref, gate_ref, g_ref, b_ref, xo_ref,
                    rows_sc, sem):
    tm = x_ref.shape[1]

    def issue(t, carry):
        for k in range(TOP_K):
            _row_copy(yb_ref.at[pl.ds(pos_ref[k, t], 1)], rows_sc.at[k, pl.ds(t, 1)], sem).start()
        return carry

    lax.fori_loop(0, tm, issue, 0)
    out = _swiglu_rows(ff_ref[0].astype(BF16), wgu_ref[...], wdn_ref[...])
    for k in range(TOP_K):
        _row_copy(yb_ref.at[pl.ds(0, tm)], rows_sc.at[k], sem).wait()
    for k in range(TOP_K):
        out = out + w_ref[:, k:k + 1] * _unpack_pairs(rows_sc[k])
    xo_ref[0] = _layer_norm(DN_ALPHA * x_ref[0] + gate_ref[0] * out, g_ref[0], b_ref[0])


def _ffn_out(x, ff, yb, pos, w, tile0, sh_gu, sh_down, gate, ln_g, ln_b):
    bsz, n, d = x.shape
    tm = min(ROUTE_TOKENS, n)
    nt = n // tm
    row = pl.BlockSpec((1, tm, d), lambda b, i: (b, i, 0))
    vecs = [gate, ln_g, ln_b]
    in_specs = [pl.BlockSpec((TOP_K, tm), lambda b, i: (0, tile0 + b * nt + i), memory_space=pltpu.SMEM),
                pl.BlockSpec((tm, TOP_K), lambda b, i: (tile0 + b * nt + i, 0)),
                row, row, pl.BlockSpec(memory_space=pl.ANY),
                pl.BlockSpec(sh_gu.shape, lambda b, i: (0, 0)), pl.BlockSpec(sh_down.shape, lambda b, i: (0, 0))]
    in_specs += [_bcast_spec(a) for a in vecs]
    vmem = TOP_K * tm * d * 2 + 2 * (3 * tm * d * 4 + sh_gu.size * 2 + sh_down.size * 2) + 6 * tm * d * 4
    return pl.pallas_call(
        _ffn_out_kernel, grid=(bsz, nt), in_specs=in_specs, out_specs=row,
        out_shape=jax.ShapeDtypeStruct((bsz, n, d), F32),
        scratch_shapes=[pltpu.VMEM((TOP_K, tm, d // 2), U32), pltpu.SemaphoreType.DMA(())],
        compiler_params=_params(("arbitrary", "arbitrary"), vmem), name="moe_combine_ffn_out",
    )(pos, w, x, ff, yb, sh_gu, sh_down, *vecs)


def _moe_dispatch_experts(tokens, packed, router_w, router_bias, w_gu, w_down, layer):
    t = tokens.shape[0]
    idx, w, rank, counts = _router(tokens, router_w, router_bias)
    pad_start, block_e, n_valid, cap = _moe_plan(counts[:, 0], t)
    pos = _positions(idx, rank, pad_start)
    xb = _dispatch(pos, packed, cap)
    return _moe_experts(xb, block_e, n_valid, w_gu, w_down, layer), pos, w


def _rope_tables(n):
    rows = n // GRID_W
    row = jnp.repeat(jnp.arange(rows), GRID_W).astype(F32)
    col = jnp.tile(jnp.arange(GRID_W), rows).astype(F32)
    half = MLA_ROPE // 2
    inv = ROPE_THETA ** (-jnp.arange(0, half, 2, dtype=F32) / half)
    ang = jnp.concatenate([row[:, None] * inv, col[:, None] * inv], -1)
    return jnp.cos(ang), jnp.sin(ang)


def _filter_kernel(fr_ref, w1_ref, b1_ref, f1_ref, w2_ref, b2_ref, f2_ref, w3_ref, dl_ref, k_ref, l1_ref, *, n):
    i = pl.program_id(0)
    tr = k_ref.shape[0]
    hp = lax.Precision.HIGHEST
    tap = i * tr + lax.broadcasted_iota(I32, (tr, 1), 0)
    lag = jnp.where(tap < n, tap, 2 * n - tap).astype(F32)
    t = lag * (1.0 / (n - 1))
    ang = (2.0 * math.pi / n) * lag * fr_ref[...]
    lane = lax.broadcasted_iota(I32, (tr, LANES), 1)
    z = jnp.where(lane == 0, t, jnp.where(lane <= HY_BANDS, jnp.cos(ang),
                                          jnp.where(lane <= 2 * HY_BANDS, -jnp.sin(ang), 0.0)))
    h = jnp.sin(f1_ref[...] * (jnp.dot(z, w1_ref[...], precision=hp, preferred_element_type=F32) + b1_ref[...]))
    h = jnp.sin(f2_ref[...] * (jnp.dot(h, w2_ref[...], precision=hp, preferred_element_type=F32) + b2_ref[...]))
    h = jnp.dot(h, w3_ref[...], precision=hp, preferred_element_type=F32)
    hsel = jnp.where(tap < n, h[:, :HY_WIDTH], h[:, HY_WIDTH:])
    k = jnp.where(tap == n, 0.0, hsel * jnp.exp(-t * dl_ref[...]))
    k_ref[...] = k

    @pl.when(i == 0)
    def _():
        l1_ref[...] = jnp.zeros(l1_ref.shape, F32)

    l1_ref[...] = l1_ref[...] + jnp.sum(jnp.abs(k), axis=0, keepdims=True)


def _hyena_filter(n, w1, b1, f1, w2, b2, f2, w3):
    nn = 2 * n
    tr = min(FILT_ROWS, nn)
    assert nn % tr == 0
    emb, ffn = w1.shape
    fr = jnp.linspace(1e-4, HY_BANDS - 1, HY_BANDS, dtype=F32)
    fr_l = jnp.concatenate([jnp.zeros((1,), F32), fr, fr, jnp.zeros((LANES - emb,), F32)])[None]
    w1p = jnp.concatenate([w1, jnp.zeros((LANES - emb, ffn), F32)], 0)
    deltas = jnp.abs(jnp.linspace(HY_DECAY_MIN, HY_DECAY_MAX, HY_WIDTH, dtype=F32))[None]
    ops = [fr_l, w1p, b1[None], f1[None], w2, b2[None], f2[None], w3, deltas]
    full = lambda a: pl.BlockSpec(a.shape, lambda i: (0,) * a.ndim)
    return pl.pallas_call(
        functools.partial(_filter_kernel, n=n), grid=(nn // tr,), in_specs=[full(a) for a in ops],
        out_specs=[pl.BlockSpec((tr, HY_WIDTH), lambda i: (i, 0)), pl.BlockSpec((1, HY_WIDTH), lambda i: (0, 0))],
        out_shape=[jax.ShapeDtypeStruct((nn, HY_WIDTH), F32), jax.ShapeDtypeStruct((1, HY_WIDTH), F32)],
        compiler_params=_params(("arbitrary",), 16 * tr * 2 * HY_WIDTH * 4), name="hyena_filter",
    )(*ops)


def _split(a):
    hi = a.astype(BF16)
    return hi, (a - hi.astype(F32)).astype(BF16)


def _dot3(a_hi, a_lo, x):
    x_hi, x_lo = _split(x)
    return (jnp.dot(a_hi, x_hi, preferred_element_type=F32) + jnp.dot(a_lo, x_hi, preferred_element_type=F32)
            + jnp.dot(a_hi, x_lo, preferred_element_type=F32))


def _cis(num, den):
    ang = (2.0 * math.pi / den) * (num % den).astype(F32)
    return jnp.cos(ang), jnp.sin(ang)


def _dft_outer_kernel(f_hi_ref, f_lo_ref, x_ref, o_ref):
    o_ref[0] = _dot3(f_hi_ref[...], f_lo_ref[...], x_ref[0])


def _dft_outer(x, f_hi, f_lo):
    bsz, k, m = x.shape
    r = f_hi.shape[0]
    tn = min(DFT_COLS, m)
    assert m % tn == 0
    vmem = 2 * (2 * f_hi.size * 2 + k * tn * 4 + r * tn * 4) + 3 * (k + r) * tn * 4
    return pl.pallas_call(
        _dft_outer_kernel, grid=(bsz, m // tn),
        in_specs=[pl.BlockSpec(f_hi.shape, lambda b, j: (0, 0)), pl.BlockSpec(f_lo.shape, lambda b, j: (0, 0)),
                  pl.BlockSpec((1, k, tn), lambda b, j: (b, 0, j))],
        out_specs=pl.BlockSpec((1, r, tn), lambda b, j: (b, 0, j)),
        out_shape=jax.ShapeDtypeStruct((bsz, r, m), F32),
        compiler_params=_params(("parallel", "parallel"), vmem), name="hyena_dft_outer",
    )(f_hi, f_lo, x)


def _dft_inner_kernel(m_hi_ref, m_lo_ref, mt_hi_ref, mt_lo_ref, a_ref, h_ref, o_ref, *, conv):
    n2 = a_ref.shape[3]
    x = a_ref[0, :, 0].reshape(2 * n2, a_ref.shape[4])
    y = _dot3(m_hi_ref[0], m_lo_ref[0], x)
    if conv:
        h = h_ref[0, :, 0].reshape(2 * n2, h_ref.shape[4])
        yr, yi, hr, hi = y[:n2], y[n2:], h[:n2], h[n2:]
        prod = jnp.concatenate([yr * hr - yi * hi, yr * hi + yi * hr], axis=0)
        y = _dot3(mt_hi_ref[0], mt_lo_ref[0], prod)
    o_ref[0, :, 0] = y.reshape(2, n2, y.shape[1])


def _dft_inner(a, h, mats, conv):
    bsz, _, n1, n2, c = a.shape
    blk = lambda sel: pl.BlockSpec((1, 2, 1, n2, c), sel)
    mat = pl.BlockSpec((1, 2 * n2, 2 * n2), lambda k, b: (k, 0, 0))
    vmem = 2 * (4 * 4 * n2 * n2 * 2 + 3 * 2 * n2 * c * 4) + 8 * 2 * n2 * c * 4
    return pl.pallas_call(
        functools.partial(_dft_inner_kernel, conv=conv), grid=(n1, bsz),
        in_specs=[mat, mat, mat, mat, blk(lambda k, b: (b, 0, k, 0, 0)), blk(lambda k, b: (0, 0, k, 0, 0))],
        out_specs=blk(lambda k, b: (b, 0, k, 0, 0)), out_shape=jax.ShapeDtypeStruct(a.shape, F32),
        compiler_params=_params(("parallel", "arbitrary"), vmem),
        name="hyena_dft_inner_conv" if conv else "hyena_dft_inner",
    )(*mats, a, h)


def _dft_final_kernel(fd_hi_ref, fd_lo_ref, a_ref, z_ref, x0_ref, skip_ref, l1_ref, o_ref):
    y = _dot3(fd_hi_ref[...], fd_lo_ref[...], a_ref[0])
    o_ref[0] = (y / l1_ref[...] + z_ref[0] * skip_ref[...]) * x0_ref[0]


def _dft_final(a, z, x0, skip_t, l1_t, fd_hi, fd_lo):
    bsz, r2, m = a.shape
    k = fd_hi.shape[0]
    tn = min(DFT_COLS, m)
    row = pl.BlockSpec((1, k, tn), lambda b, j: (b, 0, j))
    vmem = 2 * (2 * fd_hi.size * 2 + r2 * tn * 4 + 3 * k * tn * 4) + 3 * (k + r2) * tn * 4
    return pl.pallas_call(
        _dft_final_kernel, grid=(bsz, m // tn),
        in_specs=[pl.BlockSpec(fd_hi.shape, lambda b, j: (0, 0)), pl.BlockSpec(fd_lo.shape, lambda b, j: (0, 0)),
                  pl.BlockSpec((1, r2, tn), lambda b, j: (b, 0, j)), row, row,
                  pl.BlockSpec((1, tn), lambda b, j: (0, 0)), pl.BlockSpec((1, tn), lambda b, j: (0, 0))],
        out_specs=row, out_shape=jax.ShapeDtypeStruct((bsz, k, m), F32),
        compiler_params=_params(("parallel", "parallel"), vmem), name="hyena_dft_final",
    )(fd_hi, fd_lo, a, z, x0, skip_t, l1_t)


def _hyena_long_conv(z, x0, k, l1, skip):
    bsz, n, c = z.shape
    n2 = DFT_INNER
    nn = 2 * n
    n1 = nn // n2
    assert nn == n1 * n2 and n1 % 2 == 0
    half = n1 // 2
    m = n2 * c
    j1 = jnp.arange(n1)
    ca, sa = _cis(j1[:, None] * j1[None, :], n1)
    fa = jnp.concatenate([ca, -sa], axis=0)
    fd = jnp.concatenate([ca, -sa], axis=1)[:half] / nn
    j2 = jnp.arange(n2)
    cb, sb = _cis(j2[None, None, :] * (n1 * j2[None, :, None] + j1[:, None, None]), nn)
    mb = jnp.concatenate([jnp.concatenate([cb, sb], 2), jnp.concatenate([-sb, cb], 2)], 1)
    mats = _split(mb) + _split(jnp.swapaxes(mb, 1, 2))
    fa_hi, fa_lo = _split(fa)
    hk = _dft_outer(k.reshape(1, n1, m), fa_hi, fa_lo).reshape(1, 2, n1, n2, c)
    hk = _dft_inner(hk, hk, mats, conv=False)
    a = _dft_outer(z.reshape(bsz, half, m), fa_hi[:, :half], fa_lo[:, :half]).reshape(bsz, 2, n1, n2, c)
    a = _dft_inner(a, hk, mats, conv=True).reshape(bsz, 2 * n1, m)
    reps = (1, min(DFT_COLS, m) // c)
    out = _dft_final(a, z.reshape(bsz, half, m), x0.reshape(bsz, half, m), jnp.tile(skip.reshape(1, c), reps),
                     jnp.tile(l1, reps), *_split(fd))
    return out.reshape(bsz, n, c)


def _short_conv_kernel(f_hi_ref, f_lo_ref, fi_hi_ref, fi_lo_ref, z_ref, x0_ref, k_ref, skip_ref, l1_ref, o_ref):
    n = z_ref.shape[1]
    nn = 2 * n
    z = z_ref[0]
    hk = _dot3(f_hi_ref[...], f_lo_ref[...], k_ref[...])
    zs = _dot3(f_hi_ref[:, :n], f_lo_ref[:, :n], z)
    zr, zi, hr, hi = zs[:nn], zs[nn:], hk[:nn], hk[nn:]
    prod = jnp.concatenate([zr * hr - zi * hi, zr * hi + zi * hr], axis=0)
    o_ref[0] = (_dot3(fi_hi_ref[...], fi_lo_ref[...], prod) / l1_ref[...] + z * skip_ref[...]) * x0_ref[0]


def _hyena_short_conv(z, x0, k, l1, skip):
    bsz, n, c = z.shape
    nn = 2 * n
    idx = jnp.arange(nn)
    cf, sf = _cis(idx[:, None] * idx[None, :], nn)
    f = jnp.concatenate([cf, -sf], axis=0)
    fi = jnp.concatenate([cf, -sf], axis=1)[:n] / nn
    full = lambda a: pl.BlockSpec(a.shape, lambda b: (0,) * a.ndim)
    row = pl.BlockSpec((1, n, c), lambda b: (b, 0, 0))
    ops = _split(f) + _split(fi)
    vmem = 2 * (sum(a.size * 2 for a in ops) + 3 * n * c * 4 + nn * c * 4) + 12 * 2 * nn * c * 4
    return pl.pallas_call(
        _short_conv_kernel, grid=(bsz,),
        in_specs=[full(a) for a in ops] + [row, row, full(k), pl.BlockSpec((1, c), lambda b: (0, 0)),
                                           pl.BlockSpec((1, c), lambda b: (0, 0))],
        out_specs=row, out_shape=jax.ShapeDtypeStruct((bsz, n, c), F32),
        compiler_params=_params(("parallel",), vmem), name="hyena_short_conv",
    )(*ops, z, x0, k, skip.reshape(1, c), l1)


def _hyena_sequence(z, x0, filt, skip):
    n = z.shape[1]
    k, l1 = _hyena_filter(n, *filt)
    conv = _hyena_short_conv if n <= HY_DIRECT_MAX else _hyena_long_conv
    return conv(z, x0, k, l1, skip)


def _rms(x, g):
    return x * lax.rsqrt(jnp.mean(x * x, -1, keepdims=True) + RMS_EPS) * g


def _mla_in_kernel(xc_ref, xp_ref, xn_ref, sh_ref, sc_ref, why_ref, cw_ref, cb_ref, wql_ref, qg_ref, wqa_ref, wqb_ref,
                   wkvl_ref, kvg_ref, wk_ref, wv_ref, wpa_ref, wpb_ref, qa_ref, qb_ref, ka_ref, kb_ref,
                   x0_ref, z_ref, q_ref, k_ref, vt_ref):
    tm = xc_ref.shape[1]
    rows = jnp.concatenate([xp_ref[0], xc_ref[0], xn_ref[0]], axis=0)
    h = (rows * (1.0 + sc_ref[0]) + sh_ref[0]).astype(BF16)
    hc = h[SUBLANES:SUBLANES + tm]
    y = _conv3(jnp.dot(h, why_ref[...], preferred_element_type=F32), cw_ref, cb_ref, tm)
    x0_ref[0] = y[:, :HY_WIDTH]
    z_ref[0] = y[:, 2 * HY_WIDTH:] * y[:, HY_WIDTH:2 * HY_WIDTH]
    ql = _rms(jnp.dot(hc, wql_ref[...], preferred_element_type=F32), qg_ref[...]).astype(BF16)
    nh = q_ref.shape[1]
    qa = jnp.concatenate([qa_ref[...]] * nh, axis=1)
    qb = jnp.concatenate([qb_ref[...]] * nh, axis=1)
    q = (jnp.dot(ql, wqa_ref[...], preferred_element_type=F32) * qa
         + jnp.dot(ql, wqb_ref[...], preferred_element_type=F32) * qb).astype(BF16)
    kvl = _rms(jnp.dot(hc, wkvl_ref[...], preferred_element_type=F32), kvg_ref[...]).astype(BF16)
    kn = jnp.dot(kvl, wk_ref[...], preferred_element_type=F32)
    v_t = jnp.dot(kvl, wv_ref[...], preferred_element_type=F32).T
    kpe = (jnp.dot(hc, wpa_ref[...], preferred_element_type=F32) * ka_ref[...]
           + jnp.dot(hc, wpb_ref[...], preferred_element_type=F32) * kb_ref[...])
    ones = jnp.ones((ATT_ONES_ROWS, tm), BF16)
    for hd in range(nh):
        q_ref[0, hd] = q[:, hd * MLA_SLAB:(hd + 1) * MLA_SLAB]
        k_ref[0, hd] = (kn[:, hd * MLA_SLAB:(hd + 1) * MLA_SLAB] + kpe).astype(BF16)
        vt_ref[0, hd, :MLA_V, :] = v_t[hd * MLA_V:(hd + 1) * MLA_V].astype(BF16)
        vt_ref[0, hd, MLA_V:, :] = ones


def _rot_cols(w_pe):
    ev, od = w_pe[..., 0::2], w_pe[..., 1::2]
    return jnp.concatenate([ev, od], -1), jnp.concatenate([-od, ev], -1)


def _mla_weights(p):
    w_in = p['w_in']
    kin = w_in.shape[0]
    wq = p['w_qb'].reshape(MLA_Q_RANK, MLA_HEADS, MLA_QK)
    qa_pe, qb_pe = _rot_cols(wq[..., MLA_NOPE:])
    zq = jnp.zeros((MLA_Q_RANK, MLA_HEADS, MLA_SLAB - MLA_QK), F32)
    w_qa = jnp.concatenate([wq[..., :MLA_NOPE], qa_pe, zq], -1).reshape(MLA_Q_RANK, MLA_HEADS * MLA_SLAB)
    w_qb = jnp.concatenate([jnp.zeros_like(wq[..., :MLA_NOPE]), qb_pe, zq], -1)
    w_qb = w_qb.reshape(MLA_Q_RANK, MLA_HEADS * MLA_SLAB)
    wkv = p['w_kvb'].reshape(MLA_KV_RANK, MLA_HEADS, MLA_NOPE + MLA_V)
    w_k = jnp.concatenate([wkv[..., :MLA_NOPE], jnp.zeros((MLA_KV_RANK, MLA_HEADS, MLA_SLAB - MLA_NOPE), F32)], -1)
    w_k = w_k.reshape(MLA_KV_RANK, MLA_HEADS * MLA_SLAB)
    w_v = wkv[..., MLA_NOPE:].reshape(MLA_KV_RANK, MLA_HEADS * MLA_V)
    pa, pb = _rot_cols(w_in[:, OFF_KPE:])
    left, right = jnp.zeros((kin, MLA_NOPE), F32), jnp.zeros((kin, MLA_SLAB - MLA_QK), F32)
    w_pa = jnp.concatenate([left, pa, right], -1)
    w_pb = jnp.concatenate([left, pb, right], -1)
    bf = lambda a: a.astype(BF16)
    return dict(w_hy=bf(w_in[:, :OFF_Q]), conv_w=p['conv_w'], conv_b=p['conv_b'][None],
                w_ql=bf(w_in[:, OFF_Q:OFF_KV]), q_g=p['q_norm'][None], w_qa=bf(w_qa), w_qb=bf(w_qb),
                w_kvl=bf(w_in[:, OFF_KV:OFF_KPE]), kv_g=p['kv_norm'][None], w_k=bf(w_k), w_v=bf(w_v),
                w_pa=bf(w_pa), w_pb=bf(w_pb))


def _rope_slabs(n, rotate):
    one = jnp.ones((n, MLA_NOPE), F32)
    zero = jnp.zeros((n, MLA_SLAB - MLA_QK), F32)
    if rotate:
        cos, sin = _rope_tables(n)
    else:
        cos, sin = jnp.ones((n, MLA_ROPE // 2), F32), jnp.zeros((n, MLA_ROPE // 2), F32)
    return (jnp.concatenate([one, cos, cos, zero], -1), jnp.concatenate([jnp.zeros_like(one), sin, sin, zero], -1))


def _mla_in_proj(x, shift, scale, w, rotate):
    bsz, n, kin = x.shape
    tm = min(ROW_TILE, n)
    assert n % tm == 0
    ca, sb = _rope_slabs(n, rotate)
    s = MLA_SCALE * LOG2E
    tabs = [ca * s, sb * s, ca, sb]
    consts = [w['w_hy'], w['conv_w'], w['conv_b'], w['w_ql'], w['q_g'], w['w_qa'], w['w_qb'], w['w_kvl'], w['kv_g'],
              w['w_k'], w['w_v'], w['w_pa'], w['w_pb']]
    full = lambda a: pl.BlockSpec(a.shape, lambda b, i: (0,) * a.ndim)
    row = lambda c: pl.BlockSpec((1, tm, c), lambda b, i: (b, i, 0))
    head = pl.BlockSpec((1, MLA_HEADS, tm, MLA_SLAB), lambda b, i: (b, 0, i, 0))
    dva = MLA_V + ATT_ONES_ROWS
    vmem = 2 * (tm * kin * 4 + sum(a.size * a.dtype.itemsize for a in consts) + 2 * tm * HY_WIDTH * 4
                + 3 * MLA_HEADS * tm * MLA_SLAB * 2 + 4 * tm * MLA_SLAB * 4) + 8 * (tm + 16) * 3 * HY_WIDTH * 4
    return pl.pallas_call(
        _mla_in_kernel, grid=(bsz, n // tm),
        in_specs=_halo_specs(n, tm, kin) + [_bcast_spec(shift), _bcast_spec(scale)] + [full(a) for a in consts]
        + [pl.BlockSpec((tm, MLA_SLAB), lambda b, i: (i, 0))] * 4,
        out_specs=[row(HY_WIDTH), row(HY_WIDTH), head, head,
                   pl.BlockSpec((1, MLA_HEADS, dva, tm), lambda b, i: (b, 0, 0, i))],
        out_shape=[jax.ShapeDtypeStruct((bsz, n, HY_WIDTH), F32)] * 2
        + [jax.ShapeDtypeStruct((bsz, MLA_HEADS, n, MLA_SLAB), BF16)] * 2
        + [jax.ShapeDtypeStruct((bsz, MLA_HEADS, dva, n), BF16)],
        compiler_params=_params(("parallel", "parallel"), vmem), name="mla_in_proj",
    )(x, x, x, shift, scale, *consts, *tabs)


def _mixer_hyena_mla(x, ctx, sh_l, sc_l, sh_c, sc_c, p):
    n, nc = x.shape[1], ctx.shape[1]
    w = _mla_weights(p)
    x0_l, z_l, q_l, k_l, vt_l = _mla_in_proj(x, sh_l, sc_l, w, rotate=True)
    x0_c, z_c, q_c, k_c, vt_c = _mla_in_proj(ctx, sh_c, sc_c, w, rotate=False)
    k_all = jnp.concatenate([k_c, k_l], 2)
    vt_all = jnp.concatenate([vt_c, vt_l], 3)
    att_l = _attention(q_l, k_all, vt_all, tq=min(ATT_Q_TILE, n), tk=_key_tile(nc + n))
    att_c = _attention(q_c, k_c, vt_c, tq=nc, tk=_key_tile(nc))
    filt = (p['filt_w1'], p['filt_b1'], p['filt_freq1'], p['filt_w2'], p['filt_b2'], p['filt_freq2'], p['filt_w3'])
    hyo_l = _hyena_sequence(z_l, x0_l, filt, p['skip'])
    hyo_c = _hyena_sequence(z_c, x0_c, filt, p['skip'])
    return (hyo_l, att_l), (hyo_c, att_c)


def _halo_specs(n, tm, k):
    nb = n // SUBLANES
    per = tm // SUBLANES
    return [pl.BlockSpec((1, tm, k), lambda b, i: (b, i, 0)),
            pl.BlockSpec((1, SUBLANES, k), lambda b, i: (b, jnp.maximum(i * per - 1, 0), 0)),
            pl.BlockSpec((1, SUBLANES, k), lambda b, i: (b, jnp.minimum((i + 1) * per, nb - 1), 0))]


def _conv3(u, cw_ref, cb_ref, tm):
    i = pl.program_id(1)
    rows = lax.broadcasted_iota(I32, (u.shape[0], 1), 0)
    inside = jnp.logical_and(jnp.logical_or(rows >= SUBLANES, i > 0),
                             jnp.logical_or(rows < tm + SUBLANES, i < pl.num_programs(1) - 1))
    u = jnp.where(inside, u, 0.0)
    prev = pltpu.roll(u, 1, 0)[SUBLANES:SUBLANES + tm]
    nxt = pltpu.roll(u, u.shape[0] - 1, 0)[SUBLANES:SUBLANES + tm]
    return cw_ref[0:1, :] * prev + cw_ref[1:2, :] * u[SUBLANES:SUBLANES + tm] + cw_ref[2:3, :] * nxt + cb_ref[...]


def _ssd_in_kernel(xc_ref, xp_ref, xn_ref, sh_ref, sc_ref, wz_ref, wx_ref, wdt_ref, cw_ref, cb_ref, dtb_ref,
                   z_ref, xs_ref, b_ref, c_ref, dt_ref):
    tm = xc_ref.shape[1]
    rows = jnp.concatenate([xp_ref[0], xc_ref[0], xn_ref[0]], axis=0)
    h = (rows * (1.0 + sc_ref[0]) + sh_ref[0]).astype(BF16)
    hc = h[SUBLANES:SUBLANES + tm]
    z_ref[0] = jnp.dot(hc, wz_ref[...], preferred_element_type=F32)
    y = _conv3(jnp.dot(h, wx_ref[...], preferred_element_type=F32), cw_ref, cb_ref, tm)
    xbc = y * jax.nn.sigmoid(y)
    xs_ref[0] = xbc[:, :SSD_INNER]
    b_ref[0] = xbc[:, SSD_INNER:SSD_INNER + SSD_BC]
    c_ref[0] = xbc[:, SSD_INNER + SSD_BC:]
    dt = jnp.dot(hc, wdt_ref[...], preferred_element_type=F32) + dtb_ref[...]
    dt_ref[0] = (jnp.maximum(dt, 0.0) + jnp.log1p(jnp.exp(-jnp.abs(dt)))).T


def _ssd_in_proj(x, shift, scale, w_z, w_xbc, w_dt, conv_w, conv_b, dt_bias):
    bsz, n, k = x.shape
    tm = min(ROW_TILE, n)
    assert n % tm == 0
    full = lambda a: pl.BlockSpec(a.shape, lambda b, i: (0,) * a.ndim)
    row = lambda c: pl.BlockSpec((1, tm, c), lambda b, i: (b, i, 0))
    nh2 = w_dt.shape[1]
    consts = [w_z, w_xbc, w_dt, conv_w, conv_b, dt_bias]
    widths = [SSD_INNER, SSD_INNER, SSD_BC, SSD_BC]
    vmem = 2 * (tm * k * 4 + sum(a.size * a.dtype.itemsize for a in consts) + tm * (sum(widths) + nh2) * 4) \
        + 6 * (tm + 2 * SUBLANES) * SSD_XBC * 4
    return pl.pallas_call(
        _ssd_in_kernel, grid=(bsz, n // tm),
        in_specs=_halo_specs(n, tm, k) + [_bcast_spec(shift), _bcast_spec(scale)] + [full(a) for a in consts],
        out_specs=[row(c) for c in widths] + [pl.BlockSpec((1, nh2, tm), lambda b, i: (b, 0, i))],
        out_shape=[jax.ShapeDtypeStruct((bsz, n, c), F32) for c in widths]
        + [jax.ShapeDtypeStruct((bsz, nh2, n), F32)],
        compiler_params=_params(("parallel", "parallel"), vmem), name="ssd_in_proj",
    )(x, x, x, shift, scale, *consts)


def _ssd_chunk(x_ref, b_ref, c_ref, dt_ref, a_ref, st_sc, y_ref, reverse):
    q = SSD_CHUNK
    dt = dt_ref[0]
    a = dt * a_ref[0]
    si = lax.broadcasted_iota(I32, (q, q), 0)
    li = lax.broadcasted_iota(I32, (q, q), 1)
    incl = jnp.where((si >= li) if reverse else (si <= li), 1.0, 0.0)
    hp = lax.Precision.HIGHEST
    acs = jnp.dot(a, incl, precision=hp, preferred_element_type=F32)
    tot = jnp.dot(a, jnp.ones((q, LANES), F32), precision=hp, preferred_element_type=F32)
    e_in = jnp.exp(acs)
    w_end = jnp.exp(tot - acs) * dt
    e_tot = jnp.exp(tot)
    acs_t = acs.T
    e_in_t = e_in.T
    mask = (li >= si) if reverse else (li <= si)
    cmat = c_ref[0]
    cb = lax.dot_general(cmat.astype(BF16), b_ref[0].astype(BF16), (((1,), (1,)), ((), ())),
                         preferred_element_type=F32)
    bt = b_ref[0].T
    first = lax.broadcasted_iota(I32, (q, LANES), 1) < SSD_HEADDIM
    ys = []
    for pr in range(SSD_HPG // 2):
        lo, hi = pr * LANES, (pr + 1) * LANES
        x_pair = x_ref[0, :, lo:hi].astype(BF16)
        st_pair = st_sc[:, lo:hi]
        rhs = jnp.concatenate([x_pair, st_pair.astype(BF16)], axis=0)
        y2, s2 = [], []
        for r in (2 * pr, 2 * pr + 1):
            seg = acs_t[:, r:r + 1] - acs[r:r + 1, :]
            m = cb * jnp.exp(jnp.where(mask, seg, -jnp.inf)) * dt[r:r + 1, :]
            lhs = jnp.concatenate([m.astype(BF16), (cmat * e_in_t[:, r:r + 1]).astype(BF16)], axis=1)
            y2.append(jnp.dot(lhs, rhs, preferred_element_type=F32))
            btr = (bt * w_end[r:r + 1, :]).astype(BF16)
            s2.append(e_tot[r:r + 1, :] * st_pair + jnp.dot(btr, x_pair, preferred_element_type=F32))
        ys.append(jnp.where(first, y2[0], y2[1]))
        st_sc[:, lo:hi] = jnp.where(first, s2[0], s2[1])
    y_ref[0] = jnp.concatenate(ys, axis=1)


def _ssd_scan_kernel(xf_ref, bf_ref, cf_ref, dtf_ref, af_ref, s0f_ref, xr_ref, br_ref, cr_ref, dtr_ref, ar_ref,
                     s0r_ref, yf_ref, sf_ref, yr_ref, sr_ref, stf_sc, str_sc):
    ci = pl.program_id(2)

    @pl.when(ci == 0)
    def _():
        stf_sc[...] = s0f_ref[0, 0]
        str_sc[...] = s0r_ref[0, 0]

    _ssd_chunk(xf_ref, bf_ref, cf_ref, dtf_ref, af_ref, stf_sc, yf_ref, reverse=False)
    _ssd_chunk(xr_ref, br_ref, cr_ref, dtr_ref, ar_ref, str_sc, yr_ref, reverse=True)

    @pl.when(ci == pl.num_programs(2) - 1)
    def _():
        sf_ref[0, 0] = stf_sc[...]
        sr_ref[0, 0] = str_sc[...]


def _ssd_scan(xs, bm, cm, dt_t, a_rep, s0_f, s0_r):
    bsz, n, _ = xs.shape
    q = SSD_CHUNK
    nc = n // q
    assert n % q == 0 and q == LANES
    gw = SSD_HPG * SSD_HEADDIM
    state = pl.BlockSpec((1, 1, SSD_STATE, gw), lambda b, g, c: (b, g, 0, 0))

    def side(d):
        cc = (lambda c: nc - 1 - c) if d else (lambda c: c)
        specs = [pl.BlockSpec((1, q, gw), lambda b, g, c: (b, cc(c), g)),
                 pl.BlockSpec((1, q, SSD_STATE), lambda b, g, c: (b, cc(c), g)),
                 pl.BlockSpec((1, q, SSD_STATE), lambda b, g, c: (b, cc(c), g)),
                 pl.BlockSpec((1, SSD_HPG, q), lambda b, g, c: (b, d * SSD_GROUPS + g, cc(c))),
                 pl.BlockSpec((1, SSD_HPG, LANES), lambda b, g, c: (d * SSD_GROUPS + g, 0, 0)),
                 state]
        return specs, [pl.BlockSpec((1, q, gw), lambda b, g, c: (b, cc(c), g)), state]

    (in_f, out_f), (in_r, out_r) = side(0), side(1)
    shapes = [jax.ShapeDtypeStruct((bsz, n, SSD_INNER), F32),
              jax.ShapeDtypeStruct((bsz, SSD_GROUPS, SSD_STATE, gw), F32)]
    vmem = 4 * (2 * q * gw * 4 + 2 * q * SSD_STATE * 4 + 2 * SSD_STATE * gw * 4) + 2 * SSD_STATE * gw * 4 \
        + 128 * q * q * 4
    return pl.pallas_call(
        _ssd_scan_kernel, grid=(bsz, SSD_GROUPS, nc), in_specs=in_f + in_r, out_specs=out_f + out_r,
        out_shape=shapes + shapes,
        scratch_shapes=[pltpu.VMEM((SSD_STATE, gw), F32), pltpu.VMEM((SSD_STATE, gw), F32)],
        compiler_params=_params(("parallel", "parallel", "arbitrary"), vmem), name="ssd_scan",
    )(xs, bm, cm, dt_t, a_rep, s0_f, xs, bm, cm, dt_t, a_rep, s0_r)


def _ssd_out_kernel(x_ref, yf_ref, yb_ref, xs_ref, z_ref, d_ref, ng_ref, w_ref, *epilogue_refs):
    z = z_ref[0]
    y = (yf_ref[0] + yb_ref[0] + xs_ref[0] * d_ref[0]) * (z * jax.nn.sigmoid(z))
    gw = SSD_INNER // SSD_GROUPS
    parts = []
    for g in range(SSD_GROUPS):
        yg = y[:, g * gw:(g + 1) * gw]
        parts.append(yg * lax.rsqrt(jnp.mean(yg * yg, -1, keepdims=True) + RMS_EPS))
    yn = (jnp.concatenate(parts, axis=1) * ng_ref[0]).astype(BF16)
    _mix_epilogue(x_ref[0], jnp.dot(yn, w_ref[...], preferred_element_type=F32), *epilogue_refs)


def _ssd_out(x, y_f, y_b, xs, z, d_rep, norm_g, w_out, gate, ln_g, ln_b, shift2, scale2):
    bsz, n, d = x.shape
    tm = min(ROW_TILE, n)
    assert n % tm == 0
    row = lambda c: pl.BlockSpec((1, tm, c), lambda b, i: (b, i, 0))
    vecs = [gate, ln_g, ln_b, shift2, scale2]
    in_specs = [row(d)] + [row(SSD_INNER)] * 4 + [_bcast_spec(d_rep), _bcast_spec(norm_g),
                                                 pl.BlockSpec(w_out.shape, lambda b, i: (0, 0))]
    in_specs += [_bcast_spec(a) for a in vecs]
    vmem = 2 * (3 * tm * d * 4 + 4 * tm * SSD_INNER * 4 + w_out.size * 2) + 4 * tm * SSD_INNER * 4
    return pl.pallas_call(
        _ssd_out_kernel, grid=(bsz, n // tm), in_specs=in_specs, out_specs=[row(d), row(d), row(d // 2)],
        out_shape=[jax.ShapeDtypeStruct((bsz, n, d), F32)] * 2 + [jax.ShapeDtypeStruct((bsz, n, d // 2), U32)],
        compiler_params=_params(("parallel", "parallel"), vmem), name="ssd_out",
    )(x, y_f, y_b, xs, z, d_rep, norm_g, w_out, *vecs)


def _mixer_ssd(x, ctx, sh_l, sc_l, sh_c, sc_c, p):
    a_all = -jnp.exp(jnp.concatenate([p['a_log_f'], p['a_log_b']]))
    a_rep = jnp.broadcast_to(a_all.reshape(2 * SSD_GROUPS, SSD_HPG, 1), (2 * SSD_GROUPS, SSD_HPG, LANES))
    w_in = p['w_in'].astype(BF16)
    consts = (w_in[:, :SSD_INNER], w_in[:, SSD_INNER:SSD_INNER + SSD_XBC], w_in[:, SSD_INNER + SSD_XBC:],
              p['conv_w'], p['conv_b'][None], jnp.concatenate([p['dt_bias_f'], p['dt_bias_b']])[None])
    _, xc, bc, cc, dtc = _ssd_in_proj(ctx, sh_c, sc_c, *consts)
    zl, xl, bl, cl, dtl = _ssd_in_proj(x, sh_l, sc_l, *consts)
    s0 = jnp.zeros((ctx.shape[0], SSD_GROUPS, SSD_STATE, SSD_HPG * SSD_HEADDIM), F32)
    _, sc_f, _, sc_b = _ssd_scan(xc, bc, cc, dtc, a_rep, s0, s0)
    y_f, _, y_b, _ = _ssd_scan(xl, bl, cl, dtl, a_rep, sc_f, sc_b)
    return y_f, y_b, xl, zl


def kernel(x, c, ctx, c_ctx, mod_w, mod_b, ln_mix_g, ln_mix_b, ln_ffn_g, ln_ffn_b, a_w_in, hy_conv_w, hy_conv_b, hy_filt_w1, hy_filt_b1, hy_filt_freq1, hy_filt_w2, hy_filt_b2, hy_filt_freq2, hy_filt_w3, hy_skip, mla_q_norm, mla_w_qb, mla_kv_norm, mla_w_kvb, a_w_out, ssd_w_in, ssd_conv_w, ssd_conv_b, ssd_dt_bias_f, ssd_dt_bias_b, ssd_a_log_f, ssd_a_log_b, ssd_d, ssd_norm_g, ssd_w_out, router_w, router_bias, exp_w_gu, exp_w_down, sh_w_gu, sh_w_down):
    bsz, n_lat, d = x.shape
    n_ctx = ctx.shape[1]
    pad = -(bsz + 1) % SUBLANES
    cond = jnp.concatenate([c, c_ctx[None], jnp.zeros((pad, d), F32)], 0)
    for l in range(DEPTH):
        last = l == DEPTH - 1
        i = l // 2
        mods = _modulation(cond, mod_w, mod_b, l)
        mod = mods[:bsz].reshape(bsz, N_MOD, 1, d)
        mod_c = mods[bsz:bsz + 1].reshape(1, N_MOD, 1, d)
        sh1, sc1, g1, sh2, sc2, g2 = [mod[:, j] for j in range(N_MOD)]
        csh1, csc1, cg1, csh2, csc2, cg2 = [mod_c[:, j] for j in range(N_MOD)]
        vec = lambda a: a.reshape(1, 1, d)
        if l % 2 == 0:
            p = {"w_in": a_w_in[i], "conv_w": hy_conv_w[i], "conv_b": hy_conv_b[i],
                 "filt_w1": hy_filt_w1[i], "filt_b1": hy_filt_b1[i], "filt_freq1": hy_filt_freq1[i],
                 "filt_w2": hy_filt_w2[i], "filt_b2": hy_filt_b2[i], "filt_freq2": hy_filt_freq2[i],
                 "filt_w3": hy_filt_w3[i], "skip": hy_skip[i], "q_norm": mla_q_norm[i], "w_qb": mla_w_qb[i],
                 "kv_norm": mla_kv_norm[i], "w_kvb": mla_w_kvb[i]}
            ys_l, ys_c = _mixer_hyena_mla(x, ctx, sh1, sc1, csh1, csc1, p)
            w_out = a_w_out[i].astype(BF16)
            ws = [w_out[:HY_WIDTH], w_out[HY_WIDTH:]]
            x, ff_x, fp_x = _mix_out(x, ys_l, ws, g1, vec(ln_mix_g[l]), vec(ln_mix_b[l]), sh2, sc2)
        else:
            p = {"w_in": ssd_w_in[i], "conv_w": ssd_conv_w[i], "conv_b": ssd_conv_b[i],
                 "dt_bias_f": ssd_dt_bias_f[i], "dt_bias_b": ssd_dt_bias_b[i],
                 "a_log_f": ssd_a_log_f[i], "a_log_b": ssd_a_log_b[i]}
            assert last
            y_f, y_b, xs, z = _mixer_ssd(x, ctx, sh1, sc1, csh1, csc1, p)
            d_rep = jnp.repeat(ssd_d[i], SSD_HEADDIM).reshape(1, 1, SSD_INNER)
            x, ff_x, fp_x = _ssd_out(x, y_f, y_b, xs, z, d_rep, ssd_norm_g[i].reshape(1, 1, SSD_INNER),
                                     ssd_w_out[i].astype(BF16), g1, vec(ln_mix_g[l]), vec(ln_mix_b[l]), sh2, sc2)
        sh_gu = sh_w_gu[l].astype(BF16)
        sh_down = sh_w_down[l].astype(BF16)
        ln_g, ln_b = vec(ln_ffn_g[l]), vec(ln_ffn_b[l])
        moe_w = (router_w[l], router_bias[l], exp_w_gu, exp_w_down, l)
        if last:
            yb, pos, w = _moe_dispatch_experts(ff_x.reshape(-1, d), fp_x.reshape(-1, d // 2), *moe_w)
            x = _ffn_out(x, ff_x, yb, pos, w, 0, sh_gu, sh_down, g2, ln_g, ln_b)
        else:
            ctx, ff_c, fp_c = _mix_out(ctx, ys_c, ws, cg1, vec(ln_mix_g[l]), vec(ln_mix_b[l]), csh2, csc2)
            tokens = jnp.concatenate([ff_c.reshape(-1, d), ff_x.reshape(-1, d)], 0)
            packed = jnp.concatenate([fp_c.reshape(-1, d // 2), fp_x.reshape(-1, d // 2)], 0)
            yb, pos, w = _moe_dispatch_experts(tokens, packed, *moe_w)
            assert (bsz * n_ctx) % ROUTE_TOKENS == 0
            ctx = _ffn_out(ctx, ff_c, yb, pos, w, 0, sh_gu, sh_down, cg2, ln_g, ln_b)
            x = _ffn_out(x, ff_x, yb, pos, w, bsz * n_ctx // ROUTE_TOKENS, sh_gu, sh_down, g2, ln_g, ln_b)
    return x
```

```python
import functools
import math

import jax
import jax.numpy as jnp
from jax import lax
from jax.experimental import pallas as pl
from jax.experimental.pallas import tpu as pltpu

F32 = jnp.float32
BF16 = jnp.bfloat16
I32 = jnp.int32
U32 = jnp.uint32

D_MODEL = 1024
DEPTH = 2
GRID_W = 64
N_MOD = 6

HY_WIDTH = 512
HY_EMB = 33
HY_BANDS = (HY_EMB - 1) // 2
HY_TARGET = 1e-2
HY_FAST_DECAY = 0.3
HY_SLOW_DECAY = 1.5
HY_DECAY_MIN = math.log(HY_TARGET) / HY_SLOW_DECAY
HY_DECAY_MAX = math.log(HY_TARGET) / HY_FAST_DECAY

MLA_HEADS = 8
MLA_NOPE = 64
MLA_ROPE = 32
MLA_V = 64
MLA_Q_RANK = 256
MLA_KV_RANK = 128
MLA_QK = MLA_NOPE + MLA_ROPE
MLA_SCALE = MLA_QK ** -0.5
ROPE_THETA = 10000.0
LOG2E = math.log2(math.e)

OFF_Q = 3 * HY_WIDTH
OFF_KV = OFF_Q + MLA_Q_RANK
OFF_KPE = OFF_KV + MLA_KV_RANK

SSD_INNER = 2 * D_MODEL
SSD_HEADDIM = 64
SSD_HEADS = SSD_INNER // SSD_HEADDIM
SSD_GROUPS = 4
SSD_STATE = 128
SSD_CHUNK = 128
SSD_BC = SSD_GROUPS * SSD_STATE
SSD_XBC = SSD_INNER + 2 * SSD_BC
SSD_HPG = SSD_HEADS // SSD_GROUPS

N_EXPERTS = 256
TOP_K = 8
N_EXPERT_GROUPS = 8
TOPK_GROUPS = 4
EXPERT_DIM = 256
ROUTED_SCALE = 2.5

DN_ALPHA = (2 * DEPTH) ** 0.25
LN_EPS = 1e-5
RMS_EPS = 1e-6

LANES = 128
SUBLANES = 8
V7X_VMEM_CAP = 56 * 1024 * 1024

MOE_ROWS = 512
MOE_SUB_ROWS = 256
ROUTE_TOKENS = 256
ROW_TILE = 256
ATT_Q_TILE = 4096
ATT_HEADS_PER_STEP = 2
DMA_PRIORITIES = 2
ATT_ONES_ROWS = 16
MLA_SLAB = LANES
DFT_INNER = 128
DFT_COLS = 2048
HY_DIRECT_MAX = 512
FILT_ROWS = 512
MOD_COLS = 1024


def _params(semantics, vmem_bytes, **kw):
    limit = int(min(max(vmem_bytes * 5 // 4, 32 * 1024 * 1024), V7X_VMEM_CAP))
    return pltpu.CompilerParams(dimension_semantics=semantics, vmem_limit_bytes=limit, **kw)


def _bcast_spec(a):
    if a.shape[0] == 1:
        return pl.BlockSpec((1, 1, a.shape[2]), lambda b, i: (0, 0, 0))
    return pl.BlockSpec((1, 1, a.shape[2]), lambda b, i: (b, 0, 0))


def _layer_norm(r, g, b):
    mu = jnp.mean(r, -1, keepdims=True)
    c = r - mu
    var = jnp.mean(c * c, -1, keepdims=True)
    return c * lax.rsqrt(var + LN_EPS) * g + b


def _swiglu_rows(xb, w_gu, w_down):
    h = jnp.dot(xb, w_gu, preferred_element_type=F32)
    half = h.shape[1] // 2
    g, u = h[:, :half], h[:, half:]
    a = (g * jax.nn.sigmoid(g) * u).astype(BF16)
    return jnp.dot(a, w_down, preferred_element_type=F32)


def _mod_kernel(c_ref, w_ref, b_ref, o_ref):
    c = c_ref[...]
    o_ref[...] = jnp.dot(c * jax.nn.sigmoid(c), w_ref[0], precision=lax.Precision.HIGHEST,
                         preferred_element_type=F32) + b_ref[0]


def _modulation(cond, mod_w, mod_b, layer):
    r, d = cond.shape
    nout = mod_w.shape[2]
    assert nout % MOD_COLS == 0 and r % SUBLANES == 0
    return pl.pallas_call(
        _mod_kernel, grid=(nout // MOD_COLS,),
        in_specs=[pl.BlockSpec((r, d), lambda j: (0, 0)), pl.BlockSpec((1, d, MOD_COLS), lambda j: (layer, 0, j)),
                  pl.BlockSpec((1, 1, MOD_COLS), lambda j: (layer, 0, j))],
        out_specs=pl.BlockSpec((r, MOD_COLS), lambda j: (0, j)),
        out_shape=jax.ShapeDtypeStruct((r, nout), F32),
        compiler_params=_params(("parallel",), 2 * (d * MOD_COLS * 4 + 2 * r * MOD_COLS * 4)), name="modulation",
    )(cond, mod_w, mod_b.reshape(mod_b.shape[0], 1, nout))


def _pack_pairs(x):
    half = x.shape[1] // 2
    hi = pltpu.bitcast(x[:, :half].astype(BF16).astype(F32), U32)
    lo = pltpu.bitcast(x[:, half:].astype(BF16).astype(F32), U32)
    return hi | (lo >> 16)


def _unpack_pairs(u):
    hi = pltpu.bitcast(u & jnp.uint32(0xFFFF0000), F32)
    lo = pltpu.bitcast(u << 16, F32)
    return jnp.concatenate([hi, lo], axis=1)


def _mix_epilogue(x, mix, gate_ref, g_ref, b_ref, sh_ref, sc_ref, xo_ref, ff_ref, ffp_ref):
    xn = _layer_norm(DN_ALPHA * x + gate_ref[0] * mix, g_ref[0], b_ref[0])
    xo_ref[0] = xn
    ff = xn * (1.0 + sc_ref[0]) + sh_ref[0]
    ff_ref[0] = ff
    ffp_ref[0] = _pack_pairs(ff)


def _mix_out_kernel(*refs, n_y):
    x_ref = refs[0]
    y_refs = refs[1:1 + n_y]
    w_refs = refs[1 + n_y:1 + 2 * n_y]
    y = None
    for y_ref, w_ref in zip(y_refs, w_refs):
        t = jnp.dot(y_ref[0].astype(BF16), w_ref[...], preferred_element_type=F32)
        y = t if y is None else y + t
    _mix_epilogue(x_ref[0], y, *refs[1 + 2 * n_y:])


def _mix_out(x, ys, ws, gate, ln_g, ln_b, shift2, scale2):
    bsz, n, d = x.shape
    tm = min(ROW_TILE, n)
    assert n % tm == 0
    row = lambda c: pl.BlockSpec((1, tm, c), lambda b, i: (b, i, 0))
    vecs = [gate, ln_g, ln_b, shift2, scale2]
    in_specs = [row(d)] + [row(y.shape[2]) for y in ys]
    in_specs += [pl.BlockSpec(w.shape, lambda b, i: (0, 0)) for w in ws]
    in_specs += [_bcast_spec(a) for a in vecs]
    vmem = 2 * (3 * tm * d * 4 + sum(tm * y.shape[2] * 4 + w.size * 2 for y, w in zip(ys, ws)))
    return pl.pallas_call(
        functools.partial(_mix_out_kernel, n_y=len(ys)),
        grid=(bsz, n // tm), in_specs=in_specs, out_specs=[row(d), row(d), row(d // 2)],
        out_shape=[jax.ShapeDtypeStruct((bsz, n, d), F32)] * 2 + [jax.ShapeDtypeStruct((bsz, n, d // 2), U32)],
        compiler_params=_params(("parallel", "parallel"), vmem), name="mix_out",
    )(x, *ys, *ws, *vecs)


def _attn_kernel(q_ref, k_ref, vt_ref, o_ref, *, tk, hp):
    nk = k_ref.shape[2]
    tq = q_ref.shape[2]
    dva = vt_ref.shape[2]
    dv = dva - ATT_ONES_ROWS

    def body(j, carry):
        off = pl.multiple_of(j * tk, tk)
        new = []
        for h in range(hp):
            m_prev, acc = carry[h]
            st = lax.dot_general(k_ref[0, h, pl.ds(off, tk), :], q_ref[0, h], (((1,), (1,)), ((), ())),
                                 preferred_element_type=F32)
            m_new = jnp.maximum(m_prev, jnp.max(st, 0, keepdims=True))
            p = jnp.exp2(st - m_new).astype(BF16)
            alpha = jnp.exp2(m_prev - m_new)
            acc = alpha * acc + jnp.dot(vt_ref[0, h, :, pl.ds(off, tk)], p, preferred_element_type=F32)
            new.append((m_new, acc))
        return tuple(new)

    init = tuple((jnp.full((1, tq), -jnp.inf, F32), jnp.zeros((dva, tq), F32)) for _ in range(hp))
    fin = lax.fori_loop(0, nk // tk, body, init)
    outs = [acc[:dv] / acc[dv:dv + 1] for _, acc in fin]
    o_ref[0] = jnp.concatenate(outs, 0).T


def _attention(q, k, vt, tq, tk):
    bsz, h, nq, dk = q.shape
    nk, dva = k.shape[2], vt.shape[2]
    dv = dva - ATT_ONES_ROWS
    hp = ATT_HEADS_PER_STEP
    assert nq % tq == 0 and nk % tk == 0 and h % hp == 0
    vmem = 2 * hp * (tq * LANES * 2 + nk * LANES * 2 + dva * nk * 2) + 2 * tq * hp * dv * 4 + 6 * hp * tk * tq * 4
    return pl.pallas_call(
        functools.partial(_attn_kernel, tk=tk, hp=hp), grid=(bsz, h // hp, nq // tq),
        in_specs=[pl.BlockSpec((1, hp, tq, dk), lambda b, g, i: (b, g, i, 0)),
                  pl.BlockSpec((1, hp, nk, dk), lambda b, g, i: (b, g, 0, 0)),
                  pl.BlockSpec((1, hp, dva, nk), lambda b, g, i: (b, g, 0, 0))],
        out_specs=pl.BlockSpec((1, tq, hp * dv), lambda b, g, i: (b, i, g)),
        out_shape=jax.ShapeDtypeStruct((bsz, nq, h * dv), F32),
        compiler_params=_params(("parallel", "parallel", "arbitrary"), vmem), name="mla_attention",
    )(q, k, vt)


def _key_tile(nk):
    for t in (768, 512, 384, 256, 128):
        if nk % t == 0:
            return t
    return nk


def _router_kernel(x_ref, wt_ref, bias_ref, upper_ref, idx_ref, w_ref, rank_ref, cnt_ref, run_sc):
    i = pl.program_id(0)
    tm = x_ref.shape[0]

    @pl.when(i == 0)
    def _():
        run_sc[...] = jnp.zeros(run_sc.shape, F32)

    logits = lax.dot_general(wt_ref[...], x_ref[...], (((1,), (1,)), ((), ())),
                             precision=lax.Precision.HIGHEST, preferred_element_type=F32)
    sc = jax.nn.sigmoid(logits)
    ch = sc + bias_ref[:, :1]
    neg = -jnp.inf
    chg = ch.reshape(N_EXPERT_GROUPS, N_EXPERTS // N_EXPERT_GROUPS, tm)
    m1 = jnp.max(chg, axis=1)
    eq = chg == m1[:, None, :]
    cnt = jnp.sum(eq.astype(F32), axis=1)
    m2 = jnp.max(jnp.where(eq, neg, chg), axis=1)
    g2 = m1 + jnp.where(cnt >= 2.0, m1, m2)
    gi = lax.broadcasted_iota(I32, g2.shape, 0)
    beaten = jnp.zeros(g2.shape, F32)
    for g in range(N_EXPERT_GROUPS):
        row = g2[g:g + 1, :]
        beaten = beaten + jnp.where(row > g2, 1.0, jnp.where(row == g2, jnp.where(gi > g, 1.0, 0.0), 0.0))
    keep = beaten < float(TOPK_GROUPS)
    cur = jnp.where(keep[:, None, :], chg, neg).reshape(N_EXPERTS, tm)
    eidx = lax.broadcasted_iota(I32, (N_EXPERTS, tm), 0)
    multi = jnp.zeros((N_EXPERTS, tm), F32)
    hits, idx_rows, w_rows = [], [], []
    for _ in range(TOP_K):
        m = jnp.max(cur, axis=0, keepdims=True)
        sel = jnp.min(jnp.where(cur == m, eidx, N_EXPERTS), axis=0, keepdims=True)
        hit = eidx == sel
        idx_rows.append(sel)
        w_rows.append(jnp.sum(jnp.where(hit, sc, 0.0), axis=0, keepdims=True))
        cur = jnp.where(hit, neg, cur)
        multi = multi + jnp.where(hit, 1.0, 0.0)
        hits.append(hit)
    base = jnp.concatenate([run_sc[...]] * (tm // LANES), axis=1)
    before = jnp.dot(multi.astype(BF16), upper_ref[...], preferred_element_type=F32) + base
    rank_rows = [jnp.sum(jnp.where(hit, before, 0.0), axis=0, keepdims=True) for hit in hits]
    w = jnp.concatenate(w_rows, axis=0)
    idx_ref[...] = jnp.concatenate(idx_rows, axis=0)
    w_ref[...] = (w / jnp.sum(w, axis=0, keepdims=True) * ROUTED_SCALE).T
    rank_ref[...] = jnp.concatenate(rank_rows, axis=0).astype(I32)
    run_sc[...] = run_sc[...] + jnp.dot(multi.astype(BF16), jnp.ones((tm, LANES), BF16), preferred_element_type=F32)
    cnt_ref[...] = run_sc[...]


def _router(tokens, router_w, router_bias):
    t, d = tokens.shape
    tm = ROUTE_TOKENS
    assert t % tm == 0
    wt = router_w.T
    bias = jnp.broadcast_to(router_bias.astype(F32)[:, None], (N_EXPERTS, LANES))
    r = jnp.arange(tm)
    upper = (r[:, None] < r[None, :]).astype(BF16)
    col = pl.BlockSpec((TOP_K, tm), lambda i: (0, i))
    full = lambda a: pl.BlockSpec(a.shape, lambda i: (0,) * a.ndim)
    vmem = 2 * (tm * d * 4 + wt.size * 4) + 40 * N_EXPERTS * tm * 4
    return pl.pallas_call(
        _router_kernel, grid=(t // tm,),
        in_specs=[pl.BlockSpec((tm, d), lambda i: (i, 0)), full(wt), full(bias), full(upper)],
        out_specs=[col, pl.BlockSpec((tm, TOP_K), lambda i: (i, 0)), col,
                   pl.BlockSpec((N_EXPERTS, LANES), lambda i: (0, 0))],
        out_shape=[jax.ShapeDtypeStruct((TOP_K, t), I32), jax.ShapeDtypeStruct((t, TOP_K), F32),
                   jax.ShapeDtypeStruct((TOP_K, t), I32), jax.ShapeDtypeStruct((N_EXPERTS, LANES), F32)],
        scratch_shapes=[pltpu.VMEM((N_EXPERTS, LANES), F32)],
        compiler_params=_params(("arbitrary",), vmem), name="moe_router",
    )(tokens, wt, bias, upper)


def _positions_kernel(idx_ref, rank_ref, start_ref, pos_ref):
    tm = idx_ref.shape[1]
    eidx = lax.broadcasted_iota(I32, (N_EXPERTS, tm), 0)
    start = jnp.concatenate([start_ref[...]] * (tm // LANES), axis=1)
    rows = [jnp.sum(jnp.where(eidx == idx_ref[k:k + 1, :], start, 0), axis=0, keepdims=True) for k in range(TOP_K)]
    pos_ref[...] = jnp.concatenate(rows, axis=0) + rank_ref[...]


def _positions(idx, rank, pad_start):
    t = idx.shape[1]
    tm = ROUTE_TOKENS
    start = jnp.broadcast_to(pad_start.astype(I32)[:, None], (N_EXPERTS, LANES))
    col = pl.BlockSpec((TOP_K, tm), lambda i: (0, i))
    return pl.pallas_call(
        _positions_kernel, grid=(t // tm,),
        in_specs=[col, col, pl.BlockSpec((N_EXPERTS, LANES), lambda i: (0, 0))], out_specs=col,
        out_shape=jax.ShapeDtypeStruct((TOP_K, t), I32),
        compiler_params=_params(("parallel",), 8 * N_EXPERTS * tm * 4), name="moe_positions",
    )(idx, rank, start)


def _moe_plan(counts, t):
    n_blocks = -(-(t * TOP_K) // MOE_ROWS) + N_EXPERTS
    c = counts.astype(I32)
    padded = (c + MOE_ROWS - 1) // MOE_ROWS * MOE_ROWS
    pad_end = jnp.cumsum(padded)
    pad_start = pad_end - padded
    block_e = jnp.minimum(jnp.searchsorted(pad_end, jnp.arange(n_blocks, dtype=I32) * MOE_ROWS, side='right'),
                          N_EXPERTS - 1).astype(I32)
    n_valid = (pad_end[-1:] // MOE_ROWS).astype(I32)
    return pad_start, block_e, n_valid, n_blocks * MOE_ROWS


def _row_copy(src, dst, sem):
    return pltpu.make_async_copy(src, dst, sem)


def _dispatch_kernel(pos_ref, x_ref, buf_in_ref, buf_ref, sem):
    del buf_in_ref
    tm = x_ref.shape[0]

    def issue(g, carry):
        base = pl.multiple_of(g * SUBLANES, SUBLANES)
        for j in range(SUBLANES):
            for k in range(TOP_K):
                _row_copy(x_ref.at[pl.ds(base + j, 1)], buf_ref.at[pl.ds(pos_ref[k, base + j], 1)],
                          sem).start(priority=k % DMA_PRIORITIES)
        return carry

    lax.fori_loop(0, tm // SUBLANES, issue, 0)
    for k in range(TOP_K):
        _row_copy(x_ref, buf_ref.at[pl.ds(0, tm)], sem).wait()


def _dispatch(pos, tokens, cap):
    t, d = tokens.shape
    tm = ROUTE_TOKENS
    zeros = jnp.zeros((cap, d), tokens.dtype)
    return pl.pallas_call(
        _dispatch_kernel, grid=(t // tm,),
        in_specs=[pl.BlockSpec((TOP_K, tm), lambda i: (0, i), memory_space=pltpu.SMEM),
                  pl.BlockSpec((tm, d), lambda i: (i, 0)),
                  pl.BlockSpec(memory_space=pl.ANY)],
        out_specs=pl.BlockSpec(memory_space=pl.ANY),
        out_shape=jax.ShapeDtypeStruct((cap, d), tokens.dtype),
        scratch_shapes=[pltpu.SemaphoreType.DMA(())],
        input_output_aliases={2: 0},
        compiler_params=_params(("arbitrary",), 2 * tm * d * 4, has_side_effects=True),
        name="moe_dispatch",
    )(pos, tokens, zeros)


def _experts_kernel(be_ref, nv_ref, x_ref, wgu_ref, wdn_ref, o_ref, wgu_sc, wdn_sc):
    i = pl.program_id(0)

    @pl.when(i < nv_ref[0])
    def _():
        @pl.when(jnp.logical_or(i == 0, be_ref[i] != be_ref[jnp.maximum(i - 1, 0)]))
        def _():
            wgu_sc[...] = wgu_ref[0, 0].astype(BF16)
            wdn_sc[...] = wdn_ref[0, 0].astype(BF16)

        for part in range(MOE_ROWS // MOE_SUB_ROWS):
            rows = pl.ds(part * MOE_SUB_ROWS, MOE_SUB_ROWS)
            x = _unpack_pairs(x_ref[rows, :]).astype(BF16)
            o_ref[rows, :] = _pack_pairs(_swiglu_rows(x, wgu_sc[...], wdn_sc[...]))

    @pl.when(i >= nv_ref[0])
    def _():
        o_ref[...] = jnp.zeros(o_ref.shape, U32)


def _moe_experts(xb, block_e, n_valid, w_gu, w_down, layer):
    cap, dp = xb.shape
    d = 2 * dp
    gu = w_gu.shape[3]
    ed = w_down.shape[2]
    live = lambda i, be, nv: jnp.maximum(jnp.minimum(i, nv[0] - 1), 0)
    grid_spec = pltpu.PrefetchScalarGridSpec(
        num_scalar_prefetch=2, grid=(cap // MOE_ROWS,),
        in_specs=[pl.BlockSpec((MOE_ROWS, dp), lambda i, be, nv: (live(i, be, nv), 0)),
                  pl.BlockSpec((1, 1, d, gu), lambda i, be, nv: (layer, be[i], 0, 0)),
                  pl.BlockSpec((1, 1, ed, d), lambda i, be, nv: (layer, be[i], 0, 0))],
        out_specs=pl.BlockSpec((MOE_ROWS, dp), lambda i, be, nv: (i, 0)),
        scratch_shapes=[pltpu.VMEM((d, gu), BF16), pltpu.VMEM((ed, d), BF16)])
    vmem = 2 * (2 * MOE_ROWS * dp * 4 + d * gu * 4 + ed * d * 4) + (d * gu + ed * d) * 2 + 6 * MOE_ROWS * d * 4
    return pl.pallas_call(
        _experts_kernel, grid_spec=grid_spec, out_shape=jax.ShapeDtypeStruct((cap, dp), U32),
        compiler_params=_params(("arbitrary",), vmem), name="moe_experts",
    )(block_e, n_valid, xb, w_gu, w_down)


def _ffn_out_kernel(pos_ref, w_ref, x_ref, ff_ref, yb_ref, wgu_ref, wdn_ref, gate_ref, g_ref, b_ref, xo_ref,
                    rows_sc, sem):
    tm = x_ref.shape[1]

    def issue(g, carry):
        base = pl.multiple_of(g * SUBLANES, SUBLANES)
        for j in range(SUBLANES):
            for k in range(TOP_K):
                _row_copy(yb_ref.at[pl.ds(pos_ref[k, base + j], 1)], rows_sc.at[k, pl.ds(base + j, 1)],
                          sem).start(priority=k % DMA_PRIORITIES)
        return carry

    lax.fori_loop(0, tm // SUBLANES, issue, 0)
    out = _swiglu_rows(ff_ref[0].astype(BF16), wgu_ref[...], wdn_ref[...])
    for k in range(TOP_K):
        _row_copy(yb_ref.at[pl.ds(0, tm)], rows_sc.at[k], sem).wait()
    for k in range(TOP_K):
        out = out + w_ref[:, k:k + 1] * _unpack_pairs(rows_sc[k])
    xo_ref[0] = _layer_norm(DN_ALPHA * x_ref[0] + gate_ref[0] * out, g_ref[0], b_ref[0])


def _ffn_out(x, ff, yb, pos, w, tile0, sh_gu, sh_down, gate, ln_g, ln_b):
    bsz, n, d = x.shape
    tm = min(ROUTE_TOKENS, n)
    nt = n // tm
    row = pl.BlockSpec((1, tm, d), lambda b, i: (b, i, 0))
    vecs = [gate, ln_g, ln_b]
    in_specs = [pl.BlockSpec((TOP_K, tm), lambda b, i: (0, tile0 + b * nt + i), memory_space=pltpu.SMEM),
                pl.BlockSpec((tm, TOP_K), lambda b, i: (tile0 + b * nt + i, 0)),
                row, row, pl.BlockSpec(memory_space=pl.ANY),
                pl.BlockSpec(sh_gu.shape, lambda b, i: (0, 0)), pl.BlockSpec(sh_down.shape, lambda b, i: (0, 0))]
    in_specs += [_bcast_spec(a) for a in vecs]
    vmem = TOP_K * tm * d * 2 + 2 * (3 * tm * d * 4 + sh_gu.size * 2 + sh_down.size * 2) + 6 * tm * d * 4
    return pl.pallas_call(
        _ffn_out_kernel, grid=(bsz, nt), in_specs=in_specs, out_specs=row,
        out_shape=jax.ShapeDtypeStruct((bsz, n, d), F32),
        scratch_shapes=[pltpu.VMEM((TOP_K, tm, d // 2), U32), pltpu.SemaphoreType.DMA(())],
        compiler_params=_params(("arbitrary", "arbitrary"), vmem), name="moe_combine_ffn_out",
    )(pos, w, x, ff, yb, sh_gu, sh_down, *vecs)


def _moe_dispatch_experts(tokens, packed, router_w, router_bias, w_gu, w_down, layer):
    t = tokens.shape[0]
    idx, w, rank, counts = _router(tokens, router_w, router_bias)
    pad_start, block_e, n_valid, cap = _moe_plan(counts[:, 0], t)
    pos = _positions(idx, rank, pad_start)
    xb = _dispatch(pos, packed, cap)
    return _moe_experts(xb, block_e, n_valid, w_gu, w_down, layer), pos, w


def _rope_tables(n):
    rows = n // GRID_W
    row = jnp.repeat(jnp.arange(rows), GRID_W).astype(F32)
    col = jnp.tile(jnp.arange(GRID_W), rows).astype(F32)
    half = MLA_ROPE // 2
    inv = ROPE_THETA ** (-jnp.arange(0, half, 2, dtype=F32) / half)
    ang = jnp.concatenate([row[:, None] * inv, col[:, None] * inv], -1)
    return jnp.cos(ang), jnp.sin(ang)


def _filter_kernel(fr_ref, w1_ref, b1_ref, f1_ref, w2_ref, b2_ref, f2_ref, w3_ref, dl_ref, k_ref, l1_ref, *, n):
    i = pl.program_id(0)
    tr = k_ref.shape[0]
    hp = lax.Precision.HIGHEST
    tap = i * tr + lax.broadcasted_iota(I32, (tr, 1), 0)
    lag = jnp.where(tap < n, tap, 2 * n - tap).astype(F32)
    t = lag * (1.0 / (n - 1))
    ang = (2.0 * math.pi / n) * lag * fr_ref[...]
    lane = lax.broadcasted_iota(I32, (tr, LANES), 1)
    z = jnp.where(lane == 0, t, jnp.where(lane <= HY_BANDS, jnp.cos(ang),
                                          jnp.where(lane <= 2 * HY_BANDS, -jnp.sin(ang), 0.0)))
    h = jnp.sin(f1_ref[...] * (jnp.dot(z, w1_ref[...], precision=hp, preferred_element_type=F32) + b1_ref[...]))
    h = jnp.sin(f2_ref[...] * (jnp.dot(h, w2_ref[...], precision=hp, preferred_element_type=F32) + b2_ref[...]))
    h = jnp.dot(h, w3_ref[...], precision=hp, preferred_element_type=F32)
    hsel = jnp.where(tap < n, h[:, :HY_WIDTH], h[:, HY_WIDTH:])
    k = jnp.where(tap == n, 0.0, hsel * jnp.exp(-t * dl_ref[...]))
    k_ref[...] = k

    @pl.when(i == 0)
    def _():
        l1_ref[...] = jnp.zeros(l1_ref.shape, F32)

    l1_ref[...] = l1_ref[...] + jnp.sum(jnp.abs(k), axis=0, keepdims=True)


def _hyena_filter(n, w1, b1, f1, w2, b2, f2, w3):
    nn = 2 * n
    tr = min(FILT_ROWS, nn)
    assert nn % tr == 0
    emb, ffn = w1.shape
    fr = jnp.linspace(1e-4, HY_BANDS - 1, HY_BANDS, dtype=F32)
    fr_l = jnp.concatenate([jnp.zeros((1,), F32), fr, fr, jnp.zeros((LANES - emb,), F32)])[None]
    w1p = jnp.concatenate([w1, jnp.zeros((LANES - emb, ffn), F32)], 0)
    deltas = jnp.abs(jnp.linspace(HY_DECAY_MIN, HY_DECAY_MAX, HY_WIDTH, dtype=F32))[None]
    ops = [fr_l, w1p, b1[None], f1[None], w2, b2[None], f2[None], w3, deltas]
    full = lambda a: pl.BlockSpec(a.shape, lambda i: (0,) * a.ndim)
    return pl.pallas_call(
        functools.partial(_filter_kernel, n=n), grid=(nn // tr,), in_specs=[full(a) for a in ops],
        out_specs=[pl.BlockSpec((tr, HY_WIDTH), lambda i: (i, 0)), pl.BlockSpec((1, HY_WIDTH), lambda i: (0, 0))],
        out_shape=[jax.ShapeDtypeStruct((nn, HY_WIDTH), F32), jax.ShapeDtypeStruct((1, HY_WIDTH), F32)],
        compiler_params=_params(("arbitrary",), 16 * tr * 2 * HY_WIDTH * 4), name="hyena_filter",
    )(*ops)


def _split(a):
    hi = a.astype(BF16)
    return hi, (a - hi.astype(F32)).astype(BF16)


def _dot3(a_hi, a_lo, x):
    x_hi, x_lo = _split(x)
    return (jnp.dot(a_hi, x_hi, preferred_element_type=F32) + jnp.dot(a_lo, x_hi, preferred_element_type=F32)
            + jnp.dot(a_hi, x_lo, preferred_element_type=F32))


def _cis(num, den):
    ang = (2.0 * math.pi / den) * (num % den).astype(F32)
    return jnp.cos(ang), jnp.sin(ang)


def _dft_outer_kernel(f_hi_ref, f_lo_ref, x_ref, o_ref):
    o_ref[0] = _dot3(f_hi_ref[...], f_lo_ref[...], x_ref[0])


def _dft_outer(x, f_hi, f_lo):
    bsz, k, m = x.shape
    r = f_hi.shape[0]
    tn = min(DFT_COLS, m)
    assert m % tn == 0
    vmem = 2 * (2 * f_hi.size * 2 + k * tn * 4 + r * tn * 4) + 3 * (k + r) * tn * 4
    return pl.pallas_call(
        _dft_outer_kernel, grid=(bsz, m // tn),
        in_specs=[pl.BlockSpec(f_hi.shape, lambda b, j: (0, 0)), pl.BlockSpec(f_lo.shape, lambda b, j: (0, 0)),
                  pl.BlockSpec((1, k, tn), lambda b, j: (b, 0, j))],
        out_specs=pl.BlockSpec((1, r, tn), lambda b, j: (b, 0, j)),
        out_shape=jax.ShapeDtypeStruct((bsz, r, m), F32),
        compiler_params=_params(("parallel", "parallel"), vmem), name="hyena_dft_outer",
    )(f_hi, f_lo, x)


def _dft_inner_kernel(m_hi_ref, m_lo_ref, mt_hi_ref, mt_lo_ref, a_ref, h_ref, o_ref, *, conv):
    n2 = a_ref.shape[3]
    x = a_ref[0, :, 0].reshape(2 * n2, a_ref.shape[4])
    y = _dot3(m_hi_ref[0], m_lo_ref[0], x)
    if conv:
        h = h_ref[0, :, 0].reshape(2 * n2, h_ref.shape[4])
        yr, yi, hr, hi = y[:n2], y[n2:], h[:n2], h[n2:]
        prod = jnp.concatenate([yr * hr - yi * hi, yr * hi + yi * hr], axis=0)
        y = _dot3(mt_hi_ref[0], mt_lo_ref[0], prod)
    o_ref[0, :, 0] = y.reshape(2, n2, y.shape[1])


def _dft_inner(a, h, mats, conv):
    bsz, _, n1, n2, c = a.shape
    blk = lambda sel: pl.BlockSpec((1, 2, 1, n2, c), sel)
    mat = pl.BlockSpec((1, 2 * n2, 2 * n2), lambda k, b: (k, 0, 0))
    vmem = 2 * (4 * 4 * n2 * n2 * 2 + 3 * 2 * n2 * c * 4) + 8 * 2 * n2 * c * 4
    return pl.pallas_call(
        functools.partial(_dft_inner_kernel, conv=conv), grid=(n1, bsz),
        in_specs=[mat, mat, mat, mat, blk(lambda k, b: (b, 0, k, 0, 0)), blk(lambda k, b: (0, 0, k, 0, 0))],
        out_specs=blk(lambda k, b: (b, 0, k, 0, 0)), out_shape=jax.ShapeDtypeStruct(a.shape, F32),
        compiler_params=_params(("parallel", "arbitrary"), vmem),
        name="hyena_dft_inner_conv" if conv else "hyena_dft_inner",
    )(*mats, a, h)


def _dft_final_kernel(fd_hi_ref, fd_lo_ref, a_ref, z_ref, x0_ref, skip_ref, l1_ref, o_ref):
    y = _dot3(fd_hi_ref[...], fd_lo_ref[...], a_ref[0])
    o_ref[0] = (y / l1_ref[...] + z_ref[0] * skip_ref[...]) * x0_ref[0]


def _dft_final(a, z, x0, skip_t, l1_t, fd_hi, fd_lo):
    bsz, r2, m = a.shape
    k = fd_hi.shape[0]
    tn = min(DFT_COLS, m)
    row = pl.BlockSpec((1, k, tn), lambda b, j: (b, 0, j))
    vmem = 2 * (2 * fd_hi.size * 2 + r2 * tn * 4 + 3 * k * tn * 4) + 3 * (k + r2) * tn * 4
    return pl.pallas_call(
        _dft_final_kernel, grid=(bsz, m // tn),
        in_specs=[pl.BlockSpec(fd_hi.shape, lambda b, j: (0, 0)), pl.BlockSpec(fd_lo.shape, lambda b, j: (0, 0)),
                  pl.BlockSpec((1, r2, tn), lambda b, j: (b, 0, j)), row, row,
                  pl.BlockSpec((1, tn), lambda b, j: (0, 0)), pl.BlockSpec((1, tn), lambda b, j: (0, 0))],
        out_specs=row, out_shape=jax.ShapeDtypeStruct((bsz, k, m), F32),
        compiler_params=_params(("parallel", "parallel"), vmem), name="hyena_dft_final",
    )(fd_hi, fd_lo, a, z, x0, skip_t, l1_t)


def _hyena_long_conv(z, x0, k, l1, skip):
    bsz, n, c = z.shape
    n2 = DFT_INNER
    nn = 2 * n
    n1 = nn // n2
    assert nn == n1 * n2 and n1 % 2 == 0
    half = n1 // 2
    m = n2 * c
    j1 = jnp.arange(n1)
    ca, sa = _cis(j1[:, None] * j1[None, :], n1)
    fa = jnp.concatenate([ca, -sa], axis=0)
    fd = jnp.concatenate([ca, -sa], axis=1)[:half] / nn
    j2 = jnp.arange(n2)
    cb, sb = _cis(j2[None, None, :] * (n1 * j2[None, :, None] + j1[:, None, None]), nn)
    mb = jnp.concatenate([jnp.concatenate([cb, sb], 2), jnp.concatenate([-sb, cb], 2)], 1)
    mats = _split(mb) + _split(jnp.swapaxes(mb, 1, 2))
    fa_hi, fa_lo = _split(fa)
    hk = _dft_outer(k.reshape(1, n1, m), fa_hi, fa_lo).reshape(1, 2, n1, n2, c)
    hk = _dft_inner(hk, hk, mats, conv=False)
    a = _dft_outer(z.reshape(bsz, half, m), fa_hi[:, :half], fa_lo[:, :half]).reshape(bsz, 2, n1, n2, c)
    a = _dft_inner(a, hk, mats, conv=True).reshape(bsz, 2 * n1, m)
    reps = (1, min(DFT_COLS, m) // c)
    out = _dft_final(a, z.reshape(bsz, half, m), x0.reshape(bsz, half, m), jnp.tile(skip.reshape(1, c), reps),
                     jnp.tile(l1, reps), *_split(fd))
    return out.reshape(bsz, n, c)


def _short_conv_kernel(f_hi_ref, f_lo_ref, fi_hi_ref, fi_lo_ref, z_ref, x0_ref, k_ref, skip_ref, l1_ref, o_ref):
    n = z_ref.shape[1]
    nn = 2 * n
    z = z_ref[0]
    hk = _dot3(f_hi_ref[...], f_lo_ref[...], k_ref[...])
    zs = _dot3(f_hi_ref[:, :n], f_lo_ref[:, :n], z)
    zr, zi, hr, hi = zs[:nn], zs[nn:], hk[:nn], hk[nn:]
    prod = jnp.concatenate([zr * hr - zi * hi, zr * hi + zi * hr], axis=0)
    o_ref[0] = (_dot3(fi_hi_ref[...], fi_lo_ref[...], prod) / l1_ref[...] + z * skip_ref[...]) * x0_ref[0]


def _hyena_short_conv(z, x0, k, l1, skip):
    bsz, n, c = z.shape
    nn = 2 * n
    idx = jnp.arange(nn)
    cf, sf = _cis(idx[:, None] * idx[None, :], nn)
    f = jnp.concatenate([cf, -sf], axis=0)
    fi = jnp.concatenate([cf, -sf], axis=1)[:n] / nn
    full = lambda a: pl.BlockSpec(a.shape, lambda b: (0,) * a.ndim)
    row = pl.BlockSpec((1, n, c), lambda b: (b, 0, 0))
    ops = _split(f) + _split(fi)
    vmem = 2 * (sum(a.size * 2 for a in ops) + 3 * n * c * 4 + nn * c * 4) + 12 * 2 * nn * c * 4
    return pl.pallas_call(
        _short_conv_kernel, grid=(bsz,),
        in_specs=[full(a) for a in ops] + [row, row, full(k), pl.BlockSpec((1, c), lambda b: (0, 0)),
                                           pl.BlockSpec((1, c), lambda b: (0, 0))],
        out_specs=row, out_shape=jax.ShapeDtypeStruct((bsz, n, c), F32),
        compiler_params=_params(("parallel",), vmem), name="hyena_short_conv",
    )(*ops, z, x0, k, skip.reshape(1, c), l1)


def _hyena_sequence(z, x0, filt, skip):
    n = z.shape[1]
    k, l1 = _hyena_filter(n, *filt)
    conv = _hyena_short_conv if n <= HY_DIRECT_MAX else _hyena_long_conv
    return conv(z, x0, k, l1, skip)


def _rms(x, g):
    return x * lax.rsqrt(jnp.mean(x * x, -1, keepdims=True) + RMS_EPS) * g


def _mla_in_kernel(xc_ref, xp_ref, xn_ref, sh_ref, sc_ref, why_ref, cw_ref, cb_ref, wql_ref, qg_ref, wqa_ref, wqb_ref,
                   wkvl_ref, kvg_ref, wk_ref, wv_ref, wpa_ref, wpb_ref, qa_ref, qb_ref, ka_ref, kb_ref,
                   x0_ref, z_ref, q_ref, k_ref, vt_ref):
    tm = xc_ref.shape[1]
    rows = jnp.concatenate([xp_ref[0], xc_ref[0], xn_ref[0]], axis=0)
    h = (rows * (1.0 + sc_ref[0]) + sh_ref[0]).astype(BF16)
    hc = h[SUBLANES:SUBLANES + tm]
    y = _conv3(jnp.dot(h, why_ref[...], preferred_element_type=F32), cw_ref, cb_ref, tm)
    x0_ref[0] = y[:, :HY_WIDTH]
    z_ref[0] = y[:, 2 * HY_WIDTH:] * y[:, HY_WIDTH:2 * HY_WIDTH]
    ql = _rms(jnp.dot(hc, wql_ref[...], preferred_element_type=F32), qg_ref[...]).astype(BF16)
    nh = q_ref.shape[1]
    qa = jnp.concatenate([qa_ref[...]] * nh, axis=1)
    qb = jnp.concatenate([qb_ref[...]] * nh, axis=1)
    q = (jnp.dot(ql, wqa_ref[...], preferred_element_type=F32) * qa
         + jnp.dot(ql, wqb_ref[...], preferred_element_type=F32) * qb).astype(BF16)
    kvl = _rms(jnp.dot(hc, wkvl_ref[...], preferred_element_type=F32), kvg_ref[...]).astype(BF16)
    kn = jnp.dot(kvl, wk_ref[...], preferred_element_type=F32)
    v_t = jnp.dot(kvl, wv_ref[...], preferred_element_type=F32).T
    kpe = (jnp.dot(hc, wpa_ref[...], preferred_element_type=F32) * ka_ref[...]
           + jnp.dot(hc, wpb_ref[...], preferred_element_type=F32) * kb_ref[...])
    ones = jnp.ones((ATT_ONES_ROWS, tm), BF16)
    for hd in range(nh):
        q_ref[0, hd] = q[:, hd * MLA_SLAB:(hd + 1) * MLA_SLAB]
        k_ref[0, hd] = (kn[:, hd * MLA_SLAB:(hd + 1) * MLA_SLAB] + kpe).astype(BF16)
        vt_ref[0, hd, :MLA_V, :] = v_t[hd * MLA_V:(hd + 1) * MLA_V].astype(BF16)
        vt_ref[0, hd, MLA_V:, :] = ones


def _rot_cols(w_pe):
    ev, od = w_pe[..., 0::2], w_pe[..., 1::2]
    return jnp.concatenate([ev, od], -1), jnp.concatenate([-od, ev], -1)


def _mla_weights(p):
    w_in = p['w_in']
    kin = w_in.shape[0]
    wq = p['w_qb'].reshape(MLA_Q_RANK, MLA_HEADS, MLA_QK)
    qa_pe, qb_pe = _rot_cols(wq[..., MLA_NOPE:])
    zq = jnp.zeros((MLA_Q_RANK, MLA_HEADS, MLA_SLAB - MLA_QK), F32)
    w_qa = jnp.concatenate([wq[..., :MLA_NOPE], qa_pe, zq], -1).reshape(MLA_Q_RANK, MLA_HEADS * MLA_SLAB)
    w_qb = jnp.concatenate([jnp.zeros_like(wq[..., :MLA_NOPE]), qb_pe, zq], -1)
    w_qb = w_qb.reshape(MLA_Q_RANK, MLA_HEADS * MLA_SLAB)
    wkv = p['w_kvb'].reshape(MLA_KV_RANK, MLA_HEADS, MLA_NOPE + MLA_V)
    w_k = jnp.concatenate([wkv[..., :MLA_NOPE], jnp.zeros((MLA_KV_RANK, MLA_HEADS, MLA_SLAB - MLA_NOPE), F32)], -1)
    w_k = w_k.reshape(MLA_KV_RANK, MLA_HEADS * MLA_SLAB)
    w_v = wkv[..., MLA_NOPE:].reshape(MLA_KV_RANK, MLA_HEADS * MLA_V)
    pa, pb = _rot_cols(w_in[:, OFF_KPE:])
    left, right = jnp.zeros((kin, MLA_NOPE), F32), jnp.zeros((kin, MLA_SLAB - MLA_QK), F32)
    w_pa = jnp.concatenate([left, pa, right], -1)
    w_pb = jnp.concatenate([left, pb, right], -1)
    bf = lambda a: a.astype(BF16)
    return dict(w_hy=bf(w_in[:, :OFF_Q]), conv_w=p['conv_w'], conv_b=p['conv_b'][None],
                w_ql=bf(w_in[:, OFF_Q:OFF_KV]), q_g=p['q_norm'][None], w_qa=bf(w_qa), w_qb=bf(w_qb),
                w_kvl=bf(w_in[:, OFF_KV:OFF_KPE]), kv_g=p['kv_norm'][None], w_k=bf(w_k), w_v=bf(w_v),
                w_pa=bf(w_pa), w_pb=bf(w_pb))


def _rope_slabs(n, rotate):
    one = jnp.ones((n, MLA_NOPE), F32)
    zero = jnp.zeros((n, MLA_SLAB - MLA_QK), F32)
    if rotate:
        cos, sin = _rope_tables(n)
    else:
        cos, sin = jnp.ones((n, MLA_ROPE // 2), F32), jnp.zeros((n, MLA_ROPE // 2), F32)
    return (jnp.concatenate([one, cos, cos, zero], -1), jnp.concatenate([jnp.zeros_like(one), sin, sin, zero], -1))


def _mla_in_proj(x, shift, scale, w, rotate):
    bsz, n, kin = x.shape
    tm = min(ROW_TILE, n)
    assert n % tm == 0
    ca, sb = _rope_slabs(n, rotate)
    s = MLA_SCALE * LOG2E
    tabs = [ca * s, sb * s, ca, sb]
    consts = [w['w_hy'], w['conv_w'], w['conv_b'], w['w_ql'], w['q_g'], w['w_qa'], w['w_qb'], w['w_kvl'], w['kv_g'],
              w['w_k'], w['w_v'], w['w_pa'], w['w_pb']]
    full = lambda a: pl.BlockSpec(a.shape, lambda b, i: (0,) * a.ndim)
    row = lambda c: pl.BlockSpec((1, tm, c), lambda b, i: (b, i, 0))
    head = pl.BlockSpec((1, MLA_HEADS, tm, MLA_SLAB), lambda b, i: (b, 0, i, 0))
    dva = MLA_V + ATT_ONES_ROWS
    vmem = 2 * (tm * kin * 4 + sum(a.size * a.dtype.itemsize for a in consts) + 2 * tm * HY_WIDTH * 4
                + 3 * MLA_HEADS * tm * MLA_SLAB * 2 + 4 * tm * MLA_SLAB * 4) + 8 * (tm + 16) * 3 * HY_WIDTH * 4
    return pl.pallas_call(
        _mla_in_kernel, grid=(bsz, n // tm),
        in_specs=_halo_specs(n, tm, kin) + [_bcast_spec(shift), _bcast_spec(scale)] + [full(a) for a in consts]
        + [pl.BlockSpec((tm, MLA_SLAB), lambda b, i: (i, 0))] * 4,
        out_specs=[row(HY_WIDTH), row(HY_WIDTH), head, head,
                   pl.BlockSpec((1, MLA_HEADS, dva, tm), lambda b, i: (b, 0, 0, i))],
        out_shape=[jax.ShapeDtypeStruct((bsz, n, HY_WIDTH), F32)] * 2
        + [jax.ShapeDtypeStruct((bsz, MLA_HEADS, n, MLA_SLAB), BF16)] * 2
        + [jax.ShapeDtypeStruct((bsz, MLA_HEADS, dva, n), BF16)],
        compiler_params=_params(("parallel", "parallel"), vmem), name="mla_in_proj",
    )(x, x, x, shift, scale, *consts, *tabs)


def _mixer_hyena_mla(x, ctx, sh_l, sc_l, sh_c, sc_c, p):
    n, nc = x.shape[1], ctx.shape[1]
    w = _mla_weights(p)
    x0_l, z_l, q_l, k_l, vt_l = _mla_in_proj(x, sh_l, sc_l, w, rotate=True)
    x0_c, z_c, q_c, k_c, vt_c = _mla_in_proj(ctx, sh_c, sc_c, w, rotate=False)
    k_all = jnp.concatenate([k_c, k_l], 2)
    vt_all = jnp.concatenate([vt_c, vt_l], 3)
    att_l = _attention(q_l, k_all, vt_all, tq=min(ATT_Q_TILE, n), tk=_key_tile(nc + n))
    att_c = _attention(q_c, k_c, vt_c, tq=nc, tk=_key_tile(nc))
    filt = (p['filt_w1'], p['filt_b1'], p['filt_freq1'], p['filt_w2'], p['filt_b2'], p['filt_freq2'], p['filt_w3'])
    hyo_l = _hyena_sequence(z_l, x0_l, filt, p['skip'])
    hyo_c = _hyena_sequence(z_c, x0_c, filt, p['skip'])
    return (hyo_l, att_l), (hyo_c, att_c)


def _halo_specs(n, tm, k):
    nb = n // SUBLANES
    per = tm // SUBLANES
    return [pl.BlockSpec((1, tm, k), lambda b, i: (b, i, 0)),
            pl.BlockSpec((1, SUBLANES, k), lambda b, i: (b, jnp.maximum(i * per - 1, 0), 0)),
            pl.BlockSpec((1, SUBLANES, k), lambda b, i: (b, jnp.minimum((i + 1) * per, nb - 1), 0))]


def _conv3(u, cw_ref, cb_ref, tm):
    i = pl.program_id(1)
    rows = lax.broadcasted_iota(I32, (u.shape[0], 1), 0)
    inside = jnp.logical_and(jnp.logical_or(rows >= SUBLANES, i > 0),
                             jnp.logical_or(rows < tm + SUBLANES, i < pl.num_programs(1) - 1))
    u = jnp.where(inside, u, 0.0)
    prev = pltpu.roll(u, 1, 0)[SUBLANES:SUBLANES + tm]
    nxt = pltpu.roll(u, u.shape[0] - 1, 0)[SUBLANES:SUBLANES + tm]
    return cw_ref[0:1, :] * prev + cw_ref[1:2, :] * u[SUBLANES:SUBLANES + tm] + cw_ref[2:3, :] * nxt + cb_ref[...]


def _ssd_in_kernel(xc_ref, xp_ref, xn_ref, sh_ref, sc_ref, wz_ref, wx_ref, wdt_ref, cw_ref, cb_ref, dtb_ref,
                   z_ref, xs_ref, b_ref, c_ref, dt_ref):
    tm = xc_ref.shape[1]
    rows = jnp.concatenate([xp_ref[0], xc_ref[0], xn_ref[0]], axis=0)
    h = (rows * (1.0 + sc_ref[0]) + sh_ref[0]).astype(BF16)
    hc = h[SUBLANES:SUBLANES + tm]
    z_ref[0] = jnp.dot(hc, wz_ref[...], preferred_element_type=F32)
    y = _conv3(jnp.dot(h, wx_ref[...], preferred_element_type=F32), cw_ref, cb_ref, tm)
    xbc = y * jax.nn.sigmoid(y)
    xs_ref[0] = xbc[:, :SSD_INNER]
    b_ref[0] = xbc[:, SSD_INNER:SSD_INNER + SSD_BC]
    c_ref[0] = xbc[:, SSD_INNER + SSD_BC:]
    dt = jnp.dot(hc, wdt_ref[...], preferred_element_type=F32) + dtb_ref[...]
    dt_ref[0] = (jnp.maximum(dt, 0.0) + jnp.log1p(jnp.exp(-jnp.abs(dt)))).T


def _ssd_in_proj(x, shift, scale, w_z, w_xbc, w_dt, conv_w, conv_b, dt_bias):
    bsz, n, k = x.shape
    tm = min(ROW_TILE, n)
    assert n % tm == 0
    full = lambda a: pl.BlockSpec(a.shape, lambda b, i: (0,) * a.ndim)
    row = lambda c: pl.BlockSpec((1, tm, c), lambda b, i: (b, i, 0))
    nh2 = w_dt.shape[1]
    consts = [w_z, w_xbc, w_dt, conv_w, conv_b, dt_bias]
    widths = [SSD_INNER, SSD_INNER, SSD_BC, SSD_BC]
    vmem = 2 * (tm * k * 4 + sum(a.size * a.dtype.itemsize for a in consts) + tm * (sum(widths) + nh2) * 4) \
        + 6 * (tm + 2 * SUBLANES) * SSD_XBC * 4
    return pl.pallas_call(
        _ssd_in_kernel, grid=(bsz, n // tm),
        in_specs=_halo_specs(n, tm, k) + [_bcast_spec(shift), _bcast_spec(scale)] + [full(a) for a in consts],
        out_specs=[row(c) for c in widths] + [pl.BlockSpec((1, nh2, tm), lambda b, i: (b, 0, i))],
        out_shape=[jax.ShapeDtypeStruct((bsz, n, c), F32) for c in widths]
        + [jax.ShapeDtypeStruct((bsz, nh2, n), F32)],
        compiler_params=_params(("parallel", "parallel"), vmem), name="ssd_in_proj",
    )(x, x, x, shift, scale, *consts)


def _ssd_chunk(x_ref, b_ref, c_ref, dt_ref, a_ref, st_sc, y_ref, reverse):
    q = SSD_CHUNK
    dt = dt_ref[0]
    a = dt * a_ref[0]
    si = lax.broadcasted_iota(I32, (q, q), 0)
    li = lax.broadcasted_iota(I32, (q, q), 1)
    incl = jnp.where((si >= li) if reverse else (si <= li), 1.0, 0.0)
    hp = lax.Precision.HIGHEST
    acs = jnp.dot(a, incl, precision=hp, preferred_element_type=F32)
    tot = jnp.dot(a, jnp.ones((q, LANES), F32), precision=hp, preferred_element_type=F32)
    e_in = jnp.exp(acs)
    w_end = jnp.exp(tot - acs) * dt
    e_tot = jnp.exp(tot)
    acs_t = acs.T
    e_in_t = e_in.T
    mask = (li >= si) if reverse else (li <= si)
    cmat = c_ref[0]
    cb = lax.dot_general(cmat.astype(BF16), b_ref[0].astype(BF16), (((1,), (1,)), ((), ())),
                         preferred_element_type=F32)
    bt = b_ref[0].T
    first = lax.broadcasted_iota(I32, (q, LANES), 1) < SSD_HEADDIM
    ys = []
    for pr in range(SSD_HPG // 2):
        lo, hi = pr * LANES, (pr + 1) * LANES
        x_pair = x_ref[0, :, lo:hi].astype(BF16)
        st_pair = st_sc[:, lo:hi]
        rhs = jnp.concatenate([x_pair, st_pair.astype(BF16)], axis=0)
        y2, s2 = [], []
        for r in (2 * pr, 2 * pr + 1):
            seg = acs_t[:, r:r + 1] - acs[r:r + 1, :]
            m = cb * jnp.exp(jnp.where(mask, seg, -jnp.inf)) * dt[r:r + 1, :]
            lhs = jnp.concatenate([m.astype(BF16), (cmat * e_in_t[:, r:r + 1]).astype(BF16)], axis=1)
            y2.append(jnp.dot(lhs, rhs, preferred_element_type=F32))
            btr = (bt * w_end[r:r + 1, :]).astype(BF16)
            s2.append(e_tot[r:r + 1, :] * st_pair + jnp.dot(btr, x_pair, preferred_element_type=F32))
        ys.append(jnp.where(first, y2[0], y2[1]))
        st_sc[:, lo:hi] = jnp.where(first, s2[0], s2[1])
    y_ref[0] = jnp.concatenate(ys, axis=1)


def _ssd_scan_kernel(xf_ref, bf_ref, cf_ref, dtf_ref, af_ref, s0f_ref, xr_ref, br_ref, cr_ref, dtr_ref, ar_ref,
                     s0r_ref, yf_ref, sf_ref, yr_ref, sr_ref, stf_sc, str_sc):
    ci = pl.program_id(2)

    @pl.when(ci == 0)
    def _():
        stf_sc[...] = s0f_ref[0, 0]
        str_sc[...] = s0r_ref[0, 0]

    _ssd_chunk(xf_ref, bf_ref, cf_ref, dtf_ref, af_ref, stf_sc, yf_ref, reverse=False)
    _ssd_chunk(xr_ref, br_ref, cr_ref, dtr_ref, ar_ref, str_sc, yr_ref, reverse=True)

    @pl.when(ci == pl.num_programs(2) - 1)
    def _():
        sf_ref[0, 0] = stf_sc[...]
        sr_ref[0, 0] = str_sc[...]


def _ssd_scan(xs, bm, cm, dt_t, a_rep, s0_f, s0_r):
    bsz, n, _ = xs.shape
    q = SSD_CHUNK
    nc = n // q
    assert n % q == 0 and q == LANES
    gw = SSD_HPG * SSD_HEADDIM
    state = pl.BlockSpec((1, 1, SSD_STATE, gw), lambda b, g, c: (b, g, 0, 0))

    def side(d):
        cc = (lambda c: nc - 1 - c) if d else (lambda c: c)
        specs = [pl.BlockSpec((1, q, gw), lambda b, g, c: (b, cc(c), g)),
                 pl.BlockSpec((1, q, SSD_STATE), lambda b, g, c: (b, cc(c), g)),
                 pl.BlockSpec((1, q, SSD_STATE), lambda b, g, c: (b, cc(c), g)),
                 pl.BlockSpec((1, SSD_HPG, q), lambda b, g, c: (b, d * SSD_GROUPS + g, cc(c))),
                 pl.BlockSpec((1, SSD_HPG, LANES), lambda b, g, c: (d * SSD_GROUPS + g, 0, 0)),
                 state]
        return specs, [pl.BlockSpec((1, q, gw), lambda b, g, c: (b, cc(c), g)), state]

    (in_f, out_f), (in_r, out_r) = side(0), side(1)
    shapes = [jax.ShapeDtypeStruct((bsz, n, SSD_INNER), F32),
              jax.ShapeDtypeStruct((bsz, SSD_GROUPS, SSD_STATE, gw), F32)]
    vmem = 4 * (2 * q * gw * 4 + 2 * q * SSD_STATE * 4 + 2 * SSD_STATE * gw * 4) + 2 * SSD_STATE * gw * 4 \
        + 128 * q * q * 4
    return pl.pallas_call(
        _ssd_scan_kernel, grid=(bsz, SSD_GROUPS, nc), in_specs=in_f + in_r, out_specs=out_f + out_r,
        out_shape=shapes + shapes,
        scratch_shapes=[pltpu.VMEM((SSD_STATE, gw), F32), pltpu.VMEM((SSD_STATE, gw), F32)],
        compiler_params=_params(("parallel", "parallel", "arbitrary"), vmem), name="ssd_scan",
    )(xs, bm, cm, dt_t, a_rep, s0_f, xs, bm, cm, dt_t, a_rep, s0_r)


def _ssd_out_kernel(x_ref, yf_ref, yb_ref, xs_ref, z_ref, d_ref, ng_ref, w_ref, *epilogue_refs):
    z = z_ref[0]
    y = (yf_ref[0] + yb_ref[0] + xs_ref[0] * d_ref[0]) * (z * jax.nn.sigmoid(z))
    gw = SSD_INNER // SSD_GROUPS
    parts = []
    for g in range(SSD_GROUPS):
        yg = y[:, g * gw:(g + 1) * gw]
        parts.append(yg * lax.rsqrt(jnp.mean(yg * yg, -1, keepdims=True) + RMS_EPS))
    yn = (jnp.concatenate(parts, axis=1) * ng_ref[0]).astype(BF16)
    _mix_epilogue(x_ref[0], jnp.dot(yn, w_ref[...], preferred_element_type=F32), *epilogue_refs)


def _ssd_out(x, y_f, y_b, xs, z, d_rep, norm_g, w_out, gate, ln_g, ln_b, shift2, scale2):
    bsz, n, d = x.shape
    tm = min(ROW_TILE, n)
    assert n % tm == 0
    row = lambda c: pl.BlockSpec((1, tm, c), lambda b, i: (b, i, 0))
    vecs = [gate, ln_g, ln_b, shift2, scale2]
    in_specs = [row(d)] + [row(SSD_INNER)] * 4 + [_bcast_spec(d_rep), _bcast_spec(norm_g),
                                                 pl.BlockSpec(w_out.shape, lambda b, i: (0, 0))]
    in_specs += [_bcast_spec(a) for a in vecs]
    vmem = 2 * (3 * tm * d * 4 + 4 * tm * SSD_INNER * 4 + w_out.size * 2) + 4 * tm * SSD_INNER * 4
    return pl.pallas_call(
        _ssd_out_kernel, grid=(bsz, n // tm), in_specs=in_specs, out_specs=[row(d), row(d), row(d // 2)],
        out_shape=[jax.ShapeDtypeStruct((bsz, n, d), F32)] * 2 + [jax.ShapeDtypeStruct((bsz, n, d // 2), U32)],
        compiler_params=_params(("parallel", "parallel"), vmem), name="ssd_out",
    )(x, y_f, y_b, xs, z, d_rep, norm_g, w_out, *vecs)


def _mixer_ssd(x, ctx, sh_l, sc_l, sh_c, sc_c, p):
    a_all = -jnp.exp(jnp.concatenate([p['a_log_f'], p['a_log_b']]))
    a_rep = jnp.broadcast_to(a_all.reshape(2 * SSD_GROUPS, SSD_HPG, 1), (2 * SSD_GROUPS, SSD_HPG, LANES))
    w_in = p['w_in'].astype(BF16)
    consts = (w_in[:, :SSD_INNER], w_in[:, SSD_INNER:SSD_INNER + SSD_XBC], w_in[:, SSD_INNER + SSD_XBC:],
              p['conv_w'], p['conv_b'][None], jnp.concatenate([p['dt_bias_f'], p['dt_bias_b']])[None])
    _, xc, bc, cc, dtc = _ssd_in_proj(ctx, sh_c, sc_c, *consts)
    zl, xl, bl, cl, dtl = _ssd_in_proj(x, sh_l, sc_l, *consts)
    s0 = jnp.zeros((ctx.shape[0], SSD_GROUPS, SSD_STATE, SSD_HPG * SSD_HEADDIM), F32)
    _, sc_f, _, sc_b = _ssd_scan(xc, bc, cc, dtc, a_rep, s0, s0)
    y_f, _, y_b, _ = _ssd_scan(xl, bl, cl, dtl, a_rep, sc_f, sc_b)
    return y_f, y_b, xl, zl


def kernel(x, c, ctx, c_ctx, mod_w, mod_b, ln_mix_g, ln_mix_b, ln_ffn_g, ln_ffn_b, a_w_in, hy_conv_w, hy_conv_b, hy_filt_w1, hy_filt_b1, hy_filt_freq1, hy_filt_w2, hy_filt_b2, hy_filt_freq2, hy_filt_w3, hy_skip, mla_q_norm, mla_w_qb, mla_kv_norm, mla_w_kvb, a_w_out, ssd_w_in, ssd_conv_w, ssd_conv_b, ssd_dt_bias_f, ssd_dt_bias_b, ssd_a_log_f, ssd_a_log_b, ssd_d, ssd_norm_g, ssd_w_out, router_w, router_bias, exp_w_gu, exp_w_down, sh_w_gu, sh_w_down):
    bsz, n_lat, d = x.shape
    n_ctx = ctx.shape[1]
    pad = -(bsz + 1) % SUBLANES
    cond = jnp.concatenate([c, c_ctx[None], jnp.zeros((pad, d), F32)], 0)
    for l in range(DEPTH):
        last = l == DEPTH - 1
        i = l // 2
        mods = _modulation(cond, mod_w, mod_b, l)
        mod = mods[:bsz].reshape(bsz, N_MOD, 1, d)
        mod_c = mods[bsz:bsz + 1].reshape(1, N_MOD, 1, d)
        sh1, sc1, g1, sh2, sc2, g2 = [mod[:, j] for j in range(N_MOD)]
        csh1, csc1, cg1, csh2, csc2, cg2 = [mod_c[:, j] for j in range(N_MOD)]
        vec = lambda a: a.reshape(1, 1, d)
        if l % 2 == 0:
            p = {"w_in": a_w_in[i], "conv_w": hy_conv_w[i], "conv_b": hy_conv_b[i],
                 "filt_w1": hy_filt_w1[i], "filt_b1": hy_filt_b1[i], "filt_freq1": hy_filt_freq1[i],
                 "filt_w2": hy_filt_w2[i], "filt_b2": hy_filt_b2[i], "filt_freq2": hy_filt_freq2[i],
                 "filt_w3": hy_filt_w3[i], "skip": hy_skip[i], "q_norm": mla_q_norm[i], "w_qb": mla_w_qb[i],
                 "kv_norm": mla_kv_norm[i], "w_kvb": mla_w_kvb[i]}
            ys_l, ys_c = _mixer_hyena_mla(x, ctx, sh1, sc1, csh1, csc1, p)
            w_out = a_w_out[i].astype(BF16)
            ws = [w_out[:HY_WIDTH], w_out[HY_WIDTH:]]
            x, ff_x, fp_x = _mix_out(x, ys_l, ws, g1, vec(ln_mix_g[l]), vec(ln_mix_b[l]), sh2, sc2)
        else:
            p = {"w_in": ssd_w_in[i], "conv_w": ssd_conv_w[i], "conv_b": ssd_conv_b[i],
                 "dt_bias_f": ssd_dt_bias_f[i], "dt_bias_b": ssd_dt_bias_b[i],
                 "a_log_f": ssd_a_log_f[i], "a_log_b": ssd_a_log_b[i]}
            assert last
            y_f, y_b, xs, z = _mixer_ssd(x, ctx, sh1, sc1, csh1, csc1, p)
            d_rep = jnp.repeat(ssd_d[i], SSD_HEADDIM).reshape(1, 1, SSD_INNER)
            x, ff_x, fp_x = _ssd_out(x, y_f, y_b, xs, z, d_rep, ssd_norm_g[i].reshape(1, 1, SSD_INNER),
                                     ssd_w_out[i].astype(BF16), g1, vec(ln_mix_g[l]), vec(ln_mix_b[l]), sh2, sc2)
        sh_gu = sh_w_gu[l].astype(BF16)
        sh_down = sh_w_down[l].astype(BF16)
        ln_g, ln_b = vec(ln_ffn_g[l]), vec(ln_ffn_b[l])
        moe_w = (router_w[l], router_bias[l], exp_w_gu, exp_w_down, l)
        if last:
            yb, pos, w = _moe_dispatch_experts(ff_x.reshape(-1, d), fp_x.reshape(-1, d // 2), *moe_w)
            x = _ffn_out(x, ff_x, yb, pos, w, 0, sh_gu, sh_down, g2, ln_g, ln_b)
        else:
            ctx, ff_c, fp_c = _mix_out(ctx, ys_c, ws, cg1, vec(ln_mix_g[l]), vec(ln_mix_b[l]), csh2, csc2)
            tokens = jnp.concatenate([ff_c.reshape(-1, d), ff_x.reshape(-1, d)], 0)
            packed = jnp.concatenate([fp_c.reshape(-1, d // 2), fp_x.reshape(-1, d // 2)], 0)
            yb, pos, w = _moe_dispatch_experts(tokens, packed, *moe_w)
            assert (bsz * n_ctx) % ROUTE_TOKENS == 0
            ctx = _ffn_out(ctx, ff_c, yb, pos, w, 0, sh_gu, sh_down, cg2, ln_g, ln_b)
            x = _ffn_out(x, ff_x, yb, pos, w, bsz * n_ctx // ROUTE_TOKENS, sh_gu, sh_down, g2, ln_g, ln_b)
    return x
```

```python
import functools
import math

import jax
import jax.numpy as jnp
from jax import lax
from jax.experimental import pallas as pl
from jax.experimental.pallas import tpu as pltpu

F32 = jnp.float32
BF16 = jnp.bfloat16
I32 = jnp.int32
U32 = jnp.uint32

D_MODEL = 1024
DEPTH = 2
GRID_W = 64
N_MOD = 6

HY_WIDTH = 512
HY_EMB = 33
HY_BANDS = (HY_EMB - 1) // 2
HY_TARGET = 1e-2
HY_FAST_DECAY = 0.3
HY_SLOW_DECAY = 1.5
HY_DECAY_MIN = math.log(HY_TARGET) / HY_SLOW_DECAY
HY_DECAY_MAX = math.log(HY_TARGET) / HY_FAST_DECAY

MLA_HEADS = 8
MLA_NOPE = 64
MLA_ROPE = 32
MLA_V = 64
MLA_Q_RANK = 256
MLA_KV_RANK = 128
MLA_QK = MLA_NOPE + MLA_ROPE
MLA_SCALE = MLA_QK ** -0.5
ROPE_THETA = 10000.0
LOG2E = math.log2(math.e)

OFF_Q = 3 * HY_WIDTH
OFF_KV = OFF_Q + MLA_Q_RANK
OFF_KPE = OFF_KV + MLA_KV_RANK

SSD_INNER = 2 * D_MODEL
SSD_HEADDIM = 64
SSD_HEADS = SSD_INNER // SSD_HEADDIM
SSD_GROUPS = 4
SSD_STATE = 128
SSD_CHUNK = 128
SSD_BC = SSD_GROUPS * SSD_STATE
SSD_XBC = SSD_INNER + 2 * SSD_BC
SSD_HPG = SSD_HEADS // SSD_GROUPS

N_EXPERTS = 256
TOP_K = 8
N_EXPERT_GROUPS = 8
TOPK_GROUPS = 4
EXPERT_DIM = 256
ROUTED_SCALE = 2.5

DN_ALPHA = (2 * DEPTH) ** 0.25
LN_EPS = 1e-5
RMS_EPS = 1e-6

LANES = 128
SUBLANES = 8
V7X_VMEM_CAP = 56 * 1024 * 1024

MOE_ROWS = 512
MOE_SUB_ROWS = 256
ROUTE_TOKENS = 256
ROW_TILE = 256
ATT_Q_TILE = 4096
ATT_HEADS_PER_STEP = 2
DMA_PRIORITIES = 2
ATT_ONES_ROWS = 16
MLA_SLAB = LANES
DFT_INNER = 128
DFT_COLS = 2048
DFT_BATCH = 2
HY_DIRECT_MAX = 512
FILT_ROWS = 512
MOD_COLS = 1024


def _params(semantics, vmem_bytes, **kw):
    limit = int(min(max(vmem_bytes * 5 // 4, 32 * 1024 * 1024), V7X_VMEM_CAP))
    return pltpu.CompilerParams(dimension_semantics=semantics, vmem_limit_bytes=limit, **kw)


def _bcast_spec(a):
    if a.shape[0] == 1:
        return pl.BlockSpec((1, 1, a.shape[2]), lambda b, i: (0, 0, 0))
    return pl.BlockSpec((1, 1, a.shape[2]), lambda b, i: (b, 0, 0))


def _layer_norm(r, g, b):
    mu = jnp.mean(r, -1, keepdims=True)
    c = r - mu
    var = jnp.mean(c * c, -1, keepdims=True)
    return c * lax.rsqrt(var + LN_EPS) * g + b


def _swiglu_rows(xb, w_gu, w_down):
    h = jnp.dot(xb, w_gu, preferred_element_type=F32)
    half = h.shape[1] // 2
    g, u = h[:, :half], h[:, half:]
    a = (g * jax.nn.sigmoid(g) * u).astype(BF16)
    return jnp.dot(a, w_down, preferred_element_type=F32)


def _mod_kernel(c_ref, w_ref, b_ref, o_ref):
    c = c_ref[...]
    o_ref[...] = jnp.dot(c * jax.nn.sigmoid(c), w_ref[0], precision=lax.Precision.HIGHEST,
                         preferred_element_type=F32) + b_ref[0]


def _modulation(cond, mod_w, mod_b, layer):
    r, d = cond.shape
    nout = mod_w.shape[2]
    assert nout % MOD_COLS == 0 and r % SUBLANES == 0
    return pl.pallas_call(
        _mod_kernel, grid=(nout // MOD_COLS,),
        in_specs=[pl.BlockSpec((r, d), lambda j: (0, 0)), pl.BlockSpec((1, d, MOD_COLS), lambda j: (layer, 0, j)),
                  pl.BlockSpec((1, 1, MOD_COLS), lambda j: (layer, 0, j))],
        out_specs=pl.BlockSpec((r, MOD_COLS), lambda j: (0, j)),
        out_shape=jax.ShapeDtypeStruct((r, nout), F32),
        compiler_params=_params(("parallel",), 2 * (d * MOD_COLS * 4 + 2 * r * MOD_COLS * 4)), name="modulation",
    )(cond, mod_w, mod_b.reshape(mod_b.shape[0], 1, nout))


def _pack_pairs(x):
    half = x.shape[1] // 2
    hi = pltpu.bitcast(x[:, :half].astype(BF16).astype(F32), U32)
    lo = pltpu.bitcast(x[:, half:].astype(BF16).astype(F32), U32)
    return hi | (lo >> 16)


def _unpack_pairs(u):
    hi = pltpu.bitcast(u & jnp.uint32(0xFFFF0000), F32)
    lo = pltpu.bitcast(u << 16, F32)
    return jnp.concatenate([hi, lo], axis=1)


def _mix_epilogue(x, mix, gate_ref, g_ref, b_ref, sh_ref, sc_ref, xo_ref, ff_ref, ffp_ref):
    xn = _layer_norm(DN_ALPHA * x + gate_ref[0] * mix, g_ref[0], b_ref[0])
    xo_ref[0] = xn
    ff = xn * (1.0 + sc_ref[0]) + sh_ref[0]
    ff_ref[0] = ff
    ffp_ref[0] = _pack_pairs(ff)


def _mix_out_kernel(*refs, n_y):
    x_ref = refs[0]
    y_refs = refs[1:1 + n_y]
    w_refs = refs[1 + n_y:1 + 2 * n_y]
    y = None
    for y_ref, w_ref in zip(y_refs, w_refs):
        t = jnp.dot(y_ref[0].astype(BF16), w_ref[...], preferred_element_type=F32)
        y = t if y is None else y + t
    _mix_epilogue(x_ref[0], y, *refs[1 + 2 * n_y:])


def _mix_out(x, ys, ws, gate, ln_g, ln_b, shift2, scale2):
    bsz, n, d = x.shape
    tm = min(ROW_TILE, n)
    assert n % tm == 0
    row = lambda c: pl.BlockSpec((1, tm, c), lambda b, i: (b, i, 0))
    vecs = [gate, ln_g, ln_b, shift2, scale2]
    in_specs = [row(d)] + [row(y.shape[2]) for y in ys]
    in_specs += [pl.BlockSpec(w.shape, lambda b, i: (0, 0)) for w in ws]
    in_specs += [_bcast_spec(a) for a in vecs]
    vmem = 2 * (3 * tm * d * 4 + sum(tm * y.shape[2] * 4 + w.size * 2 for y, w in zip(ys, ws)))
    return pl.pallas_call(
        functools.partial(_mix_out_kernel, n_y=len(ys)),
        grid=(bsz, n // tm), in_specs=in_specs, out_specs=[row(d), row(d), row(d // 2)],
        out_shape=[jax.ShapeDtypeStruct((bsz, n, d), F32)] * 2 + [jax.ShapeDtypeStruct((bsz, n, d // 2), U32)],
        compiler_params=_params(("parallel", "parallel"), vmem), name="mix_out",
    )(x, *ys, *ws, *vecs)


def _attn_kernel(q_ref, k_ref, vt_ref, o_ref, *, tk, hp):
    nk = k_ref.shape[2]
    tq = q_ref.shape[2]
    dva = vt_ref.shape[2]
    dv = dva - ATT_ONES_ROWS

    def body(j, carry):
        off = pl.multiple_of(j * tk, tk)
        new = []
        for h in range(hp):
            m_prev, acc = carry[h]
            st = lax.dot_general(k_ref[0, h, pl.ds(off, tk), :], q_ref[0, h], (((1,), (1,)), ((), ())),
                                 preferred_element_type=F32)
            m_new = jnp.maximum(m_prev, jnp.max(st, 0, keepdims=True))
            p = jnp.exp2(st - m_new).astype(BF16)
            alpha = jnp.exp2(m_prev - m_new)
            acc = alpha * acc + jnp.dot(vt_ref[0, h, :, pl.ds(off, tk)], p, preferred_element_type=F32)
            new.append((m_new, acc))
        return tuple(new)

    init = tuple((jnp.full((1, tq), -jnp.inf, F32), jnp.zeros((dva, tq), F32)) for _ in range(hp))
    fin = lax.fori_loop(0, nk // tk, body, init)
    outs = [acc[:dv] / acc[dv:dv + 1] for _, acc in fin]
    o_ref[0] = jnp.concatenate(outs, 0).T


def _attention(q, k, vt, tq, tk):
    bsz, h, nq, dk = q.shape
    nk, dva = k.shape[2], vt.shape[2]
    dv = dva - ATT_ONES_ROWS
    hp = ATT_HEADS_PER_STEP
    assert nq % tq == 0 and nk % tk == 0 and h % hp == 0
    vmem = 2 * hp * (tq * LANES * 2 + nk * LANES * 2 + dva * nk * 2) + 2 * tq * hp * dv * 4 + 6 * hp * tk * tq * 4
    return pl.pallas_call(
        functools.partial(_attn_kernel, tk=tk, hp=hp), grid=(bsz, h // hp, nq // tq),
        in_specs=[pl.BlockSpec((1, hp, tq, dk), lambda b, g, i: (b, g, i, 0)),
                  pl.BlockSpec((1, hp, nk, dk), lambda b, g, i: (b, g, 0, 0)),
                  pl.BlockSpec((1, hp, dva, nk), lambda b, g, i: (b, g, 0, 0))],
        out_specs=pl.BlockSpec((1, tq, hp * dv), lambda b, g, i: (b, i, g)),
        out_shape=jax.ShapeDtypeStruct((bsz, nq, h * dv), F32),
        compiler_params=_params(("parallel", "parallel", "arbitrary"), vmem), name="mla_attention",
    )(q, k, vt)


def _key_tile(nk):
    for t in (768, 512, 384, 256, 128):
        if nk % t == 0:
            return t
    return nk


def _router_kernel(x_ref, wt_ref, bias_ref, upper_ref, idx_ref, w_ref, rank_ref, cnt_ref, run_sc):
    i = pl.program_id(0)
    tm = x_ref.shape[0]

    @pl.when(i == 0)
    def _():
        run_sc[...] = jnp.zeros(run_sc.shape, F32)

    logits = lax.dot_general(wt_ref[...], x_ref[...], (((1,), (1,)), ((), ())),
                             precision=lax.Precision.HIGHEST, preferred_element_type=F32)
    sc = jax.nn.sigmoid(logits)
    ch = sc + bias_ref[:, :1]
    neg = -jnp.inf
    chg = ch.reshape(N_EXPERT_GROUPS, N_EXPERTS // N_EXPERT_GROUPS, tm)
    m1 = jnp.max(chg, axis=1)
    eq = chg == m1[:, None, :]
    cnt = jnp.sum(eq.astype(F32), axis=1)
    m2 = jnp.max(jnp.where(eq, neg, chg), axis=1)
    g2 = m1 + jnp.where(cnt >= 2.0, m1, m2)
    gi = lax.broadcasted_iota(I32, g2.shape, 0)
    beaten = jnp.zeros(g2.shape, F32)
    for g in range(N_EXPERT_GROUPS):
        row = g2[g:g + 1, :]
        beaten = beaten + jnp.where(row > g2, 1.0, jnp.where(row == g2, jnp.where(gi > g, 1.0, 0.0), 0.0))
    keep = beaten < float(TOPK_GROUPS)
    cur = jnp.where(keep[:, None, :], chg, neg).reshape(N_EXPERTS, tm)
    eidx = lax.broadcasted_iota(I32, (N_EXPERTS, tm), 0)
    multi = jnp.zeros((N_EXPERTS, tm), F32)
    hits, idx_rows, w_rows = [], [], []
    for _ in range(TOP_K):
        m = jnp.max(cur, axis=0, keepdims=True)
        sel = jnp.min(jnp.where(cur == m, eidx, N_EXPERTS), axis=0, keepdims=True)
        hit = eidx == sel
        idx_rows.append(sel)
        w_rows.append(jnp.sum(jnp.where(hit, sc, 0.0), axis=0, keepdims=True))
        cur = jnp.where(hit, neg, cur)
        multi = multi + jnp.where(hit, 1.0, 0.0)
        hits.append(hit)
    base = jnp.concatenate([run_sc[...]] * (tm // LANES), axis=1)
    before = jnp.dot(multi.astype(BF16), upper_ref[...], preferred_element_type=F32) + base
    rank_rows = [jnp.sum(jnp.where(hit, before, 0.0), axis=0, keepdims=True) for hit in hits]
    w = jnp.concatenate(w_rows, axis=0)
    idx_ref[...] = jnp.concatenate(idx_rows, axis=0)
    w_ref[...] = (w / jnp.sum(w, axis=0, keepdims=True) * ROUTED_SCALE).T
    rank_ref[...] = jnp.concatenate(rank_rows, axis=0).astype(I32)
    run_sc[...] = run_sc[...] + jnp.dot(multi.astype(BF16), jnp.ones((tm, LANES), BF16), preferred_element_type=F32)
    cnt_ref[...] = run_sc[...]


def _router(tokens, router_w, router_bias):
    t, d = tokens.shape
    tm = ROUTE_TOKENS
    assert t % tm == 0
    wt = router_w.T
    bias = jnp.broadcast_to(router_bias.astype(F32)[:, None], (N_EXPERTS, LANES))
    r = jnp.arange(tm)
    upper = (r[:, None] < r[None, :]).astype(BF16)
    col = pl.BlockSpec((TOP_K, tm), lambda i: (0, i))
    full = lambda a: pl.BlockSpec(a.shape, lambda i: (0,) * a.ndim)
    vmem = 2 * (tm * d * 4 + wt.size * 4) + 40 * N_EXPERTS * tm * 4
    return pl.pallas_call(
        _router_kernel, grid=(t // tm,),
        in_specs=[pl.BlockSpec((tm, d), lambda i: (i, 0)), full(wt), full(bias), full(upper)],
        out_specs=[col, pl.BlockSpec((tm, TOP_K), lambda i: (i, 0)), col,
                   pl.BlockSpec((N_EXPERTS, LANES), lambda i: (0, 0))],
        out_shape=[jax.ShapeDtypeStruct((TOP_K, t), I32), jax.ShapeDtypeStruct((t, TOP_K), F32),
                   jax.ShapeDtypeStruct((TOP_K, t), I32), jax.ShapeDtypeStruct((N_EXPERTS, LANES), F32)],
        scratch_shapes=[pltpu.VMEM((N_EXPERTS, LANES), F32)],
        compiler_params=_params(("arbitrary",), vmem), name="moe_router",
    )(tokens, wt, bias, upper)


def _positions_kernel(idx_ref, rank_ref, start_ref, pos_ref):
    tm = idx_ref.shape[1]
    eidx = lax.broadcasted_iota(I32, (N_EXPERTS, tm), 0)
    start = jnp.concatenate([start_ref[...]] * (tm // LANES), axis=1)
    rows = [jnp.sum(jnp.where(eidx == idx_ref[k:k + 1, :], start, 0), axis=0, keepdims=True) for k in range(TOP_K)]
    pos_ref[...] = jnp.concatenate(rows, axis=0) + rank_ref[...]


def _positions(idx, rank, pad_start):
    t = idx.shape[1]
    tm = ROUTE_TOKENS
    start = jnp.broadcast_to(pad_start.astype(I32)[:, None], (N_EXPERTS, LANES))
    col = pl.BlockSpec((TOP_K, tm), lambda i: (0, i))
    return pl.pallas_call(
        _positions_kernel, grid=(t // tm,),
        in_specs=[col, col, pl.BlockSpec((N_EXPERTS, LANES), lambda i: (0, 0))], out_specs=col,
        out_shape=jax.ShapeDtypeStruct((TOP_K, t), I32),
        compiler_params=_params(("parallel",), 8 * N_EXPERTS * tm * 4), name="moe_positions",
    )(idx, rank, start)


def _moe_plan(counts, t):
    n_blocks = -(-(t * TOP_K) // MOE_ROWS) + N_EXPERTS
    c = counts.astype(I32)
    padded = (c + MOE_ROWS - 1) // MOE_ROWS * MOE_ROWS
    pad_end = jnp.cumsum(padded)
    pad_start = pad_end - padded
    block_e = jnp.minimum(jnp.searchsorted(pad_end, jnp.arange(n_blocks, dtype=I32) * MOE_ROWS, side='right'),
                          N_EXPERTS - 1).astype(I32)
    n_valid = (pad_end[-1:] // MOE_ROWS).astype(I32)
    return pad_start, block_e, n_valid, n_blocks * MOE_ROWS


def _row_copy(src, dst, sem):
    return pltpu.make_async_copy(src, dst, sem)


def _dispatch_kernel(pos_ref, x_ref, buf_in_ref, buf_ref, sem):
    del buf_in_ref
    tm = x_ref.shape[0]

    def issue(g, carry):
        base = pl.multiple_of(g * SUBLANES, SUBLANES)
        for j in range(SUBLANES):
            for k in range(TOP_K):
                _row_copy(x_ref.at[pl.ds(base + j, 1)], buf_ref.at[pl.ds(pos_ref[k, base + j], 1)],
                          sem).start(priority=k % DMA_PRIORITIES)
        return carry

    lax.fori_loop(0, tm // SUBLANES, issue, 0)
    for k in range(TOP_K):
        _row_copy(x_ref, buf_ref.at[pl.ds(0, tm)], sem).wait()


def _dispatch(pos, tokens, cap):
    t, d = tokens.shape
    tm = ROUTE_TOKENS
    zeros = jnp.zeros((cap, d), tokens.dtype)
    return pl.pallas_call(
        _dispatch_kernel, grid=(t // tm,),
        in_specs=[pl.BlockSpec((TOP_K, tm), lambda i: (0, i), memory_space=pltpu.SMEM),
                  pl.BlockSpec((tm, d), lambda i: (i, 0)),
                  pl.BlockSpec(memory_space=pl.ANY)],
        out_specs=pl.BlockSpec(memory_space=pl.ANY),
        out_shape=jax.ShapeDtypeStruct((cap, d), tokens.dtype),
        scratch_shapes=[pltpu.SemaphoreType.DMA(())],
        input_output_aliases={2: 0},
        compiler_params=_params(("arbitrary",), 2 * tm * d * 4, has_side_effects=True),
        name="moe_dispatch",
    )(pos, tokens, zeros)


def _experts_kernel(be_ref, nv_ref, x_ref, wgu_ref, wdn_ref, o_ref, wgu_sc, wdn_sc):
    i = pl.program_id(0)

    @pl.when(i < nv_ref[0])
    def _():
        @pl.when(jnp.logical_or(i == 0, be_ref[i] != be_ref[jnp.maximum(i - 1, 0)]))
        def _():
            wgu_sc[...] = wgu_ref[0, 0].astype(BF16)
            wdn_sc[...] = wdn_ref[0, 0].astype(BF16)

        for part in range(MOE_ROWS // MOE_SUB_ROWS):
            rows = pl.ds(part * MOE_SUB_ROWS, MOE_SUB_ROWS)
            x = _unpack_pairs(x_ref[rows, :]).astype(BF16)
            o_ref[rows, :] = _pack_pairs(_swiglu_rows(x, wgu_sc[...], wdn_sc[...]))

    @pl.when(i >= nv_ref[0])
    def _():
        o_ref[...] = jnp.zeros(o_ref.shape, U32)


def _moe_experts(xb, block_e, n_valid, w_gu, w_down, layer):
    cap, dp = xb.shape
    d = 2 * dp
    gu = w_gu.shape[3]
    ed = w_down.shape[2]
    live = lambda i, be, nv: jnp.maximum(jnp.minimum(i, nv[0] - 1), 0)
    grid_spec = pltpu.PrefetchScalarGridSpec(
        num_scalar_prefetch=2, grid=(cap // MOE_ROWS,),
        in_specs=[pl.BlockSpec((MOE_ROWS, dp), lambda i, be, nv: (live(i, be, nv), 0)),
                  pl.BlockSpec((1, 1, d, gu), lambda i, be, nv: (layer, be[i], 0, 0)),
                  pl.BlockSpec((1, 1, ed, d), lambda i, be, nv: (layer, be[i], 0, 0))],
        out_specs=pl.BlockSpec((MOE_ROWS, dp), lambda i, be, nv: (i, 0)),
        scratch_shapes=[pltpu.VMEM((d, gu), BF16), pltpu.VMEM((ed, d), BF16)])
    vmem = 2 * (2 * MOE_ROWS * dp * 4 + d * gu * 4 + ed * d * 4) + (d * gu + ed * d) * 2 + 6 * MOE_ROWS * d * 4
    return pl.pallas_call(
        _experts_kernel, grid_spec=grid_spec, out_shape=jax.ShapeDtypeStruct((cap, dp), U32),
        compiler_params=_params(("arbitrary",), vmem), name="moe_experts",
    )(block_e, n_valid, xb, w_gu, w_down)


def _ffn_out_kernel(pos_ref, w_ref, x_ref, ff_ref, yb_ref, wgu_ref, wdn_ref, gate_ref, g_ref, b_ref, xo_ref,
                    rows_sc, sem):
    tm = x_ref.shape[1]

    def issue(g, carry):
        base = pl.multiple_of(g * SUBLANES, SUBLANES)
        for j in range(SUBLANES):
            for k in range(TOP_K):
                _row_copy(yb_ref.at[pl.ds(pos_ref[k, base + j], 1)], rows_sc.at[k, pl.ds(base + j, 1)],
                          sem).start(priority=k % DMA_PRIORITIES)
        return carry

    lax.fori_loop(0, tm // SUBLANES, issue, 0)
    out = _swiglu_rows(ff_ref[0].astype(BF16), wgu_ref[...], wdn_ref[...])
    for k in range(TOP_K):
        _row_copy(yb_ref.at[pl.ds(0, tm)], rows_sc.at[k], sem).wait()
    for k in range(TOP_K):
        out = out + w_ref[:, k:k + 1] * _unpack_pairs(rows_sc[k])
    xo_ref[0] = _layer_norm(DN_ALPHA * x_ref[0] + gate_ref[0] * out, g_ref[0], b_ref[0])


def _ffn_out(x, ff, yb, pos, w, tile0, sh_gu, sh_down, gate, ln_g, ln_b):
    bsz, n, d = x.shape
    tm = min(ROUTE_TOKENS, n)
    nt = n // tm
    row = pl.BlockSpec((1, tm, d), lambda b, i: (b, i, 0))
    vecs = [gate, ln_g, ln_b]
    in_specs = [pl.BlockSpec((TOP_K, tm), lambda b, i: (0, tile0 + b * nt + i), memory_space=pltpu.SMEM),
                pl.BlockSpec((tm, TOP_K), lambda b, i: (tile0 + b * nt + i, 0)),
                row, row, pl.BlockSpec(memory_space=pl.ANY),
                pl.BlockSpec(sh_gu.shape, lambda b, i: (0, 0)), pl.BlockSpec(sh_down.shape, lambda b, i: (0, 0))]
    in_specs += [_bcast_spec(a) for a in vecs]
    vmem = TOP_K * tm * d * 2 + 2 * (3 * tm * d * 4 + sh_gu.size * 2 + sh_down.size * 2) + 6 * tm * d * 4
    return pl.pallas_call(
        _ffn_out_kernel, grid=(bsz, nt), in_specs=in_specs, out_specs=row,
        out_shape=jax.ShapeDtypeStruct((bsz, n, d), F32),
        scratch_shapes=[pltpu.VMEM((TOP_K, tm, d // 2), U32), pltpu.SemaphoreType.DMA(())],
        compiler_params=_params(("arbitrary", "arbitrary"), vmem), name="moe_combine_ffn_out",
    )(pos, w, x, ff, yb, sh_gu, sh_down, *vecs)


def _moe_dispatch_experts(tokens, packed, router_w, router_bias, w_gu, w_down, layer):
    t = tokens.shape[0]
    idx, w, rank, counts = _router(tokens, router_w, router_bias)
    pad_start, block_e, n_valid, cap = _moe_plan(counts[:, 0], t)
    pos = _positions(idx, rank, pad_start)
    xb = _dispatch(pos, packed, cap)
    return _moe_experts(xb, block_e, n_valid, w_gu, w_down, layer), pos, w


def _rope_tables(n):
    rows = n // GRID_W
    row = jnp.repeat(jnp.arange(rows), GRID_W).astype(F32)
    col = jnp.tile(jnp.arange(GRID_W), rows).astype(F32)
    half = MLA_ROPE // 2
    inv = ROPE_THETA ** (-jnp.arange(0, half, 2, dtype=F32) / half)
    ang = jnp.concatenate([row[:, None] * inv, col[:, None] * inv], -1)
    return jnp.cos(ang), jnp.sin(ang)


def _filter_kernel(fr_ref, w1_ref, b1_ref, f1_ref, w2_ref, b2_ref, f2_ref, w3_ref, dl_ref, k_ref, l1_ref, *, n):
    i = pl.program_id(0)
    tr = k_ref.shape[0]
    hp = lax.Precision.HIGHEST
    tap = i * tr + lax.broadcasted_iota(I32, (tr, 1), 0)
    lag = jnp.where(tap < n, tap, 2 * n - tap).astype(F32)
    t = lag * (1.0 / (n - 1))
    ang = (2.0 * math.pi / n) * lag * fr_ref[...]
    lane = lax.broadcasted_iota(I32, (tr, LANES), 1)
    z = jnp.where(lane == 0, t, jnp.where(lane <= HY_BANDS, jnp.cos(ang),
                                          jnp.where(lane <= 2 * HY_BANDS, -jnp.sin(ang), 0.0)))
    h = jnp.sin(f1_ref[...] * (jnp.dot(z, w1_ref[...], precision=hp, preferred_element_type=F32) + b1_ref[...]))
    h = jnp.sin(f2_ref[...] * (jnp.dot(h, w2_ref[...], precision=hp, preferred_element_type=F32) + b2_ref[...]))
    h = jnp.dot(h, w3_ref[...], precision=hp, preferred_element_type=F32)
    hsel = jnp.where(tap < n, h[:, :HY_WIDTH], h[:, HY_WIDTH:])
    k = jnp.where(tap == n, 0.0, hsel * jnp.exp(-t * dl_ref[...]))
    k_ref[...] = k

    @pl.when(i == 0)
    def _():
        l1_ref[...] = jnp.zeros(l1_ref.shape, F32)

    l1_ref[...] = l1_ref[...] + jnp.sum(jnp.abs(k), axis=0, keepdims=True)


def _hyena_filter(n, w1, b1, f1, w2, b2, f2, w3):
    nn = 2 * n
    tr = min(FILT_ROWS, nn)
    assert nn % tr == 0
    emb, ffn = w1.shape
    fr = jnp.linspace(1e-4, HY_BANDS - 1, HY_BANDS, dtype=F32)
    fr_l = jnp.concatenate([jnp.zeros((1,), F32), fr, fr, jnp.zeros((LANES - emb,), F32)])[None]
    w1p = jnp.concatenate([w1, jnp.zeros((LANES - emb, ffn), F32)], 0)
    deltas = jnp.abs(jnp.linspace(HY_DECAY_MIN, HY_DECAY_MAX, HY_WIDTH, dtype=F32))[None]
    ops = [fr_l, w1p, b1[None], f1[None], w2, b2[None], f2[None], w3, deltas]
    full = lambda a: pl.BlockSpec(a.shape, lambda i: (0,) * a.ndim)
    return pl.pallas_call(
        functools.partial(_filter_kernel, n=n), grid=(nn // tr,), in_specs=[full(a) for a in ops],
        out_specs=[pl.BlockSpec((tr, HY_WIDTH), lambda i: (i, 0)), pl.BlockSpec((1, HY_WIDTH), lambda i: (0, 0))],
        out_shape=[jax.ShapeDtypeStruct((nn, HY_WIDTH), F32), jax.ShapeDtypeStruct((1, HY_WIDTH), F32)],
        compiler_params=_params(("arbitrary",), 16 * tr * 2 * HY_WIDTH * 4), name="hyena_filter",
    )(*ops)


def _split(a):
    hi = a.astype(BF16)
    return hi, (a - hi.astype(F32)).astype(BF16)


def _dot3(a_hi, a_lo, x):
    x_hi, x_lo = _split(x)
    return (jnp.dot(a_hi, x_hi, preferred_element_type=F32) + jnp.dot(a_lo, x_hi, preferred_element_type=F32)
            + jnp.dot(a_hi, x_lo, preferred_element_type=F32))


def _cis(num, den):
    ang = (2.0 * math.pi / den) * (num % den).astype(F32)
    return jnp.cos(ang), jnp.sin(ang)


def _dft_outer_kernel(f_hi_ref, f_lo_ref, x_ref, o_ref):
    o_ref[0] = _dot3(f_hi_ref[...], f_lo_ref[...], x_ref[0])


def _dft_outer(x, f_hi, f_lo):
    bsz, k, m = x.shape
    r = f_hi.shape[0]
    tn = min(DFT_COLS, m)
    assert m % tn == 0
    vmem = 2 * (2 * f_hi.size * 2 + k * tn * 4 + r * tn * 4) + 3 * (k + r) * tn * 4
    return pl.pallas_call(
        _dft_outer_kernel, grid=(bsz, m // tn),
        in_specs=[pl.BlockSpec(f_hi.shape, lambda b, j: (0, 0)), pl.BlockSpec(f_lo.shape, lambda b, j: (0, 0)),
                  pl.BlockSpec((1, k, tn), lambda b, j: (b, 0, j))],
        out_specs=pl.BlockSpec((1, r, tn), lambda b, j: (b, 0, j)),
        out_shape=jax.ShapeDtypeStruct((bsz, r, m), F32),
        compiler_params=_params(("parallel", "parallel"), vmem), name="hyena_dft_outer",
    )(f_hi, f_lo, x)


def _dft_inner_kernel(m_hi_ref, m_lo_ref, mt_hi_ref, mt_lo_ref, a_ref, h_ref, o_ref, *, conv):
    n2 = a_ref.shape[3]
    for bi in range(a_ref.shape[0]):
        x = a_ref[bi, :, 0].reshape(2 * n2, a_ref.shape[4])
        y = _dot3(m_hi_ref[0], m_lo_ref[0], x)
        if conv:
            h = h_ref[0, :, 0].reshape(2 * n2, h_ref.shape[4])
            yr, yi, hr, hi = y[:n2], y[n2:], h[:n2], h[n2:]
            prod = jnp.concatenate([yr * hr - yi * hi, yr * hi + yi * hr], axis=0)
            y = _dot3(mt_hi_ref[0], mt_lo_ref[0], prod)
        o_ref[bi, :, 0] = y.reshape(2, n2, y.shape[1])


def _dft_inner(a, h, mats, conv):
    bsz, _, n1, n2, c = a.shape
    bt = DFT_BATCH if bsz % DFT_BATCH == 0 else 1
    blk = lambda rows, sel: pl.BlockSpec((rows, 2, 1, n2, c), sel)
    mat = pl.BlockSpec((1, 2 * n2, 2 * n2), lambda k, b: (k, 0, 0))
    vmem = 2 * (4 * 4 * n2 * n2 * 2 + (2 * bt + 1) * 2 * n2 * c * 4) + 8 * bt * 2 * n2 * c * 4
    return pl.pallas_call(
        functools.partial(_dft_inner_kernel, conv=conv), grid=(n1, bsz // bt),
        in_specs=[mat, mat, mat, mat, blk(bt, lambda k, b: (b, 0, k, 0, 0)), blk(1, lambda k, b: (0, 0, k, 0, 0))],
        out_specs=blk(bt, lambda k, b: (b, 0, k, 0, 0)), out_shape=jax.ShapeDtypeStruct(a.shape, F32),
        compiler_params=_params(("parallel", "arbitrary"), vmem),
        name="hyena_dft_inner_conv" if conv else "hyena_dft_inner",
    )(*mats, a, h)


def _dft_final_kernel(fd_hi_ref, fd_lo_ref, a_ref, z_ref, x0_ref, skip_ref, l1_ref, o_ref):
    y = _dot3(fd_hi_ref[...], fd_lo_ref[...], a_ref[0])
    o_ref[0] = (y / l1_ref[...] + z_ref[0] * skip_ref[...]) * x0_ref[0]


def _dft_final(a, z, x0, skip_t, l1_t, fd_hi, fd_lo):
    bsz, r2, m = a.shape
    k = fd_hi.shape[0]
    tn = min(DFT_COLS, m)
    row = pl.BlockSpec((1, k, tn), lambda b, j: (b, 0, j))
    vmem = 2 * (2 * fd_hi.size * 2 + r2 * tn * 4 + 3 * k * tn * 4) + 3 * (k + r2) * tn * 4
    return pl.pallas_call(
        _dft_final_kernel, grid=(bsz, m // tn),
        in_specs=[pl.BlockSpec(fd_hi.shape, lambda b, j: (0, 0)), pl.BlockSpec(fd_lo.shape, lambda b, j: (0, 0)),
                  pl.BlockSpec((1, r2, tn), lambda b, j: (b, 0, j)), row, row,
                  pl.BlockSpec((1, tn), lambda b, j: (0, 0)), pl.BlockSpec((1, tn), lambda b, j: (0, 0))],
        out_specs=row, out_shape=jax.ShapeDtypeStruct((bsz, k, m), F32),
        compiler_params=_params(("parallel", "parallel"), vmem), name="hyena_dft_final",
    )(fd_hi, fd_lo, a, z, x0, skip_t, l1_t)


def _hyena_long_conv(z, x0, k, l1, skip):
    bsz, n, c = z.shape
    n2 = DFT_INNER
    nn = 2 * n
    n1 = nn // n2
    assert nn == n1 * n2 and n1 % 2 == 0
    half = n1 // 2
    m = n2 * c
    j1 = jnp.arange(n1)
    ca, sa = _cis(j1[:, None] * j1[None, :], n1)
    fa = jnp.concatenate([ca, -sa], axis=0)
    fd = jnp.concatenate([ca, -sa], axis=1)[:half] / nn
    j2 = jnp.arange(n2)
    cb, sb = _cis(j2[None, None, :] * (n1 * j2[None, :, None] + j1[:, None, None]), nn)
    mb = jnp.concatenate([jnp.concatenate([cb, sb], 2), jnp.concatenate([-sb, cb], 2)], 1)
    mats = _split(mb) + _split(jnp.swapaxes(mb, 1, 2))
    fa_hi, fa_lo = _split(fa)
    hk = _dft_outer(k.reshape(1, n1, m), fa_hi, fa_lo).reshape(1, 2, n1, n2, c)
    hk = _dft_inner(hk, hk, mats, conv=False)
    a = _dft_outer(z.reshape(bsz, half, m), fa_hi[:, :half], fa_lo[:, :half]).reshape(bsz, 2, n1, n2, c)
    a = _dft_inner(a, hk, mats, conv=True).reshape(bsz, 2 * n1, m)
    reps = (1, min(DFT_COLS, m) // c)
    out = _dft_final(a, z.reshape(bsz, half, m), x0.reshape(bsz, half, m), jnp.tile(skip.reshape(1, c), reps),
                     jnp.tile(l1, reps), *_split(fd))
    return out.reshape(bsz, n, c)


def _short_conv_kernel(f_hi_ref, f_lo_ref, fi_hi_ref, fi_lo_ref, z_ref, x0_ref, k_ref, skip_ref, l1_ref, o_ref):
    n = z_ref.shape[1]
    nn = 2 * n
    z = z_ref[0]
    hk = _dot3(f_hi_ref[...], f_lo_ref[...], k_ref[...])
    zs = _dot3(f_hi_ref[:, :n], f_lo_ref[:, :n], z)
    zr, zi, hr, hi = zs[:nn], zs[nn:], hk[:nn], hk[nn:]
    prod = jnp.concatenate([zr * hr - zi * hi, zr * hi + zi * hr], axis=0)
    o_ref[0] = (_dot3(fi_hi_ref[...], fi_lo_ref[...], prod) / l1_ref[...] + z * skip_ref[...]) * x0_ref[0]


def _hyena_short_conv(z, x0, k, l1, skip):
    bsz, n, c = z.shape
    nn = 2 * n
    idx = jnp.arange(nn)
    cf, sf = _cis(idx[:, None] * idx[None, :], nn)
    f = jnp.concatenate([cf, -sf], axis=0)
    fi = jnp.concatenate([cf, -sf], axis=1)[:n] / nn
    full = lambda a: pl.BlockSpec(a.shape, lambda b: (0,) * a.ndim)
    row = pl.BlockSpec((1, n, c), lambda b: (b, 0, 0))
    ops = _split(f) + _split(fi)
    vmem = 2 * (sum(a.size * 2 for a in ops) + 3 * n * c * 4 + nn * c * 4) + 12 * 2 * nn * c * 4
    return pl.pallas_call(
        _short_conv_kernel, grid=(bsz,),
        in_specs=[full(a) for a in ops] + [row, row, full(k), pl.BlockSpec((1, c), lambda b: (0, 0)),
                                           pl.BlockSpec((1, c), lambda b: (0, 0))],
        out_specs=row, out_shape=jax.ShapeDtypeStruct((bsz, n, c), F32),
        compiler_params=_params(("parallel",), vmem), name="hyena_short_conv",
    )(*ops, z, x0, k, skip.reshape(1, c), l1)


def _hyena_sequence(z, x0, filt, skip):
    n = z.shape[1]
    k, l1 = _hyena_filter(n, *filt)
    conv = _hyena_short_conv if n <= HY_DIRECT_MAX else _hyena_long_conv
    return conv(z, x0, k, l1, skip)


def _rms(x, g):
    return x * lax.rsqrt(jnp.mean(x * x, -1, keepdims=True) + RMS_EPS) * g


def _mla_in_kernel(xc_ref, xp_ref, xn_ref, sh_ref, sc_ref, why_ref, cw_ref, cb_ref, wql_ref, qg_ref, wqa_ref, wqb_ref,
                   wkvl_ref, kvg_ref, wk_ref, wv_ref, wpa_ref, wpb_ref, qa_ref, qb_ref, ka_ref, kb_ref,
                   x0_ref, z_ref, q_ref, k_ref, vt_ref):
    tm = xc_ref.shape[1]
    rows = jnp.concatenate([xp_ref[0], xc_ref[0], xn_ref[0]], axis=0)
    h = (rows * (1.0 + sc_ref[0]) + sh_ref[0]).astype(BF16)
    hc = h[SUBLANES:SUBLANES + tm]
    y = _conv3(jnp.dot(h, why_ref[...], preferred_element_type=F32), cw_ref, cb_ref, tm)
    x0_ref[0] = y[:, :HY_WIDTH]
    z_ref[0] = y[:, 2 * HY_WIDTH:] * y[:, HY_WIDTH:2 * HY_WIDTH]
    ql = _rms(jnp.dot(hc, wql_ref[...], preferred_element_type=F32), qg_ref[...]).astype(BF16)
    nh = q_ref.shape[1]
    qa = jnp.concatenate([qa_ref[...]] * nh, axis=1)
    qb = jnp.concatenate([qb_ref[...]] * nh, axis=1)
    q = (jnp.dot(ql, wqa_ref[...], preferred_element_type=F32) * qa
         + jnp.dot(ql, wqb_ref[...], preferred_element_type=F32) * qb).astype(BF16)
    kvl = _rms(jnp.dot(hc, wkvl_ref[...], preferred_element_type=F32), kvg_ref[...]).astype(BF16)
    kn = jnp.dot(kvl, wk_ref[...], preferred_element_type=F32)
    v_t = jnp.dot(kvl, wv_ref[...], preferred_element_type=F32).T
    kpe = (jnp.dot(hc, wpa_ref[...], preferred_element_type=F32) * ka_ref[...]
           + jnp.dot(hc, wpb_ref[...], preferred_element_type=F32) * kb_ref[...])
    ones = jnp.ones((ATT_ONES_ROWS, tm), BF16)
    for hd in range(nh):
        q_ref[0, hd] = q[:, hd * MLA_SLAB:(hd + 1) * MLA_SLAB]
        k_ref[0, hd] = (kn[:, hd * MLA_SLAB:(hd + 1) * MLA_SLAB] + kpe).astype(BF16)
        vt_ref[0, hd, :MLA_V, :] = v_t[hd * MLA_V:(hd + 1) * MLA_V].astype(BF16)
        vt_ref[0, hd, MLA_V:, :] = ones


def _rot_cols(w_pe):
    ev, od = w_pe[..., 0::2], w_pe[..., 1::2]
    return jnp.concatenate([ev, od], -1), jnp.concatenate([-od, ev], -1)


def _mla_weights(p):
    w_in = p['w_in']
    kin = w_in.shape[0]
    wq = p['w_qb'].reshape(MLA_Q_RANK, MLA_HEADS, MLA_QK)
    qa_pe, qb_pe = _rot_cols(wq[..., MLA_NOPE:])
    zq = jnp.zeros((MLA_Q_RANK, MLA_HEADS, MLA_SLAB - MLA_QK), F32)
    w_qa = jnp.concatenate([wq[..., :MLA_NOPE], qa_pe, zq], -1).reshape(MLA_Q_RANK, MLA_HEADS * MLA_SLAB)
    w_qb = jnp.concatenate([jnp.zeros_like(wq[..., :MLA_NOPE]), qb_pe, zq], -1)
    w_qb = w_qb.reshape(MLA_Q_RANK, MLA_HEADS * MLA_SLAB)
    wkv = p['w_kvb'].reshape(MLA_KV_RANK, MLA_HEADS, MLA_NOPE + MLA_V)
    w_k = jnp.concatenate([wkv[..., :MLA_NOPE], jnp.zeros((MLA_KV_RANK, MLA_HEADS, MLA_SLAB - MLA_NOPE), F32)], -1)
    w_k = w_k.reshape(MLA_KV_RANK, MLA_HEADS * MLA_SLAB)
    w_v = wkv[..., MLA_NOPE:].reshape(MLA_KV_RANK, MLA_HEADS * MLA_V)
    pa, pb = _rot_cols(w_in[:, OFF_KPE:])
    left, right = jnp.zeros((kin, MLA_NOPE), F32), jnp.zeros((kin, MLA_SLAB - MLA_QK), F32)
    w_pa = jnp.concatenate([left, pa, right], -1)
    w_pb = jnp.concatenate([left, pb, right], -1)
    bf = lambda a: a.astype(BF16)
    return dict(w_hy=bf(w_in[:, :OFF_Q]), conv_w=p['conv_w'], conv_b=p['conv_b'][None],
                w_ql=bf(w_in[:, OFF_Q:OFF_KV]), q_g=p['q_norm'][None], w_qa=bf(w_qa), w_qb=bf(w_qb),
                w_kvl=bf(w_in[:, OFF_KV:OFF_KPE]), kv_g=p['kv_norm'][None], w_k=bf(w_k), w_v=bf(w_v),
                w_pa=bf(w_pa), w_pb=bf(w_pb))


def _rope_slabs(n, rotate):
    one = jnp.ones((n, MLA_NOPE), F32)
    zero = jnp.zeros((n, MLA_SLAB - MLA_QK), F32)
    if rotate:
        cos, sin = _rope_tables(n)
    else:
        cos, sin = jnp.ones((n, MLA_ROPE // 2), F32), jnp.zeros((n, MLA_ROPE // 2), F32)
    return (jnp.concatenate([one, cos, cos, zero], -1), jnp.concatenate([jnp.zeros_like(one), sin, sin, zero], -1))


def _mla_in_proj(x, shift, scale, w, rotate):
    bsz, n, kin = x.shape
    tm = min(ROW_TILE, n)
    assert n % tm == 0
    ca, sb = _rope_slabs(n, rotate)
    s = MLA_SCALE * LOG2E
    tabs = [ca * s, sb * s, ca, sb]
    consts = [w['w_hy'], w['conv_w'], w['conv_b'], w['w_ql'], w['q_g'], w['w_qa'], w['w_qb'], w['w_kvl'], w['kv_g'],
              w['w_k'], w['w_v'], w['w_pa'], w['w_pb']]
    full = lambda a: pl.BlockSpec(a.shape, lambda b, i: (0,) * a.ndim)
    row = lambda c: pl.BlockSpec((1, tm, c), lambda b, i: (b, i, 0))
    head = pl.BlockSpec((1, MLA_HEADS, tm, MLA_SLAB), lambda b, i: (b, 0, i, 0))
    dva = MLA_V + ATT_ONES_ROWS
    vmem = 2 * (tm * kin * 4 + sum(a.size * a.dtype.itemsize for a in consts) + 2 * tm * HY_WIDTH * 4
                + 3 * MLA_HEADS * tm * MLA_SLAB * 2 + 4 * tm * MLA_SLAB * 4) + 8 * (tm + 16) * 3 * HY_WIDTH * 4
    return pl.pallas_call(
        _mla_in_kernel, grid=(bsz, n // tm),
        in_specs=_halo_specs(n, tm, kin) + [_bcast_spec(shift), _bcast_spec(scale)] + [full(a) for a in consts]
        + [pl.BlockSpec((tm, MLA_SLAB), lambda b, i: (i, 0))] * 4,
        out_specs=[row(HY_WIDTH), row(HY_WIDTH), head, head,
                   pl.BlockSpec((1, MLA_HEADS, dva, tm), lambda b, i: (b, 0, 0, i))],
        out_shape=[jax.ShapeDtypeStruct((bsz, n, HY_WIDTH), F32)] * 2
        + [jax.ShapeDtypeStruct((bsz, MLA_HEADS, n, MLA_SLAB), BF16)] * 2
        + [jax.ShapeDtypeStruct((bsz, MLA_HEADS, dva, n), BF16)],
        compiler_params=_params(("parallel", "parallel"), vmem), name="mla_in_proj",
    )(x, x, x, shift, scale, *consts, *tabs)


def _mixer_hyena_mla(x, ctx, sh_l, sc_l, sh_c, sc_c, p):
    n, nc = x.shape[1], ctx.shape[1]
    w = _mla_weights(p)
    x0_l, z_l, q_l, k_l, vt_l = _mla_in_proj(x, sh_l, sc_l, w, rotate=True)
    x0_c, z_c, q_c, k_c, vt_c = _mla_in_proj(ctx, sh_c, sc_c, w, rotate=False)
    k_all = jnp.concatenate([k_c, k_l], 2)
    vt_all = jnp.concatenate([vt_c, vt_l], 3)
    att_l = _attention(q_l, k_all, vt_all, tq=min(ATT_Q_TILE, n), tk=_key_tile(nc + n))
    att_c = _attention(q_c, k_c, vt_c, tq=nc, tk=_key_tile(nc))
    filt = (p['filt_w1'], p['filt_b1'], p['filt_freq1'], p['filt_w2'], p['filt_b2'], p['filt_freq2'], p['filt_w3'])
    hyo_l = _hyena_sequence(z_l, x0_l, filt, p['skip'])
    hyo_c = _hyena_sequence(z_c, x0_c, filt, p['skip'])
    return (hyo_l, att_l), (hyo_c, att_c)


def _halo_specs(n, tm, k):
    nb = n // SUBLANES
    per = tm // SUBLANES
    return [pl.BlockSpec((1, tm, k), lambda b, i: (b, i, 0)),
            pl.BlockSpec((1, SUBLANES, k), lambda b, i: (b, jnp.maximum(i * per - 1, 0), 0)),
            pl.BlockSpec((1, SUBLANES, k), lambda b, i: (b, jnp.minimum((i + 1) * per, nb - 1), 0))]


def _conv3(u, cw_ref, cb_ref, tm):
    i = pl.program_id(1)
    rows = lax.broadcasted_iota(I32, (u.shape[0], 1), 0)
    inside = jnp.logical_and(jnp.logical_or(rows >= SUBLANES, i > 0),
                             jnp.logical_or(rows < tm + SUBLANES, i < pl.num_programs(1) - 1))
    u = jnp.where(inside, u, 0.0)
    prev = pltpu.roll(u, 1, 0)[SUBLANES:SUBLANES + tm]
    nxt = pltpu.roll(u, u.shape[0] - 1, 0)[SUBLANES:SUBLANES + tm]
    return cw_ref[0:1, :] * prev + cw_ref[1:2, :] * u[SUBLANES:SUBLANES + tm] + cw_ref[2:3, :] * nxt + cb_ref[...]


def _ssd_in_kernel(xc_ref, xp_ref, xn_ref, sh_ref, sc_ref, wz_ref, wx_ref, wdt_ref, cw_ref, cb_ref, dtb_ref,
                   z_ref, xs_ref, b_ref, c_ref, dt_ref):
    tm = xc_ref.shape[1]
    rows = jnp.concatenate([xp_ref[0], xc_ref[0], xn_ref[0]], axis=0)
    h = (rows * (1.0 + sc_ref[0]) + sh_ref[0]).astype(BF16)
    hc = h[SUBLANES:SUBLANES + tm]
    z_ref[0] = jnp.dot(hc, wz_ref[...], preferred_element_type=F32)
    y = _conv3(jnp.dot(h, wx_ref[...], preferred_element_type=F32), cw_ref, cb_ref, tm)
    xbc = y * jax.nn.sigmoid(y)
    xs_ref[0] = xbc[:, :SSD_INNER]
    b_ref[0] = xbc[:, SSD_INNER:SSD_INNER + SSD_BC]
    c_ref[0] = xbc[:, SSD_INNER + SSD_BC:]
    dt = jnp.dot(hc, wdt_ref[...], preferred_element_type=F32) + dtb_ref[...]
    dt_ref[0] = (jnp.maximum(dt, 0.0) + jnp.log1p(jnp.exp(-jnp.abs(dt)))).T


def _ssd_in_proj(x, shift, scale, w_z, w_xbc, w_dt, conv_w, conv_b, dt_bias):
    bsz, n, k = x.shape
    tm = min(ROW_TILE, n)
    assert n % tm == 0
    full = lambda a: pl.BlockSpec(a.shape, lambda b, i: (0,) * a.ndim)
    row = lambda c: pl.BlockSpec((1, tm, c), lambda b, i: (b, i, 0))
    nh2 = w_dt.shape[1]
    consts = [w_z, w_xbc, w_dt, conv_w, conv_b, dt_bias]
    widths = [SSD_INNER, SSD_INNER, SSD_BC, SSD_BC]
    vmem = 2 * (tm * k * 4 + sum(a.size * a.dtype.itemsize for a in consts) + tm * (sum(widths) + nh2) * 4) \
        + 6 * (tm + 2 * SUBLANES) * SSD_XBC * 4
    return pl.pallas_call(
        _ssd_in_kernel, grid=(bsz, n // tm),
        in_specs=_halo_specs(n, tm, k) + [_bcast_spec(shift), _bcast_spec(scale)] + [full(a) for a in consts],
        out_specs=[row(c) for c in widths] + [pl.BlockSpec((1, nh2, tm), lambda b, i: (b, 0, i))],
        out_shape=[jax.ShapeDtypeStruct((bsz, n, c), F32) for c in widths]
        + [jax.ShapeDtypeStruct((bsz, nh2, n), F32)],
        compiler_params=_params(("parallel", "parallel"), vmem), name="ssd_in_proj",
    )(x, x, x, shift, scale, *consts)


def _ssd_chunk(x_ref, b_ref, c_ref, dt_ref, a_ref, st_sc, y_ref, reverse):
    q = SSD_CHUNK
    dt = dt_ref[0]
    a = dt * a_ref[0]
    si = lax.broadcasted_iota(I32, (q, q), 0)
    li = lax.broadcasted_iota(I32, (q, q), 1)
    incl = jnp.where((si >= li) if reverse else (si <= li), 1.0, 0.0)
    hp = lax.Precision.HIGHEST
    acs = jnp.dot(a, incl, precision=hp, preferred_element_type=F32)
    tot = jnp.dot(a, jnp.ones((q, LANES), F32), precision=hp, preferred_element_type=F32)
    e_in = jnp.exp(acs)
    w_end = jnp.exp(tot - acs) * dt
    e_tot = jnp.exp(tot)
    acs_t = acs.T
    e_in_t = e_in.T
    mask = (li >= si) if reverse else (li <= si)
    cmat = c_ref[0]
    cb = lax.dot_general(cmat.astype(BF16), b_ref[0].astype(BF16), (((1,), (1,)), ((), ())),
                         preferred_element_type=F32)
    bt = b_ref[0].T
    first = lax.broadcasted_iota(I32, (q, LANES), 1) < SSD_HEADDIM
    ys = []
    for pr in range(SSD_HPG // 2):
        lo, hi = pr * LANES, (pr + 1) * LANES
        x_pair = x_ref[0, :, lo:hi].astype(BF16)
        st_pair = st_sc[:, lo:hi]
        rhs = jnp.concatenate([x_pair, st_pair.astype(BF16)], axis=0)
        y2, s2 = [], []
        for r in (2 * pr, 2 * pr + 1):
            seg = acs_t[:, r:r + 1] - acs[r:r + 1, :]
            m = cb * jnp.exp(jnp.where(mask, seg, -jnp.inf)) * dt[r:r + 1, :]
            lhs = jnp.concatenate([m.astype(BF16), (cmat * e_in_t[:, r:r + 1]).astype(BF16)], axis=1)
            y2.append(jnp.dot(lhs, rhs, preferred_element_type=F32))
            btr = (bt * w_end[r:r + 1, :]).astype(BF16)
            s2.append(e_tot[r:r + 1, :] * st_pair + jnp.dot(btr, x_pair, preferred_element_type=F32))
        ys.append(jnp.where(first, y2[0], y2[1]))
        st_sc[:, lo:hi] = jnp.where(first, s2[0], s2[1])
    y_ref[0] = jnp.concatenate(ys, axis=1)


def _ssd_scan_kernel(xf_ref, bf_ref, cf_ref, dtf_ref, af_ref, s0f_ref, xr_ref, br_ref, cr_ref, dtr_ref, ar_ref,
                     s0r_ref, yf_ref, sf_ref, yr_ref, sr_ref, stf_sc, str_sc):
    ci = pl.program_id(2)

    @pl.when(ci == 0)
    def _():
        stf_sc[...] = s0f_ref[0, 0]
        str_sc[...] = s0r_ref[0, 0]

    _ssd_chunk(xf_ref, bf_ref, cf_ref, dtf_ref, af_ref, stf_sc, yf_ref, reverse=False)
    _ssd_chunk(xr_ref, br_ref, cr_ref, dtr_ref, ar_ref, str_sc, yr_ref, reverse=True)

    @pl.when(ci == pl.num_programs(2) - 1)
    def _():
        sf_ref[0, 0] = stf_sc[...]
        sr_ref[0, 0] = str_sc[...]


def _ssd_scan(xs, bm, cm, dt_t, a_rep, s0_f, s0_r):
    bsz, n, _ = xs.shape
    q = SSD_CHUNK
    nc = n // q
    assert n % q == 0 and q == LANES
    gw = SSD_HPG * SSD_HEADDIM
    state = pl.BlockSpec((1, 1, SSD_STATE, gw), lambda b, g, c: (b, g, 0, 0))

    def side(d):
        cc = (lambda c: nc - 1 - c) if d else (lambda c: c)
        specs = [pl.BlockSpec((1, q, gw), lambda b, g, c: (b, cc(c), g)),
                 pl.BlockSpec((1, q, SSD_STATE), lambda b, g, c: (b, cc(c), g)),
                 pl.BlockSpec((1, q, SSD_STATE), lambda b, g, c: (b, cc(c), g)),
                 pl.BlockSpec((1, SSD_HPG, q), lambda b, g, c: (b, d * SSD_GROUPS + g, cc(c))),
                 pl.BlockSpec((1, SSD_HPG, LANES), lambda b, g, c: (d * SSD_GROUPS + g, 0, 0)),
                 state]
        return specs, [pl.BlockSpec((1, q, gw), lambda b, g, c: (b, cc(c), g)), state]

    (in_f, out_f), (in_r, out_r) = side(0), side(1)
    shapes = [jax.ShapeDtypeStruct((bsz, n, SSD_INNER), F32),
              jax.ShapeDtypeStruct((bsz, SSD_GROUPS, SSD_STATE, gw), F32)]
    vmem = 4 * (2 * q * gw * 4 + 2 * q * SSD_STATE * 4 + 2 * SSD_STATE * gw * 4) + 2 * SSD_STATE * gw * 4 \
        + 128 * q * q * 4
    return pl.pallas_call(
        _ssd_scan_kernel, grid=(bsz, SSD_GROUPS, nc), in_specs=in_f + in_r, out_specs=out_f + out_r,
        out_shape=shapes + shapes,
        scratch_shapes=[pltpu.VMEM((SSD_STATE, gw), F32), pltpu.VMEM((SSD_STATE, gw), F32)],
        compiler_params=_params(("parallel", "parallel", "arbitrary"), vmem), name="ssd_scan",
    )(xs, bm, cm, dt_t, a_rep, s0_f, xs, bm, cm, dt_t, a_rep, s0_r)


def _ssd_out_kernel(x_ref, yf_ref, yb_ref, xs_ref, z_ref, d_ref, ng_ref, w_ref, *epilogue_refs):
    z = z_ref[0]
    y = (yf_ref[0] + yb_ref[0] + xs_ref[0] * d_ref[0]) * (z * jax.nn.sigmoid(z))
    gw = SSD_INNER // SSD_GROUPS
    parts = []
    for g in range(SSD_GROUPS):
        yg = y[:, g * gw:(g + 1) * gw]
        parts.append(yg * lax.rsqrt(jnp.mean(yg * yg, -1, keepdims=True) + RMS_EPS))
    yn = (jnp.concatenate(parts, axis=1) * ng_ref[0]).astype(BF16)
    _mix_epilogue(x_ref[0], jnp.dot(yn, w_ref[...], preferred_element_type=F32), *epilogue_refs)


def _ssd_out(x, y_f, y_b, xs, z, d_rep, norm_g, w_out, gate, ln_g, ln_b, shift2, scale2):
    bsz, n, d = x.shape
    tm = min(ROW_TILE, n)
    assert n % tm == 0
    row = lambda c: pl.BlockSpec((1, tm, c), lambda b, i: (b, i, 0))
    vecs = [gate, ln_g, ln_b, shift2, scale2]
    in_specs = [row(d)] + [row(SSD_INNER)] * 4 + [_bcast_spec(d_rep), _bcast_spec(norm_g),
                                                 pl.BlockSpec(w_out.shape, lambda b, i: (0, 0))]
    in_specs += [_bcast_spec(a) for a in vecs]
    vmem = 2 * (3 * tm * d * 4 + 4 * tm * SSD_INNER * 4 + w_out.size * 2) + 4 * tm * SSD_INNER * 4
    return pl.pallas_call(
        _ssd_out_kernel, grid=(bsz, n // tm), in_specs=in_specs, out_specs=[row(d), row(d), row(d // 2)],
        out_shape=[jax.ShapeDtypeStruct((bsz, n, d), F32)] * 2 + [jax.ShapeDtypeStruct((bsz, n, d // 2), U32)],
        compiler_params=_params(("parallel", "parallel"), vmem), name="ssd_out",
    )(x, y_f, y_b, xs, z, d_rep, norm_g, w_out, *vecs)


def _mixer_ssd(x, ctx, sh_l, sc_l, sh_c, sc_c, p):
    a_all = -jnp.exp(jnp.concatenate([p['a_log_f'], p['a_log_b']]))
    a_rep = jnp.broadcast_to(a_all.reshape(2 * SSD_GROUPS, SSD_HPG, 1), (2 * SSD_GROUPS, SSD_HPG, LANES))
    w_in = p['w_in'].astype(BF16)
    consts = (w_in[:, :SSD_INNER], w_in[:, SSD_INNER:SSD_INNER + SSD_XBC], w_in[:, SSD_INNER + SSD_XBC:],
              p['conv_w'], p['conv_b'][None], jnp.concatenate([p['dt_bias_f'], p['dt_bias_b']])[None])
    _, xc, bc, cc, dtc = _ssd_in_proj(ctx, sh_c, sc_c, *consts)
    zl, xl, bl, cl, dtl = _ssd_in_proj(x, sh_l, sc_l, *consts)
    s0 = jnp.zeros((ctx.shape[0], SSD_GROUPS, SSD_STATE, SSD_HPG * SSD_HEADDIM), F32)
    _, sc_f, _, sc_b = _ssd_scan(xc, bc, cc, dtc, a_rep, s0, s0)
    y_f, _, y_b, _ = _ssd_scan(xl, bl, cl, dtl, a_rep, sc_f, sc_b)
    return y_f, y_b, xl, zl


def kernel(x, c, ctx, c_ctx, mod_w, mod_b, ln_mix_g, ln_mix_b, ln_ffn_g, ln_ffn_b, a_w_in, hy_conv_w, hy_conv_b, hy_filt_w1, hy_filt_b1, hy_filt_freq1, hy_filt_w2, hy_filt_b2, hy_filt_freq2, hy_filt_w3, hy_skip, mla_q_norm, mla_w_qb, mla_kv_norm, mla_w_kvb, a_w_out, ssd_w_in, ssd_conv_w, ssd_conv_b, ssd_dt_bias_f, ssd_dt_bias_b, ssd_a_log_f, ssd_a_log_b, ssd_d, ssd_norm_g, ssd_w_out, router_w, router_bias, exp_w_gu, exp_w_down, sh_w_gu, sh_w_down):
    bsz, n_lat, d = x.shape
    n_ctx = ctx.shape[1]
    pad = -(bsz + 1) % SUBLANES
    cond = jnp.concatenate([c, c_ctx[None], jnp.zeros((pad, d), F32)], 0)
    for l in range(DEPTH):
        last = l == DEPTH - 1
        i = l // 2
        mods = _modulation(cond, mod_w, mod_b, l)
        mod = mods[:bsz].reshape(bsz, N_MOD, 1, d)
        mod_c = mods[bsz:bsz + 1].reshape(1, N_MOD, 1, d)
        sh1, sc1, g1, sh2, sc2, g2 = [mod[:, j] for j in range(N_MOD)]
        csh1, csc1, cg1, csh2, csc2, cg2 = [mod_c[:, j] for j in range(N_MOD)]
        vec = lambda a: a.reshape(1, 1, d)
        if l % 2 == 0:
            p = {"w_in": a_w_in[i], "conv_w": hy_conv_w[i], "conv_b": hy_conv_b[i],
                 "filt_w1": hy_filt_w1[i], "filt_b1": hy_filt_b1[i], "filt_freq1": hy_filt_freq1[i],
                 "filt_w2": hy_filt_w2[i], "filt_b2": hy_filt_b2[i], "filt_freq2": hy_filt_freq2[i],
                 "filt_w3": hy_filt_w3[i], "skip": hy_skip[i], "q_norm": mla_q_norm[i], "w_qb": mla_w_qb[i],
                 "kv_norm": mla_kv_norm[i], "w_kvb": mla_w_kvb[i]}
            ys_l, ys_c = _mixer_hyena_mla(x, ctx, sh1, sc1, csh1, csc1, p)
            w_out = a_w_out[i].astype(BF16)
            ws = [w_out[:HY_WIDTH], w_out[HY_WIDTH:]]
            x, ff_x, fp_x = _mix_out(x, ys_l, ws, g1, vec(ln_mix_g[l]), vec(ln_mix_b[l]), sh2, sc2)
        else:
            p = {"w_in": ssd_w_in[i], "conv_w": ssd_conv_w[i], "conv_b": ssd_conv_b[i],
                 "dt_bias_f": ssd_dt_bias_f[i], "dt_bias_b": ssd_dt_bias_b[i],
                 "a_log_f": ssd_a_log_f[i], "a_log_b": ssd_a_log_b[i]}
            assert last
            y_f, y_b, xs, z = _mixer_ssd(x, ctx, sh1, sc1, csh1, csc1, p)
            d_rep = jnp.repeat(ssd_d[i], SSD_HEADDIM).reshape(1, 1, SSD_INNER)
            x, ff_x, fp_x = _ssd_out(x, y_f, y_b, xs, z, d_rep, ssd_norm_g[i].reshape(1, 1, SSD_INNER),
                                     ssd_w_out[i].astype(BF16), g1, vec(ln_mix_g[l]), vec(ln_mix_b[l]), sh2, sc2)
        sh_gu = sh_w_gu[l].astype(BF16)
        sh_down = sh_w_down[l].astype(BF16)
        ln_g, ln_b = vec(ln_ffn_g[l]), vec(ln_ffn_b[l])
        moe_w = (router_w[l], router_bias[l], exp_w_gu, exp_w_down, l)
        if last:
            yb, pos, w = _moe_dispatch_experts(ff_x.reshape(-1, d), fp_x.reshape(-1, d // 2), *moe_w)
            x = _ffn_out(x, ff_x, yb, pos, w, 0, sh_gu, sh_down, g2, ln_g, ln_b)
        else:
            ctx, ff_c, fp_c = _mix_out(ctx, ys_c, ws, cg1, vec(ln_mix_g[l]), vec(ln_mix_b[l]), csh2, csc2)
            tokens = jnp.concatenate([ff_c.reshape(-1, d), ff_x.reshape(-1, d)], 0)
            packed = jnp.concatenate([fp_c.reshape(-1, d // 2), fp_x.reshape(-1, d // 2)], 0)
            yb, pos, w = _moe_dispatch_experts(tokens, packed, *moe_w)
            assert (bsz * n_ctx) % ROUTE_TOKENS == 0
            ctx = _ffn_out(ctx, ff_c, yb, pos, w, 0, sh_gu, sh_down, cg2, ln_g, ln_b)
            x = _ffn_out(x, ff_x, yb, pos, w, bsz * n_ctx // ROUTE_TOKENS, sh_gu, sh_down, g2, ln_g, ln_b)
    return x
```

```python
import functools
import math

import jax
import jax.numpy as jnp
from jax import lax
from jax.experimental import pallas as pl
from jax.experimental.pallas import tpu as pltpu

F32 = jnp.float32
BF16 = jnp.bfloat16
I32 = jnp.int32
U32 = jnp.uint32

D_MODEL = 1024
DEPTH = 2
GRID_W = 64
N_MOD = 6

HY_WIDTH = 512
HY_EMB = 33
HY_BANDS = (HY_EMB - 1) // 2
HY_TARGET = 1e-2
HY_FAST_DECAY = 0.3
HY_SLOW_DECAY = 1.5
HY_DECAY_MIN = math.log(HY_TARGET) / HY_SLOW_DECAY
HY_DECAY_MAX = math.log(HY_TARGET) / HY_FAST_DECAY

MLA_HEADS = 8
MLA_NOPE = 64
MLA_ROPE = 32
MLA_V = 64
MLA_Q_RANK = 256
MLA_KV_RANK = 128
MLA_QK = MLA_NOPE + MLA_ROPE
MLA_SCALE = MLA_QK ** -0.5
ROPE_THETA = 10000.0
LOG2E = math.log2(math.e)

OFF_Q = 3 * HY_WIDTH
OFF_KV = OFF_Q + MLA_Q_RANK
OFF_KPE = OFF_KV + MLA_KV_RANK

SSD_INNER = 2 * D_MODEL
SSD_HEADDIM = 64
SSD_HEADS = SSD_INNER // SSD_HEADDIM
SSD_GROUPS = 4
SSD_STATE = 128
SSD_CHUNK = 128
SSD_BC = SSD_GROUPS * SSD_STATE
SSD_XBC = SSD_INNER + 2 * SSD_BC
SSD_HPG = SSD_HEADS // SSD_GROUPS

N_EXPERTS = 256
TOP_K = 8
N_EXPERT_GROUPS = 8
TOPK_GROUPS = 4
EXPERT_DIM = 256
ROUTED_SCALE = 2.5

DN_ALPHA = (2 * DEPTH) ** 0.25
LN_EPS = 1e-5
RMS_EPS = 1e-6

LANES = 128
SUBLANES = 8
V7X_VMEM_CAP = 56 * 1024 * 1024

MOE_ROWS = 512
ROUTE_TOKENS = 256
ROW_TILE = 256
ATT_Q_TILE = 4096
ATT_HEADS_PER_STEP = 2
DMA_PRIORITIES = 2
ATT_ONES_ROWS = 16
MLA_SLAB = LANES
DFT_INNER = 128
DFT_COLS = 2048
DFT_BATCH = 2
HY_DIRECT_MAX = 512
FILT_ROWS = 512
MOD_COLS = 1024


def _params(semantics, vmem_bytes, **kw):
    limit = int(min(max(vmem_bytes * 5 // 4, 32 * 1024 * 1024), V7X_VMEM_CAP))
    return pltpu.CompilerParams(dimension_semantics=semantics, vmem_limit_bytes=limit, **kw)


def _bcast_spec(a):
    if a.shape[0] == 1:
        return pl.BlockSpec((1, 1, a.shape[2]), lambda b, i: (0, 0, 0))
    return pl.BlockSpec((1, 1, a.shape[2]), lambda b, i: (b, 0, 0))


def _layer_norm(r, g, b):
    mu = jnp.mean(r, -1, keepdims=True)
    c = r - mu
    var = jnp.mean(c * c, -1, keepdims=True)
    return c * lax.rsqrt(var + LN_EPS) * g + b


def _swiglu_rows(xb, w_gu, w_down):
    h = jnp.dot(xb, w_gu, preferred_element_type=F32)
    half = h.shape[1] // 2
    g, u = h[:, :half], h[:, half:]
    a = (g * jax.nn.sigmoid(g) * u).astype(BF16)
    return jnp.dot(a, w_down, preferred_element_type=F32)


def _mod_kernel(c_ref, w_ref, b_ref, o_ref):
    c = c_ref[...]
    o_ref[...] = jnp.dot(c * jax.nn.sigmoid(c), w_ref[0], precision=lax.Precision.HIGHEST,
                         preferred_element_type=F32) + b_ref[0]


def _modulation(cond, mod_w, mod_b, layer):
    r, d = cond.shape
    nout = mod_w.shape[2]
    assert nout % MOD_COLS == 0 and r % SUBLANES == 0
    return pl.pallas_call(
        _mod_kernel, grid=(nout // MOD_COLS,),
        in_specs=[pl.BlockSpec((r, d), lambda j: (0, 0)), pl.BlockSpec((1, d, MOD_COLS), lambda j: (layer, 0, j)),
                  pl.BlockSpec((1, 1, MOD_COLS), lambda j: (layer, 0, j))],
        out_specs=pl.BlockSpec((r, MOD_COLS), lambda j: (0, j)),
        out_shape=jax.ShapeDtypeStruct((r, nout), F32),
        compiler_params=_params(("parallel",), 2 * (d * MOD_COLS * 4 + 2 * r * MOD_COLS * 4)), name="modulation",
    )(cond, mod_w, mod_b.reshape(mod_b.shape[0], 1, nout))


def _pack_pairs(x):
    half = x.shape[1] // 2
    hi = pltpu.bitcast(x[:, :half].astype(BF16).astype(F32), U32)
    lo = pltpu.bitcast(x[:, half:].astype(BF16).astype(F32), U32)
    return hi | (lo >> 16)


def _unpack_pairs(u):
    hi = pltpu.bitcast(u & jnp.uint32(0xFFFF0000), F32)
    lo = pltpu.bitcast(u << 16, F32)
    return jnp.concatenate([hi, lo], axis=1)


def _mix_epilogue(x, mix, gate_ref, g_ref, b_ref, sh_ref, sc_ref, xo_ref, ff_ref, ffp_ref):
    xn = _layer_norm(DN_ALPHA * x + gate_ref[0] * mix, g_ref[0], b_ref[0])
    xo_ref[0] = xn
    ff = xn * (1.0 + sc_ref[0]) + sh_ref[0]
    ff_ref[0] = ff
    ffp_ref[0] = _pack_pairs(ff)


def _mix_out_kernel(*refs, n_y):
    x_ref = refs[0]
    y_refs = refs[1:1 + n_y]
    w_refs = refs[1 + n_y:1 + 2 * n_y]
    y = None
    for y_ref, w_ref in zip(y_refs, w_refs):
        t = jnp.dot(y_ref[0].astype(BF16), w_ref[...], preferred_element_type=F32)
        y = t if y is None else y + t
    _mix_epilogue(x_ref[0], y, *refs[1 + 2 * n_y:])


def _mix_out(x, ys, ws, gate, ln_g, ln_b, shift2, scale2):
    bsz, n, d = x.shape
    tm = min(ROW_TILE, n)
    assert n % tm == 0
    row = lambda c: pl.BlockSpec((1, tm, c), lambda b, i: (b, i, 0))
    vecs = [gate, ln_g, ln_b, shift2, scale2]
    in_specs = [row(d)] + [row(y.shape[2]) for y in ys]
    in_specs += [pl.BlockSpec(w.shape, lambda b, i: (0, 0)) for w in ws]
    in_specs += [_bcast_spec(a) for a in vecs]
    vmem = 2 * (3 * tm * d * 4 + sum(tm * y.shape[2] * 4 + w.size * 2 for y, w in zip(ys, ws)))
    return pl.pallas_call(
        functools.partial(_mix_out_kernel, n_y=len(ys)),
        grid=(bsz, n // tm), in_specs=in_specs, out_specs=[row(d), row(d), row(d // 2)],
        out_shape=[jax.ShapeDtypeStruct((bsz, n, d), F32)] * 2 + [jax.ShapeDtypeStruct((bsz, n, d // 2), U32)],
        compiler_params=_params(("parallel", "parallel"), vmem), name="mix_out",
    )(x, *ys, *ws, *vecs)


def _attn_kernel(q_ref, k_ref, vt_ref, o_ref, *, tk, hp):
    nk = k_ref.shape[2]
    tq = q_ref.shape[2]
    dva = vt_ref.shape[2]
    dv = dva - ATT_ONES_ROWS

    def body(j, carry):
        off = pl.multiple_of(j * tk, tk)
        new = []
        for h in range(hp):
            m_prev, acc = carry[h]
            st = lax.dot_general(k_ref[0, h, pl.ds(off, tk), :], q_ref[0, h], (((1,), (1,)), ((), ())),
                                 preferred_element_type=F32)
            m_new = jnp.maximum(m_prev, jnp.max(st, 0, keepdims=True))
            p = jnp.exp2(st - m_new).astype(BF16)
            alpha = jnp.exp2(m_prev - m_new)
            acc = alpha * acc + jnp.dot(vt_ref[0, h, :, pl.ds(off, tk)], p, preferred_element_type=F32)
            new.append((m_new, acc))
        return tuple(new)

    init = tuple((jnp.full((1, tq), -jnp.inf, F32), jnp.zeros((dva, tq), F32)) for _ in range(hp))
    fin = lax.fori_loop(0, nk // tk, body, init)
    outs = [acc[:dv] / acc[dv:dv + 1] for _, acc in fin]
    o_ref[0] = jnp.concatenate(outs, 0).T


def _attention(q, k, vt, tq, tk):
    bsz, h, nq, dk = q.shape
    nk, dva = k.shape[2], vt.shape[2]
    dv = dva - ATT_ONES_ROWS
    hp = ATT_HEADS_PER_STEP
    assert nq % tq == 0 and nk % tk == 0 and h % hp == 0
    vmem = 2 * hp * (tq * LANES * 2 + nk * LANES * 2 + dva * nk * 2) + 2 * tq * hp * dv * 4 + 6 * hp * tk * tq * 4
    return pl.pallas_call(
        functools.partial(_attn_kernel, tk=tk, hp=hp), grid=(bsz, h // hp, nq // tq),
        in_specs=[pl.BlockSpec((1, hp, tq, dk), lambda b, g, i: (b, g, i, 0)),
                  pl.BlockSpec((1, hp, nk, dk), lambda b, g, i: (b, g, 0, 0)),
                  pl.BlockSpec((1, hp, dva, nk), lambda b, g, i: (b, g, 0, 0))],
        out_specs=pl.BlockSpec((1, tq, hp * dv), lambda b, g, i: (b, i, g)),
        out_shape=jax.ShapeDtypeStruct((bsz, nq, h * dv), F32),
        compiler_params=_params(("parallel", "parallel", "arbitrary"), vmem), name="mla_attention",
    )(q, k, vt)


def _key_tile(nk):
    for t in (768, 512, 384, 256, 128):
        if nk % t == 0:
            return t
    return nk


def _router_kernel(x_ref, wt_ref, bias_ref, upper_ref, idx_ref, w_ref, rank_ref, cnt_ref, run_sc):
    i = pl.program_id(0)
    tm = x_ref.shape[0]

    @pl.when(i == 0)
    def _():
        run_sc[...] = jnp.zeros(run_sc.shape, F32)

    logits = lax.dot_general(wt_ref[...], x_ref[...], (((1,), (1,)), ((), ())),
                             precision=lax.Precision.HIGHEST, preferred_element_type=F32)
    sc = jax.nn.sigmoid(logits)
    ch = sc + bias_ref[:, :1]
    neg = -jnp.inf
    chg = ch.reshape(N_EXPERT_GROUPS, N_EXPERTS // N_EXPERT_GROUPS, tm)
    m1 = jnp.max(chg, axis=1)
    eq = chg == m1[:, None, :]
    cnt = jnp.sum(eq.astype(F32), axis=1)
    m2 = jnp.max(jnp.where(eq, neg, chg), axis=1)
    g2 = m1 + jnp.where(cnt >= 2.0, m1, m2)
    gi = lax.broadcasted_iota(I32, g2.shape, 0)
    beaten = jnp.zeros(g2.shape, F32)
    for g in range(N_EXPERT_GROUPS):
        row = g2[g:g + 1, :]
        beaten = beaten + jnp.where(row > g2, 1.0, jnp.where(row == g2, jnp.where(gi > g, 1.0, 0.0), 0.0))
    keep = beaten < float(TOPK_GROUPS)
    cur = jnp.where(keep[:, None, :], chg, neg).reshape(N_EXPERTS, tm)
    eidx = lax.broadcasted_iota(I32, (N_EXPERTS, tm), 0)
    multi = jnp.zeros((N_EXPERTS, tm), F32)
    hits, idx_rows, w_rows = [], [], []
    for _ in range(TOP_K):
        m = jnp.max(cur, axis=0, keepdims=True)
        sel = jnp.min(jnp.where(cur == m, eidx, N_EXPERTS), axis=0, keepdims=True)
        hit = eidx == sel
        idx_rows.append(sel)
        w_rows.append(jnp.sum(jnp.where(hit, sc, 0.0), axis=0, keepdims=True))
        cur = jnp.where(hit, neg, cur)
        multi = multi + jnp.where(hit, 1.0, 0.0)
        hits.append(hit)
    base = jnp.concatenate([run_sc[...]] * (tm // LANES), axis=1)
    before = jnp.dot(multi.astype(BF16), upper_ref[...], preferred_element_type=F32) + base
    rank_rows = [jnp.sum(jnp.where(hit, before, 0.0), axis=0, keepdims=True) for hit in hits]
    w = jnp.concatenate(w_rows, axis=0)
    idx_ref[...] = jnp.concatenate(idx_rows, axis=0)
    w_ref[...] = (w / jnp.sum(w, axis=0, keepdims=True) * ROUTED_SCALE).T
    rank_ref[...] = jnp.concatenate(rank_rows, axis=0).astype(I32)
    run_sc[...] = run_sc[...] + jnp.dot(multi.astype(BF16), jnp.ones((tm, LANES), BF16), preferred_element_type=F32)
    cnt_ref[...] = run_sc[...]


def _router(tokens, router_w, router_bias):
    t, d = tokens.shape
    tm = ROUTE_TOKENS
    assert t % tm == 0
    wt = router_w.T
    bias = jnp.broadcast_to(router_bias.astype(F32)[:, None], (N_EXPERTS, LANES))
    r = jnp.arange(tm)
    upper = (r[:, None] < r[None, :]).astype(BF16)
    col = pl.BlockSpec((TOP_K, tm), lambda i: (0, i))
    full = lambda a: pl.BlockSpec(a.shape, lambda i: (0,) * a.ndim)
    vmem = 2 * (tm * d * 4 + wt.size * 4) + 40 * N_EXPERTS * tm * 4
    return pl.pallas_call(
        _router_kernel, grid=(t // tm,),
        in_specs=[pl.BlockSpec((tm, d), lambda i: (i, 0)), full(wt), full(bias), full(upper)],
        out_specs=[col, pl.BlockSpec((tm, TOP_K), lambda i: (i, 0)), col,
                   pl.BlockSpec((N_EXPERTS, LANES), lambda i: (0, 0))],
        out_shape=[jax.ShapeDtypeStruct((TOP_K, t), I32), jax.ShapeDtypeStruct((t, TOP_K), F32),
                   jax.ShapeDtypeStruct((TOP_K, t), I32), jax.ShapeDtypeStruct((N_EXPERTS, LANES), F32)],
        scratch_shapes=[pltpu.VMEM((N_EXPERTS, LANES), F32)],
        compiler_params=_params(("arbitrary",), vmem), name="moe_router",
    )(tokens, wt, bias, upper)


def _positions_kernel(idx_ref, rank_ref, start_ref, pos_ref):
    tm = idx_ref.shape[1]
    eidx = lax.broadcasted_iota(I32, (N_EXPERTS, tm), 0)
    start = jnp.concatenate([start_ref[...]] * (tm // LANES), axis=1)
    rows = [jnp.sum(jnp.where(eidx == idx_ref[k:k + 1, :], start, 0), axis=0, keepdims=True) for k in range(TOP_K)]
    pos_ref[...] = jnp.concatenate(rows, axis=0) + rank_ref[...]


def _positions(idx, rank, pad_start):
    t = idx.shape[1]
    tm = ROUTE_TOKENS
    start = jnp.broadcast_to(pad_start.astype(I32)[:, None], (N_EXPERTS, LANES))
    col = pl.BlockSpec((TOP_K, tm), lambda i: (0, i))
    return pl.pallas_call(
        _positions_kernel, grid=(t // tm,),
        in_specs=[col, col, pl.BlockSpec((N_EXPERTS, LANES), lambda i: (0, 0))], out_specs=col,
        out_shape=jax.ShapeDtypeStruct((TOP_K, t), I32),
        compiler_params=_params(("parallel",), 8 * N_EXPERTS * tm * 4), name="moe_positions",
    )(idx, rank, start)


def _moe_plan(counts, t):
    n_blocks = -(-(t * TOP_K) // MOE_ROWS) + N_EXPERTS
    c = counts.astype(I32)
    padded = (c + MOE_ROWS - 1) // MOE_ROWS * MOE_ROWS
    pad_end = jnp.cumsum(padded)
    pad_start = pad_end - padded
    block_e = jnp.minimum(jnp.searchsorted(pad_end, jnp.arange(n_blocks, dtype=I32) * MOE_ROWS, side='right'),
                          N_EXPERTS - 1).astype(I32)
    n_valid = (pad_end[-1:] // MOE_ROWS).astype(I32)
    return pad_start, block_e, n_valid, n_blocks * MOE_ROWS


def _row_copy(src, dst, sem):
    return pltpu.make_async_copy(src, dst, sem)


def _dispatch_kernel(pos_ref, x_ref, buf_in_ref, buf_ref, sem):
    del buf_in_ref
    tm = x_ref.shape[0]

    def issue(g, carry):
        base = pl.multiple_of(g * SUBLANES, SUBLANES)
        for j in range(SUBLANES):
            for k in range(TOP_K):
                _row_copy(x_ref.at[pl.ds(base + j, 1)], buf_ref.at[pl.ds(pos_ref[k, base + j], 1)],
                          sem).start(priority=k % DMA_PRIORITIES)
        return carry

    lax.fori_loop(0, tm // SUBLANES, issue, 0)
    for k in range(TOP_K):
        _row_copy(x_ref, buf_ref.at[pl.ds(0, tm)], sem).wait()


def _dispatch(pos, tokens, cap):
    t, d = tokens.shape
    tm = ROUTE_TOKENS
    zeros = jnp.zeros((cap, d), tokens.dtype)
    return pl.pallas_call(
        _dispatch_kernel, grid=(t // tm,),
        in_specs=[pl.BlockSpec((TOP_K, tm), lambda i: (0, i), memory_space=pltpu.SMEM),
                  pl.BlockSpec((tm, d), lambda i: (i, 0)),
                  pl.BlockSpec(memory_space=pl.ANY)],
        out_specs=pl.BlockSpec(memory_space=pl.ANY),
        out_shape=jax.ShapeDtypeStruct((cap, d), tokens.dtype),
        scratch_shapes=[pltpu.SemaphoreType.DMA(())],
        input_output_aliases={2: 0},
        compiler_params=_params(("arbitrary",), 2 * tm * d * 4, has_side_effects=True),
        name="moe_dispatch",
    )(pos, tokens, zeros)


def _experts_kernel(be_ref, nv_ref, x_ref, wgu_ref, wdn_ref, o_ref, wgu_sc, wdn_sc):
    i = pl.program_id(0)

    @pl.when(i < nv_ref[0])
    def _():
        @pl.when(jnp.logical_or(i == 0, be_ref[i] != be_ref[jnp.maximum(i - 1, 0)]))
        def _():
            wgu_sc[...] = wgu_ref[0, 0].astype(BF16)
            wdn_sc[...] = wdn_ref[0, 0].astype(BF16)

        x = _unpack_pairs(x_ref[...]).astype(BF16)
        o_ref[...] = _pack_pairs(_swiglu_rows(x, wgu_sc[...], wdn_sc[...]))

    @pl.when(i >= nv_ref[0])
    def _():
        o_ref[...] = jnp.zeros(o_ref.shape, U32)


def _moe_experts(xb, block_e, n_valid, w_gu, w_down, layer):
    cap, dp = xb.shape
    d = 2 * dp
    gu = w_gu.shape[3]
    ed = w_down.shape[2]
    live = lambda i, be, nv: jnp.maximum(jnp.minimum(i, nv[0] - 1), 0)
    grid_spec = pltpu.PrefetchScalarGridSpec(
        num_scalar_prefetch=2, grid=(cap // MOE_ROWS,),
        in_specs=[pl.BlockSpec((MOE_ROWS, dp), lambda i, be, nv: (live(i, be, nv), 0)),
                  pl.BlockSpec((1, 1, d, gu), lambda i, be, nv: (layer, be[i], 0, 0)),
                  pl.BlockSpec((1, 1, ed, d), lambda i, be, nv: (layer, be[i], 0, 0))],
        out_specs=pl.BlockSpec((MOE_ROWS, dp), lambda i, be, nv: (i, 0)),
        scratch_shapes=[pltpu.VMEM((d, gu), BF16), pltpu.VMEM((ed, d), BF16)])
    vmem = 2 * (2 * MOE_ROWS * dp * 4 + d * gu * 4 + ed * d * 4) + (d * gu + ed * d) * 2 + 6 * MOE_ROWS * d * 4
    return pl.pallas_call(
        _experts_kernel, grid_spec=grid_spec, out_shape=jax.ShapeDtypeStruct((cap, dp), U32),
        compiler_params=_params(("arbitrary",), vmem), name="moe_experts",
    )(block_e, n_valid, xb, w_gu, w_down)


def _ffn_out_kernel(pos_ref, w_ref, x_ref, ff_ref, yb_ref, wgu_ref, wdn_ref, gate_ref, g_ref, b_ref, xo_ref,
                    rows_sc, sem):
    tm = x_ref.shape[1]

    def issue(g, carry):
        base = pl.multiple_of(g * SUBLANES, SUBLANES)
        for j in range(SUBLANES):
            for k in range(TOP_K):
                _row_copy(yb_ref.at[pl.ds(pos_ref[k, base + j], 1)], rows_sc.at[k, pl.ds(base + j, 1)],
                          sem).start(priority=k % DMA_PRIORITIES)
        return carry

    lax.fori_loop(0, tm // SUBLANES, issue, 0)
    out = _swiglu_rows(ff_ref[0].astype(BF16), wgu_ref[...], wdn_ref[...])
    for k in range(TOP_K):
        _row_copy(yb_ref.at[pl.ds(0, tm)], rows_sc.at[k], sem).wait()
    for k in range(TOP_K):
        out = out + w_ref[:, k:k + 1] * _unpack_pairs(rows_sc[k])
    xo_ref[0] = _layer_norm(DN_ALPHA * x_ref[0] + gate_ref[0] * out, g_ref[0], b_ref[0])


def _ffn_out(x, ff, yb, pos, w, tile0, sh_gu, sh_down, gate, ln_g, ln_b):
    bsz, n, d = x.shape
    tm = min(ROUTE_TOKENS, n)
    nt = n // tm
    row = pl.BlockSpec((1, tm, d), lambda b, i: (b, i, 0))
    vecs = [gate, ln_g, ln_b]
    in_specs = [pl.BlockSpec((TOP_K, tm), lambda b, i: (0, tile0 + b * nt + i), memory_space=pltpu.SMEM),
                pl.BlockSpec((tm, TOP_K), lambda b, i: (tile0 + b * nt + i, 0)),
                row, row, pl.BlockSpec(memory_space=pl.ANY),
                pl.BlockSpec(sh_gu.shape, lambda b, i: (0, 0)), pl.BlockSpec(sh_down.shape, lambda b, i: (0, 0))]
    in_specs += [_bcast_spec(a) for a in vecs]
    vmem = TOP_K * tm * d * 2 + 2 * (3 * tm * d * 4 + sh_gu.size * 2 + sh_down.size * 2) + 6 * tm * d * 4
    return pl.pallas_call(
        _ffn_out_kernel, grid=(bsz, nt), in_specs=in_specs, out_specs=row,
        out_shape=jax.ShapeDtypeStruct((bsz, n, d), F32),
        scratch_shapes=[pltpu.VMEM((TOP_K, tm, d // 2), U32), pltpu.SemaphoreType.DMA(())],
        compiler_params=_params(("arbitrary", "arbitrary"), vmem), name="moe_combine_ffn_out",
    )(pos, w, x, ff, yb, sh_gu, sh_down, *vecs)


def _moe_dispatch_experts(tokens, packed, router_w, router_bias, w_gu, w_down, layer):
    t = tokens.shape[0]
    idx, w, rank, counts = _router(tokens, router_w, router_bias)
    pad_start, block_e, n_valid, cap = _moe_plan(counts[:, 0], t)
    pos = _positions(idx, rank, pad_start)
    xb = _dispatch(pos, packed, cap)
    return _moe_experts(xb, block_e, n_valid, w_gu, w_down, layer), pos, w


def _rope_tables(n):
    rows = n // GRID_W
    row = jnp.repeat(jnp.arange(rows), GRID_W).astype(F32)
    col = jnp.tile(jnp.arange(GRID_W), rows).astype(F32)
    half = MLA_ROPE // 2
    inv = ROPE_THETA ** (-jnp.arange(0, half, 2, dtype=F32) / half)
    ang = jnp.concatenate([row[:, None] * inv, col[:, None] * inv], -1)
    return jnp.cos(ang), jnp.sin(ang)


def _filter_kernel(fr_ref, w1_ref, b1_ref, f1_ref, w2_ref, b2_ref, f2_ref, w3_ref, dl_ref, k_ref, l1_ref, *, n):
    i = pl.program_id(0)
    tr = k_ref.shape[0]
    hp = lax.Precision.HIGHEST
    tap = i * tr + lax.broadcasted_iota(I32, (tr, 1), 0)
    lag = jnp.where(tap < n, tap, 2 * n - tap).astype(F32)
    t = lag * (1.0 / (n - 1))
    ang = (2.0 * math.pi / n) * lag * fr_ref[...]
    lane = lax.broadcasted_iota(I32, (tr, LANES), 1)
    z = jnp.where(lane == 0, t, jnp.where(lane <= HY_BANDS, jnp.cos(ang),
                                          jnp.where(lane <= 2 * HY_BANDS, -jnp.sin(ang), 0.0)))
    h = jnp.sin(f1_ref[...] * (jnp.dot(z, w1_ref[...], precision=hp, preferred_element_type=F32) + b1_ref[...]))
    h = jnp.sin(f2_ref[...] * (jnp.dot(h, w2_ref[...], precision=hp, preferred_element_type=F32) + b2_ref[...]))
    h = jnp.dot(h, w3_ref[...], precision=hp, preferred_element_type=F32)
    hsel = jnp.where(tap < n, h[:, :HY_WIDTH], h[:, HY_WIDTH:])
    k = jnp.where(tap == n, 0.0, hsel * jnp.exp(-t * dl_ref[...]))
    k_ref[...] = k

    @pl.when(i == 0)
    def _():
        l1_ref[...] = jnp.zeros(l1_ref.shape, F32)

    l1_ref[...] = l1_ref[...] + jnp.sum(jnp.abs(k), axis=0, keepdims=True)


def _hyena_filter(n, w1, b1, f1, w2, b2, f2, w3):
    nn = 2 * n
    tr = min(FILT_ROWS, nn)
    assert nn % tr == 0
    emb, ffn = w1.shape
    fr = jnp.linspace(1e-4, HY_BANDS - 1, HY_BANDS, dtype=F32)
    fr_l = jnp.concatenate([jnp.zeros((1,), F32), fr, fr, jnp.zeros((LANES - emb,), F32)])[None]
    w1p = jnp.concatenate([w1, jnp.zeros((LANES - emb, ffn), F32)], 0)
    deltas = jnp.abs(jnp.linspace(HY_DECAY_MIN, HY_DECAY_MAX, HY_WIDTH, dtype=F32))[None]
    ops = [fr_l, w1p, b1[None], f1[None], w2, b2[None], f2[None], w3, deltas]
    full = lambda a: pl.BlockSpec(a.shape, lambda i: (0,) * a.ndim)
    return pl.pallas_call(
        functools.partial(_filter_kernel, n=n), grid=(nn // tr,), in_specs=[full(a) for a in ops],
        out_specs=[pl.BlockSpec((tr, HY_WIDTH), lambda i: (i, 0)), pl.BlockSpec((1, HY_WIDTH), lambda i: (0, 0))],
        out_shape=[jax.ShapeDtypeStruct((nn, HY_WIDTH), F32), jax.ShapeDtypeStruct((1, HY_WIDTH), F32)],
        compiler_params=_params(("arbitrary",), 16 * tr * 2 * HY_WIDTH * 4), name="hyena_filter",
    )(*ops)


def _split(a):
    hi = a.astype(BF16)
    return hi, (a - hi.astype(F32)).astype(BF16)


def _dot3(a_hi, a_lo, x):
    x_hi, x_lo = _split(x)
    return (jnp.dot(a_hi, x_hi, preferred_element_type=F32) + jnp.dot(a_lo, x_hi, preferred_element_type=F32)
            + jnp.dot(a_hi, x_lo, preferred_element_type=F32))


def _cis(num, den):
    ang = (2.0 * math.pi / den) * (num % den).astype(F32)
    return jnp.cos(ang), jnp.sin(ang)


def _dft_outer_kernel(f_hi_ref, f_lo_ref, x_ref, o_ref):
    o_ref[0] = _dot3(f_hi_ref[...], f_lo_ref[...], x_ref[0])


def _dft_outer(x, f_hi, f_lo):
    bsz, k, m = x.shape
    r = f_hi.shape[0]
    tn = min(DFT_COLS, m)
    assert m % tn == 0
    vmem = 2 * (2 * f_hi.size * 2 + k * tn * 4 + r * tn * 4) + 3 * (k + r) * tn * 4
    return pl.pallas_call(
        _dft_outer_kernel, grid=(bsz, m // tn),
        in_specs=[pl.BlockSpec(f_hi.shape, lambda b, j: (0, 0)), pl.BlockSpec(f_lo.shape, lambda b, j: (0, 0)),
                  pl.BlockSpec((1, k, tn), lambda b, j: (b, 0, j))],
        out_specs=pl.BlockSpec((1, r, tn), lambda b, j: (b, 0, j)),
        out_shape=jax.ShapeDtypeStruct((bsz, r, m), F32),
        compiler_params=_params(("parallel", "parallel"), vmem), name="hyena_dft_outer",
    )(f_hi, f_lo, x)


def _dft_inner_kernel(m_hi_ref, m_lo_ref, mt_hi_ref, mt_lo_ref, a_ref, h_ref, o_ref, *, conv):
    n2 = a_ref.shape[3]
    for bi in range(a_ref.shape[0]):
        x = a_ref[bi, :, 0].reshape(2 * n2, a_ref.shape[4])
        y = _dot3(m_hi_ref[0], m_lo_ref[0], x)
        if conv:
            h = h_ref[0, :, 0].reshape(2 * n2, h_ref.shape[4])
            yr, yi, hr, hi = y[:n2], y[n2:], h[:n2], h[n2:]
            prod = jnp.concatenate([yr * hr - yi * hi, yr * hi + yi * hr], axis=0)
            y = _dot3(mt_hi_ref[0], mt_lo_ref[0], prod)
        o_ref[bi, :, 0] = y.reshape(2, n2, y.shape[1])


def _dft_inner(a, h, mats, conv):
    bsz, _, n1, n2, c = a.shape
    bt = DFT_BATCH if bsz % DFT_BATCH == 0 else 1
    blk = lambda rows, sel: pl.BlockSpec((rows, 2, 1, n2, c), sel)
    mat = pl.BlockSpec((1, 2 * n2, 2 * n2), lambda k, b: (k, 0, 0))
    vmem = 2 * (4 * 4 * n2 * n2 * 2 + (2 * bt + 1) * 2 * n2 * c * 4) + 8 * bt * 2 * n2 * c * 4
    return pl.pallas_call(
        functools.partial(_dft_inner_kernel, conv=conv), grid=(n1, bsz // bt),
        in_specs=[mat, mat, mat, mat, blk(bt, lambda k, b: (b, 0, k, 0, 0)), blk(1, lambda k, b: (0, 0, k, 0, 0))],
        out_specs=blk(bt, lambda k, b: (b, 0, k, 0, 0)), out_shape=jax.ShapeDtypeStruct(a.shape, F32),
        compiler_params=_params(("parallel", "arbitrary"), vmem),
        name="hyena_dft_inner_conv" if conv else "hyena_dft_inner",
    )(*mats, a, h)


def _dft_final_kernel(fd_hi_ref, fd_lo_ref, a_ref, z_ref, x0_ref, skip_ref, l1_ref, o_ref):
    y = _dot3(fd_hi_ref[...], fd_lo_ref[...], a_ref[0])
    o_ref[0] = (y / l1_ref[...] + z_ref[0] * skip_ref[...]) * x0_ref[0]


def _dft_final(a, z, x0, skip_t, l1_t, fd_hi, fd_lo):
    bsz, r2, m = a.shape
    k = fd_hi.shape[0]
    tn = min(DFT_COLS, m)
    row = pl.BlockSpec((1, k, tn), lambda b, j: (b, 0, j))
    vmem = 2 * (2 * fd_hi.size * 2 + r2 * tn * 4 + 3 * k * tn * 4) + 3 * (k + r2) * tn * 4
    return pl.pallas_call(
        _dft_final_kernel, grid=(bsz, m // tn),
        in_specs=[pl.BlockSpec(fd_hi.shape, lambda b, j: (0, 0)), pl.BlockSpec(fd_lo.shape, lambda b, j: (0, 0)),
                  pl.BlockSpec((1, r2, tn), lambda b, j: (b, 0, j)), row, row,
                  pl.BlockSpec((1, tn), lambda b, j: (0, 0)), pl.BlockSpec((1, tn), lambda b, j: (0, 0))],
        out_specs=row, out_shape=jax.ShapeDtypeStruct((bsz, k, m), F32),
        compiler_params=_params(("parallel", "parallel"), vmem), name="hyena_dft_final",
    )(fd_hi, fd_lo, a, z, x0, skip_t, l1_t)


def _hyena_long_conv(z, x0, k, l1, skip):
    bsz, n, c = z.shape
    n2 = DFT_INNER
    nn = 2 * n
    n1 = nn // n2
    assert nn == n1 * n2 and n1 % 2 == 0
    half = n1 // 2
    m = n2 * c
    j1 = jnp.arange(n1)
    ca, sa = _cis(j1[:, None] * j1[None, :], n1)
    fa = jnp.concatenate([ca, -sa], axis=0)
    fd = jnp.concatenate([ca, -sa], axis=1)[:half] / nn
    j2 = jnp.arange(n2)
    cb, sb = _cis(j2[None, None, :] * (n1 * j2[None, :, None] + j1[:, None, None]), nn)
    mb = jnp.concatenate([jnp.concatenate([cb, sb], 2), jnp.concatenate([-sb, cb], 2)], 1)
    mats = _split(mb) + _split(jnp.swapaxes(mb, 1, 2))
    fa_hi, fa_lo = _split(fa)
    hk = _dft_outer(k.reshape(1, n1, m), fa_hi, fa_lo).reshape(1, 2, n1, n2, c)
    hk = _dft_inner(hk, hk, mats, conv=False)
    a = _dft_outer(z.reshape(bsz, half, m), fa_hi[:, :half], fa_lo[:, :half]).reshape(bsz, 2, n1, n2, c)
    a = _dft_inner(a, hk, mats, conv=True).reshape(bsz, 2 * n1, m)
    reps = (1, min(DFT_COLS, m) // c)
    out = _dft_final(a, z.reshape(bsz, half, m), x0.reshape(bsz, half, m), jnp.tile(skip.reshape(1, c), reps),
                     jnp.tile(l1, reps), *_split(fd))
    return out.reshape(bsz, n, c)


def _short_conv_kernel(f_hi_ref, f_lo_ref, fi_hi_ref, fi_lo_ref, z_ref, x0_ref, k_ref, skip_ref, l1_ref, o_ref):
    n = z_ref.shape[1]
    nn = 2 * n
    z = z_ref[0]
    hk = _dot3(f_hi_ref[...], f_lo_ref[...], k_ref[...])
    zs = _dot3(f_hi_ref[:, :n], f_lo_ref[:, :n], z)
    zr, zi, hr, hi = zs[:nn], zs[nn:], hk[:nn], hk[nn:]
    prod = jnp.concatenate([zr * hr - zi * hi, zr * hi + zi * hr], axis=0)
    o_ref[0] = (_dot3(fi_hi_ref[...], fi_lo_ref[...], prod) / l1_ref[...] + z * skip_ref[...]) * x0_ref[0]


def _hyena_short_conv(z, x0, k, l1, skip):
    bsz, n, c = z.shape
    nn = 2 * n
    idx = jnp.arange(nn)
    cf, sf = _cis(idx[:, None] * idx[None, :], nn)
    f = jnp.concatenate([cf, -sf], axis=0)
    fi = jnp.concatenate([cf, -sf], axis=1)[:n] / nn
    full = lambda a: pl.BlockSpec(a.shape, lambda b: (0,) * a.ndim)
    row = pl.BlockSpec((1, n, c), lambda b: (b, 0, 0))
    ops = _split(f) + _split(fi)
    vmem = 2 * (sum(a.size * 2 for a in ops) + 3 * n * c * 4 + nn * c * 4) + 12 * 2 * nn * c * 4
    return pl.pallas_call(
        _short_conv_kernel, grid=(bsz,),
        in_specs=[full(a) for a in ops] + [row, row, full(k), pl.BlockSpec((1, c), lambda b: (0, 0)),
                                           pl.BlockSpec((1, c), lambda b: (0, 0))],
        out_specs=row, out_shape=jax.ShapeDtypeStruct((bsz, n, c), F32),
        compiler_params=_params(("parallel",), vmem), name="hyena_short_conv",
    )(*ops, z, x0, k, skip.reshape(1, c), l1)


def _hyena_sequence(z, x0, filt, skip):
    n = z.shape[1]
    k, l1 = _hyena_filter(n, *filt)
    conv = _hyena_short_conv if n <= HY_DIRECT_MAX else _hyena_long_conv
    return conv(z, x0, k, l1, skip)


def _rms(x, g):
    return x * lax.rsqrt(jnp.mean(x * x, -1, keepdims=True) + RMS_EPS) * g


def _mla_in_kernel(xc_ref, xp_ref, xn_ref, sh_ref, sc_ref, why_ref, cw_ref, cb_ref, wql_ref, qg_ref, wqa_ref, wqb_ref,
                   wkvl_ref, kvg_ref, wk_ref, wv_ref, wpa_ref, wpb_ref, qa_ref, qb_ref, ka_ref, kb_ref,
                   x0_ref, z_ref, q_ref, k_ref, vt_ref):
    tm = xc_ref.shape[1]
    rows = jnp.concatenate([xp_ref[0], xc_ref[0], xn_ref[0]], axis=0)
    h = (rows * (1.0 + sc_ref[0]) + sh_ref[0]).astype(BF16)
    hc = h[SUBLANES:SUBLANES + tm]
    y = _conv3(jnp.dot(h, why_ref[...], preferred_element_type=F32), cw_ref, cb_ref, tm)
    x0_ref[0] = y[:, :HY_WIDTH]
    z_ref[0] = y[:, 2 * HY_WIDTH:] * y[:, HY_WIDTH:2 * HY_WIDTH]
    ql = _rms(jnp.dot(hc, wql_ref[...], preferred_element_type=F32), qg_ref[...]).astype(BF16)
    nh = q_ref.shape[1]
    qa = jnp.concatenate([qa_ref[...]] * nh, axis=1)
    qb = jnp.concatenate([qb_ref[...]] * nh, axis=1)
    q = (jnp.dot(ql, wqa_ref[...], preferred_element_type=F32) * qa
         + jnp.dot(ql, wqb_ref[...], preferred_element_type=F32) * qb).astype(BF16)
    kvl = _rms(jnp.dot(hc, wkvl_ref[...], preferred_element_type=F32), kvg_ref[...]).astype(BF16)
    kn = jnp.dot(kvl, wk_ref[...], preferred_element_type=F32)
    v_t = jnp.dot(kvl, wv_ref[...], preferred_element_type=F32).T
    kpe = (jnp.dot(hc, wpa_ref[...], preferred_element_type=F32) * ka_ref[...]
           + jnp.dot(hc, wpb_ref[...], preferred_element_type=F32) * kb_ref[...])
    ones = jnp.ones((ATT_ONES_ROWS, tm), BF16)
    for hd in range(nh):
        q_ref[0, hd] = q[:, hd * MLA_SLAB:(hd + 1) * MLA_SLAB]
        k_ref[0, hd] = (kn[:, hd * MLA_SLAB:(hd + 1) * MLA_SLAB] + kpe).astype(BF16)
        vt_ref[0, hd, :MLA_V, :] = v_t[hd * MLA_V:(hd + 1) * MLA_V].astype(BF16)
        vt_ref[0, hd, MLA_V:, :] = ones


def _rot_cols(w_pe):
    ev, od = w_pe[..., 0::2], w_pe[..., 1::2]
    return jnp.concatenate([ev, od], -1), jnp.concatenate([-od, ev], -1)


def _mla_weights(p):
    w_in = p['w_in']
    kin = w_in.shape[0]
    wq = p['w_qb'].reshape(MLA_Q_RANK, MLA_HEADS, MLA_QK)
    qa_pe, qb_pe = _rot_cols(wq[..., MLA_NOPE:])
    zq = jnp.zeros((MLA_Q_RANK, MLA_HEADS, MLA_SLAB - MLA_QK), F32)
    w_qa = jnp.concatenate([wq[..., :MLA_NOPE], qa_pe, zq], -1).reshape(MLA_Q_RANK, MLA_HEADS * MLA_SLAB)
    w_qb = jnp.concatenate([jnp.zeros_like(wq[..., :MLA_NOPE]), qb_pe, zq], -1)
    w_qb = w_qb.reshape(MLA_Q_RANK, MLA_HEADS * MLA_SLAB)
    wkv = p['w_kvb'].reshape(MLA_KV_RANK, MLA_HEADS, MLA_NOPE + MLA_V)
    w_k = jnp.concatenate([wkv[..., :MLA_NOPE], jnp.zeros((MLA_KV_RANK, MLA_HEADS, MLA_SLAB - MLA_NOPE), F32)], -1)
    w_k = w_k.reshape(MLA_KV_RANK, MLA_HEADS * MLA_SLAB)
    w_v = wkv[..., MLA_NOPE:].reshape(MLA_KV_RANK, MLA_HEADS * MLA_V)
    pa, pb = _rot_cols(w_in[:, OFF_KPE:])
    left, right = jnp.zeros((kin, MLA_NOPE), F32), jnp.zeros((kin, MLA_SLAB - MLA_QK), F32)
    w_pa = jnp.concatenate([left, pa, right], -1)
    w_pb = jnp.concatenate([left, pb, right], -1)
    bf = lambda a: a.astype(BF16)
    return dict(w_hy=bf(w_in[:, :OFF_Q]), conv_w=p['conv_w'], conv_b=p['conv_b'][None],
                w_ql=bf(w_in[:, OFF_Q:OFF_KV]), q_g=p['q_norm'][None], w_qa=bf(w_qa), w_qb=bf(w_qb),
                w_kvl=bf(w_in[:, OFF_KV:OFF_KPE]), kv_g=p['kv_norm'][None], w_k=bf(w_k), w_v=bf(w_v),
                w_pa=bf(w_pa), w_pb=bf(w_pb))


def _rope_slabs(n, rotate):
    one = jnp.ones((n, MLA_NOPE), F32)
    zero = jnp.zeros((n, MLA_SLAB - MLA_QK), F32)
    if rotate:
        cos, sin = _rope_tables(n)
    else:
        cos, sin = jnp.ones((n, MLA_ROPE // 2), F32), jnp.zeros((n, MLA_ROPE // 2), F32)
    return (jnp.concatenate([one, cos, cos, zero], -1), jnp.concatenate([jnp.zeros_like(one), sin, sin, zero], -1))


def _mla_in_proj(x, shift, scale, w, rotate):
    bsz, n, kin = x.shape
    tm = min(ROW_TILE, n)
    assert n % tm == 0
    ca, sb = _rope_slabs(n, rotate)
    s = MLA_SCALE * LOG2E
    tabs = [ca * s, sb * s, ca, sb]
    consts = [w['w_hy'], w['conv_w'], w['conv_b'], w['w_ql'], w['q_g'], w['w_qa'], w['w_qb'], w['w_kvl'], w['kv_g'],
              w['w_k'], w['w_v'], w['w_pa'], w['w_pb']]
    full = lambda a: pl.BlockSpec(a.shape, lambda b, i: (0,) * a.ndim)
    row = lambda c: pl.BlockSpec((1, tm, c), lambda b, i: (b, i, 0))
    head = pl.BlockSpec((1, MLA_HEADS, tm, MLA_SLAB), lambda b, i: (b, 0, i, 0))
    dva = MLA_V + ATT_ONES_ROWS
    vmem = 2 * (tm * kin * 4 + sum(a.size * a.dtype.itemsize for a in consts) + 2 * tm * HY_WIDTH * 4
                + 3 * MLA_HEADS * tm * MLA_SLAB * 2 + 4 * tm * MLA_SLAB * 4) + 8 * (tm + 16) * 3 * HY_WIDTH * 4
    return pl.pallas_call(
        _mla_in_kernel, grid=(bsz, n // tm),
        in_specs=_halo_specs(n, tm, kin) + [_bcast_spec(shift), _bcast_spec(scale)] + [full(a) for a in consts]
        + [pl.BlockSpec((tm, MLA_SLAB), lambda b, i: (i, 0))] * 4,
        out_specs=[row(HY_WIDTH), row(HY_WIDTH), head, head,
                   pl.BlockSpec((1, MLA_HEADS, dva, tm), lambda b, i: (b, 0, 0, i))],
        out_shape=[jax.ShapeDtypeStruct((bsz, n, HY_WIDTH), F32)] * 2
        + [jax.ShapeDtypeStruct((bsz, MLA_HEADS, n, MLA_SLAB), BF16)] * 2
        + [jax.ShapeDtypeStruct((bsz, MLA_HEADS, dva, n), BF16)],
        compiler_params=_params(("parallel", "parallel"), vmem), name="mla_in_proj",
    )(x, x, x, shift, scale, *consts, *tabs)


def _mixer_hyena_mla(x, ctx, sh_l, sc_l, sh_c, sc_c, p):
    n, nc = x.shape[1], ctx.shape[1]
    w = _mla_weights(p)
    x0_l, z_l, q_l, k_l, vt_l = _mla_in_proj(x, sh_l, sc_l, w, rotate=True)
    x0_c, z_c, q_c, k_c, vt_c = _mla_in_proj(ctx, sh_c, sc_c, w, rotate=False)
    k_all = jnp.concatenate([k_c, k_l], 2)
    vt_all = jnp.concatenate([vt_c, vt_l], 3)
    att_l = _attention(q_l, k_all, vt_all, tq=min(ATT_Q_TILE, n), tk=_key_tile(nc + n))
    att_c = _attention(q_c, k_c, vt_c, tq=nc, tk=_key_tile(nc))
    filt = (p['filt_w1'], p['filt_b1'], p['filt_freq1'], p['filt_w2'], p['filt_b2'], p['filt_freq2'], p['filt_w3'])
    hyo_l = _hyena_sequence(z_l, x0_l, filt, p['skip'])
    hyo_c = _hyena_sequence(z_c, x0_c, filt, p['skip'])
    return (hyo_l, att_l), (hyo_c, att_c)


def _halo_specs(n, tm, k):
    nb = n // SUBLANES
    per = tm // SUBLANES
    return [pl.BlockSpec((1, tm, k), lambda b, i: (b, i, 0)),
            pl.BlockSpec((1, SUBLANES, k), lambda b, i: (b, jnp.maximum(i * per - 1, 0), 0)),
            pl.BlockSpec((1, SUBLANES, k), lambda b, i: (b, jnp.minimum((i + 1) * per, nb - 1), 0))]


def _conv3(u, cw_ref, cb_ref, tm):
    i = pl.program_id(1)
    rows = lax.broadcasted_iota(I32, (u.shape[0], 1), 0)
    inside = jnp.logical_and(jnp.logical_or(rows >= SUBLANES, i > 0),
                             jnp.logical_or(rows < tm + SUBLANES, i < pl.num_programs(1) - 1))
    u = jnp.where(inside, u, 0.0)
    prev = pltpu.roll(u, 1, 0)[SUBLANES:SUBLANES + tm]
    nxt = pltpu.roll(u, u.shape[0] - 1, 0)[SUBLANES:SUBLANES + tm]
    return cw_ref[0:1, :] * prev + cw_ref[1:2, :] * u[SUBLANES:SUBLANES + tm] + cw_ref[2:3, :] * nxt + cb_ref[...]


def _ssd_in_kernel(xc_ref, xp_ref, xn_ref, sh_ref, sc_ref, wz_ref, wx_ref, wdt_ref, cw_ref, cb_ref, dtb_ref,
                   z_ref, xs_ref, b_ref, c_ref, dt_ref):
    tm = xc_ref.shape[1]
    rows = jnp.concatenate([xp_ref[0], xc_ref[0], xn_ref[0]], axis=0)
    h = (rows * (1.0 + sc_ref[0]) + sh_ref[0]).astype(BF16)
    hc = h[SUBLANES:SUBLANES + tm]
    z_ref[0] = jnp.dot(hc, wz_ref[...], preferred_element_type=F32)
    y = _conv3(jnp.dot(h, wx_ref[...], preferred_element_type=F32), cw_ref, cb_ref, tm)
    xbc = y * jax.nn.sigmoid(y)
    xs_ref[0] = xbc[:, :SSD_INNER]
    b_ref[0] = xbc[:, SSD_INNER:SSD_INNER + SSD_BC]
    c_ref[0] = xbc[:, SSD_INNER + SSD_BC:]
    dt = jnp.dot(hc, wdt_ref[...], preferred_element_type=F32) + dtb_ref[...]
    dt_ref[0] = (jnp.maximum(dt, 0.0) + jnp.log1p(jnp.exp(-jnp.abs(dt)))).T


def _ssd_in_proj(x, shift, scale, w_z, w_xbc, w_dt, conv_w, conv_b, dt_bias):
    bsz, n, k = x.shape
    tm = min(ROW_TILE, n)
    assert n % tm == 0
    full = lambda a: pl.BlockSpec(a.shape, lambda b, i: (0,) * a.ndim)
    row = lambda c: pl.BlockSpec((1, tm, c), lambda b, i: (b, i, 0))
    nh2 = w_dt.shape[1]
    consts = [w_z, w_xbc, w_dt, conv_w, conv_b, dt_bias]
    widths = [SSD_INNER, SSD_INNER, SSD_BC, SSD_BC]
    vmem = 2 * (tm * k * 4 + sum(a.size * a.dtype.itemsize for a in consts) + tm * (sum(widths) + nh2) * 4) \
        + 6 * (tm + 2 * SUBLANES) * SSD_XBC * 4
    return pl.pallas_call(
        _ssd_in_kernel, grid=(bsz, n // tm),
        in_specs=_halo_specs(n, tm, k) + [_bcast_spec(shift), _bcast_spec(scale)] + [full(a) for a in consts],
        out_specs=[row(c) for c in widths] + [pl.BlockSpec((1, nh2, tm), lambda b, i: (b, 0, i))],
        out_shape=[jax.ShapeDtypeStruct((bsz, n, c), F32) for c in widths]
        + [jax.ShapeDtypeStruct((bsz, nh2, n), F32)],
        compiler_params=_params(("parallel", "parallel"), vmem), name="ssd_in_proj",
    )(x, x, x, shift, scale, *consts)


def _ssd_chunk(x_ref, b_ref, c_ref, dt_ref, a_ref, st_sc, y_ref, reverse):
    q = SSD_CHUNK
    dt = dt_ref[0]
    a = dt * a_ref[0]
    si = lax.broadcasted_iota(I32, (q, q), 0)
    li = lax.broadcasted_iota(I32, (q, q), 1)
    incl = jnp.where((si >= li) if reverse else (si <= li), 1.0, 0.0)
    hp = lax.Precision.HIGHEST
    acs = jnp.dot(a, incl, precision=hp, preferred_element_type=F32)
    tot = jnp.dot(a, jnp.ones((q, LANES), F32), precision=hp, preferred_element_type=F32)
    e_in = jnp.exp(acs)
    w_end = jnp.exp(tot - acs) * dt
    e_tot = jnp.exp(tot)
    acs_t = acs.T
    e_in_t = e_in.T
    mask = (li >= si) if reverse else (li <= si)
    cmat = c_ref[0]
    cb = lax.dot_general(cmat.astype(BF16), b_ref[0].astype(BF16), (((1,), (1,)), ((), ())),
                         preferred_element_type=F32)
    bt = b_ref[0].T
    first = lax.broadcasted_iota(I32, (q, LANES), 1) < SSD_HEADDIM
    ys = []
    for pr in range(SSD_HPG // 2):
        lo, hi = pr * LANES, (pr + 1) * LANES
        x_pair = x_ref[0, :, lo:hi].astype(BF16)
        st_pair = st_sc[:, lo:hi]
        rhs = jnp.concatenate([x_pair, st_pair.astype(BF16)], axis=0)
        y2, s2 = [], []
        for r in (2 * pr, 2 * pr + 1):
            seg = acs_t[:, r:r + 1] - acs[r:r + 1, :]
            m = cb * jnp.exp(jnp.where(mask, seg, -jnp.inf)) * dt[r:r + 1, :]
            lhs = jnp.concatenate([m.astype(BF16), (cmat * e_in_t[:, r:r + 1]).astype(BF16)], axis=1)
            y2.append(jnp.dot(lhs, rhs, preferred_element_type=F32))
            btr = (bt * w_end[r:r + 1, :]).astype(BF16)
            s2.append(e_tot[r:r + 1, :] * st_pair + jnp.dot(btr, x_pair, preferred_element_type=F32))
        ys.append(jnp.where(first, y2[0], y2[1]))
        st_sc[:, lo:hi] = jnp.where(first, s2[0], s2[1])
    y_ref[0] = jnp.concatenate(ys, axis=1)


def _ssd_scan_kernel(xf_ref, bf_ref, cf_ref, dtf_ref, af_ref, s0f_ref, xr_ref, br_ref, cr_ref, dtr_ref, ar_ref,
                     s0r_ref, yf_ref, sf_ref, yr_ref, sr_ref, stf_sc, str_sc):
    ci = pl.program_id(2)

    @pl.when(ci == 0)
    def _():
        stf_sc[...] = s0f_ref[0, 0]
        str_sc[...] = s0r_ref[0, 0]

    _ssd_chunk(xf_ref, bf_ref, cf_ref, dtf_ref, af_ref, stf_sc, yf_ref, reverse=False)
    _ssd_chunk(xr_ref, br_ref, cr_ref, dtr_ref, ar_ref, str_sc, yr_ref, reverse=True)

    @pl.when(ci == pl.num_programs(2) - 1)
    def _():
        sf_ref[0, 0] = stf_sc[...]
        sr_ref[0, 0] = str_sc[...]


def _ssd_scan(xs, bm, cm, dt_t, a_rep, s0_f, s0_r):
    bsz, n, _ = xs.shape
    q = SSD_CHUNK
    nc = n // q
    assert n % q == 0 and q == LANES
    gw = SSD_HPG * SSD_HEADDIM
    state = pl.BlockSpec((1, 1, SSD_STATE, gw), lambda b, g, c: (b, g, 0, 0))

    def side(d):
        cc = (lambda c: nc - 1 - c) if d else (lambda c: c)
        specs = [pl.BlockSpec((1, q, gw), lambda b, g, c: (b, cc(c), g)),
                 pl.BlockSpec((1, q, SSD_STATE), lambda b, g, c: (b, cc(c), g)),
                 pl.BlockSpec((1, q, SSD_STATE), lambda b, g, c: (b, cc(c), g)),
                 pl.BlockSpec((1, SSD_HPG, q), lambda b, g, c: (b, d * SSD_GROUPS + g, cc(c))),
                 pl.BlockSpec((1, SSD_HPG, LANES), lambda b, g, c: (d * SSD_GROUPS + g, 0, 0)),
                 state]
        return specs, [pl.BlockSpec((1, q, gw), lambda b, g, c: (b, cc(c), g)), state]

    (in_f, out_f), (in_r, out_r) = side(0), side(1)
    shapes = [jax.ShapeDtypeStruct((bsz, n, SSD_INNER), F32),
              jax.ShapeDtypeStruct((bsz, SSD_GROUPS, SSD_STATE, gw), F32)]
    vmem = 4 * (2 * q * gw * 4 + 2 * q * SSD_STATE * 4 + 2 * SSD_STATE * gw * 4) + 2 * SSD_STATE * gw * 4 \
        + 128 * q * q * 4
    return pl.pallas_call(
        _ssd_scan_kernel, grid=(bsz, SSD_GROUPS, nc), in_specs=in_f + in_r, out_specs=out_f + out_r,
        out_shape=shapes + shapes,
        scratch_shapes=[pltpu.VMEM((SSD_STATE, gw), F32), pltpu.VMEM((SSD_STATE, gw), F32)],
        compiler_params=_params(("parallel", "parallel", "arbitrary"), vmem), name="ssd_scan",
    )(xs, bm, cm, dt_t, a_rep, s0_f, xs, bm, cm, dt_t, a_rep, s0_r)


def _ssd_out_kernel(x_ref, yf_ref, yb_ref, xs_ref, z_ref, d_ref, ng_ref, w_ref, *epilogue_refs):
    z = z_ref[0]
    y = (yf_ref[0] + yb_ref[0] + xs_ref[0] * d_ref[0]) * (z * jax.nn.sigmoid(z))
    gw = SSD_INNER // SSD_GROUPS
    parts = []
    for g in range(SSD_GROUPS):
        yg = y[:, g * gw:(g + 1) * gw]
        parts.append(yg * lax.rsqrt(jnp.mean(yg * yg, -1, keepdims=True) + RMS_EPS))
    yn = (jnp.concatenate(parts, axis=1) * ng_ref[0]).astype(BF16)
    _mix_epilogue(x_ref[0], jnp.dot(yn, w_ref[...], preferred_element_type=F32), *epilogue_refs)


def _ssd_out(x, y_f, y_b, xs, z, d_rep, norm_g, w_out, gate, ln_g, ln_b, shift2, scale2):
    bsz, n, d = x.shape
    tm = min(ROW_TILE, n)
    assert n % tm == 0
    row = lambda c: pl.BlockSpec((1, tm, c), lambda b, i: (b, i, 0))
    vecs = [gate, ln_g, ln_b, shift2, scale2]
    in_specs = [row(d)] + [row(SSD_INNER)] * 4 + [_bcast_spec(d_rep), _bcast_spec(norm_g),
                                                 pl.BlockSpec(w_out.shape, lambda b, i: (0, 0))]
    in_specs += [_bcast_spec(a) for a in vecs]
    vmem = 2 * (3 * tm * d * 4 + 4 * tm * SSD_INNER * 4 + w_out.size * 2) + 4 * tm * SSD_INNER * 4
    return pl.pallas_call(
        _ssd_out_kernel, grid=(bsz, n // tm), in_specs=in_specs, out_specs=[row(d), row(d), row(d // 2)],
        out_shape=[jax.ShapeDtypeStruct((bsz, n, d), F32)] * 2 + [jax.ShapeDtypeStruct((bsz, n, d // 2), U32)],
        compiler_params=_params(("parallel", "parallel"), vmem), name="ssd_out",
    )(x, y_f, y_b, xs, z, d_rep, norm_g, w_out, *vecs)


def _mixer_ssd(x, ctx, sh_l, sc_l, sh_c, sc_c, p):
    a_all = -jnp.exp(jnp.concatenate([p['a_log_f'], p['a_log_b']]))
    a_rep = jnp.broadcast_to(a_all.reshape(2 * SSD_GROUPS, SSD_HPG, 1), (2 * SSD_GROUPS, SSD_HPG, LANES))
    w_in = p['w_in'].astype(BF16)
    consts = (w_in[:, :SSD_INNER], w_in[:, SSD_INNER:SSD_INNER + SSD_XBC], w_in[:, SSD_INNER + SSD_XBC:],
              p['conv_w'], p['conv_b'][None], jnp.concatenate([p['dt_bias_f'], p['dt_bias_b']])[None])
    _, xc, bc, cc, dtc = _ssd_in_proj(ctx, sh_c, sc_c, *consts)
    zl, xl, bl, cl, dtl = _ssd_in_proj(x, sh_l, sc_l, *consts)
    s0 = jnp.zeros((ctx.shape[0], SSD_GROUPS, SSD_STATE, SSD_HPG * SSD_HEADDIM), F32)
    _, sc_f, _, sc_b = _ssd_scan(xc, bc, cc, dtc, a_rep, s0, s0)
    y_f, _, y_b, _ = _ssd_scan(xl, bl, cl, dtl, a_rep, sc_f, sc_b)
    return y_f, y_b, xl, zl


def kernel(x, c, ctx, c_ctx, mod_w, mod_b, ln_mix_g, ln_mix_b, ln_ffn_g, ln_ffn_b, a_w_in, hy_conv_w, hy_conv_b, hy_filt_w1, hy_filt_b1, hy_filt_freq1, hy_filt_w2, hy_filt_b2, hy_filt_freq2, hy_filt_w3, hy_skip, mla_q_norm, mla_w_qb, mla_kv_norm, mla_w_kvb, a_w_out, ssd_w_in, ssd_conv_w, ssd_conv_b, ssd_dt_bias_f, ssd_dt_bias_b, ssd_a_log_f, ssd_a_log_b, ssd_d, ssd_norm_g, ssd_w_out, router_w, router_bias, exp_w_gu, exp_w_down, sh_w_gu, sh_w_down):
    bsz, n_lat, d = x.shape
    n_ctx = ctx.shape[1]
    pad = -(bsz + 1) % SUBLANES
    cond = jnp.concatenate([c, c_ctx[None], jnp.zeros((pad, d), F32)], 0)
    for l in range(DEPTH):
        last = l == DEPTH - 1
        i = l // 2
        mods = _modulation(cond, mod_w, mod_b, l)
        mod = mods[:bsz].reshape(bsz, N_MOD, 1, d)
        mod_c = mods[bsz:bsz + 1].reshape(1, N_MOD, 1, d)
        sh1, sc1, g1, sh2, sc2, g2 = [mod[:, j] for j in range(N_MOD)]
        csh1, csc1, cg1, csh2, csc2, cg2 = [mod_c[:, j] for j in range(N_MOD)]
        vec = lambda a: a.reshape(1, 1, d)
        if l % 2 == 0:
            p = {"w_in": a_w_in[i], "conv_w": hy_conv_w[i], "conv_b": hy_conv_b[i],
                 "filt_w1": hy_filt_w1[i], "filt_b1": hy_filt_b1[i], "filt_freq1": hy_filt_freq1[i],
                 "filt_w2": hy_filt_w2[i], "filt_b2": hy_filt_b2[i], "filt_freq2": hy_filt_freq2[i],
                 "filt_w3": hy_filt_w3[i], "skip": hy_skip[i], "q_norm": mla_q_norm[i], "w_qb": mla_w_qb[i],
                 "kv_norm": mla_kv_norm[i], "w_kvb": mla_w_kvb[i]}
            ys_l, ys_c = _mixer_hyena_mla(x, ctx, sh1, sc1, csh1, csc1, p)
            w_out = a_w_out[i].astype(BF16)
            ws = [w_out[:HY_WIDTH], w_out[HY_WIDTH:]]
            x, ff_x, fp_x = _mix_out(x, ys_l, ws, g1, vec(ln_mix_g[l]), vec(ln_mix_b[l]), sh2, sc2)
        else:
            p = {"w_in": ssd_w_in[i], "conv_w": ssd_conv_w[i], "conv_b": ssd_conv_b[i],
                 "dt_bias_f": ssd_dt_bias_f[i], "dt_bias_b": ssd_dt_bias_b[i],
                 "a_log_f": ssd_a_log_f[i], "a_log_b": ssd_a_log_b[i]}
            assert last
            y_f, y_b, xs, z = _mixer_ssd(x, ctx, sh1, sc1, csh1, csc1, p)
            d_rep = jnp.repeat(ssd_d[i], SSD_HEADDIM).reshape(1, 1, SSD_INNER)
            x, ff_x, fp_x = _ssd_out(x, y_f, y_b, xs, z, d_rep, ssd_norm_g[i].reshape(1, 1, SSD_INNER),
                                     ssd_w_out[i].astype(BF16), g1, vec(ln_mix_g[l]), vec(ln_mix_b[l]), sh2, sc2)
        sh_gu = sh_w_gu[l].astype(BF16)
        sh_down = sh_w_down[l].astype(BF16)
        ln_g, ln_b = vec(ln_ffn_g[l]), vec(ln_ffn_b[l])
        moe_w = (router_w[l], router_bias[l], exp_w_gu, exp_w_down, l)
        if last:
            yb, pos, w = _moe_dispatch_experts(ff_x.reshape(-1, d), fp_x.reshape(-1, d // 2), *moe_w)
            x = _ffn_out(x, ff_x, yb, pos, w, 0, sh_gu, sh_down, g2, ln_g, ln_b)
        else:
            ctx, ff_c, fp_c = _mix_out(ctx, ys_c, ws, cg1, vec(ln_mix_g[l]), vec(ln_mix_b[l]), csh2, csc2)
            tokens = jnp.concatenate([ff_c.reshape(-1, d), ff_x.reshape(-1, d)], 0)
            packed = jnp.concatenate([fp_c.reshape(-1, d // 2), fp_x.reshape(-1, d // 2)], 0)
            yb, pos, w = _moe_dispatch_experts(tokens, packed, *moe_w)
            assert (bsz * n_ctx) % ROUTE_TOKENS == 0
            ctx = _ffn_out(ctx, ff_c, yb, pos, w, 0, sh_gu, sh_down, cg2, ln_g, ln_b)
            x = _ffn_out(x, ff_x, yb, pos, w, bsz * n_ctx // ROUTE_TOKENS, sh_gu, sh_down, g2, ln_g, ln_b)
    return x
```

```python
import functools
import math

import jax
import jax.numpy as jnp
from jax import lax
from jax.experimental import pallas as pl
from jax.experimental.pallas import tpu as pltpu

F32 = jnp.float32
BF16 = jnp.bfloat16
I32 = jnp.int32
U32 = jnp.uint32

D_MODEL = 1024
DEPTH = 2
GRID_W = 64
N_MOD = 6

HY_WIDTH = 512
HY_EMB = 33
HY_BANDS = (HY_EMB - 1) // 2
HY_TARGET = 1e-2
HY_FAST_DECAY = 0.3
HY_SLOW_DECAY = 1.5
HY_DECAY_MIN = math.log(HY_TARGET) / HY_SLOW_DECAY
HY_DECAY_MAX = math.log(HY_TARGET) / HY_FAST_DECAY

MLA_HEADS = 8
MLA_NOPE = 64
MLA_ROPE = 32
MLA_V = 64
MLA_Q_RANK = 256
MLA_KV_RANK = 128
MLA_QK = MLA_NOPE + MLA_ROPE
MLA_SCALE = MLA_QK ** -0.5
ROPE_THETA = 10000.0
LOG2E = math.log2(math.e)

OFF_Q = 3 * HY_WIDTH
OFF_KV = OFF_Q + MLA_Q_RANK
OFF_KPE = OFF_KV + MLA_KV_RANK

SSD_INNER = 2 * D_MODEL
SSD_HEADDIM = 64
SSD_HEADS = SSD_INNER // SSD_HEADDIM
SSD_GROUPS = 4
SSD_STATE = 128
SSD_CHUNK = 128
SSD_BC = SSD_GROUPS * SSD_STATE
SSD_XBC = SSD_INNER + 2 * SSD_BC
SSD_HPG = SSD_HEADS // SSD_GROUPS

N_EXPERTS = 256
TOP_K = 8
N_EXPERT_GROUPS = 8
TOPK_GROUPS = 4
EXPERT_DIM = 256
ROUTED_SCALE = 2.5

DN_ALPHA = (2 * DEPTH) ** 0.25
LN_EPS = 1e-5
RMS_EPS = 1e-6

LANES = 128
SUBLANES = 8
V7X_VMEM_CAP = 56 * 1024 * 1024

MOE_ROWS = 512
ROUTE_TOKENS = 256
ROW_TILE = 256
ATT_Q_TILE = 4096
ATT_HEADS_PER_STEP = 2
DMA_PRIORITIES = 2
ATT_ONES_ROWS = 16
MLA_SLAB = LANES
DFT_INNER = 128
DFT_COLS = 2048
DFT_BATCH = 2
HY_DIRECT_MAX = 512
FILT_ROWS = 512
MOD_COLS = 1024


def _params(semantics, vmem_bytes, **kw):
    limit = int(min(max(vmem_bytes * 5 // 4, 32 * 1024 * 1024), V7X_VMEM_CAP))
    return pltpu.CompilerParams(dimension_semantics=semantics, vmem_limit_bytes=limit, **kw)


def _bcast_spec(a):
    if a.shape[0] == 1:
        return pl.BlockSpec((1, 1, a.shape[2]), lambda b, i: (0, 0, 0))
    return pl.BlockSpec((1, 1, a.shape[2]), lambda b, i: (b, 0, 0))


def _layer_norm(r, g, b):
    mu = jnp.mean(r, -1, keepdims=True)
    c = r - mu
    var = jnp.mean(c * c, -1, keepdims=True)
    return c * lax.rsqrt(var + LN_EPS) * g + b


def _swiglu_rows(xb, w_gu, w_down):
    h = jnp.dot(xb, w_gu, preferred_element_type=F32)
    half = h.shape[1] // 2
    g, u = h[:, :half], h[:, half:]
    a = (g * jax.nn.sigmoid(g) * u).astype(BF16)
    return jnp.dot(a, w_down, preferred_element_type=F32)


def _mod_kernel(c_ref, w_ref, b_ref, o_ref):
    c = c_ref[...]
    o_ref[...] = jnp.dot(c * jax.nn.sigmoid(c), w_ref[0], precision=lax.Precision.HIGHEST,
                         preferred_element_type=F32) + b_ref[0]


def _modulation(cond, mod_w, mod_b, layer):
    r, d = cond.shape
    nout = mod_w.shape[2]
    assert nout % MOD_COLS == 0 and r % SUBLANES == 0
    return pl.pallas_call(
        _mod_kernel, grid=(nout // MOD_COLS,),
        in_specs=[pl.BlockSpec((r, d), lambda j: (0, 0)), pl.BlockSpec((1, d, MOD_COLS), lambda j: (layer, 0, j)),
                  pl.BlockSpec((1, 1, MOD_COLS), lambda j: (layer, 0, j))],
        out_specs=pl.BlockSpec((r, MOD_COLS), lambda j: (0, j)),
        out_shape=jax.ShapeDtypeStruct((r, nout), F32),
        compiler_params=_params(("parallel",), 2 * (d * MOD_COLS * 4 + 2 * r * MOD_COLS * 4)), name="modulation",
    )(cond, mod_w, mod_b.reshape(mod_b.shape[0], 1, nout))


def _pack_pairs(x):
    half = x.shape[1] // 2
    hi = pltpu.bitcast(x[:, :half].astype(BF16).astype(F32), U32)
    lo = pltpu.bitcast(x[:, half:].astype(BF16).astype(F32), U32)
    return hi | (lo >> 16)


def _unpack_pairs(u):
    hi = pltpu.bitcast(u & jnp.uint32(0xFFFF0000), F32)
    lo = pltpu.bitcast(u << 16, F32)
    return jnp.concatenate([hi, lo], axis=1)


def _mix_epilogue(x, mix, gate_ref, g_ref, b_ref, sh_ref, sc_ref, xo_ref, ff_ref, ffp_ref):
    xn = _layer_norm(DN_ALPHA * x + gate_ref[0] * mix, g_ref[0], b_ref[0])
    xo_ref[0] = xn
    ff = xn * (1.0 + sc_ref[0]) + sh_ref[0]
    ff_ref[0] = ff
    ffp_ref[0] = _pack_pairs(ff)


def _mix_out_kernel(*refs, n_y):
    x_ref = refs[0]
    y_refs = refs[1:1 + n_y]
    w_refs = refs[1 + n_y:1 + 2 * n_y]
    y = None
    for y_ref, w_ref in zip(y_refs, w_refs):
        t = jnp.dot(y_ref[0].astype(BF16), w_ref[...], preferred_element_type=F32)
        y = t if y is None else y + t
    _mix_epilogue(x_ref[0], y, *refs[1 + 2 * n_y:])


def _mix_out(x, ys, ws, gate, ln_g, ln_b, shift2, scale2):
    bsz, n, d = x.shape
    tm = min(ROW_TILE, n)
    assert n % tm == 0
    row = lambda c: pl.BlockSpec((1, tm, c), lambda b, i: (b, i, 0))
    vecs = [gate, ln_g, ln_b, shift2, scale2]
    in_specs = [row(d)] + [row(y.shape[2]) for y in ys]
    in_specs += [pl.BlockSpec(w.shape, lambda b, i: (0, 0)) for w in ws]
    in_specs += [_bcast_spec(a) for a in vecs]
    vmem = 2 * (3 * tm * d * 4 + sum(tm * y.shape[2] * 4 + w.size * 2 for y, w in zip(ys, ws)))
    return pl.pallas_call(
        functools.partial(_mix_out_kernel, n_y=len(ys)),
        grid=(bsz, n // tm), in_specs=in_specs, out_specs=[row(d), row(d), row(d // 2)],
        out_shape=[jax.ShapeDtypeStruct((bsz, n, d), F32)] * 2 + [jax.ShapeDtypeStruct((bsz, n, d // 2), U32)],
        compiler_params=_params(("parallel", "parallel"), vmem), name="mix_out",
    )(x, *ys, *ws, *vecs)


def _attn_kernel(q_ref, k_ref, vt_ref, o_ref, *, tk, hp):
    nk = k_ref.shape[2]
    tq = q_ref.shape[2]
    dva = vt_ref.shape[2]
    dv = dva - ATT_ONES_ROWS

    def body(j, carry):
        off = pl.multiple_of(j * tk, tk)
        new = []
        for h in range(hp):
            m_prev, acc = carry[h]
            st = lax.dot_general(k_ref[0, h, pl.ds(off, tk), :], q_ref[0, h], (((1,), (1,)), ((), ())),
                                 preferred_element_type=F32)
            m_new = jnp.maximum(m_prev, jnp.max(st, 0, keepdims=True))
            p = jnp.exp2(st - m_new).astype(BF16)
            alpha = jnp.exp2(m_prev - m_new)
            acc = alpha * acc + jnp.dot(vt_ref[0, h, :, pl.ds(off, tk)], p, preferred_element_type=F32)
            new.append((m_new, acc))
        return tuple(new)

    init = tuple((jnp.full((1, tq), -jnp.inf, F32), jnp.zeros((dva, tq), F32)) for _ in range(hp))
    fin = lax.fori_loop(0, nk // tk, body, init)
    outs = [acc[:dv] / acc[dv:dv + 1] for _, acc in fin]
    o_ref[0] = jnp.concatenate(outs, 0).T


def _attention(q, k, vt, tq, tk):
    bsz, h, nq, dk = q.shape
    nk, dva = k.shape[2], vt.shape[2]
    dv = dva - ATT_ONES_ROWS
    hp = ATT_HEADS_PER_STEP
    assert nq % tq == 0 and nk % tk == 0 and h % hp == 0
    vmem = 2 * hp * (tq * LANES * 2 + nk * LANES * 2 + dva * nk * 2) + 2 * tq * hp * dv * 4 + 6 * hp * tk * tq * 4
    return pl.pallas_call(
        functools.partial(_attn_kernel, tk=tk, hp=hp), grid=(bsz, h // hp, nq // tq),
        in_specs=[pl.BlockSpec((1, hp, tq, dk), lambda b, g, i: (b, g, i, 0)),
                  pl.BlockSpec((1, hp, nk, dk), lambda b, g, i: (b, g, 0, 0)),
                  pl.BlockSpec((1, hp, dva, nk), lambda b, g, i: (b, g, 0, 0))],
        out_specs=pl.BlockSpec((1, tq, hp * dv), lambda b, g, i: (b, i, g)),
        out_shape=jax.ShapeDtypeStruct((bsz, nq, h * dv), F32),
        compiler_params=_params(("parallel", "parallel", "arbitrary"), vmem), name="mla_attention",
    )(q, k, vt)


def _key_tile(nk):
    for t in (768, 512, 384, 256, 128):
        if nk % t == 0:
            return t
    return nk


def _router_kernel(x_ref, wt_ref, bias_ref, upper_ref, idx_ref, w_ref, rank_ref, cnt_ref, run_sc):
    i = pl.program_id(0)
    tm = x_ref.shape[0]

    @pl.when(i == 0)
    def _():
        run_sc[...] = jnp.zeros(run_sc.shape, F32)

    logits = lax.dot_general(wt_ref[...], x_ref[...], (((1,), (1,)), ((), ())),
                             precision=lax.Precision.HIGHEST, preferred_element_type=F32)
    sc = jax.nn.sigmoid(logits)
    ch = sc + bias_ref[:, :1]
    neg = -jnp.inf
    chg = ch.reshape(N_EXPERT_GROUPS, N_EXPERTS // N_EXPERT_GROUPS, tm)
    m1 = jnp.max(chg, axis=1)
    eq = chg == m1[:, None, :]
    cnt = jnp.sum(eq.astype(F32), axis=1)
    m2 = jnp.max(jnp.where(eq, neg, chg), axis=1)
    g2 = m1 + jnp.where(cnt >= 2.0, m1, m2)
    gi = lax.broadcasted_iota(I32, g2.shape, 0)
    beaten = jnp.zeros(g2.shape, F32)
    for g in range(N_EXPERT_GROUPS):
        row = g2[g:g + 1, :]
        beaten = beaten + jnp.where(row > g2, 1.0, jnp.where(row == g2, jnp.where(gi > g, 1.0, 0.0), 0.0))
    keep = beaten < float(TOPK_GROUPS)
    cur = jnp.where(keep[:, None, :], chg, neg).reshape(N_EXPERTS, tm)
    eidx = lax.broadcasted_iota(I32, (N_EXPERTS, tm), 0)
    multi = jnp.zeros((N_EXPERTS, tm), F32)
    hits, idx_rows, w_rows = [], [], []
    for _ in range(TOP_K):
        m = jnp.max(cur, axis=0, keepdims=True)
        sel = jnp.min(jnp.where(cur == m, eidx, N_EXPERTS), axis=0, keepdims=True)
        hit = eidx == sel
        idx_rows.append(sel)
        w_rows.append(jnp.sum(jnp.where(hit, sc, 0.0), axis=0, keepdims=True))
        cur = jnp.where(hit, neg, cur)
        multi = multi + jnp.where(hit, 1.0, 0.0)
        hits.append(hit)
    base = jnp.concatenate([run_sc[...]] * (tm // LANES), axis=1)
    before = jnp.dot(multi.astype(BF16), upper_ref[...], preferred_element_type=F32) + base
    rank_rows = [jnp.sum(jnp.where(hit, before, 0.0), axis=0, keepdims=True) for hit in hits]
    w = jnp.concatenate(w_rows, axis=0)
    idx_ref[...] = jnp.concatenate(idx_rows, axis=0)
    w_ref[...] = (w / jnp.sum(w, axis=0, keepdims=True) * ROUTED_SCALE).T
    rank_ref[...] = jnp.concatenate(rank_rows, axis=0).astype(I32)
    run_sc[...] = run_sc[...] + jnp.dot(multi.astype(BF16), jnp.ones((tm, LANES), BF16), preferred_element_type=F32)
    cnt_ref[...] = run_sc[...]


def _router(tokens, router_w, router_bias):
    t, d = tokens.shape
    tm = ROUTE_TOKENS
    assert t % tm == 0
    wt = router_w.T
    bias = jnp.broadcast_to(router_bias.astype(F32)[:, None], (N_EXPERTS, LANES))
    r = jnp.arange(tm)
    upper = (r[:, None] < r[None, :]).astype(BF16)
    col = pl.BlockSpec((TOP_K, tm), lambda i: (0, i))
    full = lambda a: pl.BlockSpec(a.shape, lambda i: (0,) * a.ndim)
    vmem = 2 * (tm * d * 4 + wt.size * 4) + 40 * N_EXPERTS * tm * 4
    return pl.pallas_call(
        _router_kernel, grid=(t // tm,),
        in_specs=[pl.BlockSpec((tm, d), lambda i: (i, 0)), full(wt), full(bias), full(upper)],
        out_specs=[col, pl.BlockSpec((tm, TOP_K), lambda i: (i, 0)), col,
                   pl.BlockSpec((N_EXPERTS, LANES), lambda i: (0, 0))],
        out_shape=[jax.ShapeDtypeStruct((TOP_K, t), I32), jax.ShapeDtypeStruct((t, TOP_K), F32),
                   jax.ShapeDtypeStruct((TOP_K, t), I32), jax.ShapeDtypeStruct((N_EXPERTS, LANES), F32)],
        scratch_shapes=[pltpu.VMEM((N_EXPERTS, LANES), F32)],
        compiler_params=_params(("arbitrary",), vmem), name="moe_router",
    )(tokens, wt, bias, upper)


def _positions_kernel(idx_ref, rank_ref, start_ref, pos_ref):
    tm = idx_ref.shape[1]
    eidx = lax.broadcasted_iota(I32, (N_EXPERTS, tm), 0)
    start = jnp.concatenate([start_ref[...]] * (tm // LANES), axis=1)
    rows = [jnp.sum(jnp.where(eidx == idx_ref[k:k + 1, :], start, 0), axis=0, keepdims=True) for k in range(TOP_K)]
    pos_ref[...] = jnp.concatenate(rows, axis=0) + rank_ref[...]


def _positions(idx, rank, pad_start):
    t = idx.shape[1]
    tm = ROUTE_TOKENS
    start = jnp.broadcast_to(pad_start.astype(I32)[:, None], (N_EXPERTS, LANES))
    col = pl.BlockSpec((TOP_K, tm), lambda i: (0, i))
    return pl.pallas_call(
        _positions_kernel, grid=(t // tm,),
        in_specs=[col, col, pl.BlockSpec((N_EXPERTS, LANES), lambda i: (0, 0))], out_specs=col,
        out_shape=jax.ShapeDtypeStruct((TOP_K, t), I32),
        compiler_params=_params(("parallel",), 8 * N_EXPERTS * tm * 4), name="moe_positions",
    )(idx, rank, start)


def _moe_plan(counts, t_cap):
    n_blocks = -(-(t_cap * TOP_K) // MOE_ROWS) + N_EXPERTS
    c = counts.astype(I32)
    padded = (c + MOE_ROWS - 1) // MOE_ROWS * MOE_ROWS
    pad_end = jnp.cumsum(padded)
    pad_start = pad_end - padded
    block_e = jnp.minimum(jnp.searchsorted(pad_end, jnp.arange(n_blocks, dtype=I32) * MOE_ROWS, side='right'),
                          N_EXPERTS - 1).astype(I32)
    n_valid = (pad_end[-1:] // MOE_ROWS).astype(I32)
    return pad_start, block_e, n_valid, n_blocks * MOE_ROWS


def _row_copy(src, dst, sem):
    return pltpu.make_async_copy(src, dst, sem)


def _dispatch_kernel(pos_ref, x_ref, buf_in_ref, buf_ref, sem):
    del buf_in_ref
    tm = x_ref.shape[0]

    def issue(g, carry):
        base = pl.multiple_of(g * SUBLANES, SUBLANES)
        for j in range(SUBLANES):
            for k in range(TOP_K):
                _row_copy(x_ref.at[pl.ds(base + j, 1)], buf_ref.at[pl.ds(pos_ref[k, base + j], 1)],
                          sem).start(priority=k % DMA_PRIORITIES)
        return carry

    lax.fori_loop(0, tm // SUBLANES, issue, 0)
    for k in range(TOP_K):
        _row_copy(x_ref, buf_ref.at[pl.ds(0, tm)], sem).wait()


def _dispatch(pos, tokens, init):
    t, d = tokens.shape
    tm = ROUTE_TOKENS
    cap = init.shape[0]
    return pl.pallas_call(
        _dispatch_kernel, grid=(t // tm,),
        in_specs=[pl.BlockSpec((TOP_K, tm), lambda i: (0, i), memory_space=pltpu.SMEM),
                  pl.BlockSpec((tm, d), lambda i: (i, 0)),
                  pl.BlockSpec(memory_space=pl.ANY)],
        out_specs=pl.BlockSpec(memory_space=pl.ANY),
        out_shape=jax.ShapeDtypeStruct((cap, d), tokens.dtype),
        scratch_shapes=[pltpu.SemaphoreType.DMA(())],
        input_output_aliases={2: 0},
        compiler_params=_params(("arbitrary",), 2 * tm * d * 4, has_side_effects=True),
        name="moe_dispatch",
    )(pos, tokens, init)


def _experts_kernel(be_ref, nv_ref, x_ref, wgu_ref, wdn_ref, o_ref, wgu_sc, wdn_sc):
    i = pl.program_id(0)

    @pl.when(i < nv_ref[0])
    def _():
        @pl.when(jnp.logical_or(i == 0, be_ref[i] != be_ref[jnp.maximum(i - 1, 0)]))
        def _():
            wgu_sc[...] = wgu_ref[0, 0].astype(BF16)
            wdn_sc[...] = wdn_ref[0, 0].astype(BF16)

        x = _unpack_pairs(x_ref[...]).astype(BF16)
        o_ref[...] = _pack_pairs(_swiglu_rows(x, wgu_sc[...], wdn_sc[...]))

    @pl.when(i >= nv_ref[0])
    def _():
        o_ref[...] = jnp.zeros(o_ref.shape, U32)


def _moe_experts(xb, block_e, n_valid, w_gu, w_down, layer):
    cap, dp = xb.shape
    d = 2 * dp
    gu = w_gu.shape[3]
    ed = w_down.shape[2]
    live = lambda i, be, nv: jnp.maximum(jnp.minimum(i, nv[0] - 1), 0)
    grid_spec = pltpu.PrefetchScalarGridSpec(
        num_scalar_prefetch=2, grid=(cap // MOE_ROWS,),
        in_specs=[pl.BlockSpec((MOE_ROWS, dp), lambda i, be, nv: (live(i, be, nv), 0)),
                  pl.BlockSpec((1, 1, d, gu), lambda i, be, nv: (layer, be[i], 0, 0)),
                  pl.BlockSpec((1, 1, ed, d), lambda i, be, nv: (layer, be[i], 0, 0))],
        out_specs=pl.BlockSpec((MOE_ROWS, dp), lambda i, be, nv: (i, 0)),
        scratch_shapes=[pltpu.VMEM((d, gu), BF16), pltpu.VMEM((ed, d), BF16)])
    vmem = 2 * (2 * MOE_ROWS * dp * 4 + d * gu * 4 + ed * d * 4) + (d * gu + ed * d) * 2 + 6 * MOE_ROWS * d * 4
    return pl.pallas_call(
        _experts_kernel, grid_spec=grid_spec, out_shape=jax.ShapeDtypeStruct((cap, dp), U32),
        compiler_params=_params(("arbitrary",), vmem), name="moe_experts",
    )(block_e, n_valid, xb, w_gu, w_down)


def _ffn_out_kernel(pos_ref, w_ref, x_ref, ff_ref, yb_ref, wgu_ref, wdn_ref, gate_ref, g_ref, b_ref, xo_ref,
                    rows_sc, sem):
    tm = x_ref.shape[1]

    def issue(g, carry):
        base = pl.multiple_of(g * SUBLANES, SUBLANES)
        for j in range(SUBLANES):
            for k in range(TOP_K):
                _row_copy(yb_ref.at[pl.ds(pos_ref[k, base + j], 1)], rows_sc.at[k, pl.ds(base + j, 1)],
                          sem).start(priority=k % DMA_PRIORITIES)
        return carry

    lax.fori_loop(0, tm // SUBLANES, issue, 0)
    out = _swiglu_rows(ff_ref[0].astype(BF16), wgu_ref[...], wdn_ref[...])
    for k in range(TOP_K):
        _row_copy(yb_ref.at[pl.ds(0, tm)], rows_sc.at[k], sem).wait()
    for k in range(TOP_K):
        out = out + w_ref[:, k:k + 1] * _unpack_pairs(rows_sc[k])
    xo_ref[0] = _layer_norm(DN_ALPHA * x_ref[0] + gate_ref[0] * out, g_ref[0], b_ref[0])


def _ffn_out(x, ff, yb, pos, w, tile0, sh_gu, sh_down, gate, ln_g, ln_b):
    bsz, n, d = x.shape
    tm = min(ROUTE_TOKENS, n)
    nt = n // tm
    row = pl.BlockSpec((1, tm, d), lambda b, i: (b, i, 0))
    vecs = [gate, ln_g, ln_b]
    in_specs = [pl.BlockSpec((TOP_K, tm), lambda b, i: (0, tile0 + b * nt + i), memory_space=pltpu.SMEM),
                pl.BlockSpec((tm, TOP_K), lambda b, i: (tile0 + b * nt + i, 0)),
                row, row, pl.BlockSpec(memory_space=pl.ANY),
                pl.BlockSpec(sh_gu.shape, lambda b, i: (0, 0)), pl.BlockSpec(sh_down.shape, lambda b, i: (0, 0))]
    in_specs += [_bcast_spec(a) for a in vecs]
    vmem = TOP_K * tm * d * 2 + 2 * (3 * tm * d * 4 + sh_gu.size * 2 + sh_down.size * 2) + 6 * tm * d * 4
    return pl.pallas_call(
        _ffn_out_kernel, grid=(bsz, nt), in_specs=in_specs, out_specs=row,
        out_shape=jax.ShapeDtypeStruct((bsz, n, d), F32),
        scratch_shapes=[pltpu.VMEM((TOP_K, tm, d // 2), U32), pltpu.SemaphoreType.DMA(())],
        compiler_params=_params(("arbitrary", "arbitrary"), vmem), name="moe_combine_ffn_out",
    )(pos, w, x, ff, yb, sh_gu, sh_down, *vecs)


def _moe_dispatch_experts(tokens, packed, router_w, router_bias, w_gu, w_down, layer, t_cap, init=None):
    idx, w, rank, counts = _router(tokens, router_w, router_bias)
    pad_start, block_e, n_valid, cap = _moe_plan(counts[:, 0], t_cap)
    pos = _positions(idx, rank, pad_start)
    if init is None:
        init = jnp.zeros((cap, packed.shape[1]), packed.dtype)
    xb = _dispatch(pos, packed, init)
    return _moe_experts(xb, block_e, n_valid, w_gu, w_down, layer), pos, w


def _rope_tables(n):
    rows = n // GRID_W
    row = jnp.repeat(jnp.arange(rows), GRID_W).astype(F32)
    col = jnp.tile(jnp.arange(GRID_W), rows).astype(F32)
    half = MLA_ROPE // 2
    inv = ROPE_THETA ** (-jnp.arange(0, half, 2, dtype=F32) / half)
    ang = jnp.concatenate([row[:, None] * inv, col[:, None] * inv], -1)
    return jnp.cos(ang), jnp.sin(ang)


def _filter_kernel(fr_ref, w1_ref, b1_ref, f1_ref, w2_ref, b2_ref, f2_ref, w3_ref, dl_ref, k_ref, l1_ref, *, n):
    i = pl.program_id(0)
    tr = k_ref.shape[0]
    hp = lax.Precision.HIGHEST
    tap = i * tr + lax.broadcasted_iota(I32, (tr, 1), 0)
    lag = jnp.where(tap < n, tap, 2 * n - tap).astype(F32)
    t = lag * (1.0 / (n - 1))
    ang = (2.0 * math.pi / n) * lag * fr_ref[...]
    lane = lax.broadcasted_iota(I32, (tr, LANES), 1)
    z = jnp.where(lane == 0, t, jnp.where(lane <= HY_BANDS, jnp.cos(ang),
                                          jnp.where(lane <= 2 * HY_BANDS, -jnp.sin(ang), 0.0)))
    h = jnp.sin(f1_ref[...] * (jnp.dot(z, w1_ref[...], precision=hp, preferred_element_type=F32) + b1_ref[...]))
    h = jnp.sin(f2_ref[...] * (jnp.dot(h, w2_ref[...], precision=hp, preferred_element_type=F32) + b2_ref[...]))
    h = jnp.dot(h, w3_ref[...], precision=hp, preferred_element_type=F32)
    hsel = jnp.where(tap < n, h[:, :HY_WIDTH], h[:, HY_WIDTH:])
    k = jnp.where(tap == n, 0.0, hsel * jnp.exp(-t * dl_ref[...]))
    k_ref[...] = k

    @pl.when(i == 0)
    def _():
        l1_ref[...] = jnp.zeros(l1_ref.shape, F32)

    l1_ref[...] = l1_ref[...] + jnp.sum(jnp.abs(k), axis=0, keepdims=True)


def _hyena_filter(n, w1, b1, f1, w2, b2, f2, w3):
    nn = 2 * n
    tr = min(FILT_ROWS, nn)
    assert nn % tr == 0
    emb, ffn = w1.shape
    fr = jnp.linspace(1e-4, HY_BANDS - 1, HY_BANDS, dtype=F32)
    fr_l = jnp.concatenate([jnp.zeros((1,), F32), fr, fr, jnp.zeros((LANES - emb,), F32)])[None]
    w1p = jnp.concatenate([w1, jnp.zeros((LANES - emb, ffn), F32)], 0)
    deltas = jnp.abs(jnp.linspace(HY_DECAY_MIN, HY_DECAY_MAX, HY_WIDTH, dtype=F32))[None]
    ops = [fr_l, w1p, b1[None], f1[None], w2, b2[None], f2[None], w3, deltas]
    full = lambda a: pl.BlockSpec(a.shape, lambda i: (0,) * a.ndim)
    return pl.pallas_call(
        functools.partial(_filter_kernel, n=n), grid=(nn // tr,), in_specs=[full(a) for a in ops],
        out_specs=[pl.BlockSpec((tr, HY_WIDTH), lambda i: (i, 0)), pl.BlockSpec((1, HY_WIDTH), lambda i: (0, 0))],
        out_shape=[jax.ShapeDtypeStruct((nn, HY_WIDTH), F32), jax.ShapeDtypeStruct((1, HY_WIDTH), F32)],
        compiler_params=_params(("arbitrary",), 16 * tr * 2 * HY_WIDTH * 4), name="hyena_filter",
    )(*ops)


def _split(a):
    hi = a.astype(BF16)
    return hi, (a - hi.astype(F32)).astype(BF16)


def _dot3(a_hi, a_lo, x):
    x_hi, x_lo = _split(x)
    return (jnp.dot(a_hi, x_hi, preferred_element_type=F32) + jnp.dot(a_lo, x_hi, preferred_element_type=F32)
            + jnp.dot(a_hi, x_lo, preferred_element_type=F32))


def _cis(num, den):
    ang = (2.0 * math.pi / den) * (num % den).astype(F32)
    return jnp.cos(ang), jnp.sin(ang)


def _dft_outer_kernel(f_hi_ref, f_lo_ref, x_ref, o_ref):
    o_ref[0] = _dot3(f_hi_ref[...], f_lo_ref[...], x_ref[0])


def _dft_outer(x, f_hi, f_lo):
    bsz, k, m = x.shape
    r = f_hi.shape[0]
    tn = min(DFT_COLS, m)
    assert m % tn == 0
    vmem = 2 * (2 * f_hi.size * 2 + k * tn * 4 + r * tn * 4) + 3 * (k + r) * tn * 4
    return pl.pallas_call(
        _dft_outer_kernel, grid=(bsz, m // tn),
        in_specs=[pl.BlockSpec(f_hi.shape, lambda b, j: (0, 0)), pl.BlockSpec(f_lo.shape, lambda b, j: (0, 0)),
                  pl.BlockSpec((1, k, tn), lambda b, j: (b, 0, j))],
        out_specs=pl.BlockSpec((1, r, tn), lambda b, j: (b, 0, j)),
        out_shape=jax.ShapeDtypeStruct((bsz, r, m), F32),
        compiler_params=_params(("parallel", "parallel"), vmem), name="hyena_dft_outer",
    )(f_hi, f_lo, x)


def _dft_inner_kernel(m_hi_ref, m_lo_ref, mt_hi_ref, mt_lo_ref, a_ref, h_ref, o_ref, *, conv):
    n2 = a_ref.shape[3]
    for bi in range(a_ref.shape[0]):
        x = a_ref[bi, :, 0].reshape(2 * n2, a_ref.shape[4])
        y = _dot3(m_hi_ref[0], m_lo_ref[0], x)
        if conv:
            h = h_ref[0, :, 0].reshape(2 * n2, h_ref.shape[4])
            yr, yi, hr, hi = y[:n2], y[n2:], h[:n2], h[n2:]
            prod = jnp.concatenate([yr * hr - yi * hi, yr * hi + yi * hr], axis=0)
            y = _dot3(mt_hi_ref[0], mt_lo_ref[0], prod)
        o_ref[bi, :, 0] = y.reshape(2, n2, y.shape[1])


def _dft_inner(a, h, mats, conv):
    bsz, _, n1, n2, c = a.shape
    bt = DFT_BATCH if bsz % DFT_BATCH == 0 else 1
    blk = lambda rows, sel: pl.BlockSpec((rows, 2, 1, n2, c), sel)
    mat = pl.BlockSpec((1, 2 * n2, 2 * n2), lambda k, b: (k, 0, 0))
    vmem = 2 * (4 * 4 * n2 * n2 * 2 + (2 * bt + 1) * 2 * n2 * c * 4) + 8 * bt * 2 * n2 * c * 4
    return pl.pallas_call(
        functools.partial(_dft_inner_kernel, conv=conv), grid=(n1, bsz // bt),
        in_specs=[mat, mat, mat, mat, blk(bt, lambda k, b: (b, 0, k, 0, 0)), blk(1, lambda k, b: (0, 0, k, 0, 0))],
        out_specs=blk(bt, lambda k, b: (b, 0, k, 0, 0)), out_shape=jax.ShapeDtypeStruct(a.shape, F32),
        compiler_params=_params(("parallel", "arbitrary"), vmem),
        name="hyena_dft_inner_conv" if conv else "hyena_dft_inner",
    )(*mats, a, h)


def _dft_final_kernel(fd_hi_ref, fd_lo_ref, a_ref, z_ref, x0_ref, skip_ref, l1_ref, o_ref):
    y = _dot3(fd_hi_ref[...], fd_lo_ref[...], a_ref[0])
    o_ref[0] = (y / l1_ref[...] + z_ref[0] * skip_ref[...]) * x0_ref[0]


def _dft_final(a, z, x0, skip_t, l1_t, fd_hi, fd_lo):
    bsz, r2, m = a.shape
    k = fd_hi.shape[0]
    tn = min(DFT_COLS, m)
    row = pl.BlockSpec((1, k, tn), lambda b, j: (b, 0, j))
    vmem = 2 * (2 * fd_hi.size * 2 + r2 * tn * 4 + 3 * k * tn * 4) + 3 * (k + r2) * tn * 4
    return pl.pallas_call(
        _dft_final_kernel, grid=(bsz, m // tn),
        in_specs=[pl.BlockSpec(fd_hi.shape, lambda b, j: (0, 0)), pl.BlockSpec(fd_lo.shape, lambda b, j: (0, 0)),
                  pl.BlockSpec((1, r2, tn), lambda b, j: (b, 0, j)), row, row,
                  pl.BlockSpec((1, tn), lambda b, j: (0, 0)), pl.BlockSpec((1, tn), lambda b, j: (0, 0))],
        out_specs=row, out_shape=jax.ShapeDtypeStruct((bsz, k, m), F32),
        compiler_params=_params(("parallel", "parallel"), vmem), name="hyena_dft_final",
    )(fd_hi, fd_lo, a, z, x0, skip_t, l1_t)


def _hyena_long_conv(z, x0, k, l1, skip):
    bsz, n, c = z.shape
    n2 = DFT_INNER
    nn = 2 * n
    n1 = nn // n2
    assert nn == n1 * n2 and n1 % 2 == 0
    half = n1 // 2
    m = n2 * c
    j1 = jnp.arange(n1)
    ca, sa = _cis(j1[:, None] * j1[None, :], n1)
    fa = jnp.concatenate([ca, -sa], axis=0)
    fd = jnp.concatenate([ca, -sa], axis=1)[:half] / nn
    j2 = jnp.arange(n2)
    cb, sb = _cis(j2[None, None, :] * (n1 * j2[None, :, None] + j1[:, None, None]), nn)
    mb = jnp.concatenate([jnp.concatenate([cb, sb], 2), jnp.concatenate([-sb, cb], 2)], 1)
    mats = _split(mb) + _split(jnp.swapaxes(mb, 1, 2))
    fa_hi, fa_lo = _split(fa)
    hk = _dft_outer(k.reshape(1, n1, m), fa_hi, fa_lo).reshape(1, 2, n1, n2, c)
    hk = _dft_inner(hk, hk, mats, conv=False)
    a = _dft_outer(z.reshape(bsz, half, m), fa_hi[:, :half], fa_lo[:, :half]).reshape(bsz, 2, n1, n2, c)
    a = _dft_inner(a, hk, mats, conv=True).reshape(bsz, 2 * n1, m)
    reps = (1, min(DFT_COLS, m) // c)
    out = _dft_final(a, z.reshape(bsz, half, m), x0.reshape(bsz, half, m), jnp.tile(skip.reshape(1, c), reps),
                     jnp.tile(l1, reps), *_split(fd))
    return out.reshape(bsz, n, c)


def _short_conv_kernel(f_hi_ref, f_lo_ref, fi_hi_ref, fi_lo_ref, z_ref, x0_ref, k_ref, skip_ref, l1_ref, o_ref):
    n = z_ref.shape[1]
    nn = 2 * n
    z = z_ref[0]
    hk = _dot3(f_hi_ref[...], f_lo_ref[...], k_ref[...])
    zs = _dot3(f_hi_ref[:, :n], f_lo_ref[:, :n], z)
    zr, zi, hr, hi = zs[:nn], zs[nn:], hk[:nn], hk[nn:]
    prod = jnp.concatenate([zr * hr - zi * hi, zr * hi + zi * hr], axis=0)
    o_ref[0] = (_dot3(fi_hi_ref[...], fi_lo_ref[...], prod) / l1_ref[...] + z * skip_ref[...]) * x0_ref[0]


def _hyena_short_conv(z, x0, k, l1, skip):
    bsz, n, c = z.shape
    nn = 2 * n
    idx = jnp.arange(nn)
    cf, sf = _cis(idx[:, None] * idx[None, :], nn)
    f = jnp.concatenate([cf, -sf], axis=0)
    fi = jnp.concatenate([cf, -sf], axis=1)[:n] / nn
    full = lambda a: pl.BlockSpec(a.shape, lambda b: (0,) * a.ndim)
    row = pl.BlockSpec((1, n, c), lambda b: (b, 0, 0))
    ops = _split(f) + _split(fi)
    vmem = 2 * (sum(a.size * 2 for a in ops) + 3 * n * c * 4 + nn * c * 4) + 12 * 2 * nn * c * 4
    return pl.pallas_call(
        _short_conv_kernel, grid=(bsz,),
        in_specs=[full(a) for a in ops] + [row, row, full(k), pl.BlockSpec((1, c), lambda b: (0, 0)),
                                           pl.BlockSpec((1, c), lambda b: (0, 0))],
        out_specs=row, out_shape=jax.ShapeDtypeStruct((bsz, n, c), F32),
        compiler_params=_params(("parallel",), vmem), name="hyena_short_conv",
    )(*ops, z, x0, k, skip.reshape(1, c), l1)


def _hyena_sequence(z, x0, filt, skip):
    n = z.shape[1]
    k, l1 = _hyena_filter(n, *filt)
    conv = _hyena_short_conv if n <= HY_DIRECT_MAX else _hyena_long_conv
    return conv(z, x0, k, l1, skip)


def _rms(x, g):
    return x * lax.rsqrt(jnp.mean(x * x, -1, keepdims=True) + RMS_EPS) * g


def _mla_in_kernel(xc_ref, xp_ref, xn_ref, sh_ref, sc_ref, why_ref, cw_ref, cb_ref, wql_ref, qg_ref, wqa_ref, wqb_ref,
                   wkvl_ref, kvg_ref, wk_ref, wv_ref, wpa_ref, wpb_ref, qa_ref, qb_ref, ka_ref, kb_ref,
                   x0_ref, z_ref, q_ref, k_ref, vt_ref):
    tm = xc_ref.shape[1]
    rows = jnp.concatenate([xp_ref[0], xc_ref[0], xn_ref[0]], axis=0)
    h = (rows * (1.0 + sc_ref[0]) + sh_ref[0]).astype(BF16)
    hc = h[SUBLANES:SUBLANES + tm]
    y = _conv3(jnp.dot(h, why_ref[...], preferred_element_type=F32), cw_ref, cb_ref, tm)
    x0_ref[0] = y[:, :HY_WIDTH]
    z_ref[0] = y[:, 2 * HY_WIDTH:] * y[:, HY_WIDTH:2 * HY_WIDTH]
    ql = _rms(jnp.dot(hc, wql_ref[...], preferred_element_type=F32), qg_ref[...]).astype(BF16)
    nh = q_ref.shape[1]
    qa = jnp.concatenate([qa_ref[...]] * nh, axis=1)
    qb = jnp.concatenate([qb_ref[...]] * nh, axis=1)
    q = (jnp.dot(ql, wqa_ref[...], preferred_element_type=F32) * qa
         + jnp.dot(ql, wqb_ref[...], preferred_element_type=F32) * qb).astype(BF16)
    kvl = _rms(jnp.dot(hc, wkvl_ref[...], preferred_element_type=F32), kvg_ref[...]).astype(BF16)
    kn = jnp.dot(kvl, wk_ref[...], preferred_element_type=F32)
    v_t = jnp.dot(kvl, wv_ref[...], preferred_element_type=F32).T
    kpe = (jnp.dot(hc, wpa_ref[...], preferred_element_type=F32) * ka_ref[...]
           + jnp.dot(hc, wpb_ref[...], preferred_element_type=F32) * kb_ref[...])
    ones = jnp.ones((ATT_ONES_ROWS, tm), BF16)
    for hd in range(nh):
        q_ref[0, hd] = q[:, hd * MLA_SLAB:(hd + 1) * MLA_SLAB]
        k_ref[0, hd] = (kn[:, hd * MLA_SLAB:(hd + 1) * MLA_SLAB] + kpe).astype(BF16)
        vt_ref[0, hd, :MLA_V, :] = v_t[hd * MLA_V:(hd + 1) * MLA_V].astype(BF16)
        vt_ref[0, hd, MLA_V:, :] = ones


def _rot_cols(w_pe):
    ev, od = w_pe[..., 0::2], w_pe[..., 1::2]
    return jnp.concatenate([ev, od], -1), jnp.concatenate([-od, ev], -1)


def _mla_weights(p):
    w_in = p['w_in']
    kin = w_in.shape[0]
    wq = p['w_qb'].reshape(MLA_Q_RANK, MLA_HEADS, MLA_QK)
    qa_pe, qb_pe = _rot_cols(wq[..., MLA_NOPE:])
    zq = jnp.zeros((MLA_Q_RANK, MLA_HEADS, MLA_SLAB - MLA_QK), F32)
    w_qa = jnp.concatenate([wq[..., :MLA_NOPE], qa_pe, zq], -1).reshape(MLA_Q_RANK, MLA_HEADS * MLA_SLAB)
    w_qb = jnp.concatenate([jnp.zeros_like(wq[..., :MLA_NOPE]), qb_pe, zq], -1)
    w_qb = w_qb.reshape(MLA_Q_RANK, MLA_HEADS * MLA_SLAB)
    wkv = p['w_kvb'].reshape(MLA_KV_RANK, MLA_HEADS, MLA_NOPE + MLA_V)
    w_k = jnp.concatenate([wkv[..., :MLA_NOPE], jnp.zeros((MLA_KV_RANK, MLA_HEADS, MLA_SLAB - MLA_NOPE), F32)], -1)
    w_k = w_k.reshape(MLA_KV_RANK, MLA_HEADS * MLA_SLAB)
    w_v = wkv[..., MLA_NOPE:].reshape(MLA_KV_RANK, MLA_HEADS * MLA_V)
    pa, pb = _rot_cols(w_in[:, OFF_KPE:])
    left, right = jnp.zeros((kin, MLA_NOPE), F32), jnp.zeros((kin, MLA_SLAB - MLA_QK), F32)
    w_pa = jnp.concatenate([left, pa, right], -1)
    w_pb = jnp.concatenate([left, pb, right], -1)
    bf = lambda a: a.astype(BF16)
    return dict(w_hy=bf(w_in[:, :OFF_Q]), conv_w=p['conv_w'], conv_b=p['conv_b'][None],
                w_ql=bf(w_in[:, OFF_Q:OFF_KV]), q_g=p['q_norm'][None], w_qa=bf(w_qa), w_qb=bf(w_qb),
                w_kvl=bf(w_in[:, OFF_KV:OFF_KPE]), kv_g=p['kv_norm'][None], w_k=bf(w_k), w_v=bf(w_v),
                w_pa=bf(w_pa), w_pb=bf(w_pb))


def _rope_slabs(n, rotate):
    one = jnp.ones((n, MLA_NOPE), F32)
    zero = jnp.zeros((n, MLA_SLAB - MLA_QK), F32)
    if rotate:
        cos, sin = _rope_tables(n)
    else:
        cos, sin = jnp.ones((n, MLA_ROPE // 2), F32), jnp.zeros((n, MLA_ROPE // 2), F32)
    return (jnp.concatenate([one, cos, cos, zero], -1), jnp.concatenate([jnp.zeros_like(one), sin, sin, zero], -1))


def _mla_in_proj(x, shift, scale, w, rotate):
    bsz, n, kin = x.shape
    tm = min(ROW_TILE, n)
    assert n % tm == 0
    ca, sb = _rope_slabs(n, rotate)
    s = MLA_SCALE * LOG2E
    tabs = [ca * s, sb * s, ca, sb]
    consts = [w['w_hy'], w['conv_w'], w['conv_b'], w['w_ql'], w['q_g'], w['w_qa'], w['w_qb'], w['w_kvl'], w['kv_g'],
              w['w_k'], w['w_v'], w['w_pa'], w['w_pb']]
    full = lambda a: pl.BlockSpec(a.shape, lambda b, i: (0,) * a.ndim)
    row = lambda c: pl.BlockSpec((1, tm, c), lambda b, i: (b, i, 0))
    head = pl.BlockSpec((1, MLA_HEADS, tm, MLA_SLAB), lambda b, i: (b, 0, i, 0))
    dva = MLA_V + ATT_ONES_ROWS
    vmem = 2 * (tm * kin * 4 + sum(a.size * a.dtype.itemsize for a in consts) + 2 * tm * HY_WIDTH * 4
                + 3 * MLA_HEADS * tm * MLA_SLAB * 2 + 4 * tm * MLA_SLAB * 4) + 8 * (tm + 16) * 3 * HY_WIDTH * 4
    return pl.pallas_call(
        _mla_in_kernel, grid=(bsz, n // tm),
        in_specs=_halo_specs(n, tm, kin) + [_bcast_spec(shift), _bcast_spec(scale)] + [full(a) for a in consts]
        + [pl.BlockSpec((tm, MLA_SLAB), lambda b, i: (i, 0))] * 4,
        out_specs=[row(HY_WIDTH), row(HY_WIDTH), head, head,
                   pl.BlockSpec((1, MLA_HEADS, dva, tm), lambda b, i: (b, 0, 0, i))],
        out_shape=[jax.ShapeDtypeStruct((bsz, n, HY_WIDTH), F32)] * 2
        + [jax.ShapeDtypeStruct((bsz, MLA_HEADS, n, MLA_SLAB), BF16)] * 2
        + [jax.ShapeDtypeStruct((bsz, MLA_HEADS, dva, n), BF16)],
        compiler_params=_params(("parallel", "parallel"), vmem), name="mla_in_proj",
    )(x, x, x, shift, scale, *consts, *tabs)


def _mixer_hyena_mla(x, ctx, sh_l, sc_l, sh_c, sc_c, p):
    n, nc = x.shape[1], ctx.shape[1]
    w = _mla_weights(p)
    x0_l, z_l, q_l, k_l, vt_l = _mla_in_proj(x, sh_l, sc_l, w, rotate=True)
    x0_c, z_c, q_c, k_c, vt_c = _mla_in_proj(ctx, sh_c, sc_c, w, rotate=False)
    k_all = jnp.concatenate([k_c, k_l], 2)
    vt_all = jnp.concatenate([vt_c, vt_l], 3)
    att_l = _attention(q_l, k_all, vt_all, tq=min(ATT_Q_TILE, n), tk=_key_tile(nc + n))
    att_c = _attention(q_c, k_c, vt_c, tq=nc, tk=_key_tile(nc))
    filt = (p['filt_w1'], p['filt_b1'], p['filt_freq1'], p['filt_w2'], p['filt_b2'], p['filt_freq2'], p['filt_w3'])
    hyo_l = _hyena_sequence(z_l, x0_l, filt, p['skip'])
    hyo_c = _hyena_sequence(z_c, x0_c, filt, p['skip'])
    return (hyo_l, att_l), (hyo_c, att_c)


def _halo_specs(n, tm, k):
    nb = n // SUBLANES
    per = tm // SUBLANES
    return [pl.BlockSpec((1, tm, k), lambda b, i: (b, i, 0)),
            pl.BlockSpec((1, SUBLANES, k), lambda b, i: (b, jnp.maximum(i * per - 1, 0), 0)),
            pl.BlockSpec((1, SUBLANES, k), lambda b, i: (b, jnp.minimum((i + 1) * per, nb - 1), 0))]


def _conv3(u, cw_ref, cb_ref, tm):
    i = pl.program_id(1)
    rows = lax.broadcasted_iota(I32, (u.shape[0], 1), 0)
    inside = jnp.logical_and(jnp.logical_or(rows >= SUBLANES, i > 0),
                             jnp.logical_or(rows < tm + SUBLANES, i < pl.num_programs(1) - 1))
    u = jnp.where(inside, u, 0.0)
    prev = pltpu.roll(u, 1, 0)[SUBLANES:SUBLANES + tm]
    nxt = pltpu.roll(u, u.shape[0] - 1, 0)[SUBLANES:SUBLANES + tm]
    return cw_ref[0:1, :] * prev + cw_ref[1:2, :] * u[SUBLANES:SUBLANES + tm] + cw_ref[2:3, :] * nxt + cb_ref[...]


def _ssd_in_kernel(xc_ref, xp_ref, xn_ref, sh_ref, sc_ref, wz_ref, wx_ref, wdt_ref, cw_ref, cb_ref, dtb_ref,
                   z_ref, xs_ref, b_ref, c_ref, dt_ref):
    tm = xc_ref.shape[1]
    rows = jnp.concatenate([xp_ref[0], xc_ref[0], xn_ref[0]], axis=0)
    h = (rows * (1.0 + sc_ref[0]) + sh_ref[0]).astype(BF16)
    hc = h[SUBLANES:SUBLANES + tm]
    z_ref[0] = jnp.dot(hc, wz_ref[...], preferred_element_type=F32)
    y = _conv3(jnp.dot(h, wx_ref[...], preferred_element_type=F32), cw_ref, cb_ref, tm)
    xbc = y * jax.nn.sigmoid(y)
    xs_ref[0] = xbc[:, :SSD_INNER]
    b_ref[0] = xbc[:, SSD_INNER:SSD_INNER + SSD_BC]
    c_ref[0] = xbc[:, SSD_INNER + SSD_BC:]
    dt = jnp.dot(hc, wdt_ref[...], preferred_element_type=F32) + dtb_ref[...]
    dt_ref[0] = (jnp.maximum(dt, 0.0) + jnp.log1p(jnp.exp(-jnp.abs(dt)))).T


def _ssd_in_proj(x, shift, scale, w_z, w_xbc, w_dt, conv_w, conv_b, dt_bias):
    bsz, n, k = x.shape
    tm = min(ROW_TILE, n)
    assert n % tm == 0
    full = lambda a: pl.BlockSpec(a.shape, lambda b, i: (0,) * a.ndim)
    row = lambda c: pl.BlockSpec((1, tm, c), lambda b, i: (b, i, 0))
    nh2 = w_dt.shape[1]
    consts = [w_z, w_xbc, w_dt, conv_w, conv_b, dt_bias]
    widths = [SSD_INNER, SSD_INNER, SSD_BC, SSD_BC]
    vmem = 2 * (tm * k * 4 + sum(a.size * a.dtype.itemsize for a in consts) + tm * (sum(widths) + nh2) * 4) \
        + 6 * (tm + 2 * SUBLANES) * SSD_XBC * 4
    return pl.pallas_call(
        _ssd_in_kernel, grid=(bsz, n // tm),
        in_specs=_halo_specs(n, tm, k) + [_bcast_spec(shift), _bcast_spec(scale)] + [full(a) for a in consts],
        out_specs=[row(c) for c in widths] + [pl.BlockSpec((1, nh2, tm), lambda b, i: (b, 0, i))],
        out_shape=[jax.ShapeDtypeStruct((bsz, n, c), F32) for c in widths]
        + [jax.ShapeDtypeStruct((bsz, nh2, n), F32)],
        compiler_params=_params(("parallel", "parallel"), vmem), name="ssd_in_proj",
    )(x, x, x, shift, scale, *consts)


def _ssd_chunk(x_ref, b_ref, c_ref, dt_ref, a_ref, st_sc, y_ref, reverse):
    q = SSD_CHUNK
    dt = dt_ref[0]
    a = dt * a_ref[0]
    si = lax.broadcasted_iota(I32, (q, q), 0)
    li = lax.broadcasted_iota(I32, (q, q), 1)
    incl = jnp.where((si >= li) if reverse else (si <= li), 1.0, 0.0)
    hp = lax.Precision.HIGHEST
    acs = jnp.dot(a, incl, precision=hp, preferred_element_type=F32)
    tot = jnp.dot(a, jnp.ones((q, LANES), F32), precision=hp, preferred_element_type=F32)
    e_in = jnp.exp(acs)
    w_end = jnp.exp(tot - acs) * dt
    e_tot = jnp.exp(tot)
    acs_t = acs.T
    e_in_t = e_in.T
    mask = (li >= si) if reverse else (li <= si)
    cmat = c_ref[0]
    cb = lax.dot_general(cmat.astype(BF16), b_ref[0].astype(BF16), (((1,), (1,)), ((), ())),
                         preferred_element_type=F32)
    bt = b_ref[0].T
    first = lax.broadcasted_iota(I32, (q, LANES), 1) < SSD_HEADDIM
    ys = []
    for pr in range(SSD_HPG // 2):
        lo, hi = pr * LANES, (pr + 1) * LANES
        x_pair = x_ref[0, :, lo:hi].astype(BF16)
        st_pair = st_sc[:, lo:hi]
        rhs = jnp.concatenate([x_pair, st_pair.astype(BF16)], axis=0)
        y2, s2 = [], []
        for r in (2 * pr, 2 * pr + 1):
            seg = acs_t[:, r:r + 1] - acs[r:r + 1, :]
            m = cb * jnp.exp(jnp.where(mask, seg, -jnp.inf)) * dt[r:r + 1, :]
            lhs = jnp.concatenate([m.astype(BF16), (cmat * e_in_t[:, r:r + 1]).astype(BF16)], axis=1)
            y2.append(jnp.dot(lhs, rhs, preferred_element_type=F32))
            btr = (bt * w_end[r:r + 1, :]).astype(BF16)
            s2.append(e_tot[r:r + 1, :] * st_pair + jnp.dot(btr, x_pair, preferred_element_type=F32))
        ys.append(jnp.where(first, y2[0], y2[1]))
        st_sc[:, lo:hi] = jnp.where(first, s2[0], s2[1])
    y_ref[0] = jnp.concatenate(ys, axis=1)


def _ssd_scan_kernel(xf_ref, bf_ref, cf_ref, dtf_ref, af_ref, s0f_ref, xr_ref, br_ref, cr_ref, dtr_ref, ar_ref,
                     s0r_ref, yf_ref, sf_ref, yr_ref, sr_ref, stf_sc, str_sc):
    ci = pl.program_id(2)

    @pl.when(ci == 0)
    def _():
        stf_sc[...] = s0f_ref[0, 0]
        str_sc[...] = s0r_ref[0, 0]

    _ssd_chunk(xf_ref, bf_ref, cf_ref, dtf_ref, af_ref, stf_sc, yf_ref, reverse=False)
    _ssd_chunk(xr_ref, br_ref, cr_ref, dtr_ref, ar_ref, str_sc, yr_ref, reverse=True)

    @pl.when(ci == pl.num_programs(2) - 1)
    def _():
        sf_ref[0, 0] = stf_sc[...]
        sr_ref[0, 0] = str_sc[...]


def _ssd_scan(xs, bm, cm, dt_t, a_rep, s0_f, s0_r):
    bsz, n, _ = xs.shape
    q = SSD_CHUNK
    nc = n // q
    assert n % q == 0 and q == LANES
    gw = SSD_HPG * SSD_HEADDIM
    state = pl.BlockSpec((1, 1, SSD_STATE, gw), lambda b, g, c: (b, g, 0, 0))

    def side(d):
        cc = (lambda c: nc - 1 - c) if d else (lambda c: c)
        specs = [pl.BlockSpec((1, q, gw), lambda b, g, c: (b, cc(c), g)),
                 pl.BlockSpec((1, q, SSD_STATE), lambda b, g, c: (b, cc(c), g)),
                 pl.BlockSpec((1, q, SSD_STATE), lambda b, g, c: (b, cc(c), g)),
                 pl.BlockSpec((1, SSD_HPG, q), lambda b, g, c: (b, d * SSD_GROUPS + g, cc(c))),
                 pl.BlockSpec((1, SSD_HPG, LANES), lambda b, g, c: (d * SSD_GROUPS + g, 0, 0)),
                 state]
        return specs, [pl.BlockSpec((1, q, gw), lambda b, g, c: (b, cc(c), g)), state]

    (in_f, out_f), (in_r, out_r) = side(0), side(1)
    shapes = [jax.ShapeDtypeStruct((bsz, n, SSD_INNER), F32),
              jax.ShapeDtypeStruct((bsz, SSD_GROUPS, SSD_STATE, gw), F32)]
    vmem = 4 * (2 * q * gw * 4 + 2 * q * SSD_STATE * 4 + 2 * SSD_STATE * gw * 4) + 2 * SSD_STATE * gw * 4 \
        + 128 * q * q * 4
    return pl.pallas_call(
        _ssd_scan_kernel, grid=(bsz, SSD_GROUPS, nc), in_specs=in_f + in_r, out_specs=out_f + out_r,
        out_shape=shapes + shapes,
        scratch_shapes=[pltpu.VMEM((SSD_STATE, gw), F32), pltpu.VMEM((SSD_STATE, gw), F32)],
        compiler_params=_params(("parallel", "parallel", "arbitrary"), vmem), name="ssd_scan",
    )(xs, bm, cm, dt_t, a_rep, s0_f, xs, bm, cm, dt_t, a_rep, s0_r)


def _ssd_out_kernel(x_ref, yf_ref, yb_ref, xs_ref, z_ref, d_ref, ng_ref, w_ref, *epilogue_refs):
    z = z_ref[0]
    y = (yf_ref[0] + yb_ref[0] + xs_ref[0] * d_ref[0]) * (z * jax.nn.sigmoid(z))
    gw = SSD_INNER // SSD_GROUPS
    parts = []
    for g in range(SSD_GROUPS):
        yg = y[:, g * gw:(g + 1) * gw]
        parts.append(yg * lax.rsqrt(jnp.mean(yg * yg, -1, keepdims=True) + RMS_EPS))
    yn = (jnp.concatenate(parts, axis=1) * ng_ref[0]).astype(BF16)
    _mix_epilogue(x_ref[0], jnp.dot(yn, w_ref[...], preferred_element_type=F32), *epilogue_refs)


def _ssd_out(x, y_f, y_b, xs, z, d_rep, norm_g, w_out, gate, ln_g, ln_b, shift2, scale2):
    bsz, n, d = x.shape
    tm = min(ROW_TILE, n)
    assert n % tm == 0
    row = lambda c: pl.BlockSpec((1, tm, c), lambda b, i: (b, i, 0))
    vecs = [gate, ln_g, ln_b, shift2, scale2]
    in_specs = [row(d)] + [row(SSD_INNER)] * 4 + [_bcast_spec(d_rep), _bcast_spec(norm_g),
                                                 pl.BlockSpec(w_out.shape, lambda b, i: (0, 0))]
    in_specs += [_bcast_spec(a) for a in vecs]
    vmem = 2 * (3 * tm * d * 4 + 4 * tm * SSD_INNER * 4 + w_out.size * 2) + 4 * tm * SSD_INNER * 4
    return pl.pallas_call(
        _ssd_out_kernel, grid=(bsz, n // tm), in_specs=in_specs, out_specs=[row(d), row(d), row(d // 2)],
        out_shape=[jax.ShapeDtypeStruct((bsz, n, d), F32)] * 2 + [jax.ShapeDtypeStruct((bsz, n, d // 2), U32)],
        compiler_params=_params(("parallel", "parallel"), vmem), name="ssd_out",
    )(x, y_f, y_b, xs, z, d_rep, norm_g, w_out, *vecs)


def _mixer_ssd(x, ctx, sh_l, sc_l, sh_c, sc_c, p):
    a_all = -jnp.exp(jnp.concatenate([p['a_log_f'], p['a_log_b']]))
    a_rep = jnp.broadcast_to(a_all.reshape(2 * SSD_GROUPS, SSD_HPG, 1), (2 * SSD_GROUPS, SSD_HPG, LANES))
    w_in = p['w_in'].astype(BF16)
    consts = (w_in[:, :SSD_INNER], w_in[:, SSD_INNER:SSD_INNER + SSD_XBC], w_in[:, SSD_INNER + SSD_XBC:],
              p['conv_w'], p['conv_b'][None], jnp.concatenate([p['dt_bias_f'], p['dt_bias_b']])[None])
    _, xc, bc, cc, dtc = _ssd_in_proj(ctx, sh_c, sc_c, *consts)
    zl, xl, bl, cl, dtl = _ssd_in_proj(x, sh_l, sc_l, *consts)
    s0 = jnp.zeros((ctx.shape[0], SSD_GROUPS, SSD_STATE, SSD_HPG * SSD_HEADDIM), F32)
    _, sc_f, _, sc_b = _ssd_scan(xc, bc, cc, dtc, a_rep, s0, s0)
    y_f, _, y_b, _ = _ssd_scan(xl, bl, cl, dtl, a_rep, sc_f, sc_b)
    return y_f, y_b, xl, zl


def kernel(x, c, ctx, c_ctx, mod_w, mod_b, ln_mix_g, ln_mix_b, ln_ffn_g, ln_ffn_b, a_w_in, hy_conv_w, hy_conv_b, hy_filt_w1, hy_filt_b1, hy_filt_freq1, hy_filt_w2, hy_filt_b2, hy_filt_freq2, hy_filt_w3, hy_skip, mla_q_norm, mla_w_qb, mla_kv_norm, mla_w_kvb, a_w_out, ssd_w_in, ssd_conv_w, ssd_conv_b, ssd_dt_bias_f, ssd_dt_bias_b, ssd_a_log_f, ssd_a_log_b, ssd_d, ssd_norm_g, ssd_w_out, router_w, router_bias, exp_w_gu, exp_w_down, sh_w_gu, sh_w_down):
    bsz, n_lat, d = x.shape
    n_ctx = ctx.shape[1]
    t_cap = bsz * (n_ctx + n_lat)
    spare = None
    pad = -(bsz + 1) % SUBLANES
    cond = jnp.concatenate([c, c_ctx[None], jnp.zeros((pad, d), F32)], 0)
    for l in range(DEPTH):
        last = l == DEPTH - 1
        i = l // 2
        mods = _modulation(cond, mod_w, mod_b, l)
        mod = mods[:bsz].reshape(bsz, N_MOD, 1, d)
        mod_c = mods[bsz:bsz + 1].reshape(1, N_MOD, 1, d)
        sh1, sc1, g1, sh2, sc2, g2 = [mod[:, j] for j in range(N_MOD)]
        csh1, csc1, cg1, csh2, csc2, cg2 = [mod_c[:, j] for j in range(N_MOD)]
        vec = lambda a: a.reshape(1, 1, d)
        if l % 2 == 0:
            p = {"w_in": a_w_in[i], "conv_w": hy_conv_w[i], "conv_b": hy_conv_b[i],
                 "filt_w1": hy_filt_w1[i], "filt_b1": hy_filt_b1[i], "filt_freq1": hy_filt_freq1[i],
                 "filt_w2": hy_filt_w2[i], "filt_b2": hy_filt_b2[i], "filt_freq2": hy_filt_freq2[i],
                 "filt_w3": hy_filt_w3[i], "skip": hy_skip[i], "q_norm": mla_q_norm[i], "w_qb": mla_w_qb[i],
                 "kv_norm": mla_kv_norm[i], "w_kvb": mla_w_kvb[i]}
            ys_l, ys_c = _mixer_hyena_mla(x, ctx, sh1, sc1, csh1, csc1, p)
            w_out = a_w_out[i].astype(BF16)
            ws = [w_out[:HY_WIDTH], w_out[HY_WIDTH:]]
            x, ff_x, fp_x = _mix_out(x, ys_l, ws, g1, vec(ln_mix_g[l]), vec(ln_mix_b[l]), sh2, sc2)
        else:
            p = {"w_in": ssd_w_in[i], "conv_w": ssd_conv_w[i], "conv_b": ssd_conv_b[i],
                 "dt_bias_f": ssd_dt_bias_f[i], "dt_bias_b": ssd_dt_bias_b[i],
                 "a_log_f": ssd_a_log_f[i], "a_log_b": ssd_a_log_b[i]}
            assert last
            y_f, y_b, xs, z = _mixer_ssd(x, ctx, sh1, sc1, csh1, csc1, p)
            d_rep = jnp.repeat(ssd_d[i], SSD_HEADDIM).reshape(1, 1, SSD_INNER)
            x, ff_x, fp_x = _ssd_out(x, y_f, y_b, xs, z, d_rep, ssd_norm_g[i].reshape(1, 1, SSD_INNER),
                                     ssd_w_out[i].astype(BF16), g1, vec(ln_mix_g[l]), vec(ln_mix_b[l]), sh2, sc2)
        sh_gu = sh_w_gu[l].astype(BF16)
        sh_down = sh_w_down[l].astype(BF16)
        ln_g, ln_b = vec(ln_ffn_g[l]), vec(ln_ffn_b[l])
        moe_w = (router_w[l], router_bias[l], exp_w_gu, exp_w_down, l)
        if last:
            yb, pos, w = _moe_dispatch_experts(ff_x.reshape(-1, d), fp_x.reshape(-1, d // 2), *moe_w, t_cap, spare)
            x = _ffn_out(x, ff_x, yb, pos, w, 0, sh_gu, sh_down, g2, ln_g, ln_b)
        else:
            ctx, ff_c, fp_c = _mix_out(ctx, ys_c, ws, cg1, vec(ln_mix_g[l]), vec(ln_mix_b[l]), csh2, csc2)
            tokens = jnp.concatenate([ff_c.reshape(-1, d), ff_x.reshape(-1, d)], 0)
            packed = jnp.concatenate([fp_c.reshape(-1, d // 2), fp_x.reshape(-1, d // 2)], 0)
            yb, pos, w = _moe_dispatch_experts(tokens, packed, *moe_w, t_cap, spare)
            assert (bsz * n_ctx) % ROUTE_TOKENS == 0
            ctx = _ffn_out(ctx, ff_c, yb, pos, w, 0, sh_gu, sh_down, cg2, ln_g, ln_b)
            x = _ffn_out(x, ff_x, yb, pos, w, bsz * n_ctx // ROUTE_TOKENS, sh_gu, sh_down, g2, ln_g, ln_b)
            spare = yb
    return x
```
